```python
import jax, jax.numpy as jnp
from jax import lax
import numpy as np

D_MODEL = 1024
BATCH = 8
SEQ = 16384
DEPTH = 4

N_MIXERS = 3
SWA_HEADS = 16
SWA_KV_HEADS = 4
SWA_HEAD_DIM = 64
SWA_WINDOW = 128
SWA_QKV_DIM = (SWA_HEADS + 2 * SWA_KV_HEADS) * SWA_HEAD_DIM
HGRN_HEADS = 8
HGRN_EXPAND = 128
HGRN_HEAD_DIM = D_MODEL // HGRN_HEADS
HGRN_CHUNK = 64
HGRN_IN_DIM = 2 * HGRN_HEADS * HGRN_EXPAND + 2 * HGRN_HEADS * HGRN_HEAD_DIM
FOX_HEADS = 16
FOX_HEAD_DIM = 64
FOX_BLOCK = 128
FOX_IN_DIM = 3 * FOX_HEADS * FOX_HEAD_DIM + FOX_HEADS
FOX_FORGET_BIAS_INIT = 2.0
D_FF = 4 * D_MODEL
EPS = 1e-6

N_A = (DEPTH + 2) // 3
N_B = (DEPTH + 1) // 3
N_C = DEPTH // 3

kernel_name = "hybrid_swa_hgrn2_fox_trunk"


def rms_norm(x, g):
    xf = x.astype(jnp.float32)
    y = xf * lax.rsqrt(jnp.mean(xf * xf, axis=-1, keepdims=True) + EPS)
    return (y * g.astype(jnp.float32)).astype(x.dtype)


def swa_sink_attention(h, w_qkv, b_qkv, sinks, w_o):
    B, T, _ = h.shape
    W, KV, dh = SWA_WINDOW, SWA_KV_HEADS, SWA_HEAD_DIM
    G = SWA_HEADS // KV
    nb = T // W
    qkv = h @ w_qkv + b_qkv
    q, k, v = jnp.split(qkv, [SWA_HEADS * dh, (SWA_HEADS + KV) * dh], axis=-1)
    q = q.reshape(B, nb, W, KV, G, dh) * (dh ** -0.5)
    k = k.reshape(B, nb, W, KV, dh)
    v = v.reshape(B, nb, W, KV, dh)
    k_band = jnp.concatenate([jnp.pad(k[:, :-1], ((0, 0), (1, 0), (0, 0), (0, 0), (0, 0))), k], axis=2)
    v_band = jnp.concatenate([jnp.pad(v[:, :-1], ((0, 0), (1, 0), (0, 0), (0, 0), (0, 0))), v], axis=2)
    s = jnp.einsum('bnqhgd,bnkhd->bnhgqk', q, k_band).astype(jnp.float32)
    qi = jnp.arange(W)[:, None]
    kk = jnp.arange(2 * W)[None, :]
    rel = qi + W - kk
    in_window = (rel >= 0) & (rel < W)
    blk = jnp.arange(nb)[:, None, None]
    mask = in_window[None] & ((blk > 0) | (kk[None] >= W))
    s = jnp.where(mask[None, :, None, None], s, -jnp.inf)
    sink = sinks.astype(jnp.float32).reshape(KV, G)[None, None, :, :, None, None]
    sink = jnp.broadcast_to(sink, s.shape[:-1] + (1,))
    p = jax.nn.softmax(jnp.concatenate([s, sink], axis=-1), axis=-1)[..., :-1]
    o = jnp.einsum('bnhgqk,bnkhd->bnqhgd', p.astype(v.dtype), v_band)
    return o.reshape(B, T, SWA_HEADS * dh) @ w_o


def hgrn2_mixer(h, w_in, lb, g_norm, w_o):
    B, T, _ = h.shape
    H, K, V, C = HGRN_HEADS, HGRN_EXPAND, HGRN_HEAD_DIM, HGRN_CHUNK
    nc = T // C
    proj = h @ w_in
    q, f_logit, i_in, g = jnp.split(proj, [H * K, 2 * H * K, 2 * H * K + H * V], axis=-1)
    f = lb + (1.0 - lb) * jax.nn.sigmoid(f_logit.astype(jnp.float32))
    log_f = jnp.log(f)
    k = 1.0 - f

    def to_chunks(a, d):
        return a.astype(jnp.float32).reshape(B, nc, C, H, d).transpose(1, 0, 3, 2, 4)

    xs = (to_chunks(jax.nn.silu(q), K), to_chunks(k, K), to_chunks(i_in, V), to_chunks(log_f, K))
    causal = jnp.tril(jnp.ones((C, C), dtype=bool))[:, :, None]

    def step(S, inp):
        qc, kc, vc, gc = inp
        bcum = jnp.cumsum(gc, axis=2)
        diff = jnp.where(causal, bcum[:, :, :, None, :] - bcum[:, :, None, :, :], -jnp.inf)
        A = jnp.einsum('bhtk,bhsk,bhtsk->bhts', qc, kc, jnp.exp(diff))
        o = jnp.einsum('bhts,bhsv->bhtv', A, vc) + jnp.einsum('bhtk,bhkv->bhtv', qc * jnp.exp(bcum), S)
        b_last = bcum[:, :, -1]
        S_new = jnp.exp(b_last)[..., None] * S + jnp.einsum(
            'bhsk,bhsv->bhkv', kc * jnp.exp(b_last[:, :, None, :] - bcum), vc)
        return S_new, o

    S0 = jnp.zeros((B, H, K, V), jnp.float32)
    _, o = lax.scan(step, S0, xs)
    o = o.transpose(1, 0, 3, 2, 4).reshape(B, T, H, V)
    o = o * lax.rsqrt(jnp.mean(o * o, axis=-1, keepdims=True) + EPS)
    o = o.reshape(B, T, H * V) * g_norm.astype(jnp.float32) * jax.nn.silu(g.astype(jnp.float32))
    return o.astype(h.dtype) @ w_o


def fox_attention(h, w_in, b_in, w_o):
    B, T, _ = h.shape
    H, dh, Q = FOX_HEADS, FOX_HEAD_DIM, FOX_BLOCK
    nb = T // Q
    proj = h @ w_in + b_in
    q, k, v, f_logit = jnp.split(proj, [H * dh, 2 * H * dh, 3 * H * dh], axis=-1)
    log_f = jax.nn.log_sigmoid(f_logit.astype(jnp.float32))
    c = jnp.cumsum(log_f, axis=1).transpose(0, 2, 1)
    q = q.reshape(B, T, H, dh) * (dh ** -0.5)
    k = k.reshape(B, T, H, dh)
    v = v.reshape(B, T, H, dh)
    q_blocks = q.reshape(B, nb, Q, H, dh).transpose(1, 0, 2, 3, 4)
    c_blocks = c.reshape(B, H, nb, Q).transpose(2, 0, 1, 3)
    pos_blocks = jnp.arange(T).reshape(nb, Q)
    kpos = jnp.arange(T)

    def block(args):
        qb, cb, pb = args
        s = jnp.einsum('bqhd,bkhd->bhqk', qb, k).astype(jnp.float32)
        s = s + (cb[..., :, None] - c[..., None, :])
        s = jnp.where((pb[:, None] >= kpos[None, :])[None, None], s, -jnp.inf)
        p = jax.nn.softmax(s, axis=-1)
        return jnp.einsum('bhqk,bkhd->bqhd', p.astype(v.dtype), v)

    o = lax.map(block, (q_blocks, c_blocks, pos_blocks))
    o = o.transpose(1, 0, 2, 3, 4).reshape(B, T, H * dh)
    return o @ w_o


def _fwd_setup_inputs(seed: int = 0) -> dict:
    key = jax.random.key(seed)
    ks = jax.random.split(key, 18)
    f32 = jnp.float32

    def nrm(k, shape, fan_in):
        return jax.random.normal(k, shape, f32) * (fan_in ** -0.5)

    def gain(k, shape):
        return 1.0 + 0.02 * jax.random.normal(k, shape, f32)

    fox_b_in = 0.02 * jax.random.normal(ks[16], (N_C, FOX_IN_DIM), f32)
    fox_b_in = fox_b_in.at[:, 3 * FOX_HEADS * FOX_HEAD_DIM:].add(FOX_FORGET_BIAS_INIT)
    return {
        "x": jax.random.normal(ks[0], (BATCH, SEQ, D_MODEL), f32),
        "norm_mix": gain(ks[1], (DEPTH, D_MODEL)),
        "norm_mlp": gain(ks[2], (DEPTH, D_MODEL)),
        "norm_final": gain(ks[3], (D_MODEL,)),
        "w_up": nrm(ks[4], (DEPTH, D_MODEL, D_FF), D_MODEL),
        "w_down": nrm(ks[5], (DEPTH, D_FF, D_MODEL), D_FF),
        "swa_w_qkv": nrm(ks[6], (N_A, D_MODEL, SWA_QKV_DIM), D_MODEL),
        "swa_b_qkv": 0.02 * jax.random.normal(ks[7], (N_A, SWA_QKV_DIM), f32),
        "swa_sinks": 0.5 * jax.random.normal(ks[8], (N_A, SWA_HEADS), f32),
        "swa_w_o": nrm(ks[9], (N_A, SWA_HEADS * SWA_HEAD_DIM, D_MODEL), SWA_HEADS * SWA_HEAD_DIM),
        "hgrn_w_in": nrm(ks[10], (N_B, D_MODEL, HGRN_IN_DIM), D_MODEL),
        "hgrn_lb_logits": 0.1 * jax.random.normal(ks[11], (DEPTH, HGRN_HEADS * HGRN_EXPAND), f32),
        "hgrn_g_norm": gain(ks[12], (N_B, HGRN_HEADS * HGRN_HEAD_DIM)),
        "hgrn_w_o": nrm(ks[13], (N_B, HGRN_HEADS * HGRN_HEAD_DIM, D_MODEL), HGRN_HEADS * HGRN_HEAD_DIM),
        "fox_w_in": nrm(ks[14], (N_C, D_MODEL, FOX_IN_DIM), D_MODEL),
        "fox_b_in": fox_b_in,
        "fox_w_o": nrm(ks[15], (N_C, FOX_HEADS * FOX_HEAD_DIM, D_MODEL), FOX_HEADS * FOX_HEAD_DIM),
    }


def _fwd_reference(x, norm_mix, norm_mlp, norm_final, w_up, w_down,
              swa_w_qkv, swa_b_qkv, swa_sinks, swa_w_o,
              hgrn_w_in, hgrn_lb_logits, hgrn_g_norm, hgrn_w_o,
              fox_w_in, fox_b_in, fox_w_o):
    lb_soft = jax.nn.softmax(hgrn_lb_logits.astype(jnp.float32), axis=0)
    lower_bounds = jnp.cumsum(lb_soft, axis=0) - lb_soft[0]
    for i in range(DEPTH):
        h = rms_norm(x, norm_mix[i])
        m, j = i % N_MIXERS, i // N_MIXERS
        if m == 0:
            y = swa_sink_attention(h, swa_w_qkv[j], swa_b_qkv[j], swa_sinks[j], swa_w_o[j])
        elif m == 1:
            y = hgrn2_mixer(h, hgrn_w_in[j], lower_bounds[i], hgrn_g_norm[j], hgrn_w_o[j])
        else:
            y = fox_attention(h, fox_w_in[j], fox_b_in[j], fox_w_o[j])
        x = x + y
        h = rms_norm(x, norm_mlp[i])
        x = x + jnp.square(jax.nn.relu(h @ w_up[i])) @ w_down[i]
    return rms_norm(x, norm_final)


import jax as _jax
import jax.numpy as _jnp

TWIN_FORMAT = 'train_step'
FWD_PARAMS = ['x', 'norm_mix', 'norm_mlp', 'norm_final', 'w_up', 'w_down', 'swa_w_qkv', 'swa_b_qkv', 'swa_sinks', 'swa_w_o', 'hgrn_w_in', 'hgrn_lb_logits', 'hgrn_g_norm', 'hgrn_w_o', 'fox_w_in', 'fox_b_in', 'fox_w_o']
TWIN_WEIGHTS = ['norm_mix', 'norm_mlp', 'norm_final', 'w_up', 'w_down', 'swa_w_qkv', 'swa_b_qkv', 'swa_sinks', 'swa_w_o', 'hgrn_w_in', 'hgrn_lb_logits', 'hgrn_g_norm', 'hgrn_w_o', 'fox_w_in', 'fox_b_in', 'fox_w_o']
TWIN_DIFF_INPUT = 'x'
TWIN_INPUTS = ['x', 'norm_mix', 'norm_mlp', 'norm_final', 'w_up', 'w_down', 'swa_w_qkv', 'swa_b_qkv', 'swa_sinks', 'swa_w_o', 'hgrn_w_in', 'hgrn_lb_logits', 'hgrn_g_norm', 'hgrn_w_o', 'fox_w_in', 'fox_b_in', 'fox_w_o', 'loss_target', 'm_norm_mix', 'm_norm_mlp', 'm_norm_final', 'm_w_up', 'm_w_down', 'm_swa_w_qkv', 'm_swa_b_qkv', 'm_swa_sinks', 'm_swa_w_o', 'm_hgrn_w_in', 'm_hgrn_lb_logits', 'm_hgrn_g_norm', 'm_hgrn_w_o', 'm_fox_w_in', 'm_fox_b_in', 'm_fox_w_o', 'v_norm_mix', 'v_norm_mlp', 'v_norm_final', 'v_w_up', 'v_w_down', 'v_swa_w_qkv', 'v_swa_b_qkv', 'v_swa_sinks', 'v_swa_w_o', 'v_hgrn_w_in', 'v_hgrn_lb_logits', 'v_hgrn_g_norm', 'v_hgrn_w_o', 'v_fox_w_in', 'v_fox_b_in', 'v_fox_w_o']
TWIN_OUTPUTS = ['loss', 'grad_x', 'grad_norm_mix', 'grad_norm_mlp', 'grad_norm_final', 'grad_w_up', 'grad_w_down', 'grad_swa_w_qkv', 'grad_swa_b_qkv', 'grad_swa_sinks', 'grad_swa_w_o', 'grad_hgrn_w_in', 'grad_hgrn_lb_logits', 'grad_hgrn_g_norm', 'grad_hgrn_w_o', 'grad_fox_w_in', 'grad_fox_b_in', 'grad_fox_w_o', 'delta_norm_mix', 'delta_norm_mlp', 'delta_norm_final', 'delta_w_up', 'delta_w_down', 'delta_swa_w_qkv', 'delta_swa_b_qkv', 'delta_swa_sinks', 'delta_swa_w_o', 'delta_hgrn_w_in', 'delta_hgrn_lb_logits', 'delta_hgrn_g_norm', 'delta_hgrn_w_o', 'delta_fox_w_in', 'delta_fox_b_in', 'delta_fox_w_o', 'new_m_norm_mix', 'new_m_norm_mlp', 'new_m_norm_final', 'new_m_w_up', 'new_m_w_down', 'new_m_swa_w_qkv', 'new_m_swa_b_qkv', 'new_m_swa_sinks', 'new_m_swa_w_o', 'new_m_hgrn_w_in', 'new_m_hgrn_lb_logits', 'new_m_hgrn_g_norm', 'new_m_hgrn_w_o', 'new_m_fox_w_in', 'new_m_fox_b_in', 'new_m_fox_w_o', 'new_v_norm_mix', 'new_v_norm_mlp', 'new_v_norm_final', 'new_v_w_up', 'new_v_w_down', 'new_v_swa_w_qkv', 'new_v_swa_b_qkv', 'new_v_swa_sinks', 'new_v_swa_w_o', 'new_v_hgrn_w_in', 'new_v_hgrn_lb_logits', 'new_v_hgrn_g_norm', 'new_v_hgrn_w_o', 'new_v_fox_w_in', 'new_v_fox_b_in', 'new_v_fox_w_o']
TWIN_LEAF_KINDS = {'loss': 'loss', 'grad_x': 'grad_x', 'grad_norm_mix': 'grad_w', 'grad_norm_mlp': 'grad_w', 'grad_norm_final': 'grad_w', 'grad_w_up': 'grad_w', 'grad_w_down': 'grad_w', 'grad_swa_w_qkv': 'grad_w', 'grad_swa_b_qkv': 'grad_w', 'grad_swa_sinks': 'grad_w', 'grad_swa_w_o': 'grad_w', 'grad_hgrn_w_in': 'grad_w', 'grad_hgrn_lb_logits': 'grad_w', 'grad_hgrn_g_norm': 'grad_w', 'grad_hgrn_w_o': 'grad_w', 'grad_fox_w_in': 'grad_w', 'grad_fox_b_in': 'grad_w', 'grad_fox_w_o': 'grad_w', 'delta_norm_mix': 'delta_w', 'delta_norm_mlp': 'delta_w', 'delta_norm_final': 'delta_w', 'delta_w_up': 'delta_w', 'delta_w_down': 'delta_w', 'delta_swa_w_qkv': 'delta_w', 'delta_swa_b_qkv': 'delta_w', 'delta_swa_sinks': 'delta_w', 'delta_swa_w_o': 'delta_w', 'delta_hgrn_w_in': 'delta_w', 'delta_hgrn_lb_logits': 'delta_w', 'delta_hgrn_g_norm': 'delta_w', 'delta_hgrn_w_o': 'delta_w', 'delta_fox_w_in': 'delta_w', 'delta_fox_b_in': 'delta_w', 'delta_fox_w_o': 'delta_w', 'new_m_norm_mix': 'new_m', 'new_m_norm_mlp': 'new_m', 'new_m_norm_final': 'new_m', 'new_m_w_up': 'new_m', 'new_m_w_down': 'new_m', 'new_m_swa_w_qkv': 'new_m', 'new_m_swa_b_qkv': 'new_m', 'new_m_swa_sinks': 'new_m', 'new_m_swa_w_o': 'new_m', 'new_m_hgrn_w_in': 'new_m', 'new_m_hgrn_lb_logits': 'new_m', 'new_m_hgrn_g_norm': 'new_m', 'new_m_hgrn_w_o': 'new_m', 'new_m_fox_w_in': 'new_m', 'new_m_fox_b_in': 'new_m', 'new_m_fox_w_o': 'new_m', 'new_v_norm_mix': 'new_v', 'new_v_norm_mlp': 'new_v', 'new_v_norm_final': 'new_v', 'new_v_w_up': 'new_v', 'new_v_w_down': 'new_v', 'new_v_swa_w_qkv': 'new_v', 'new_v_swa_b_qkv': 'new_v', 'new_v_swa_sinks': 'new_v', 'new_v_swa_w_o': 'new_v', 'new_v_hgrn_w_in': 'new_v', 'new_v_hgrn_lb_logits': 'new_v', 'new_v_hgrn_g_norm': 'new_v', 'new_v_hgrn_w_o': 'new_v', 'new_v_fox_w_in': 'new_v', 'new_v_fox_b_in': 'new_v', 'new_v_fox_w_o': 'new_v'}


def _forward(args):
    return _fwd_reference(*[args[k] for k in FWD_PARAMS])


def _output_shape():
    def fwd():
        inp = _fwd_setup_inputs(0)
        return _fwd_reference(*[inp[k] for k in FWD_PARAMS])
    out = _jax.eval_shape(fwd)
    return out.shape, out.dtype

N_MICROBATCH = 1
ADAM_LR = 0.001
ADAM_B1 = 0.9
ADAM_B2 = 0.999
ADAM_EPS = 1e-08
ADAM_WD = 0.01
ADAM_STEP = 10
PER_EXAMPLE_BATCH_AXIS = {'x': 0, 'loss_target': 0}
SHARED_INPUTS = []
_WEIGHT_DTYPES = {'norm_mix': _jnp.float32, 'norm_mlp': _jnp.float32, 'norm_final': _jnp.float32, 'w_up': _jnp.float32, 'w_down': _jnp.float32, 'swa_w_qkv': _jnp.float32, 'swa_b_qkv': _jnp.float32, 'swa_sinks': _jnp.float32, 'swa_w_o': _jnp.float32, 'hgrn_w_in': _jnp.float32, 'hgrn_lb_logits': _jnp.float32, 'hgrn_g_norm': _jnp.float32, 'hgrn_w_o': _jnp.float32, 'fox_w_in': _jnp.float32, 'fox_b_in': _jnp.float32, 'fox_w_o': _jnp.float32}
MOMENT_SCALE = {'norm_mix': 1.690583e-01, 'norm_mlp': 2.970739e-01, 'norm_final': 1.334978e+02, 'w_up': 1.441204e-01, 'w_down': 3.494129e-01, 'swa_w_qkv': 9.857922e-02, 'swa_b_qkv': 4.986948e-01, 'swa_sinks': 4.404261e-02, 'swa_w_o': 9.537563e-02, 'hgrn_w_in': 1.080681e-01, 'hgrn_lb_logits': 3.721601e-03, 'hgrn_g_norm': 1.425579e-01, 'hgrn_w_o': 1.639248e-01, 'fox_w_in': 1.162296e-01, 'fox_b_in': 3.492011e-01, 'fox_w_o': 1.898544e-01}


def _to_microbatches(a, axis):
    t = _jnp.moveaxis(a, axis, 0)
    t = t.reshape((N_MICROBATCH, t.shape[0] // N_MICROBATCH) + t.shape[1:])
    return _jnp.moveaxis(t, 1, axis + 1)


def setup_inputs(seed: int = 0) -> dict:
    inp = _fwd_setup_inputs(seed)
    key = _jax.random.fold_in(_jax.random.key(seed), 7919)
    shape, _ = _output_shape()
    out = dict(inp)
    out["loss_target"] = _jax.random.normal(_jax.random.fold_in(key, 0), shape, _jnp.float32)
    for i, name in enumerate(TWIN_WEIGHTS):
        w = inp[name].astype(_jnp.float32)
        if MOMENT_SCALE is None:
            s = _jnp.sqrt(_jnp.mean(_jnp.square(w)) + 1e-30)
        else:
            s = MOMENT_SCALE[name]
        km, kv = _jax.random.split(_jax.random.fold_in(key, i + 1))
        out[name] = w
        out["m_" + name] = s * _jax.random.normal(km, w.shape, _jnp.float32)
        out["v_" + name] = (s * s) * _jax.random.uniform(kv, w.shape, _jnp.float32, 0.5, 1.5)
    if N_MICROBATCH > 1:
        for name, axis in PER_EXAMPLE_BATCH_AXIS.items():
            out[name] = _to_microbatches(out[name], axis)
    return {'x': out['x'], 'norm_mix': out['norm_mix'], 'norm_mlp': out['norm_mlp'], 'norm_final': out['norm_final'], 'w_up': out['w_up'], 'w_down': out['w_down'], 'swa_w_qkv': out['swa_w_qkv'], 'swa_b_qkv': out['swa_b_qkv'], 'swa_sinks': out['swa_sinks'], 'swa_w_o': out['swa_w_o'], 'hgrn_w_in': out['hgrn_w_in'], 'hgrn_lb_logits': out['hgrn_lb_logits'], 'hgrn_g_norm': out['hgrn_g_norm'], 'hgrn_w_o': out['hgrn_w_o'], 'fox_w_in': out['fox_w_in'], 'fox_b_in': out['fox_b_in'], 'fox_w_o': out['fox_w_o'], 'loss_target': out['loss_target'], 'm_norm_mix': out['m_norm_mix'], 'm_norm_mlp': out['m_norm_mlp'], 'm_norm_final': out['m_norm_final'], 'm_w_up': out['m_w_up'], 'm_w_down': out['m_w_down'], 'm_swa_w_qkv': out['m_swa_w_qkv'], 'm_swa_b_qkv': out['m_swa_b_qkv'], 'm_swa_sinks': out['m_swa_sinks'], 'm_swa_w_o': out['m_swa_w_o'], 'm_hgrn_w_in': out['m_hgrn_w_in'], 'm_hgrn_lb_logits': out['m_hgrn_lb_logits'], 'm_hgrn_g_norm': out['m_hgrn_g_norm'], 'm_hgrn_w_o': out['m_hgrn_w_o'], 'm_fox_w_in': out['m_fox_w_in'], 'm_fox_b_in': out['m_fox_b_in'], 'm_fox_w_o': out['m_fox_w_o'], 'v_norm_mix': out['v_norm_mix'], 'v_norm_mlp': out['v_norm_mlp'], 'v_norm_final': out['v_norm_final'], 'v_w_up': out['v_w_up'], 'v_w_down': out['v_w_down'], 'v_swa_w_qkv': out['v_swa_w_qkv'], 'v_swa_b_qkv': out['v_swa_b_qkv'], 'v_swa_sinks': out['v_swa_sinks'], 'v_swa_w_o': out['v_swa_w_o'], 'v_hgrn_w_in': out['v_hgrn_w_in'], 'v_hgrn_lb_logits': out['v_hgrn_lb_logits'], 'v_hgrn_g_norm': out['v_hgrn_g_norm'], 'v_hgrn_w_o': out['v_hgrn_w_o'], 'v_fox_w_in': out['v_fox_w_in'], 'v_fox_b_in': out['v_fox_b_in'], 'v_fox_w_o': out['v_fox_w_o']}


def _loss(weights, diff, rest, loss_target):
    with _jax.named_scope("forward"):
        args = {**rest, TWIN_DIFF_INPUT: diff, **{k: w.astype(_WEIGHT_DTYPES[k]) for k, w in weights.items()}}
        y = _forward(args)
    with _jax.named_scope("loss_head"):
        err = _jnp.square(y.astype(_jnp.float32) - loss_target)
        return 0.5 * _jnp.sum(_jnp.mean(err, axis=-1)) if err.ndim else 0.5 * err


def _adamw(w, g, m, v):
    m = ADAM_B1 * m + (1.0 - ADAM_B1) * g
    v = ADAM_B2 * v + (1.0 - ADAM_B2) * _jnp.square(g)
    m_hat = m / (1.0 - ADAM_B1 ** ADAM_STEP)
    v_hat = v / (1.0 - ADAM_B2 ** ADAM_STEP)
    delta = -ADAM_LR * (m_hat / (_jnp.sqrt(v_hat) + ADAM_EPS) + ADAM_WD * w)
    return delta, m, v


def reference(x, norm_mix, norm_mlp, norm_final, w_up, w_down, swa_w_qkv, swa_b_qkv, swa_sinks, swa_w_o, hgrn_w_in, hgrn_lb_logits, hgrn_g_norm, hgrn_w_o, fox_w_in, fox_b_in, fox_w_o, loss_target, m_norm_mix, m_norm_mlp, m_norm_final, m_w_up, m_w_down, m_swa_w_qkv, m_swa_b_qkv, m_swa_sinks, m_swa_w_o, m_hgrn_w_in, m_hgrn_lb_logits, m_hgrn_g_norm, m_hgrn_w_o, m_fox_w_in, m_fox_b_in, m_fox_w_o, v_norm_mix, v_norm_mlp, v_norm_final, v_w_up, v_w_down, v_swa_w_qkv, v_swa_b_qkv, v_swa_sinks, v_swa_w_o, v_hgrn_w_in, v_hgrn_lb_logits, v_hgrn_g_norm, v_hgrn_w_o, v_fox_w_in, v_fox_b_in, v_fox_w_o):
    given = dict(x=x, norm_mix=norm_mix, norm_mlp=norm_mlp, norm_final=norm_final, w_up=w_up, w_down=w_down, swa_w_qkv=swa_w_qkv, swa_b_qkv=swa_b_qkv, swa_sinks=swa_sinks, swa_w_o=swa_w_o, hgrn_w_in=hgrn_w_in, hgrn_lb_logits=hgrn_lb_logits, hgrn_g_norm=hgrn_g_norm, hgrn_w_o=hgrn_w_o, fox_w_in=fox_w_in, fox_b_in=fox_b_in, fox_w_o=fox_w_o, loss_target=loss_target, m_norm_mix=m_norm_mix, m_norm_mlp=m_norm_mlp, m_norm_final=m_norm_final, m_w_up=m_w_up, m_w_down=m_w_down, m_swa_w_qkv=m_swa_w_qkv, m_swa_b_qkv=m_swa_b_qkv, m_swa_sinks=m_swa_sinks, m_swa_w_o=m_swa_w_o, m_hgrn_w_in=m_hgrn_w_in, m_hgrn_lb_logits=m_hgrn_lb_logits, m_hgrn_g_norm=m_hgrn_g_norm, m_hgrn_w_o=m_hgrn_w_o, m_fox_w_in=m_fox_w_in, m_fox_b_in=m_fox_b_in, m_fox_w_o=m_fox_w_o, v_norm_mix=v_norm_mix, v_norm_mlp=v_norm_mlp, v_norm_final=v_norm_final, v_w_up=v_w_up, v_w_down=v_w_down, v_swa_w_qkv=v_swa_w_qkv, v_swa_b_qkv=v_swa_b_qkv, v_swa_sinks=v_swa_sinks, v_swa_w_o=v_swa_w_o, v_hgrn_w_in=v_hgrn_w_in, v_hgrn_lb_logits=v_hgrn_lb_logits, v_hgrn_g_norm=v_hgrn_g_norm, v_hgrn_w_o=v_hgrn_w_o, v_fox_w_in=v_fox_w_in, v_fox_b_in=v_fox_b_in, v_fox_w_o=v_fox_w_o)
    weights = {n: given[n] for n in TWIN_WEIGHTS}
    shared = {n: given[n] for n in SHARED_INPUTS}
    per_example = {n: given[n] for n in ['x']}
    grad_fn = _jax.value_and_grad(_loss, argnums=(0, 1))

    def one_microbatch(ex, loss_target):
        ex = dict(ex)
        diff = ex.pop(TWIN_DIFF_INPUT)
        return grad_fn(weights, diff, {**shared, **ex}, loss_target)

    if N_MICROBATCH == 1:
        loss, (grad_w, grad_x) = one_microbatch(per_example, given["loss_target"])
    else:
        def body(carry, xs):
            loss_sum, grad_sum = carry
            l_k, (gw_k, gx_k) = one_microbatch(xs[0], xs[1])
            with _jax.named_scope("update"):
                return (loss_sum + l_k, _jax.tree.map(_jnp.add, grad_sum, gw_k)), gx_k

        init = (_jnp.zeros((), _jnp.float32), _jax.tree.map(_jnp.zeros_like, weights))
        (loss, grad_w), grad_x = _jax.lax.scan(body, init, (per_example, given["loss_target"]))
    with _jax.named_scope("update"):
        delta_w, new_m, new_v = {}, {}, {}
        for n in TWIN_WEIGHTS:
            delta_w[n], new_m[n], new_v[n] = _adamw(weights[n], grad_w[n], given["m_" + n], given["v_" + n])
    return (loss, grad_x, *[grad_w[n] for n in TWIN_WEIGHTS], *[delta_w[n] for n in TWIN_WEIGHTS],
            *[new_m[n] for n in TWIN_WEIGHTS], *[new_v[n] for n in TWIN_WEIGHTS])
```

```python
import functools

import jax
import jax.numpy as jnp
from jax import lax
from jax.experimental import pallas as pl
from jax.experimental.pallas import tpu as pltpu

F32 = jnp.float32
BF16 = jnp.bfloat16
HI = lax.Precision.HIGHEST

N_DEV = 8
D_MODEL = 1024
DEPTH = 4
EPS = 1e-6
SWA_WINDOW = 128
HGRN_CHUNK = 64
LANES = 128
VMEM_LIMIT = 56 << 20

ADAM_LR, ADAM_B1, ADAM_B2, ADAM_EPS, ADAM_WD, ADAM_STEP = 0.001, 0.9, 0.999, 1e-08, 0.01, 10

TM = 512
TF = 512
TK = 512
FOX_T = 1024
SWA_TQ = 512
HGRN_TG = 512
SCAN_T = 256

WEIGHTS = ['norm_mix', 'norm_mlp', 'norm_final', 'w_up', 'w_down', 'swa_w_qkv', 'swa_b_qkv', 'swa_sinks', 'swa_w_o',
           'hgrn_w_in', 'hgrn_lb_logits', 'hgrn_g_norm', 'hgrn_w_o', 'fox_w_in', 'fox_b_in', 'fox_w_o']
SHARD_AXIS = {'norm_mix': None, 'norm_mlp': None, 'norm_final': None, 'w_up': 2, 'w_down': 1, 'swa_w_qkv': 2,
              'swa_b_qkv': 1, 'swa_sinks': None, 'swa_w_o': 1, 'hgrn_w_in': 2, 'hgrn_lb_logits': None,
              'hgrn_g_norm': None, 'hgrn_w_o': 1, 'fox_w_in': 2, 'fox_b_in': 1, 'fox_w_o': 1}
SHARDED = [n for n in WEIGHTS if SHARD_AXIS[n] is not None]
REPLICATED = [n for n in WEIGHTS if SHARD_AXIS[n] is None]
BIASES = ('swa_b_qkv', 'fox_b_in')


def _cparams(*sem):
    return pltpu.CompilerParams(dimension_semantics=sem, vmem_limit_bytes=VMEM_LIMIT)


def _nt(a, b):
    return lax.dot_general(a, b, (((1,), (1,)), ((), ())), preferred_element_type=F32)


def _tn(a, b):
    return lax.dot_general(a, b, (((0,), (0,)), ((), ())), preferred_element_type=F32)


def _dot(a, b):
    return jnp.dot(a, b, preferred_element_type=F32)


def _sigmoid(x):
    return 1.0 / (1.0 + jnp.exp(-x))


def _rms(xv):
    return lax.rsqrt(jnp.mean(xv * xv, axis=-1, keepdims=True) + EPS)


def _rms_bwd(xv, g, dh):
    r = _rms(xv)
    xhat = xv * r
    dhg = dh * g
    dx = r * (dhg - xhat * jnp.mean(dhg * xhat, axis=-1, keepdims=True))
    return dx, jnp.sum(dh * xhat, axis=0, keepdims=True)


def _my_id():
    return lax.axis_index("x"), lax.axis_index("y"), lax.axis_index("c")


def _peer(x, y, c, k):
    return (lax.rem(x + ((k >> 2) & 1), 2), lax.rem(y + ((k >> 1) & 1), 2), lax.rem(c + (k & 1), 2))


def all_gather_rows(local):
    def body(x_ref, o_ref, send_sems, recv_sems, loc_sem):
        x, y, c = _my_id()
        me = 4 * x + 2 * y + c
        mine = pltpu.make_async_copy(x_ref, o_ref.at[me], loc_sem)
        mine.start()
        copies = []
        for k in range(1, N_DEV):
            px, py, pc = _peer(x, y, c, k)
            cp = pltpu.make_async_remote_copy(
                src_ref=x_ref, dst_ref=o_ref.at[me], send_sem=send_sems.at[k - 1], recv_sem=recv_sems.at[k - 1],
                device_id=(px, py, pc), device_id_type=pl.DeviceIdType.MESH)
            cp.start()
            copies.append(cp)
        for cp in copies:
            cp.wait()
        mine.wait()

    return pl.pallas_call(
        body, name="all_gather_weights",
        out_shape=jax.ShapeDtypeStruct((N_DEV,) + local.shape, local.dtype),
        in_specs=[pl.BlockSpec(memory_space=pl.ANY)],
        out_specs=pl.BlockSpec(memory_space=pl.ANY),
        scratch_shapes=[pltpu.SemaphoreType.DMA((N_DEV - 1,)), pltpu.SemaphoreType.DMA((N_DEV - 1,)),
                        pltpu.SemaphoreType.DMA],
    )(local)


def all_to_all_rows(send):
    def body(s_ref, r_ref, send_sems, recv_sems, loc_sem):
        x, y, c = _my_id()
        me = 4 * x + 2 * y + c
        mine = pltpu.make_async_copy(s_ref.at[me], r_ref.at[me], loc_sem)
        mine.start()
        copies = []
        for k in range(1, N_DEV):
            px, py, pc = _peer(x, y, c, k)
            cp = pltpu.make_async_remote_copy(
                src_ref=s_ref.at[4 * px + 2 * py + pc], dst_ref=r_ref.at[me],
                send_sem=send_sems.at[k - 1], recv_sem=recv_sems.at[k - 1],
                device_id=(px, py, pc), device_id_type=pl.DeviceIdType.MESH)
            cp.start()
            copies.append(cp)
        for cp in copies:
            cp.wait()
        mine.wait()

    return pl.pallas_call(
        body, name="all_to_all_grads",
        out_shape=jax.ShapeDtypeStruct(send.shape, send.dtype),
        in_specs=[pl.BlockSpec(memory_space=pl.ANY)],
        out_specs=pl.BlockSpec(memory_space=pl.ANY),
        scratch_shapes=[pltpu.SemaphoreType.DMA((N_DEV - 1,)), pltpu.SemaphoreType.DMA((N_DEV - 1,)),
                        pltpu.SemaphoreType.DMA],
    )(send)


def reduce_adamw(recv, w, m, v):
    R = w.shape[0]
    tr = max(t for t in range(8, 257, 8) if R % t == 0)
    c1 = 1.0 / (1.0 - ADAM_B1 ** ADAM_STEP)
    c2 = 1.0 / (1.0 - ADAM_B2 ** ADAM_STEP)

    def body(r_ref, w_ref, m_ref, v_ref, g_ref, d_ref, nm_ref, nv_ref):
        g = r_ref[0]
        for s in range(1, N_DEV):
            g = g + r_ref[s]
        m2 = ADAM_B1 * m_ref[...] + (1.0 - ADAM_B1) * g
        v2 = ADAM_B2 * v_ref[...] + (1.0 - ADAM_B2) * (g * g)
        g_ref[...] = g
        nm_ref[...] = m2
        nv_ref[...] = v2
        d_ref[...] = -ADAM_LR * ((m2 * c1) / (jnp.sqrt(v2 * c2) + ADAM_EPS) + ADAM_WD * w_ref[...])

    row = pl.BlockSpec((tr, D_MODEL), lambda i: (i, 0))
    shp = jax.ShapeDtypeStruct((R, D_MODEL), F32)
    return pl.pallas_call(
        body, name="reduce_adamw", grid=(R // tr,),
        in_specs=[pl.BlockSpec((N_DEV, tr, D_MODEL), lambda i: (0, i, 0)), row, row, row],
        out_specs=[row, row, row, row], out_shape=[shp, shp, shp, shp],
        compiler_params=_cparams("parallel"),
    )(recv, w, m, v)


def norm_matmul(x, g, w, b, out_dtype, name):
    T, N = x.shape[0], w.shape[1]
    tm, tn = min(TM, T), min(512, N)

    def body(x_ref, g_ref, w_ref, b_ref, o_ref, h_sc):
        @pl.when(pl.program_id(1) == 0)
        def _():
            xv = x_ref[...]
            h_sc[...] = (xv * _rms(xv) * g_ref[...]).astype(BF16)
        o_ref[...] = (_dot(h_sc[...], w_ref[...]) + b_ref[...]).astype(o_ref.dtype)

    return pl.pallas_call(
        body, name=name, grid=(T // tm, N // tn),
        in_specs=[pl.BlockSpec((tm, D_MODEL), lambda i, j: (i, 0)), pl.BlockSpec((1, D_MODEL), lambda i, j: (0, 0)),
                  pl.BlockSpec((D_MODEL, tn), lambda i, j: (0, j)), pl.BlockSpec((1, tn), lambda i, j: (0, j))],
        out_specs=pl.BlockSpec((tm, tn), lambda i, j: (i, j)),
        out_shape=jax.ShapeDtypeStruct((T, N), out_dtype),
        scratch_shapes=[pltpu.VMEM((tm, D_MODEL), BF16)],
        compiler_params=_cparams("parallel", "arbitrary"),
    )(x, g, w, b)


def matmul(a, w, out_dtype, name, res=None):
    T, K = a.shape
    N = w.shape[1]
    tm = min(TM, T)

    def body(*refs):
        if res is None:
            a_ref, w_ref, o_ref = refs
            acc = _dot(a_ref[...].astype(BF16), w_ref[...])
        else:
            a_ref, w_ref, r_ref, o_ref = refs
            acc = r_ref[...] + _dot(a_ref[...].astype(BF16), w_ref[...])
        o_ref[...] = acc.astype(o_ref.dtype)

    in_specs = [pl.BlockSpec((tm, K), lambda i: (i, 0)), pl.BlockSpec((K, N), lambda i: (0, 0))]
    ops = [a, w]
    if res is not None:
        in_specs.append(pl.BlockSpec((tm, N), lambda i: (i, 0)))
        ops.append(res)
    return pl.pallas_call(
        body, name=name, grid=(T // tm,), in_specs=in_specs,
        out_specs=pl.BlockSpec((tm, N), lambda i: (i, 0)),
        out_shape=jax.ShapeDtypeStruct((T, N), out_dtype),
        compiler_params=_cparams("parallel"),
    )(*ops)


def tn_matmul(a, b, name, colsum=False):
    T, M = a.shape
    N = b.shape[1]
    tk = min(TK, T)
    tmm = min(1024, M)
    tn = N if N <= 1024 else (1024 if N % 1024 == 0 else N)

    def body(a_ref, b_ref, o_ref, *rest):
        k = pl.program_id(2)
        bv = b_ref[...]

        @pl.when(k == 0)
        def _():
            o_ref[...] = jnp.zeros(o_ref.shape, F32)
            if colsum:
                rest[0][...] = jnp.zeros(rest[0].shape, F32)

        o_ref[...] += _tn(a_ref[...].astype(BF16), bv.astype(BF16))
        if colsum:
            rest[0][...] += jnp.sum(bv.astype(F32), axis=0, keepdims=True)

    out_specs = [pl.BlockSpec((tmm, tn), lambda i, j, k: (i, j))]
    out_shape = [jax.ShapeDtypeStruct((M, N), F32)]
    if colsum:
        assert M == tmm
        out_specs.append(pl.BlockSpec((1, tn), lambda i, j, k: (0, j)))
        out_shape.append(jax.ShapeDtypeStruct((1, N), F32))
    out = pl.pallas_call(
        body, name=name, grid=(M // tmm, N // tn, T // tk),
        in_specs=[pl.BlockSpec((tk, tmm), lambda i, j, k: (k, i)), pl.BlockSpec((tk, tn), lambda i, j, k: (k, j))],
        out_specs=out_specs, out_shape=out_shape,
        compiler_params=_cparams("parallel", "parallel", "arbitrary"),
    )(a, b)
    return out if colsum else out[0]


def mlp_fwd(x, g, w_up, w_down, name):
    T, F = x.shape[0], w_up.shape[1]
    tm, tf = min(TM, T), min(TF, F)
    nf = F // tf

    def body(x_ref, g_ref, wu_ref, wd_ref, o_ref, h_sc, acc_sc):
        f = pl.program_id(1)

        @pl.when(f == 0)
        def _():
            xv = x_ref[...]
            h_sc[...] = (xv * _rms(xv) * g_ref[...]).astype(BF16)
            acc_sc[...] = xv

        u = jnp.maximum(_dot(h_sc[...], wu_ref[...]), 0.0)
        acc_sc[...] += _dot((u * u).astype(BF16), wd_ref[...])

        @pl.when(f == nf - 1)
        def _():
            o_ref[...] = acc_sc[...]

    return pl.pallas_call(
        body, name=name, grid=(T // tm, nf),
        in_specs=[pl.BlockSpec((tm, D_MODEL), lambda i, f: (i, 0)), pl.BlockSpec((1, D_MODEL), lambda i, f: (0, 0)),
                  pl.BlockSpec((D_MODEL, tf), lambda i, f: (0, f)), pl.BlockSpec((tf, D_MODEL), lambda i, f: (f, 0))],
        out_specs=pl.BlockSpec((tm, D_MODEL), lambda i, f: (i, 0)),
        out_shape=jax.ShapeDtypeStruct((T, D_MODEL), F32),
        scratch_shapes=[pltpu.VMEM((tm, D_MODEL), BF16), pltpu.VMEM((tm, D_MODEL), F32)],
        compiler_params=_cparams("parallel", "arbitrary"),
    )(x, g, w_up, w_down)


def mlp_bwd(x, g, w_up, w_up_t, w_down_t, dy, name):
    T, F = x.shape[0], w_up.shape[1]
    tm, tf = min(TM, T), min(TF, F)
    nf = F // tf

    def body(x_ref, g_ref, wu_ref, wut_ref, wdt_ref, dy_ref, dx_ref, h_ref, a_ref, du_ref, dg_ref, h_sc, dyb_sc, dh_sc):
        i, f = pl.program_id(0), pl.program_id(1)

        @pl.when(f == 0)
        def _():
            xv = x_ref[...]
            h = (xv * _rms(xv) * g_ref[...]).astype(BF16)
            h_sc[...] = h
            h_ref[...] = h
            dyb_sc[...] = dy_ref[...].astype(BF16)
            dh_sc[...] = jnp.zeros(dh_sc.shape, F32)

        @pl.when((i == 0) & (f == 0))
        def _():
            dg_ref[...] = jnp.zeros(dg_ref.shape, F32)

        u = jnp.maximum(_dot(h_sc[...], wu_ref[...]), 0.0)
        a_ref[...] = (u * u).astype(BF16)
        du = (_dot(dyb_sc[...], wdt_ref[...]) * (2.0 * u)).astype(BF16)
        du_ref[...] = du
        dh_sc[...] += _dot(du, wut_ref[...])

        @pl.when(f == nf - 1)
        def _():
            dx, dg = _rms_bwd(x_ref[...], g_ref[...], dh_sc[...])
            dx_ref[...] = dy_ref[...] + dx
            dg_ref[...] += dg

    row = pl.BlockSpec((tm, D_MODEL), lambda i, f: (i, 0))
    hid = pl.BlockSpec((tm, tf), lambda i, f: (i, f))
    return pl.pallas_call(
        body, name=name, grid=(T // tm, nf),
        in_specs=[row, pl.BlockSpec((1, D_MODEL), lambda i, f: (0, 0)),
                  pl.BlockSpec((D_MODEL, tf), lambda i, f: (0, f)), pl.BlockSpec((tf, D_MODEL), lambda i, f: (f, 0)),
                  pl.BlockSpec((D_MODEL, tf), lambda i, f: (0, f)), row],
        out_specs=[row, row, hid, hid, pl.BlockSpec((1, D_MODEL), lambda i, f: (0, 0))],
        out_shape=[jax.ShapeDtypeStruct((T, D_MODEL), F32), jax.ShapeDtypeStruct((T, D_MODEL), BF16),
                   jax.ShapeDtypeStruct((T, F), BF16), jax.ShapeDtypeStruct((T, F), BF16),
                   jax.ShapeDtypeStruct((1, D_MODEL), F32)],
        scratch_shapes=[pltpu.VMEM((tm, D_MODEL), BF16), pltpu.VMEM((tm, D_MODEL), BF16),
                        pltpu.VMEM((tm, D_MODEL), F32)],
        compiler_params=_cparams("arbitrary", "arbitrary"),
    )(x, g, w_up, w_up_t, w_down_t, dy)


def proj_bwd(x, g, dres, parts, name):
    T = x.shape[0]
    tm = min(TM, T)
    n = len(parts)

    def body(*refs):
        x_ref, g_ref, dr_ref = refs[:3]
        da_refs, wt_refs = refs[3:3 + n], refs[3 + n:3 + 2 * n]
        dx_ref, h_ref, dg_ref = refs[3 + 2 * n:]

        @pl.when(pl.program_id(0) == 0)
        def _():
            dg_ref[...] = jnp.zeros(dg_ref.shape, F32)

        xv = x_ref[...]
        dh = _dot(da_refs[0][...].astype(BF16), wt_refs[0][...])
        for a_ref, w_ref in zip(da_refs[1:], wt_refs[1:]):
            dh = dh + _dot(a_ref[...].astype(BF16), w_ref[...])
        h_ref[...] = (xv * _rms(xv) * g_ref[...]).astype(BF16)
        dx, dg = _rms_bwd(xv, g_ref[...], dh)
        dx_ref[...] = dr_ref[...] + dx
        dg_ref[...] += dg

    row = pl.BlockSpec((tm, D_MODEL), lambda i: (i, 0))
    one = pl.BlockSpec((1, D_MODEL), lambda i: (0, 0))
    in_specs = [row, one, row]
    in_specs += [pl.BlockSpec((tm, da.shape[1]), lambda i: (i, 0)) for da, _ in parts]
    in_specs += [pl.BlockSpec(wt.shape, lambda i: (0, 0)) for _, wt in parts]
    return pl.pallas_call(
        body, name=name, grid=(T // tm,), in_specs=in_specs,
        out_specs=[row, row, one],
        out_shape=[jax.ShapeDtypeStruct((T, D_MODEL), F32), jax.ShapeDtypeStruct((T, D_MODEL), BF16),
                   jax.ShapeDtypeStruct((1, D_MODEL), F32)],
        compiler_params=_cparams("arbitrary"),
    )(x, g, dres, *[da for da, _ in parts], *[wt for _, wt in parts])


def final_loss(x, g, tgt, name):
    T = x.shape[0]
    tm = min(TM, T)

    def body(x_ref, g_ref, t_ref, l_ref, dx_ref, dg_ref):
        @pl.when(pl.program_id(0) == 0)
        def _():
            l_ref[...] = jnp.zeros(l_ref.shape, F32)
            dg_ref[...] = jnp.zeros(dg_ref.shape, F32)

        xv = x_ref[...]
        gv = g_ref[...]
        err = xv * _rms(xv) * gv - t_ref[...]
        l_ref[...] += 0.5 * jnp.sum(jnp.mean(err * err, axis=-1, keepdims=True), axis=0, keepdims=True)
        dx, dg = _rms_bwd(xv, gv, err * (1.0 / D_MODEL))
        dx_ref[...] = dx
        dg_ref[...] += dg

    row = pl.BlockSpec((tm, D_MODEL), lambda i: (i, 0))
    one = pl.BlockSpec((1, D_MODEL), lambda i: (0, 0))
    return pl.pallas_call(
        body, name=name, grid=(T // tm,), in_specs=[row, one, row],
        out_specs=[pl.BlockSpec((8, LANES), lambda i: (0, 0)), row, one],
        out_shape=[jax.ShapeDtypeStruct((8, LANES), F32), jax.ShapeDtypeStruct((T, D_MODEL), F32),
                   jax.ShapeDtypeStruct((1, D_MODEL), F32)],
        compiler_params=_cparams("arbitrary"),
    )(x, g, tgt)


def _swa_specs(tq):
    r = tq // SWA_WINDOW
    cur = lambda ix: pl.BlockSpec((tq, LANES), lambda kv, i: (ix(i), kv))
    prev = lambda ix: pl.BlockSpec((SWA_WINDOW, LANES), lambda kv, i: (jnp.maximum(ix(i) * r - 1, 0), kv))
    return cur, prev


def _swa_visible(tq, tile):
    r = lax.broadcasted_iota(jnp.int32, (tq, tq + SWA_WINDOW), 0)
    c = lax.broadcasted_iota(jnp.int32, (tq, tq + SWA_WINDOW), 1)
    rel = r + SWA_WINDOW - c
    return (rel >= 0) & (rel < SWA_WINDOW) & ((c >= SWA_WINDOW) | (tile > 0))


def _swa_probs(qm, kcat, vis, sk):
    s = jnp.where(vis, _nt(qm, kcat) * 0.125, -1e30)
    m = jnp.maximum(jnp.max(s, axis=1, keepdims=True), sk)
    e = jnp.exp(s - m)
    esk = jnp.exp(sk - m)
    inv = 1.0 / (jnp.sum(e, axis=1, keepdims=True) + esk)
    return e * inv, esk * inv


def swa_fwd(qkv, kdup, vdup, sinks_b, name):
    T = qkv.shape[0]
    tq = min(SWA_TQ, T)
    cur, prev = _swa_specs(tq)
    ident = lambda i: i

    def body(q_ref, kc_ref, kp_ref, vc_ref, vp_ref, sk_ref, o_ref):
        i = pl.program_id(1)
        kcat = jnp.concatenate([kp_ref[...], kc_ref[...]], axis=0)
        vcat = jnp.concatenate([vp_ref[...], vc_ref[...]], axis=0)
        vis = _swa_visible(tq, i)
        lane = lax.broadcasted_iota(jnp.int32, (1, LANES), 1)
        for pp in range(2):
            q2 = q_ref[:, pp * LANES:(pp + 1) * LANES]
            outs = []
            for hf in range(2):
                lm = (lane < 64) if hf == 0 else (lane >= 64)
                qm = jnp.where(lm, q2, jnp.zeros_like(q2))
                p, _ = _swa_probs(qm, kcat, vis, sk_ref[2 * pp + hf:2 * pp + hf + 1, 0:1])
                outs.append(_dot(p.astype(BF16), vcat))
            o_ref[:, pp * LANES:(pp + 1) * LANES] = jnp.where(lane < 64, outs[0], outs[1]).astype(BF16)

    return pl.pallas_call(
        body, name=name, grid=(4, T // tq),
        in_specs=[pl.BlockSpec((tq, 2 * LANES), lambda kv, i: (i, kv)), cur(ident), prev(ident), cur(ident), prev(ident),
                  pl.BlockSpec((None, 8, LANES), lambda kv, i: (kv, 0, 0))],
        out_specs=pl.BlockSpec((tq, 2 * LANES), lambda kv, i: (i, kv)),
        out_shape=jax.ShapeDtypeStruct((T, D_MODEL), BF16),
        compiler_params=_cparams("parallel", "arbitrary"),
    )(qkv, kdup, kdup, vdup, vdup, sinks_b)


def swa_bwd(qkv, kdup, vdup, sinks_b, o, do, name):
    T = qkv.shape[0]
    tq = min(SWA_TQ, T)
    n = T // tq
    cur, prev = _swa_specs(tq)
    rev = lambda i: n - 1 - i

    def body(q_ref, kc_ref, kp_ref, vc_ref, vp_ref, sk_ref, o_ref, do_ref, dq_ref, dk_ref, dv_ref, dsk_ref, ck_sc, cv_sc):
        i = pl.program_id(1)

        @pl.when(i == 0)
        def _():
            ck_sc[...] = jnp.zeros(ck_sc.shape, F32)
            cv_sc[...] = jnp.zeros(cv_sc.shape, F32)
            dsk_ref[...] = jnp.zeros(dsk_ref.shape, F32)

        kcat = jnp.concatenate([kp_ref[...], kc_ref[...]], axis=0)
        vcat = jnp.concatenate([vp_ref[...], vc_ref[...]], axis=0)
        vis = _swa_visible(tq, n - 1 - i)
        lane = lax.broadcasted_iota(jnp.int32, (1, LANES), 1)
        dkc = jnp.zeros((tq + SWA_WINDOW, LANES), F32)
        dvc = jnp.zeros((tq + SWA_WINDOW, LANES), F32)
        for pp in range(2):
            sl = slice(pp * LANES, (pp + 1) * LANES)
            q2, do2, o2 = q_ref[:, sl], do_ref[:, sl], o_ref[:, sl]
            dqs = []
            for hf in range(2):
                g = 2 * pp + hf
                lm = (lane < 64) if hf == 0 else (lane >= 64)
                qm = jnp.where(lm, q2, jnp.zeros_like(q2))
                dom = jnp.where(lm, do2, jnp.zeros_like(do2))
                p, psk = _swa_probs(qm, kcat, vis, sk_ref[g:g + 1, 0:1])
                delta = jnp.sum(dom.astype(F32) * o2.astype(F32), axis=1, keepdims=True)
                ds = p * (_nt(dom, vcat) - delta)
                dsk_ref[g:g + 1, :] += jnp.zeros((1, LANES), F32) - jnp.sum(psk * delta, axis=0, keepdims=True)
                dsb = (ds * 0.125).astype(BF16)
                dqs.append(_dot(dsb, kcat))
                dkc = dkc + _tn(dsb, qm)
                dvc = dvc + _tn(p.astype(BF16), dom)
            dq_ref[:, sl] = jnp.where(lane < 64, dqs[0], dqs[1]).astype(BF16)
        dkc = dkc + pltpu.roll(dkc, 64, 1)
        dvc = dvc + pltpu.roll(dvc, 64, 1)
        for full, ref, carry in ((dkc, dk_ref, ck_sc), (dvc, dv_ref, cv_sc)):
            if tq > SWA_WINDOW:
                ref[0:tq - SWA_WINDOW, :] = full[SWA_WINDOW:tq, :]
            ref[tq - SWA_WINDOW:tq, :] = full[tq:tq + SWA_WINDOW, :] + carry[...]
            carry[...] = full[0:SWA_WINDOW, :]

    wide = pl.BlockSpec((tq, 2 * LANES), lambda kv, i: (rev(i), kv))
    return pl.pallas_call(
        body, name=name, grid=(4, n),
        in_specs=[wide, cur(rev), prev(rev), cur(rev), prev(rev),
                  pl.BlockSpec((None, 8, LANES), lambda kv, i: (kv, 0, 0)), wide, wide],
        out_specs=[wide, cur(rev), cur(rev), pl.BlockSpec((None, 8, LANES), lambda kv, i: (kv, 0, 0))],
        out_shape=[jax.ShapeDtypeStruct((T, D_MODEL), BF16), jax.ShapeDtypeStruct((T, 4 * LANES), F32),
                   jax.ShapeDtypeStruct((T, 4 * LANES), F32), jax.ShapeDtypeStruct((4, 8, LANES), F32)],
        scratch_shapes=[pltpu.VMEM((SWA_WINDOW, LANES), F32), pltpu.VMEM((SWA_WINDOW, LANES), F32)],
        compiler_params=_cparams("arbitrary", "arbitrary"),
    )(qkv, kdup, kdup, vdup, vdup, sinks_b, o, do)


def fox_gate_fwd(fl, name):
    T = fl.shape[0]
    ts = min(SCAN_T, T)

    def body(fl_ref, c_ref, carry):
        @pl.when(pl.program_id(0) == 0)
        def _():
            carry[...] = jnp.zeros(carry.shape, F32)

        xv = fl_ref[...]
        ls = jnp.minimum(xv, 0.0) - jnp.log(1.0 + jnp.exp(-jnp.abs(xv)))
        tri = (lax.broadcasted_iota(jnp.int32, (ts, ts), 0) >= lax.broadcasted_iota(jnp.int32, (ts, ts), 1)).astype(F32)
        cs = jnp.dot(tri, ls, precision=HI, preferred_element_type=F32) + carry[...]
        c_ref[...] = cs
        carry[...] = cs[ts - 1:ts, :]

    blk = pl.BlockSpec((ts, LANES), lambda i: (i, 0))
    return pl.pallas_call(
        body, name=name, grid=(T // ts,), in_specs=[blk], out_specs=blk,
        out_shape=jax.ShapeDtypeStruct((T, LANES), F32), scratch_shapes=[pltpu.VMEM((1, LANES), F32)],
        compiler_params=_cparams("arbitrary"),
    )(fl)


def fox_gate_bwd(fl, dc, name):
    T = fl.shape[0]
    ts = min(SCAN_T, T)
    n = T // ts

    def body(fl_ref, dc_ref, o_ref, carry):
        @pl.when(pl.program_id(0) == 0)
        def _():
            carry[...] = jnp.zeros(carry.shape, F32)

        tri = (lax.broadcasted_iota(jnp.int32, (ts, ts), 0) <= lax.broadcasted_iota(jnp.int32, (ts, ts), 1)).astype(F32)
        rs = jnp.dot(tri, dc_ref[...], precision=HI, preferred_element_type=F32) + carry[...]
        carry[...] = rs[0:1, :]
        o_ref[...] = rs * (1.0 / (1.0 + jnp.exp(fl_ref[...])))

    blk = pl.BlockSpec((ts, LANES), lambda i: (n - 1 - i, 0))
    return pl.pallas_call(
        body, name=name, grid=(n,), in_specs=[blk, blk], out_specs=blk,
        out_shape=jax.ShapeDtypeStruct((T, LANES), F32), scratch_shapes=[pltpu.VMEM((1, LANES), F32)],
        compiler_params=_cparams("arbitrary"),
    )(fl, dc)


def _fox_scores(qm, k2, cc, cr, vis):
    return jnp.where(vis, _nt(qm, k2) * 0.125 + (cc - cr), -1e30)


def _fox_visible(t, i, j):
    return (i * t + lax.broadcasted_iota(jnp.int32, (t, t), 0)) >= (j * t + lax.broadcasted_iota(jnp.int32, (t, t), 1))


def _half(lane, hf):
    return (lane < 64) if hf == 0 else (lane >= 64)


def fox_fwd(qkv, crep, crow, name):
    T = qkv.shape[0]
    t = min(FOX_T, T)
    n = T // t

    def body(q_ref, k_ref, v_ref, cc_ref, cr_ref, o_ref, lse_ref, m_sc, l_sc, acc_sc):
        i, j = pl.program_id(1), pl.program_id(2)
        lane = lax.broadcasted_iota(jnp.int32, (1, LANES), 1)

        @pl.when(j == 0)
        def _():
            m_sc[...] = jnp.full(m_sc.shape, -1e30, F32)
            l_sc[...] = jnp.zeros(l_sc.shape, F32)
            acc_sc[...] = jnp.zeros(acc_sc.shape, F32)

        @pl.when(j <= i)
        def _():
            q2, k2, v2 = q_ref[...], k_ref[...], v_ref[...]
            vis = _fox_visible(t, i, j)
            for hf in range(2):
                qm = jnp.where(_half(lane, hf), q2, jnp.zeros_like(q2))
                s = _fox_scores(qm, k2, cc_ref[:, 64 * hf:64 * hf + 1], cr_ref[hf:hf + 1, :], vis)
                m_old = m_sc[hf]
                m_new = jnp.maximum(m_old, jnp.max(s, axis=1, keepdims=True))
                alpha = jnp.exp(m_old - m_new)
                p = jnp.exp(s - m_new)
                l_sc[hf] = alpha * l_sc[hf] + jnp.sum(p, axis=1, keepdims=True)
                acc_sc[hf] = alpha * acc_sc[hf] + _dot(p.astype(BF16), v2)
                m_sc[hf] = m_new

        @pl.when(j == i)
        def _():
            o_ref[...] = jnp.where(lane < 64, acc_sc[0] / l_sc[0], acc_sc[1] / l_sc[1]).astype(BF16)
            lse_ref[...] = jnp.where(lane < 64, m_sc[0] + jnp.log(l_sc[0]), m_sc[1] + jnp.log(l_sc[1]))

    qblk = pl.BlockSpec((t, LANES), lambda p, i, j: (i, p))
    return pl.pallas_call(
        body, name=name, grid=(8, n, n),
        in_specs=[qblk, pl.BlockSpec((t, LANES), lambda p, i, j: (jnp.minimum(j, i), 8 + p)),
                  pl.BlockSpec((t, LANES), lambda p, i, j: (jnp.minimum(j, i), 16 + p)), qblk,
                  pl.BlockSpec((None, 2, t), lambda p, i, j: (p, 0, jnp.minimum(j, i)))],
        out_specs=[qblk, qblk],
        out_shape=[jax.ShapeDtypeStruct((T, D_MODEL), BF16), jax.ShapeDtypeStruct((T, D_MODEL), F32)],
        scratch_shapes=[pltpu.VMEM((2, t, 1), F32), pltpu.VMEM((2, t, 1), F32), pltpu.VMEM((2, t, LANES), F32)],
        compiler_params=_cparams("parallel", "parallel", "arbitrary"),
    )(qkv, qkv, qkv, crep, crow)


def fox_bwd_dq(qkv, crep, crow, o, lse, do, name):
    T = qkv.shape[0]
    t = min(FOX_T, T)
    n = T // t

    def body(q_ref, k_ref, v_ref, cc_ref, cr_ref, o_ref, lse_ref, do_ref, dq_ref, dl_ref, rs_ref, acc_sc, dl_sc, rs_sc):
        i, j = pl.program_id(1), pl.program_id(2)
        lane = lax.broadcasted_iota(jnp.int32, (1, LANES), 1)

        @pl.when(j == 0)
        def _():
            acc_sc[...] = jnp.zeros(acc_sc.shape, F32)
            rs_sc[...] = jnp.zeros(rs_sc.shape, F32)
            d = do_ref[...].astype(F32) * o_ref[...].astype(F32)
            for hf in range(2):
                dl_sc[hf] = jnp.sum(jnp.where(_half(lane, hf), d, 0.0), axis=1, keepdims=True)
            dl_ref[...] = jnp.where(lane < 64, dl_sc[0], dl_sc[1])

        @pl.when(j <= i)
        def _():
            q2, k2, v2, do2 = q_ref[...], k_ref[...], v_ref[...], do_ref[...]
            vis = _fox_visible(t, i, j)
            for hf in range(2):
                lm = _half(lane, hf)
                qm = jnp.where(lm, q2, jnp.zeros_like(q2))
                dom = jnp.where(lm, do2, jnp.zeros_like(do2))
                s = _fox_scores(qm, k2, cc_ref[:, 64 * hf:64 * hf + 1], cr_ref[hf:hf + 1, :], vis)
                p = jnp.exp(s - lse_ref[:, 64 * hf:64 * hf + 1])
                ds = p * (_nt(dom, v2) - dl_sc[hf])
                acc_sc[hf] += _dot((ds * 0.125).astype(BF16), k2)
                rs_sc[hf] += jnp.sum(ds, axis=1, keepdims=True)

        @pl.when(j == i)
        def _():
            dq_ref[...] = jnp.where(lane < 64, acc_sc[0], acc_sc[1]).astype(BF16)
            rs_ref[...] = jnp.where(lane < 64, rs_sc[0], rs_sc[1])

    qblk = pl.BlockSpec((t, LANES), lambda p, i, j: (i, p))
    return pl.pallas_call(
        body, name=name, grid=(8, n, n),
        in_specs=[qblk, pl.BlockSpec((t, LANES), lambda p, i, j: (jnp.minimum(j, i), 8 + p)),
                  pl.BlockSpec((t, LANES), lambda p, i, j: (jnp.minimum(j, i), 16 + p)), qblk,
                  pl.BlockSpec((None, 2, t), lambda p, i, j: (p, 0, jnp.minimum(j, i))), qblk, qblk, qblk],
        out_specs=[qblk, qblk, qblk],
        out_shape=[jax.ShapeDtypeStruct((T, D_MODEL), BF16), jax.ShapeDtypeStruct((T, D_MODEL), F32),
                   jax.ShapeDtypeStruct((T, D_MODEL), F32)],
        scratch_shapes=[pltpu.VMEM((2, t, LANES), F32), pltpu.VMEM((2, t, 1), F32), pltpu.VMEM((2, t, 1), F32)],
        compiler_params=_cparams("parallel", "parallel", "arbitrary"),
    )(qkv, qkv, qkv, crep, crow, o, lse, do)


def fox_bwd_dkv(qkv, crep, crow, lse, delta, do, name):
    T = qkv.shape[0]
    t = min(FOX_T, T)
    n = T // t

    def body(q_ref, k_ref, v_ref, cc_ref, cr_ref, lse_ref, dl_ref, do_ref, dk_ref, dv_ref, dc_ref, dk_sc, dv_sc, dc_sc):
        j, i = pl.program_id(1), pl.program_id(2)
        lane = lax.broadcasted_iota(jnp.int32, (1, LANES), 1)

        @pl.when(i == 0)
        def _():
            dk_sc[...] = jnp.zeros(dk_sc.shape, F32)
            dv_sc[...] = jnp.zeros(dv_sc.shape, F32)
            dc_sc[...] = jnp.zeros(dc_sc.shape, F32)

        @pl.when(i >= j)
        def _():
            q2, k2, v2, do2 = q_ref[...], k_ref[...], v_ref[...], do_ref[...]
            vis = _fox_visible(t, i, j)
            for hf in range(2):
                lm = _half(lane, hf)
                qm = jnp.where(lm, q2, jnp.zeros_like(q2))
                dom = jnp.where(lm, do2, jnp.zeros_like(do2))
                s = _fox_scores(qm, k2, cc_ref[:, 64 * hf:64 * hf + 1], cr_ref[hf:hf + 1, :], vis)
                p = jnp.exp(s - lse_ref[:, 64 * hf:64 * hf + 1])
                ds = p * (_nt(dom, v2) - dl_ref[:, 64 * hf:64 * hf + 1])
                dv_sc[...] += _tn(p.astype(BF16), dom)
                dk_sc[...] += _tn((ds * 0.125).astype(BF16), qm)
                dc_sc[hf:hf + 1, :] += -jnp.sum(ds, axis=0, keepdims=True)

        @pl.when(i == n - 1)
        def _():
            dk_ref[...] = dk_sc[...].astype(BF16)
            dv_ref[...] = dv_sc[...].astype(BF16)
            dc_ref[...] = dc_sc[...]

    qblk = pl.BlockSpec((t, LANES), lambda p, j, i: (jnp.maximum(i, j), p))
    kblk = pl.BlockSpec((t, LANES), lambda p, j, i: (j, p))
    rblk = pl.BlockSpec((None, 2, t), lambda p, j, i: (p, 0, j))
    return pl.pallas_call(
        body, name=name, grid=(8, n, n),
        in_specs=[qblk, pl.BlockSpec((t, LANES), lambda p, j, i: (j, 8 + p)),
                  pl.BlockSpec((t, LANES), lambda p, j, i: (j, 16 + p)), qblk, rblk, qblk, qblk, qblk],
        out_specs=[kblk, kblk, rblk],
        out_shape=[jax.ShapeDtypeStruct((T, D_MODEL), BF16), jax.ShapeDtypeStruct((T, D_MODEL), BF16),
                   jax.ShapeDtypeStruct((8, 2, T), F32)],
        scratch_shapes=[pltpu.VMEM((t, LANES), F32), pltpu.VMEM((t, LANES), F32), pltpu.VMEM((2, t), F32)],
        compiler_params=_cparams("parallel", "parallel", "arbitrary"),
    )(qkv, qkv, qkv, crep, crow, lse, delta, do)


C = HGRN_CHUNK
LEVELS = (64, 32, 16, 8, 4, 2)


def _pivot(b, B, row):
    if B == C:
        return jnp.broadcast_to(b[C // 2 - 1:C // 2, :], b.shape)
    if B >= 8:
        b3 = b.reshape(C // B, B, LANES)
        return jnp.broadcast_to(b3[:, B // 2 - 1:B // 2, :], b3.shape).reshape(C, LANES)
    if B == 4:
        y = jnp.where((row & 3) == 1, b, 0.0)
        return y + pltpu.roll(y, 1, 0) + pltpu.roll(y, 2, 0) + pltpu.roll(y, C - 1, 0)
    y = jnp.where((row & 1) == 0, b, 0.0)
    return y + pltpu.roll(y, 1, 0)


def _level_factors(bcum):
    row = lax.broadcasted_iota(jnp.int32, (C, 1), 0)
    out = []
    for B in LEVELS:
        upper = (row & (B - 1)) >= B // 2
        e = jnp.exp(-jnp.abs(bcum - _pivot(bcum, B, row)))
        out.append((B, jnp.where(upper, e, 0.0), jnp.where(upper, 0.0, e)))
    return out


def _same_block(B):
    sh = B.bit_length() - 1
    r = lax.broadcasted_iota(jnp.int32, (C, C), 0)
    c = lax.broadcasted_iota(jnp.int32, (C, C), 1)
    return (r >> sh) == (c >> sh)


def _hgrn_gates(q, fl, lb):
    sg = _sigmoid(fl)
    f = lb + (1.0 - lb) * sg
    sq = _sigmoid(q)
    return sg, f, jnp.log(f), 1.0 - f, sq, q * sq


def _cumsum_rows(x, reverse=False):
    r = lax.broadcasted_iota(jnp.int32, (C, C), 0)
    c = lax.broadcasted_iota(jnp.int32, (C, C), 1)
    tri = ((r <= c) if reverse else (r >= c)).astype(F32)
    return jnp.dot(tri, x, precision=HI, preferred_element_type=F32)


def _intra(qs, k, factors):
    r = lax.broadcasted_iota(jnp.int32, (C, C), 0)
    c = lax.broadcasted_iota(jnp.int32, (C, C), 1)
    a = jnp.where(r == c, jnp.sum(qs * k, axis=1, keepdims=True), 0.0)
    ops = []
    for B, eq, ek in factors:
        ql, kl = (qs * eq).astype(BF16), (k * ek).astype(BF16)
        al = _nt(ql, kl)
        a = a + (al if B == C else jnp.where(_same_block(B), al, 0.0))
        ops.append((ql, kl))
    return a, ops


def hgrn_fwd(proj, lb, gn, name):
    T = proj.shape[0]
    tg = min(HGRN_TG, T)
    nch = tg // C

    def body(q_ref, fl_ref, v_ref, g_ref, lb_ref, gn_ref, ao_ref, o_ref, st_ref, st_sc):
        @pl.when(pl.program_id(1) == 0)
        def _():
            st_sc[...] = jnp.zeros(st_sc.shape, F32)

        lb_v, gn_v = lb_ref[...], gn_ref[...]

        def chunk(ci, carry):
            rows = pl.ds(pl.multiple_of(ci * C, C), C)
            _, f, lf, k, _, qs = _hgrn_gates(q_ref[rows, :], fl_ref[rows, :], lb_v)
            vb = v_ref[rows, :].astype(BF16)
            gv = g_ref[rows, :]
            bcum = _cumsum_rows(lf)
            blast = bcum[C - 1:C, :]
            a, _ = _intra(qs, k, _level_factors(bcum))
            st = st_sc[...]
            st_ref[ci] = st
            o = _dot(a.astype(BF16), vb) + _nt((qs * jnp.exp(bcum)).astype(BF16), st.astype(BF16))
            st_sc[...] = st * jnp.exp(blast) + _tn(vb, (k * jnp.exp(blast - bcum)).astype(BF16))
            o_ref[rows, :] = o
            ao_ref[rows, :] = (o * _rms(o) * gn_v * (gv * _sigmoid(gv))).astype(BF16)
            return carry

        lax.fori_loop(0, nch, chunk, 0)

    col = lambda off: pl.BlockSpec((tg, LANES), lambda h, i: (i, off + h))
    one = pl.BlockSpec((1, LANES), lambda h, i: (0, h))
    return pl.pallas_call(
        body, name=name, grid=(8, T // tg),
        in_specs=[col(0), col(8), col(16), col(24), one, one],
        out_specs=[col(0), col(0), pl.BlockSpec((None, nch, LANES, LANES), lambda h, i: (h, i, 0, 0))],
        out_shape=[jax.ShapeDtypeStruct((T, D_MODEL), BF16), jax.ShapeDtypeStruct((T, D_MODEL), F32),
                   jax.ShapeDtypeStruct((8, T // C, LANES, LANES), F32)],
        scratch_shapes=[pltpu.VMEM((LANES, LANES), F32)],
        compiler_params=_cparams("parallel", "arbitrary"),
    )(proj, proj, proj, proj, lb, gn)


def hgrn_bwd(proj, lb, gn, o_raw, states, dao, name):
    T = proj.shape[0]
    tg = min(HGRN_TG, T)
    nch = tg // C
    n = T // tg

    def body(q_ref, fl_ref, v_ref, g_ref, lb_ref, gn_ref, o_ref, st_ref, dao_ref,
             dq_ref, dfl_ref, dv_ref, dg_ref, dlb_ref, dgn_ref, dst_sc):
        @pl.when(pl.program_id(1) == 0)
        def _():
            dst_sc[...] = jnp.zeros(dst_sc.shape, F32)
            dlb_ref[...] = jnp.zeros(dlb_ref.shape, F32)
            dgn_ref[...] = jnp.zeros(dgn_ref.shape, F32)

        lb_v, gn_v = lb_ref[...], gn_ref[...]
        r64 = lax.broadcasted_iota(jnp.int32, (C, C), 0)
        c64 = lax.broadcasted_iota(jnp.int32, (C, C), 1)
        row = lax.broadcasted_iota(jnp.int32, (C, 1), 0)

        def chunk(cr, carry):
            ci = nch - 1 - cr
            rows = pl.ds(pl.multiple_of(ci * C, C), C)
            q, fl, gv = q_ref[rows, :], fl_ref[rows, :], g_ref[rows, :]
            sg, f, lf, k, sq, qs = _hgrn_gates(q, fl, lb_v)
            vb = v_ref[rows, :].astype(BF16)
            o = o_ref[rows, :]
            ro = _rms(o)
            on = o * ro
            sgg = _sigmoid(gv)
            gate = gv * sgg
            dao_v = dao_ref[rows, :].astype(F32)
            dg_ref[rows, :] = (dao_v * on * gn_v * (sgg * (1.0 + gv * (1.0 - sgg)))).astype(BF16)
            dgn_ref[...] += jnp.sum(dao_v * on * gate, axis=0, keepdims=True)
            don = dao_v * gn_v * gate
            do = ro * (don - on * jnp.mean(don * on, axis=-1, keepdims=True))
            dob = do.astype(BF16)
            bcum = _cumsum_rows(lf)
            blast = bcum[C - 1:C, :]
            factors = _level_factors(bcum)
            a, ops = _intra(qs, k, factors)
            eb = jnp.exp(bcum)
            ekb = jnp.exp(blast - bcum)
            qb = qs * eb
            kb = k * ekb
            st = st_ref[ci]
            dst = dst_sc[...]
            dstb = dst.astype(BF16)
            da = jnp.where(r64 >= c64, _nt(dob, vb), 0.0)
            dv_ref[rows, :] = (_tn(a.astype(BF16), dob) + _nt(kb.astype(BF16), dstb)).astype(BF16)
            dqb = _dot(dob, st.astype(BF16))
            dkb = _dot(vb, dstb)
            eblast = jnp.exp(blast)
            dst_sc[...] = dst * eblast + _tn(dob, qb.astype(BF16))
            dblast = eblast * jnp.sum(dst * st, axis=0, keepdims=True) + jnp.sum(dkb * kb, axis=0, keepdims=True)
            dad = jnp.sum(jnp.where(r64 == c64, da, 0.0), axis=1, keepdims=True)
            dqs = dqb * eb + dad * k
            dk = dkb * ekb + dad * qs
            dbcum = dqb * qb - dkb * kb + jnp.where(row == C - 1, dblast, 0.0)
            for (B, eq, ek), (ql, kl) in zip(factors, ops):
                dal = (da if B == C else jnp.where(_same_block(B), da, 0.0)).astype(BF16)
                dql, dkl = _dot(dal, kl), _tn(dal, ql)
                dqs = dqs + dql * eq
                dk = dk + dkl * ek
                dbcum = dbcum + (dql * ql.astype(F32) - dkl * kl.astype(F32))
            df = _cumsum_rows(dbcum, reverse=True) / f - dk
            dfl_ref[rows, :] = (df * (1.0 - lb_v) * sg * (1.0 - sg)).astype(BF16)
            dlb_ref[...] += jnp.sum(df * (1.0 - sg), axis=0, keepdims=True)
            dq_ref[rows, :] = (dqs * (sq * (1.0 + q * (1.0 - sq)))).astype(BF16)
            return carry

        lax.fori_loop(0, nch, chunk, 0)

    col = lambda off: pl.BlockSpec((tg, LANES), lambda h, i: (n - 1 - i, off + h))
    one = pl.BlockSpec((1, LANES), lambda h, i: (0, h))
    big = jax.ShapeDtypeStruct((T, D_MODEL), BF16)
    small = jax.ShapeDtypeStruct((1, D_MODEL), F32)
    return pl.pallas_call(
        body, name=name, grid=(8, n),
        in_specs=[col(0), col(8), col(16), col(24), one, one, col(0),
                  pl.BlockSpec((None, nch, LANES, LANES), lambda h, i: (h, n - 1 - i, 0, 0)), col(0)],
        out_specs=[col(0), col(0), col(0), col(0), one, one],
        out_shape=[big, big, big, big, small, small],
        scratch_shapes=[pltpu.VMEM((LANES, LANES), F32)],
        compiler_params=_cparams("arbitrary", "arbitrary"),
    )(proj, proj, proj, proj, lb, gn, o_raw, states, dao)


def lower_bound_fwd(logits, name):
    def body(l_ref, s_ref):
        lv = l_ref[...]
        e = jnp.exp(lv - jnp.max(lv, axis=0, keepdims=True))
        s_ref[...] = e / jnp.sum(e, axis=0, keepdims=True)

    return pl.pallas_call(body, name=name, out_shape=jax.ShapeDtypeStruct(logits.shape, F32))(logits)


def lower_bound_bwd(sm, dlb, name):
    def body(s_ref, d_ref, o_ref):
        s = s_ref[...]
        row = lax.broadcasted_iota(jnp.int32, s.shape, 0)
        o_ref[...] = d_ref[...] * s[1:2, :] * (jnp.where(row == 1, 1.0, 0.0) - s)

    return pl.pallas_call(body, name=name, out_shape=jax.ShapeDtypeStruct(sm.shape, F32))(sm, dlb)


def _pad_rows(flat, mult):
    rows = -(-flat.shape[-1] // D_MODEL)
    rows = -(-rows // mult) * mult
    pad = rows * D_MODEL - flat.shape[-1]
    flat = jnp.pad(flat, [(0, 0)] * (flat.ndim - 1) + [(0, pad)])
    return flat.reshape(flat.shape[:-1] + (rows, D_MODEL))


def _gather_weights(w):
    pieces = []
    for nme in SHARDED:
        a = w[nme]
        if nme in BIASES:
            pieces.append(lax.bitcast_convert_type(a, BF16).reshape(-1))
        else:
            pieces.append(a.astype(BF16).reshape(-1))
    flat = _pad_rows(jnp.concatenate(pieces), 16)
    got = all_gather_rows(flat).reshape(N_DEV, -1)
    full, off = {}, 0
    for nme in SHARDED:
        shp = w[nme].shape
        cnt = 1
        for s in shp:
            cnt *= s
        if nme in BIASES:
            seg = got[:, off:off + 2 * cnt].reshape((N_DEV,) + shp + (2,))
            seg = lax.bitcast_convert_type(seg, F32)
            off += 2 * cnt
        else:
            seg = got[:, off:off + cnt].reshape((N_DEV,) + shp)
            off += cnt
        full[nme] = jnp.concatenate([seg[d] for d in range(N_DEV)], axis=SHARD_AXIS[nme])
    return full


def _pieces(gfull, axis):
    shp = gfull.shape
    a = gfull.reshape(shp[:axis] + (N_DEV, shp[axis] // N_DEV) + shp[axis + 1:])
    return jnp.moveaxis(a, axis, 0).reshape(N_DEV, -1)


def _flat_local(vals):
    return _pad_rows(jnp.concatenate([vals[n].reshape(-1) for n in SHARDED] + [vals[n].reshape(-1) for n in REPLICATED]
                                     + [jnp.zeros((1,), F32)]), 32)


def kernel(x, norm_mix, norm_mlp, norm_final, w_up, w_down, swa_w_qkv, swa_b_qkv, swa_sinks, swa_w_o, hgrn_w_in, hgrn_lb_logits, hgrn_g_norm, hgrn_w_o, fox_w_in, fox_b_in, fox_w_o, loss_target, m_norm_mix, m_norm_mlp, m_norm_final, m_w_up, m_w_down, m_swa_w_qkv, m_swa_b_qkv, m_swa_sinks, m_swa_w_o, m_hgrn_w_in, m_hgrn_lb_logits, m_hgrn_g_norm, m_hgrn_w_o, m_fox_w_in, m_fox_b_in, m_fox_w_o, v_norm_mix, v_norm_mlp, v_norm_final, v_w_up, v_w_down, v_swa_w_qkv, v_swa_b_qkv, v_swa_sinks, v_swa_w_o, v_hgrn_w_in, v_hgrn_lb_logits, v_hgrn_g_norm, v_hgrn_w_o, v_fox_w_in, v_fox_b_in, v_fox_w_o):
    w = dict(norm_mix=norm_mix, norm_mlp=norm_mlp, norm_final=norm_final, w_up=w_up, w_down=w_down,
             swa_w_qkv=swa_w_qkv, swa_b_qkv=swa_b_qkv, swa_sinks=swa_sinks, swa_w_o=swa_w_o, hgrn_w_in=hgrn_w_in,
             hgrn_lb_logits=hgrn_lb_logits, hgrn_g_norm=hgrn_g_norm, hgrn_w_o=hgrn_w_o, fox_w_in=fox_w_in,
             fox_b_in=fox_b_in, fox_w_o=fox_w_o)
    mom = dict(norm_mix=m_norm_mix, norm_mlp=m_norm_mlp, norm_final=m_norm_final, w_up=m_w_up, w_down=m_w_down,
               swa_w_qkv=m_swa_w_qkv, swa_b_qkv=m_swa_b_qkv, swa_sinks=m_swa_sinks, swa_w_o=m_swa_w_o,
               hgrn_w_in=m_hgrn_w_in, hgrn_lb_logits=m_hgrn_lb_logits, hgrn_g_norm=m_hgrn_g_norm, hgrn_w_o=m_hgrn_w_o,
               fox_w_in=m_fox_w_in, fox_b_in=m_fox_b_in, fox_w_o=m_fox_w_o)
    var = dict(norm_mix=v_norm_mix, norm_mlp=v_norm_mlp, norm_final=v_norm_final, w_up=v_w_up, w_down=v_w_down,
               swa_w_qkv=v_swa_w_qkv, swa_b_qkv=v_swa_b_qkv, swa_sinks=v_swa_sinks, swa_w_o=v_swa_w_o,
               hgrn_w_in=v_hgrn_w_in, hgrn_lb_logits=v_hgrn_lb_logits, hgrn_g_norm=v_hgrn_g_norm, hgrn_w_o=v_hgrn_w_o,
               fox_w_in=v_fox_w_in, fox_b_in=v_fox_b_in, fox_w_o=v_fox_w_o)
    T = x.shape[1]
    x0 = x[0]
    tgt = loss_target[0]
    W = _gather_weights(w)
    zeros_b = jnp.zeros((1, 4 * D_MODEL), F32)

    def swa_layer(xin, i, j):
        qkv = norm_matmul(xin, norm_mix[i:i + 1], W['swa_w_qkv'][j], W['swa_b_qkv'][j:j + 1], BF16, f"swa_qkv_L{i}")
        dup = lambda a: jnp.broadcast_to(a.reshape(T, 4, 1, 64), (T, 4, 2, 64)).reshape(T, 4 * LANES)
        kdup, vdup = dup(qkv[:, 1024:1280]), dup(qkv[:, 1280:1536])
        sk = jnp.broadcast_to(jnp.pad(swa_sinks[j].reshape(4, 4), ((0, 0), (0, 4)))[:, :, None], (4, 8, LANES))
        ao = swa_fwd(qkv, kdup, vdup, sk, f"swa_fwd_L{i}")
        xmid = matmul(ao, W['swa_w_o'][j], F32, f"swa_out_L{i}", res=xin)
        return xmid, (qkv, kdup, vdup, sk, ao)

    def swa_layer_bwd(xin, saved, dmid, i, j, grads):
        qkv, kdup, vdup, sk, ao = saved
        dao = matmul(dmid, W['swa_w_o'][j].T, BF16, f"swa_dout_L{i}")
        grads['swa_w_o'][j] = tn_matmul(ao, dmid, f"swa_dwo_L{i}")
        dq, dk, dv, dsk = swa_bwd(qkv, kdup, vdup, sk, ao, dao, f"swa_bwd_L{i}")
        fold = lambda a: a.reshape(T, 4, LANES)[:, :, :64].reshape(T, 256).astype(BF16)
        dqkv = jnp.concatenate([dq, fold(dk), fold(dv)], axis=1)
        dx, h, dg = proj_bwd(xin, norm_mix[i:i + 1], dmid, [(dqkv, W['swa_w_qkv'][j].T)], f"swa_din_L{i}")
        grads['swa_w_qkv'][j], dbq = tn_matmul(h, dqkv, f"swa_dwqkv_L{i}", colsum=True)
        grads['swa_b_qkv'][j] = dbq[0]
        grads['swa_sinks'][j] = dsk[:, :4, 0].reshape(16)
        grads['norm_mix'][i] = dg[0]
        return dx

    lb_soft = lower_bound_fwd(hgrn_lb_logits, "hgrn_lb_fwd")
    lb = lb_soft[1:2]

    def hgrn_layer(xin, i, j):
        proj = norm_matmul(xin, norm_mix[i:i + 1], W['hgrn_w_in'][j], zeros_b, F32, f"hgrn_in_L{i}")
        ao, o_raw, states = hgrn_fwd(proj, lb, hgrn_g_norm[j:j + 1], f"hgrn_fwd_L{i}")
        xmid = matmul(ao, W['hgrn_w_o'][j], F32, f"hgrn_out_L{i}", res=xin)
        return xmid, (proj, ao, o_raw, states)

    def hgrn_layer_bwd(xin, saved, dmid, i, j, grads):
        proj, ao, o_raw, states = saved
        dao = matmul(dmid, W['hgrn_w_o'][j].T, BF16, f"hgrn_dout_L{i}")
        grads['hgrn_w_o'][j] = tn_matmul(ao, dmid, f"hgrn_dwo_L{i}")
        dq, dfl, dv, dgt, dlb, dgn = hgrn_bwd(proj, lb, hgrn_g_norm[j:j + 1], o_raw, states, dao, f"hgrn_bwd_L{i}")
        dproj = jnp.concatenate([dq, dfl, dv, dgt], axis=1)
        dx, h, dg = proj_bwd(xin, norm_mix[i:i + 1], dmid, [(dproj, W['hgrn_w_in'][j].T)], f"hgrn_din_L{i}")
        grads['hgrn_w_in'][j] = tn_matmul(h, dproj, f"hgrn_dwin_L{i}")
        grads['hgrn_g_norm'][j] = dgn[0]
        grads['hgrn_lb_logits'] = lower_bound_bwd(lb_soft, dlb, "hgrn_lb_bwd")
        grads['norm_mix'][i] = dg[0]
        return dx

    def fox_layer(xin, i, j):
        w_in = W['fox_w_in'][j]
        b_in = W['fox_b_in'][j:j + 1]
        qkv = norm_matmul(xin, norm_mix[i:i + 1], w_in[:, :3072], b_in[:, :3072], BF16, f"fox_qkv_L{i}")
        wf = jnp.pad(w_in[:, 3072:], ((0, 0), (0, LANES - 16)))
        bf = jnp.pad(b_in[:, 3072:], ((0, 0), (0, LANES - 16)))
        fl = norm_matmul(xin, norm_mix[i:i + 1], wf, bf, F32, f"fox_f_L{i}")
        cs = fox_gate_fwd(fl, f"fox_gate_L{i}")[:, :16]
        crep = jnp.repeat(cs, 64, axis=1)
        crow = cs.T.reshape(8, 2, T)
        ao, lse = fox_fwd(qkv, crep, crow, f"fox_fwd_L{i}")
        xmid = matmul(ao, W['fox_w_o'][j], F32, f"fox_out_L{i}", res=xin)
        return xmid, (qkv, fl, crep, crow, ao, lse, wf)

    def fox_layer_bwd(xin, saved, dmid, i, j, grads):
        qkv, fl, crep, crow, ao, lse, wf = saved
        dao = matmul(dmid, W['fox_w_o'][j].T, BF16, f"fox_dout_L{i}")
        grads['fox_w_o'][j] = tn_matmul(ao, dmid, f"fox_dwo_L{i}")
        dq, delta, rowsum = fox_bwd_dq(qkv, crep, crow, ao, lse, dao, f"fox_dq_L{i}")
        dk, dv, dc = fox_bwd_dkv(qkv, crep, crow, lse, delta, dao, f"fox_dkv_L{i}")
        dcp = jnp.pad(rowsum[:, ::64] + dc.reshape(16, T).T, ((0, 0), (0, LANES - 16)))
        dfl = fox_gate_bwd(fl, dcp, f"fox_dgate_L{i}")
        dqkv = jnp.concatenate([dq, dk, dv], axis=1)
        dx, h, dg = proj_bwd(xin, norm_mix[i:i + 1], dmid,
                             [(dqkv, W['fox_w_in'][j][:, :3072].T), (dfl, wf.T)], f"fox_din_L{i}")
        dwq, dbq = tn_matmul(h, dqkv, f"fox_dwqkv_L{i}", colsum=True)
        dwf, dbf = tn_matmul(h, dfl, f"fox_dwf_L{i}", colsum=True)
        grads['fox_w_in'][j] = jnp.concatenate([dwq, dwf[:, :16]], axis=1)
        grads['fox_b_in'][j] = jnp.concatenate([dbq[0], dbf[0, :16]])
        grads['norm_mix'][i] = dg[0]
        return dx

    mixers = [(swa_layer, swa_layer_bwd), (hgrn_layer, hgrn_layer_bwd), (fox_layer, fox_layer_bwd)]

    xs, mids, saves = [x0], [], []
    for i in range(DEPTH):
        xmid, saved = mixers[i % 3][0](xs[-1], i, i // 3)
        mids.append(xmid)
        saves.append(saved)
        xs.append(mlp_fwd(xmid, norm_mlp[i:i + 1], W['w_up'][i], W['w_down'][i], f"mlp_fwd_L{i}"))

    grads = {n: [None] * w[n].shape[0] for n in WEIGHTS if n not in ('norm_final', 'hgrn_lb_logits')}
    loss_part, dx, dgf = final_loss(xs[-1], norm_final.reshape(1, D_MODEL), tgt, "final_loss")
    grads['norm_final'] = dgf[0]
    for i in reversed(range(DEPTH)):
        dmid, h, a, du, dg = mlp_bwd(mids[i], norm_mlp[i:i + 1], W['w_up'][i], W['w_up'][i].T, W['w_down'][i].T, dx,
                                     f"mlp_bwd_L{i}")
        grads['w_up'][i] = tn_matmul(h, du, f"mlp_dwup_L{i}")
        grads['w_down'][i] = tn_matmul(a, dx, f"mlp_dwdown_L{i}")
        grads['norm_mlp'][i] = dg[0]
        dx = mixers[i % 3][1](xs[i], saves[i], dmid, i, i // 3, grads)
    gfull = {n: (g if not isinstance(g, list) else jnp.stack(g)) for n, g in grads.items()}

    small = jnp.concatenate([gfull[n].reshape(-1) for n in REPLICATED] + [loss_part[0, 0:1]])
    send = jnp.concatenate([_pieces(gfull[n], SHARD_AXIS[n]) for n in SHARDED]
                           + [jnp.broadcast_to(small[None], (N_DEV, small.shape[0]))], axis=1)
    recv = all_to_all_rows(_pad_rows(send, 32))
    outs = reduce_adamw(recv, _flat_local(w), _flat_local(mom), _flat_local(var))
    res = [{}, {}, {}, {}]
    off = 0
    for nme in SHARDED + REPLICATED:
        cnt = w[nme].size
        for o, r in zip(outs, res):
            r[nme] = o.reshape(-1)[off:off + cnt].reshape(w[nme].shape)
        off += cnt
    loss = outs[0].reshape(-1)[off]
    return (loss, dx[None], *[res[0][n] for n in WEIGHTS], *[res[1][n] for n in WEIGHTS],
            *[res[2][n] for n in WEIGHTS], *[res[3][n] for n in WEIGHTS])
```

```python
import functools

import jax
import jax.numpy as jnp
from jax import lax
from jax.experimental import pallas as pl
from jax.experimental.pallas import tpu as pltpu

F32 = jnp.float32
BF16 = jnp.bfloat16
HI = lax.Precision.HIGHEST

N_DEV = 8
D_MODEL = 1024
DEPTH = 4
EPS = 1e-6
SWA_WINDOW = 128
HGRN_CHUNK = 64
LANES = 128
VMEM_LIMIT = 56 << 20

ADAM_LR, ADAM_B1, ADAM_B2, ADAM_EPS, ADAM_WD, ADAM_STEP = 0.001, 0.9, 0.999, 1e-08, 0.01, 10

TM = 512
TF = 512
TK = 512
FOX_T = 1024
SWA_TQ = 512
HGRN_TG = 512
SCAN_T = 256

WEIGHTS = ['norm_mix', 'norm_mlp', 'norm_final', 'w_up', 'w_down', 'swa_w_qkv', 'swa_b_qkv', 'swa_sinks', 'swa_w_o',
           'hgrn_w_in', 'hgrn_lb_logits', 'hgrn_g_norm', 'hgrn_w_o', 'fox_w_in', 'fox_b_in', 'fox_w_o']
SHARD_AXIS = {'norm_mix': None, 'norm_mlp': None, 'norm_final': None, 'w_up': 2, 'w_down': 1, 'swa_w_qkv': 2,
              'swa_b_qkv': 1, 'swa_sinks': None, 'swa_w_o': 1, 'hgrn_w_in': 2, 'hgrn_lb_logits': None,
              'hgrn_g_norm': None, 'hgrn_w_o': 1, 'fox_w_in': 2, 'fox_b_in': 1, 'fox_w_o': 1}
SHARDED = [n for n in WEIGHTS if SHARD_AXIS[n] is not None]
REPLICATED = [n for n in WEIGHTS if SHARD_AXIS[n] is None]
BIASES = ('swa_b_qkv', 'fox_b_in')


def _cparams(*sem):
    return pltpu.CompilerParams(dimension_semantics=sem, vmem_limit_bytes=VMEM_LIMIT)


def _nt(a, b):
    return lax.dot_general(a, b, (((1,), (1,)), ((), ())), preferred_element_type=F32)


def _tn(a, b):
    return lax.dot_general(a, b, (((0,), (0,)), ((), ())), preferred_element_type=F32)


def _dot(a, b):
    return jnp.dot(a, b, preferred_element_type=F32)


def _sigmoid(x):
    return 1.0 / (1.0 + jnp.exp(-x))


def _rms(xv):
    return lax.rsqrt(jnp.mean(xv * xv, axis=-1, keepdims=True) + EPS)


def _rms_bwd(xv, g, dh):
    r = _rms(xv)
    xhat = xv * r
    dhg = dh * g
    dx = r * (dhg - xhat * jnp.mean(dhg * xhat, axis=-1, keepdims=True))
    return dx, jnp.sum(dh * xhat, axis=0, keepdims=True)


def _my_id():
    return lax.axis_index("x"), lax.axis_index("y"), lax.axis_index("c")


def _peer(x, y, c, k):
    return (lax.rem(x + ((k >> 2) & 1), 2), lax.rem(y + ((k >> 1) & 1), 2), lax.rem(c + (k & 1), 2))


def all_gather_rows(local):
    def body(x_ref, o_ref, send_sems, recv_sems, loc_sem):
        x, y, c = _my_id()
        me = 4 * x + 2 * y + c
        mine = pltpu.make_async_copy(x_ref, o_ref.at[me], loc_sem)
        mine.start()
        copies = []
        for k in range(1, N_DEV):
            px, py, pc = _peer(x, y, c, k)
            cp = pltpu.make_async_remote_copy(
                src_ref=x_ref, dst_ref=o_ref.at[me], send_sem=send_sems.at[k - 1], recv_sem=recv_sems.at[k - 1],
                device_id=(px, py, pc), device_id_type=pl.DeviceIdType.MESH)
            cp.start()
            copies.append(cp)
        for cp in copies:
            cp.wait()
        mine.wait()

    return pl.pallas_call(
        body, name="all_gather_weights",
        out_shape=jax.ShapeDtypeStruct((N_DEV,) + local.shape, local.dtype),
        in_specs=[pl.BlockSpec(memory_space=pl.ANY)],
        out_specs=pl.BlockSpec(memory_space=pl.ANY),
        scratch_shapes=[pltpu.SemaphoreType.DMA((N_DEV - 1,)), pltpu.SemaphoreType.DMA((N_DEV - 1,)),
                        pltpu.SemaphoreType.DMA],
    )(local)


def all_to_all_rows(send):
    def body(s_ref, r_ref, send_sems, recv_sems, loc_sem):
        x, y, c = _my_id()
        me = 4 * x + 2 * y + c
        mine = pltpu.make_async_copy(s_ref.at[me], r_ref.at[me], loc_sem)
        mine.start()
        copies = []
        for k in range(1, N_DEV):
            px, py, pc = _peer(x, y, c, k)
            cp = pltpu.make_async_remote_copy(
                src_ref=s_ref.at[4 * px + 2 * py + pc], dst_ref=r_ref.at[me],
                send_sem=send_sems.at[k - 1], recv_sem=recv_sems.at[k - 1],
                device_id=(px, py, pc), device_id_type=pl.DeviceIdType.MESH)
            cp.start()
            copies.append(cp)
        for cp in copies:
            cp.wait()
        mine.wait()

    return pl.pallas_call(
        body, name="all_to_all_grads",
        out_shape=jax.ShapeDtypeStruct(send.shape, send.dtype),
        in_specs=[pl.BlockSpec(memory_space=pl.ANY)],
        out_specs=pl.BlockSpec(memory_space=pl.ANY),
        scratch_shapes=[pltpu.SemaphoreType.DMA((N_DEV - 1,)), pltpu.SemaphoreType.DMA((N_DEV - 1,)),
                        pltpu.SemaphoreType.DMA],
    )(send)


def reduce_adamw(recv, w, m, v):
    R = w.shape[0]
    tr = max(t for t in range(8, 257, 8) if R % t == 0)
    c1 = 1.0 / (1.0 - ADAM_B1 ** ADAM_STEP)
    c2 = 1.0 / (1.0 - ADAM_B2 ** ADAM_STEP)

    def body(r_ref, w_ref, m_ref, v_ref, g_ref, d_ref, nm_ref, nv_ref):
        g = r_ref[0]
        for s in range(1, N_DEV):
            g = g + r_ref[s]
        m2 = ADAM_B1 * m_ref[...] + (1.0 - ADAM_B1) * g
        v2 = ADAM_B2 * v_ref[...] + (1.0 - ADAM_B2) * (g * g)
        g_ref[...] = g
        nm_ref[...] = m2
        nv_ref[...] = v2
        d_ref[...] = -ADAM_LR * ((m2 * c1) / (jnp.sqrt(v2 * c2) + ADAM_EPS) + ADAM_WD * w_ref[...])

    row = pl.BlockSpec((tr, D_MODEL), lambda i: (i, 0))
    shp = jax.ShapeDtypeStruct((R, D_MODEL), F32)
    return pl.pallas_call(
        body, name="reduce_adamw", grid=(R // tr,),
        in_specs=[pl.BlockSpec((N_DEV, tr, D_MODEL), lambda i: (0, i, 0)), row, row, row],
        out_specs=[row, row, row, row], out_shape=[shp, shp, shp, shp],
        compiler_params=_cparams("parallel"),
    )(recv, w, m, v)


def norm_matmul(x, g, w, b, out_dtype, name):
    T, N = x.shape[0], w.shape[1]
    tm, tn = min(TM, T), min(512, N)

    def body(x_ref, g_ref, w_ref, b_ref, o_ref, h_sc):
        @pl.when(pl.program_id(1) == 0)
        def _():
            xv = x_ref[...]
            h_sc[...] = (xv * _rms(xv) * g_ref[...]).astype(BF16)
        o_ref[...] = (_dot(h_sc[...], w_ref[...]) + b_ref[...]).astype(o_ref.dtype)

    return pl.pallas_call(
        body, name=name, grid=(T // tm, N // tn),
        in_specs=[pl.BlockSpec((tm, D_MODEL), lambda i, j: (i, 0)), pl.BlockSpec((1, D_MODEL), lambda i, j: (0, 0)),
                  pl.BlockSpec((D_MODEL, tn), lambda i, j: (0, j)), pl.BlockSpec((1, tn), lambda i, j: (0, j))],
        out_specs=pl.BlockSpec((tm, tn), lambda i, j: (i, j)),
        out_shape=jax.ShapeDtypeStruct((T, N), out_dtype),
        scratch_shapes=[pltpu.VMEM((tm, D_MODEL), BF16)],
        compiler_params=_cparams("parallel", "arbitrary"),
    )(x, g, w, b)


def matmul(a, w, out_dtype, name, res=None):
    T, K = a.shape
    N = w.shape[1]
    tm = min(TM, T)

    def body(*refs):
        if res is None:
            a_ref, w_ref, o_ref = refs
            acc = _dot(a_ref[...].astype(BF16), w_ref[...])
        else:
            a_ref, w_ref, r_ref, o_ref = refs
            acc = r_ref[...] + _dot(a_ref[...].astype(BF16), w_ref[...])
        o_ref[...] = acc.astype(o_ref.dtype)

    in_specs = [pl.BlockSpec((tm, K), lambda i: (i, 0)), pl.BlockSpec((K, N), lambda i: (0, 0))]
    ops = [a, w]
    if res is not None:
        in_specs.append(pl.BlockSpec((tm, N), lambda i: (i, 0)))
        ops.append(res)
    return pl.pallas_call(
        body, name=name, grid=(T // tm,), in_specs=in_specs,
        out_specs=pl.BlockSpec((tm, N), lambda i: (i, 0)),
        out_shape=jax.ShapeDtypeStruct((T, N), out_dtype),
        compiler_params=_cparams("parallel"),
    )(*ops)


def tn_matmul(a, b, name, colsum=False):
    T, M = a.shape
    N = b.shape[1]
    tk = min(TK, T)
    tmm = min(1024, M)
    tn = N if N <= 1024 else (1024 if N % 1024 == 0 else N)

    def body(a_ref, b_ref, o_ref, *rest):
        k = pl.program_id(2)
        bv = b_ref[...]

        @pl.when(k == 0)
        def _():
            o_ref[...] = jnp.zeros(o_ref.shape, F32)
            if colsum:
                rest[0][...] = jnp.zeros(rest[0].shape, F32)

        o_ref[...] += _tn(a_ref[...].astype(BF16), bv.astype(BF16))
        if colsum:
            rest[0][...] += jnp.sum(bv.astype(F32), axis=0, keepdims=True)

    out_specs = [pl.BlockSpec((tmm, tn), lambda i, j, k: (i, j))]
    out_shape = [jax.ShapeDtypeStruct((M, N), F32)]
    if colsum:
        assert M == tmm
        out_specs.append(pl.BlockSpec((1, tn), lambda i, j, k: (0, j)))
        out_shape.append(jax.ShapeDtypeStruct((1, N), F32))
    out = pl.pallas_call(
        body, name=name, grid=(M // tmm, N // tn, T // tk),
        in_specs=[pl.BlockSpec((tk, tmm), lambda i, j, k: (k, i)), pl.BlockSpec((tk, tn), lambda i, j, k: (k, j))],
        out_specs=out_specs, out_shape=out_shape,
        compiler_params=_cparams("parallel", "parallel", "arbitrary"),
    )(a, b)
    return out if colsum else out[0]


def mlp_fwd(x, g, w_up, w_down, name):
    T, F = x.shape[0], w_up.shape[1]
    tm, tf = min(TM, T), min(TF, F)
    nf = F // tf

    def body(x_ref, g_ref, wu_ref, wd_ref, o_ref, h_sc, acc_sc):
        f = pl.program_id(1)

        @pl.when(f == 0)
        def _():
            xv = x_ref[...]
            h_sc[...] = (xv * _rms(xv) * g_ref[...]).astype(BF16)
            acc_sc[...] = xv

        u = jnp.maximum(_dot(h_sc[...], wu_ref[...]), 0.0)
        acc_sc[...] += _dot((u * u).astype(BF16), wd_ref[...])

        @pl.when(f == nf - 1)
        def _():
            o_ref[...] = acc_sc[...]

    return pl.pallas_call(
        body, name=name, grid=(T // tm, nf),
        in_specs=[pl.BlockSpec((tm, D_MODEL), lambda i, f: (i, 0)), pl.BlockSpec((1, D_MODEL), lambda i, f: (0, 0)),
                  pl.BlockSpec((D_MODEL, tf), lambda i, f: (0, f)), pl.BlockSpec((tf, D_MODEL), lambda i, f: (f, 0))],
        out_specs=pl.BlockSpec((tm, D_MODEL), lambda i, f: (i, 0)),
        out_shape=jax.ShapeDtypeStruct((T, D_MODEL), F32),
        scratch_shapes=[pltpu.VMEM((tm, D_MODEL), BF16), pltpu.VMEM((tm, D_MODEL), F32)],
        compiler_params=_cparams("parallel", "arbitrary"),
    )(x, g, w_up, w_down)


def mlp_bwd(x, g, w_up, w_up_t, w_down_t, dy, name):
    T, F = x.shape[0], w_up.shape[1]
    tm, tf = min(TM, T), min(TF, F)
    nf = F // tf

    def body(x_ref, g_ref, wu_ref, wut_ref, wdt_ref, dy_ref, dx_ref, h_ref, a_ref, du_ref, dg_ref, h_sc, dyb_sc, dh_sc):
        i, f = pl.program_id(0), pl.program_id(1)

        @pl.when(f == 0)
        def _():
            xv = x_ref[...]
            h = (xv * _rms(xv) * g_ref[...]).astype(BF16)
            h_sc[...] = h
            h_ref[...] = h
            dyb_sc[...] = dy_ref[...].astype(BF16)
            dh_sc[...] = jnp.zeros(dh_sc.shape, F32)

        @pl.when((i == 0) & (f == 0))
        def _():
            dg_ref[...] = jnp.zeros(dg_ref.shape, F32)

        u = jnp.maximum(_dot(h_sc[...], wu_ref[...]), 0.0)
        a_ref[...] = (u * u).astype(BF16)
        du = (_dot(dyb_sc[...], wdt_ref[...]) * (2.0 * u)).astype(BF16)
        du_ref[...] = du
        dh_sc[...] += _dot(du, wut_ref[...])

        @pl.when(f == nf - 1)
        def _():
            dx, dg = _rms_bwd(x_ref[...], g_ref[...], dh_sc[...])
            dx_ref[...] = dy_ref[...] + dx
            dg_ref[...] += dg

    row = pl.BlockSpec((tm, D_MODEL), lambda i, f: (i, 0))
    hid = pl.BlockSpec((tm, tf), lambda i, f: (i, f))
    return pl.pallas_call(
        body, name=name, grid=(T // tm, nf),
        in_specs=[row, pl.BlockSpec((1, D_MODEL), lambda i, f: (0, 0)),
                  pl.BlockSpec((D_MODEL, tf), lambda i, f: (0, f)), pl.BlockSpec((tf, D_MODEL), lambda i, f: (f, 0)),
                  pl.BlockSpec((D_MODEL, tf), lambda i, f: (0, f)), row],
        out_specs=[row, row, hid, hid, pl.BlockSpec((1, D_MODEL), lambda i, f: (0, 0))],
        out_shape=[jax.ShapeDtypeStruct((T, D_MODEL), F32), jax.ShapeDtypeStruct((T, D_MODEL), BF16),
                   jax.ShapeDtypeStruct((T, F), BF16), jax.ShapeDtypeStruct((T, F), BF16),
                   jax.ShapeDtypeStruct((1, D_MODEL), F32)],
        scratch_shapes=[pltpu.VMEM((tm, D_MODEL), BF16), pltpu.VMEM((tm, D_MODEL), BF16),
                        pltpu.VMEM((tm, D_MODEL), F32)],
        compiler_params=_cparams("arbitrary", "arbitrary"),
    )(x, g, w_up, w_up_t, w_down_t, dy)


def proj_bwd(x, g, dres, parts, name):
    T = x.shape[0]
    tm = min(TM, T)
    n = len(parts)

    def body(*refs):
        x_ref, g_ref, dr_ref = refs[:3]
        da_refs, wt_refs = refs[3:3 + n], refs[3 + n:3 + 2 * n]
        dx_ref, h_ref, dg_ref = refs[3 + 2 * n:]

        @pl.when(pl.program_id(0) == 0)
        def _():
            dg_ref[...] = jnp.zeros(dg_ref.shape, F32)

        xv = x_ref[...]
        dh = _dot(da_refs[0][...].astype(BF16), wt_refs[0][...])
        for a_ref, w_ref in zip(da_refs[1:], wt_refs[1:]):
            dh = dh + _dot(a_ref[...].astype(BF16), w_ref[...])
        h_ref[...] = (xv * _rms(xv) * g_ref[...]).astype(BF16)
        dx, dg = _rms_bwd(xv, g_ref[...], dh)
        dx_ref[...] = dr_ref[...] + dx
        dg_ref[...] += dg

    row = pl.BlockSpec((tm, D_MODEL), lambda i: (i, 0))
    one = pl.BlockSpec((1, D_MODEL), lambda i: (0, 0))
    in_specs = [row, one, row]
    in_specs += [pl.BlockSpec((tm, da.shape[1]), lambda i: (i, 0)) for da, _ in parts]
    in_specs += [pl.BlockSpec(wt.shape, lambda i: (0, 0)) for _, wt in parts]
    return pl.pallas_call(
        body, name=name, grid=(T // tm,), in_specs=in_specs,
        out_specs=[row, row, one],
        out_shape=[jax.ShapeDtypeStruct((T, D_MODEL), F32), jax.ShapeDtypeStruct((T, D_MODEL), BF16),
                   jax.ShapeDtypeStruct((1, D_MODEL), F32)],
        compiler_params=_cparams("arbitrary"),
    )(x, g, dres, *[da for da, _ in parts], *[wt for _, wt in parts])


def final_loss(x, g, tgt, name):
    T = x.shape[0]
    tm = min(TM, T)

    def body(x_ref, g_ref, t_ref, l_ref, dx_ref, dg_ref):
        @pl.when(pl.program_id(0) == 0)
        def _():
            l_ref[...] = jnp.zeros(l_ref.shape, F32)
            dg_ref[...] = jnp.zeros(dg_ref.shape, F32)

        xv = x_ref[...]
        gv = g_ref[...]
        err = xv * _rms(xv) * gv - t_ref[...]
        l_ref[...] += 0.5 * jnp.sum(jnp.mean(err * err, axis=-1, keepdims=True), axis=0, keepdims=True)
        dx, dg = _rms_bwd(xv, gv, err * (1.0 / D_MODEL))
        dx_ref[...] = dx
        dg_ref[...] += dg

    row = pl.BlockSpec((tm, D_MODEL), lambda i: (i, 0))
    one = pl.BlockSpec((1, D_MODEL), lambda i: (0, 0))
    return pl.pallas_call(
        body, name=name, grid=(T // tm,), in_specs=[row, one, row],
        out_specs=[pl.BlockSpec((8, LANES), lambda i: (0, 0)), row, one],
        out_shape=[jax.ShapeDtypeStruct((8, LANES), F32), jax.ShapeDtypeStruct((T, D_MODEL), F32),
                   jax.ShapeDtypeStruct((1, D_MODEL), F32)],
        compiler_params=_cparams("arbitrary"),
    )(x, g, tgt)


def _swa_specs(tq):
    r = tq // SWA_WINDOW
    cur = lambda ix: pl.BlockSpec((tq, LANES), lambda kv, i: (ix(i), kv))
    prev = lambda ix: pl.BlockSpec((SWA_WINDOW, LANES), lambda kv, i: (jnp.maximum(ix(i) * r - 1, 0), kv))
    return cur, prev


def _swa_visible(tq, tile):
    r = lax.broadcasted_iota(jnp.int32, (tq, tq + SWA_WINDOW), 0)
    c = lax.broadcasted_iota(jnp.int32, (tq, tq + SWA_WINDOW), 1)
    rel = r + SWA_WINDOW - c
    return (rel >= 0) & (rel < SWA_WINDOW) & ((c >= SWA_WINDOW) | (tile > 0))


def _swa_probs(qm, kcat, vis, sk):
    s = jnp.where(vis, _nt(qm, kcat) * 0.125, -1e30)
    m = jnp.maximum(jnp.max(s, axis=1, keepdims=True), sk)
    e = jnp.exp(s - m)
    esk = jnp.exp(sk - m)
    inv = 1.0 / (jnp.sum(e, axis=1, keepdims=True) + esk)
    return e * inv, esk * inv


def swa_fwd(qkv, kdup, vdup, sinks_b, name):
    T = qkv.shape[0]
    tq = min(SWA_TQ, T)
    cur, prev = _swa_specs(tq)
    ident = lambda i: i

    def body(q_ref, kc_ref, kp_ref, vc_ref, vp_ref, sk_ref, o_ref):
        i = pl.program_id(1)
        kcat = jnp.concatenate([kp_ref[...], kc_ref[...]], axis=0)
        vcat = jnp.concatenate([vp_ref[...], vc_ref[...]], axis=0)
        vis = _swa_visible(tq, i)
        lane = lax.broadcasted_iota(jnp.int32, (1, LANES), 1)
        for pp in range(2):
            q2 = q_ref[:, pp * LANES:(pp + 1) * LANES]
            outs = []
            for hf in range(2):
                lm = (lane < 64) if hf == 0 else (lane >= 64)
                qm = jnp.where(lm, q2, jnp.zeros_like(q2))
                p, _ = _swa_probs(qm, kcat, vis, sk_ref[2 * pp + hf:2 * pp + hf + 1, 0:1])
                outs.append(_dot(p.astype(BF16), vcat))
            o_ref[:, pp * LANES:(pp + 1) * LANES] = jnp.where(lane < 64, outs[0], outs[1]).astype(BF16)

    return pl.pallas_call(
        body, name=name, grid=(4, T // tq),
        in_specs=[pl.BlockSpec((tq, 2 * LANES), lambda kv, i: (i, kv)), cur(ident), prev(ident), cur(ident), prev(ident),
                  pl.BlockSpec((None, 8, LANES), lambda kv, i: (kv, 0, 0))],
        out_specs=pl.BlockSpec((tq, 2 * LANES), lambda kv, i: (i, kv)),
        out_shape=jax.ShapeDtypeStruct((T, D_MODEL), BF16),
        compiler_params=_cparams("parallel", "arbitrary"),
    )(qkv, kdup, kdup, vdup, vdup, sinks_b)


def swa_bwd(qkv, kdup, vdup, sinks_b, o, do, name):
    T = qkv.shape[0]
    tq = min(SWA_TQ, T)
    n = T // tq
    cur, prev = _swa_specs(tq)
    rev = lambda i: n - 1 - i

    def body(q_ref, kc_ref, kp_ref, vc_ref, vp_ref, sk_ref, o_ref, do_ref, dq_ref, dk_ref, dv_ref, dsk_ref, ck_sc, cv_sc):
        i = pl.program_id(1)

        @pl.when(i == 0)
        def _():
            ck_sc[...] = jnp.zeros(ck_sc.shape, F32)
            cv_sc[...] = jnp.zeros(cv_sc.shape, F32)
            dsk_ref[...] = jnp.zeros(dsk_ref.shape, F32)

        kcat = jnp.concatenate([kp_ref[...], kc_ref[...]], axis=0)
        vcat = jnp.concatenate([vp_ref[...], vc_ref[...]], axis=0)
        vis = _swa_visible(tq, n - 1 - i)
        lane = lax.broadcasted_iota(jnp.int32, (1, LANES), 1)
        dkc = jnp.zeros((tq + SWA_WINDOW, LANES), F32)
        dvc = jnp.zeros((tq + SWA_WINDOW, LANES), F32)
        for pp in range(2):
            sl = slice(pp * LANES, (pp + 1) * LANES)
            q2, do2, o2 = q_ref[:, sl], do_ref[:, sl], o_ref[:, sl]
            dqs = []
            for hf in range(2):
                g = 2 * pp + hf
                lm = (lane < 64) if hf == 0 else (lane >= 64)
                qm = jnp.where(lm, q2, jnp.zeros_like(q2))
                dom = jnp.where(lm, do2, jnp.zeros_like(do2))
                p, psk = _swa_probs(qm, kcat, vis, sk_ref[g:g + 1, 0:1])
                delta = jnp.sum(dom.astype(F32) * o2.astype(F32), axis=1, keepdims=True)
                ds = p * (_nt(dom, vcat) - delta)
                dsk_ref[g:g + 1, :] += jnp.zeros((1, LANES), F32) - jnp.sum(psk * delta, axis=0, keepdims=True)
                dsb = (ds * 0.125).astype(BF16)
                dqs.append(_dot(dsb, kcat))
                dkc = dkc + _tn(dsb, qm)
                dvc = dvc + _tn(p.astype(BF16), dom)
            dq_ref[:, sl] = jnp.where(lane < 64, dqs[0], dqs[1]).astype(BF16)
        dkc = dkc + pltpu.roll(dkc, 64, 1)
        dvc = dvc + pltpu.roll(dvc, 64, 1)
        for full, ref, carry in ((dkc, dk_ref, ck_sc), (dvc, dv_ref, cv_sc)):
            if tq > SWA_WINDOW:
                ref[0:tq - SWA_WINDOW, :] = full[SWA_WINDOW:tq, :]
            ref[tq - SWA_WINDOW:tq, :] = full[tq:tq + SWA_WINDOW, :] + carry[...]
            carry[...] = full[0:SWA_WINDOW, :]

    wide = pl.BlockSpec((tq, 2 * LANES), lambda kv, i: (rev(i), kv))
    return pl.pallas_call(
        body, name=name, grid=(4, n),
        in_specs=[wide, cur(rev), prev(rev), cur(rev), prev(rev),
                  pl.BlockSpec((None, 8, LANES), lambda kv, i: (kv, 0, 0)), wide, wide],
        out_specs=[wide, cur(rev), cur(rev), pl.BlockSpec((None, 8, LANES), lambda kv, i: (kv, 0, 0))],
        out_shape=[jax.ShapeDtypeStruct((T, D_MODEL), BF16), jax.ShapeDtypeStruct((T, 4 * LANES), F32),
                   jax.ShapeDtypeStruct((T, 4 * LANES), F32), jax.ShapeDtypeStruct((4, 8, LANES), F32)],
        scratch_shapes=[pltpu.VMEM((SWA_WINDOW, LANES), F32), pltpu.VMEM((SWA_WINDOW, LANES), F32)],
        compiler_params=_cparams("arbitrary", "arbitrary"),
    )(qkv, kdup, kdup, vdup, vdup, sinks_b, o, do)


def fox_gate_fwd(fl, name):
    T = fl.shape[0]
    ts = min(SCAN_T, T)

    def body(fl_ref, c_ref, carry):
        @pl.when(pl.program_id(0) == 0)
        def _():
            carry[...] = jnp.zeros(carry.shape, F32)

        xv = fl_ref[...]
        ls = jnp.minimum(xv, 0.0) - jnp.log(1.0 + jnp.exp(-jnp.abs(xv)))
        tri = (lax.broadcasted_iota(jnp.int32, (ts, ts), 0) >= lax.broadcasted_iota(jnp.int32, (ts, ts), 1)).astype(F32)
        cs = jnp.dot(tri, ls, precision=HI, preferred_element_type=F32) + carry[...]
        c_ref[...] = cs
        carry[...] = cs[ts - 1:ts, :]

    blk = pl.BlockSpec((ts, LANES), lambda i: (i, 0))
    return pl.pallas_call(
        body, name=name, grid=(T // ts,), in_specs=[blk], out_specs=blk,
        out_shape=jax.ShapeDtypeStruct((T, LANES), F32), scratch_shapes=[pltpu.VMEM((1, LANES), F32)],
        compiler_params=_cparams("arbitrary"),
    )(fl)


def fox_gate_bwd(fl, dc, name):
    T = fl.shape[0]
    ts = min(SCAN_T, T)
    n = T // ts

    def body(fl_ref, dc_ref, o_ref, carry):
        @pl.when(pl.program_id(0) == 0)
        def _():
            carry[...] = jnp.zeros(carry.shape, F32)

        tri = (lax.broadcasted_iota(jnp.int32, (ts, ts), 0) <= lax.broadcasted_iota(jnp.int32, (ts, ts), 1)).astype(F32)
        rs = jnp.dot(tri, dc_ref[...], precision=HI, preferred_element_type=F32) + carry[...]
        carry[...] = rs[0:1, :]
        o_ref[...] = rs * (1.0 / (1.0 + jnp.exp(fl_ref[...])))

    blk = pl.BlockSpec((ts, LANES), lambda i: (n - 1 - i, 0))
    return pl.pallas_call(
        body, name=name, grid=(n,), in_specs=[blk, blk], out_specs=blk,
        out_shape=jax.ShapeDtypeStruct((T, LANES), F32), scratch_shapes=[pltpu.VMEM((1, LANES), F32)],
        compiler_params=_cparams("arbitrary"),
    )(fl, dc)


FOX_RB = 32


def fox_augment(qkv, c):
    T = qkv.shape[0]
    q = (qkv[:, :D_MODEL].astype(F32) * 0.125).astype(BF16).reshape(T, 16, 64)
    k = qkv[:, D_MODEL:2 * D_MODEL].reshape(T, 16, 64)
    c1 = lax.reduce_precision(c, 8, 7)
    c2 = lax.reduce_precision(c - c1, 8, 7)
    c3 = lax.reduce_precision(c - c1 - c2, 8, 7)
    cp = jnp.stack([c1, c2, c3], axis=-1).astype(BF16)
    ones = jnp.ones((T, 16, 3), BF16)
    zeros = jnp.zeros((T, 16, 64 - 6), BF16)
    qa = jnp.concatenate([q, cp, ones, zeros], axis=-1).reshape(T, 16 * LANES)
    ka = jnp.concatenate([k, ones, -cp, zeros], axis=-1).reshape(T, 16 * LANES)
    return qa, ka


def _half(lane, hf):
    return (lane < 64) if hf == 0 else (lane >= 64)


def _pair(lane, a, b):
    return jnp.where(lane < 64, a, pltpu.roll(b, 64, 1)), jnp.where(lane < 64, pltpu.roll(a, 64, 1), b)


def fox_fwd(qa, ka, qkv, name):
    T = qa.shape[0]
    t = min(FOX_T, T)
    n = T // t

    def body(qa_ref, ka_ref, v_ref, o_ref, lse_ref, m_sc, l_sc, acc_sc, ls_sc, s_sc, p_sc):
        i, j = pl.program_id(1), pl.program_id(2)
        lane = lax.broadcasted_iota(jnp.int32, (1, LANES), 1)

        @pl.when(j == 0)
        def _():
            m_sc[...] = jnp.full(m_sc.shape, -1e30, F32)
            l_sc[...] = jnp.zeros(l_sc.shape, F32)
            acc_sc[...] = jnp.zeros(acc_sc.shape, F32)

        def tile(diag):
            v2 = v_ref[...]
            for hf in range(2):
                hs = slice(hf * LANES, (hf + 1) * LANES)
                sv = _nt(qa_ref[:, hs], ka_ref[:, hs])
                if diag:
                    vis = lax.broadcasted_iota(jnp.int32, (t, t), 0) >= lax.broadcasted_iota(jnp.int32, (t, t), 1)
                    sv = jnp.where(vis, sv, -1e30)
                s_sc[...] = sv
                m_old = m_sc[hf]
                m_new = jnp.maximum(m_old, jnp.max(s_sc[...], axis=1, keepdims=True))
                al = jnp.exp(m_old - m_new)
                m_sc[hf] = m_new
                for r0 in range(0, t, FOX_RB):
                    rs = slice(r0, r0 + FOX_RB)
                    mrow = m_new[rs, :]
                    part, pieces = None, []
                    for cb in range(0, t, LANES):
                        pc = jnp.exp(s_sc[rs, cb:cb + LANES] - mrow)
                        part = pc if part is None else part + pc
                        pieces.append(pc.astype(BF16))
                    p_sc[rs, :] = jnp.concatenate(pieces, axis=1)
                    ls_sc[rs, :] = part
                l_sc[hf] = al * l_sc[hf] + ls_sc[...]
                acc_sc[hf] = al * acc_sc[hf] + _dot(p_sc[...], v2)

        @pl.when(j < i)
        def _():
            tile(False)

        @pl.when(j == i)
        def _():
            tile(True)
            l0 = jnp.sum(l_sc[0], axis=1, keepdims=True)
            l1 = jnp.sum(l_sc[1], axis=1, keepdims=True)
            o_ref[...] = jnp.where(lane < 64, acc_sc[0] / l0, acc_sc[1] / l1).astype(BF16)
            lse_ref[...] = jnp.where(lane < 64, m_sc[0] + jnp.log(l0), m_sc[1] + jnp.log(l1))

    oblk = pl.BlockSpec((t, LANES), lambda p, i, j: (i, p))
    return pl.pallas_call(
        body, name=name, grid=(8, n, n),
        in_specs=[pl.BlockSpec((t, 2 * LANES), lambda p, i, j: (i, p)),
                  pl.BlockSpec((t, 2 * LANES), lambda p, i, j: (jnp.minimum(j, i), p)),
                  pl.BlockSpec((t, LANES), lambda p, i, j: (jnp.minimum(j, i), 16 + p))],
        out_specs=[oblk, oblk],
        out_shape=[jax.ShapeDtypeStruct((T, D_MODEL), BF16), jax.ShapeDtypeStruct((T, D_MODEL), F32)],
        scratch_shapes=[pltpu.VMEM((2, t, LANES), F32), pltpu.VMEM((2, t, LANES), F32), pltpu.VMEM((2, t, LANES), F32),
                        pltpu.VMEM((t, LANES), F32), pltpu.VMEM((t, t), F32), pltpu.VMEM((t, t), BF16)],
        compiler_params=_cparams("parallel", "parallel", "arbitrary"),
    )(qa, ka, qkv)


def fox_bwd_dq(qa, ka, qkv, o, lse, do, name):
    T = qa.shape[0]
    t = min(FOX_T, T)
    n = T // t

    def body(qa_ref, ka_ref, v_ref, o_ref, lse_ref, do_ref, dq_ref, dl_ref, aux_ref, acc_sc, dl_sc, s_sc, dp_sc, ds_sc):
        i, j = pl.program_id(1), pl.program_id(2)
        lane = lax.broadcasted_iota(jnp.int32, (1, LANES), 1)

        @pl.when(j == 0)
        def _():
            acc_sc[...] = jnp.zeros(acc_sc.shape, F32)
            d = do_ref[...].astype(F32) * o_ref[...].astype(F32)
            for hf in range(2):
                dl_sc[hf] = jnp.sum(jnp.where(_half(lane, hf), d, 0.0), axis=1, keepdims=True)
            dl_ref[...] = jnp.where(lane < 64, dl_sc[0], dl_sc[1])

        def tile(diag):
            v2, do2 = v_ref[...], do_ref[...]
            for hf in range(2):
                hs = slice(hf * LANES, (hf + 1) * LANES)
                kh = ka_ref[:, hs]
                s_sc[...] = _nt(qa_ref[:, hs], kh)
                dp_sc[...] = _nt(jnp.where(_half(lane, hf), do2, jnp.zeros_like(do2)), v2)
                for r0 in range(0, t, FOX_RB):
                    rs = slice(r0, r0 + FOX_RB)
                    sv = s_sc[rs, :]
                    if diag:
                        vis = (r0 + lax.broadcasted_iota(jnp.int32, (FOX_RB, t), 0)) >= lax.broadcasted_iota(jnp.int32, (FOX_RB, t), 1)
                        sv = jnp.where(vis, sv, -1e30)
                    p = jnp.exp(sv - lse_ref[rs, 64 * hf:64 * hf + 1])
                    ds_sc[rs, :] = (p * (dp_sc[rs, :] - dl_sc[hf, rs, :])).astype(BF16)
                acc_sc[hf] += _dot(ds_sc[...], kh)

        @pl.when(j < i)
        def _():
            tile(False)

        @pl.when(j == i)
        def _():
            tile(True)
            dq, aux = _pair(lane, acc_sc[0], acc_sc[1])
            dq_ref[...] = (dq * 0.125).astype(BF16)
            aux_ref[...] = aux

    oblk = pl.BlockSpec((t, LANES), lambda p, i, j: (i, p))
    return pl.pallas_call(
        body, name=name, grid=(8, n, n),
        in_specs=[pl.BlockSpec((t, 2 * LANES), lambda p, i, j: (i, p)),
                  pl.BlockSpec((t, 2 * LANES), lambda p, i, j: (jnp.minimum(j, i), p)),
                  pl.BlockSpec((t, LANES), lambda p, i, j: (jnp.minimum(j, i), 16 + p)), oblk, oblk, oblk],
        out_specs=[oblk, oblk, oblk],
        out_shape=[jax.ShapeDtypeStruct((T, D_MODEL), BF16), jax.ShapeDtypeStruct((T, D_MODEL), F32),
                   jax.ShapeDtypeStruct((T, D_MODEL), F32)],
        scratch_shapes=[pltpu.VMEM((2, t, LANES), F32), pltpu.VMEM((2, t, 1), F32), pltpu.VMEM((t, t), F32),
                        pltpu.VMEM((t, t), F32), pltpu.VMEM((t, t), BF16)],
        compiler_params=_cparams("parallel", "parallel", "arbitrary"),
    )(qa, ka, qkv, o, lse, do)


def fox_bwd_dkv(qa, ka, qkv, lse_row, delta_row, do, name):
    T = qa.shape[0]
    t = min(FOX_T, T)
    n = T // t

    def body(qa_ref, ka_ref, v_ref, lr_ref, dr_ref, do_ref, dk_ref, dv_ref, aux_ref, dk_sc, dv_sc, s_sc, dp_sc, p_sc, ds_sc):
        j, i = pl.program_id(1), pl.program_id(2)
        lane = lax.broadcasted_iota(jnp.int32, (1, LANES), 1)

        @pl.when(i == 0)
        def _():
            dk_sc[...] = jnp.zeros(dk_sc.shape, F32)
            dv_sc[...] = jnp.zeros(dv_sc.shape, F32)

        def tile(diag):
            v2, do2 = v_ref[...], do_ref[...]
            for hf in range(2):
                hs = slice(hf * LANES, (hf + 1) * LANES)
                lm = _half(lane, hf)
                qh = qa_ref[:, hs]
                s_sc[...] = _nt(ka_ref[:, hs], qh)
                dp_sc[...] = _nt(jnp.where(lm, v2, jnp.zeros_like(v2)), do2)
                lrow, drow = lr_ref[hf:hf + 1, :], dr_ref[hf:hf + 1, :]
                for r0 in range(0, t, FOX_RB):
                    rs = slice(r0, r0 + FOX_RB)
                    sv = s_sc[rs, :]
                    if diag:
                        vis = lax.broadcasted_iota(jnp.int32, (FOX_RB, t), 1) >= (r0 + lax.broadcasted_iota(jnp.int32, (FOX_RB, t), 0))
                        sv = jnp.where(vis, sv, -1e30)
                    p = jnp.exp(sv - lrow)
                    p_sc[rs, :] = p.astype(BF16)
                    ds_sc[rs, :] = (p * (dp_sc[rs, :] - drow)).astype(BF16)
                dv_sc[...] += _dot(p_sc[...], jnp.where(lm, do2, jnp.zeros_like(do2)))
                dk_sc[hf] += _dot(ds_sc[...], qh)

        @pl.when(i > j)
        def _():
            tile(False)

        @pl.when(i == j)
        def _():
            tile(True)

        @pl.when(i == n - 1)
        def _():
            dk, aux = _pair(lane, dk_sc[0], dk_sc[1])
            dk_ref[...] = dk.astype(BF16)
            aux_ref[...] = aux
            dv_ref[...] = dv_sc[...].astype(BF16)

    qblk = pl.BlockSpec((t, LANES), lambda p, j, i: (jnp.maximum(i, j), p))
    kblk = pl.BlockSpec((t, LANES), lambda p, j, i: (j, p))
    rblk = pl.BlockSpec((None, 2, t), lambda p, j, i: (p, 0, jnp.maximum(i, j)))
    return pl.pallas_call(
        body, name=name, grid=(8, n, n),
        in_specs=[pl.BlockSpec((t, 2 * LANES), lambda p, j, i: (jnp.maximum(i, j), p)),
                  pl.BlockSpec((t, 2 * LANES), lambda p, j, i: (j, p)),
                  pl.BlockSpec((t, LANES), lambda p, j, i: (j, 16 + p)), rblk, rblk, qblk],
        out_specs=[kblk, kblk, kblk],
        out_shape=[jax.ShapeDtypeStruct((T, D_MODEL), BF16), jax.ShapeDtypeStruct((T, D_MODEL), BF16),
                   jax.ShapeDtypeStruct((T, D_MODEL), F32)],
        scratch_shapes=[pltpu.VMEM((2, t, LANES), F32), pltpu.VMEM((t, LANES), F32), pltpu.VMEM((t, t), F32),
                        pltpu.VMEM((t, t), F32), pltpu.VMEM((t, t), BF16), pltpu.VMEM((t, t), BF16)],
        compiler_params=_cparams("parallel", "parallel", "arbitrary"),
    )(qa, ka, qkv, lse_row, delta_row, do)


C = HGRN_CHUNK
LEVELS = (64, 32, 16, 8, 4, 2)


def _pivot(b, B, row):
    if B == C:
        return jnp.broadcast_to(b[C // 2 - 1:C // 2, :], b.shape)
    if B >= 8:
        b3 = b.reshape(C // B, B, LANES)
        return jnp.broadcast_to(b3[:, B // 2 - 1:B // 2, :], b3.shape).reshape(C, LANES)
    if B == 4:
        y = jnp.where((row & 3) == 1, b, 0.0)
        return y + pltpu.roll(y, 1, 0) + pltpu.roll(y, 2, 0) + pltpu.roll(y, C - 1, 0)
    y = jnp.where((row & 1) == 0, b, 0.0)
    return y + pltpu.roll(y, 1, 0)


def _level_factors(bcum):
    row = lax.broadcasted_iota(jnp.int32, (C, 1), 0)
    out = []
    for B in LEVELS:
        upper = (row & (B - 1)) >= B // 2
        e = jnp.exp(-jnp.abs(bcum - _pivot(bcum, B, row)))
        out.append((B, jnp.where(upper, e, 0.0), jnp.where(upper, 0.0, e)))
    return out


def _same_block(B):
    sh = B.bit_length() - 1
    r = lax.broadcasted_iota(jnp.int32, (C, C), 0)
    c = lax.broadcasted_iota(jnp.int32, (C, C), 1)
    return (r >> sh) == (c >> sh)


def _hgrn_gates(q, fl, lb):
    sg = _sigmoid(fl)
    f = lb + (1.0 - lb) * sg
    sq = _sigmoid(q)
    return sg, f, jnp.log(f), 1.0 - f, sq, q * sq


def _cumsum_rows(x, reverse=False):
    r = lax.broadcasted_iota(jnp.int32, (C, C), 0)
    c = lax.broadcasted_iota(jnp.int32, (C, C), 1)
    tri = ((r <= c) if reverse else (r >= c)).astype(F32)
    return jnp.dot(tri, x, precision=HI, preferred_element_type=F32)


def _intra(qs, k, factors):
    r = lax.broadcasted_iota(jnp.int32, (C, C), 0)
    c = lax.broadcasted_iota(jnp.int32, (C, C), 1)
    a = jnp.where(r == c, jnp.sum(qs * k, axis=1, keepdims=True), 0.0)
    ops = []
    for B, eq, ek in factors:
        ql, kl = (qs * eq).astype(BF16), (k * ek).astype(BF16)
        al = _nt(ql, kl)
        a = a + (al if B == C else jnp.where(_same_block(B), al, 0.0))
        ops.append((ql, kl))
    return a, ops


def hgrn_fwd(proj, lb, gn, name):
    T = proj.shape[0]
    tg = min(HGRN_TG, T)
    nch = tg // C

    def body(q_ref, fl_ref, v_ref, g_ref, lb_ref, gn_ref, ao_ref, o_ref, st_ref, st_sc):
        @pl.when(pl.program_id(1) == 0)
        def _():
            st_sc[...] = jnp.zeros(st_sc.shape, F32)

        lb_v, gn_v = lb_ref[...], gn_ref[...]

        def chunk(ci, carry):
            rows = pl.ds(pl.multiple_of(ci * C, C), C)
            _, f, lf, k, _, qs = _hgrn_gates(q_ref[rows, :], fl_ref[rows, :], lb_v)
            vb = v_ref[rows, :].astype(BF16)
            gv = g_ref[rows, :]
            bcum = _cumsum_rows(lf)
            blast = bcum[C - 1:C, :]
            a, _ = _intra(qs, k, _level_factors(bcum))
            st = st_sc[...]
            st_ref[ci] = st
            o = _dot(a.astype(BF16), vb) + _nt((qs * jnp.exp(bcum)).astype(BF16), st.astype(BF16))
            st_sc[...] = st * jnp.exp(blast) + _tn(vb, (k * jnp.exp(blast - bcum)).astype(BF16))
            o_ref[rows, :] = o
            ao_ref[rows, :] = (o * _rms(o) * gn_v * (gv * _sigmoid(gv))).astype(BF16)
            return carry

        lax.fori_loop(0, nch, chunk, 0)

    col = lambda off: pl.BlockSpec((tg, LANES), lambda h, i: (i, off + h))
    one = pl.BlockSpec((1, LANES), lambda h, i: (0, h))
    return pl.pallas_call(
        body, name=name, grid=(8, T // tg),
        in_specs=[col(0), col(8), col(16), col(24), one, one],
        out_specs=[col(0), col(0), pl.BlockSpec((None, nch, LANES, LANES), lambda h, i: (h, i, 0, 0))],
        out_shape=[jax.ShapeDtypeStruct((T, D_MODEL), BF16), jax.ShapeDtypeStruct((T, D_MODEL), F32),
                   jax.ShapeDtypeStruct((8, T // C, LANES, LANES), F32)],
        scratch_shapes=[pltpu.VMEM((LANES, LANES), F32)],
        compiler_params=_cparams("parallel", "arbitrary"),
    )(proj, proj, proj, proj, lb, gn)


def hgrn_bwd(proj, lb, gn, o_raw, states, dao, name):
    T = proj.shape[0]
    tg = min(HGRN_TG, T)
    nch = tg // C
    n = T // tg

    def body(q_ref, fl_ref, v_ref, g_ref, lb_ref, gn_ref, o_ref, st_ref, dao_ref,
             dq_ref, dfl_ref, dv_ref, dg_ref, dlb_ref, dgn_ref, dst_sc):
        @pl.when(pl.program_id(1) == 0)
        def _():
            dst_sc[...] = jnp.zeros(dst_sc.shape, F32)
            dlb_ref[...] = jnp.zeros(dlb_ref.shape, F32)
            dgn_ref[...] = jnp.zeros(dgn_ref.shape, F32)

        lb_v, gn_v = lb_ref[...], gn_ref[...]
        r64 = lax.broadcasted_iota(jnp.int32, (C, C), 0)
        c64 = lax.broadcasted_iota(jnp.int32, (C, C), 1)
        row = lax.broadcasted_iota(jnp.int32, (C, 1), 0)

        def chunk(cr, carry):
            ci = nch - 1 - cr
            rows = pl.ds(pl.multiple_of(ci * C, C), C)
            q, fl, gv = q_ref[rows, :], fl_ref[rows, :], g_ref[rows, :]
            sg, f, lf, k, sq, qs = _hgrn_gates(q, fl, lb_v)
            vb = v_ref[rows, :].astype(BF16)
            o = o_ref[rows, :]
            ro = _rms(o)
            on = o * ro
            sgg = _sigmoid(gv)
            gate = gv * sgg
            dao_v = dao_ref[rows, :].astype(F32)
            dg_ref[rows, :] = (dao_v * on * gn_v * (sgg * (1.0 + gv * (1.0 - sgg)))).astype(BF16)
            dgn_ref[...] += jnp.sum(dao_v * on * gate, axis=0, keepdims=True)
            don = dao_v * gn_v * gate
            do = ro * (don - on * jnp.mean(don * on, axis=-1, keepdims=True))
            dob = do.astype(BF16)
            bcum = _cumsum_rows(lf)
            blast = bcum[C - 1:C, :]
            factors = _level_factors(bcum)
            a, ops = _intra(qs, k, factors)
            eb = jnp.exp(bcum)
            ekb = jnp.exp(blast - bcum)
            qb = qs * eb
            kb = k * ekb
            st = st_ref[ci]
            dst = dst_sc[...]
            dstb = dst.astype(BF16)
            da = jnp.where(r64 >= c64, _nt(dob, vb), 0.0)
            dv_ref[rows, :] = (_tn(a.astype(BF16), dob) + _nt(kb.astype(BF16), dstb)).astype(BF16)
            dqb = _dot(dob, st.astype(BF16))
            dkb = _dot(vb, dstb)
            eblast = jnp.exp(blast)
            dst_sc[...] = dst * eblast + _tn(dob, qb.astype(BF16))
            dblast = eblast * jnp.sum(dst * st, axis=0, keepdims=True) + jnp.sum(dkb * kb, axis=0, keepdims=True)
            dad = jnp.sum(jnp.where(r64 == c64, da, 0.0), axis=1, keepdims=True)
            dqs = dqb * eb + dad * k
            dk = dkb * ekb + dad * qs
            dbcum = dqb * qb - dkb * kb + jnp.where(row == C - 1, dblast, 0.0)
            for (B, eq, ek), (ql, kl) in zip(factors, ops):
                dal = (da if B == C else jnp.where(_same_block(B), da, 0.0)).astype(BF16)
                dql, dkl = _dot(dal, kl), _tn(dal, ql)
                dqs = dqs + dql * eq
                dk = dk + dkl * ek
                dbcum = dbcum + (dql * ql.astype(F32) - dkl * kl.astype(F32))
            df = _cumsum_rows(dbcum, reverse=True) / f - dk
            dfl_ref[rows, :] = (df * (1.0 - lb_v) * sg * (1.0 - sg)).astype(BF16)
            dlb_ref[...] += jnp.sum(df * (1.0 - sg), axis=0, keepdims=True)
            dq_ref[rows, :] = (dqs * (sq * (1.0 + q * (1.0 - sq)))).astype(BF16)
            return carry

        lax.fori_loop(0, nch, chunk, 0)

    col = lambda off: pl.BlockSpec((tg, LANES), lambda h, i: (n - 1 - i, off + h))
    one = pl.BlockSpec((1, LANES), lambda h, i: (0, h))
    big = jax.ShapeDtypeStruct((T, D_MODEL), BF16)
    small = jax.ShapeDtypeStruct((1, D_MODEL), F32)
    return pl.pallas_call(
        body, name=name, grid=(8, n),
        in_specs=[col(0), col(8), col(16), col(24), one, one, col(0),
                  pl.BlockSpec((None, nch, LANES, LANES), lambda h, i: (h, n - 1 - i, 0, 0)), col(0)],
        out_specs=[col(0), col(0), col(0), col(0), one, one],
        out_shape=[big, big, big, big, small, small],
        scratch_shapes=[pltpu.VMEM((LANES, LANES), F32)],
        compiler_params=_cparams("arbitrary", "arbitrary"),
    )(proj, proj, proj, proj, lb, gn, o_raw, states, dao)


def lower_bound_fwd(logits, name):
    def body(l_ref, s_ref):
        lv = l_ref[...]
        e = jnp.exp(lv - jnp.max(lv, axis=0, keepdims=True))
        s_ref[...] = e / jnp.sum(e, axis=0, keepdims=True)

    return pl.pallas_call(body, name=name, out_shape=jax.ShapeDtypeStruct(logits.shape, F32))(logits)


def lower_bound_bwd(sm, dlb, name):
    def body(s_ref, d_ref, o_ref):
        s = s_ref[...]
        row = lax.broadcasted_iota(jnp.int32, s.shape, 0)
        o_ref[...] = d_ref[...] * s[1:2, :] * (jnp.where(row == 1, 1.0, 0.0) - s)

    return pl.pallas_call(body, name=name, out_shape=jax.ShapeDtypeStruct(sm.shape, F32))(sm, dlb)


def _pad_rows(flat, mult):
    rows = -(-flat.shape[-1] // D_MODEL)
    rows = -(-rows // mult) * mult
    pad = rows * D_MODEL - flat.shape[-1]
    flat = jnp.pad(flat, [(0, 0)] * (flat.ndim - 1) + [(0, pad)])
    return flat.reshape(flat.shape[:-1] + (rows, D_MODEL))


def _gather_weights(w):
    pieces = []
    for nme in SHARDED:
        a = w[nme]
        if nme in BIASES:
            pieces.append(lax.bitcast_convert_type(a, BF16).reshape(-1))
        else:
            pieces.append(a.astype(BF16).reshape(-1))
    flat = _pad_rows(jnp.concatenate(pieces), 16)
    got = all_gather_rows(flat).reshape(N_DEV, -1)
    full, off = {}, 0
    for nme in SHARDED:
        shp = w[nme].shape
        cnt = 1
        for s in shp:
            cnt *= s
        if nme in BIASES:
            seg = got[:, off:off + 2 * cnt].reshape((N_DEV,) + shp + (2,))
            seg = lax.bitcast_convert_type(seg, F32)
            off += 2 * cnt
        else:
            seg = got[:, off:off + cnt].reshape((N_DEV,) + shp)
            off += cnt
        full[nme] = jnp.concatenate([seg[d] for d in range(N_DEV)], axis=SHARD_AXIS[nme])
    return full


def _pieces(gfull, axis):
    shp = gfull.shape
    a = gfull.reshape(shp[:axis] + (N_DEV, shp[axis] // N_DEV) + shp[axis + 1:])
    return jnp.moveaxis(a, axis, 0).reshape(N_DEV, -1)


def _flat_local(vals):
    return _pad_rows(jnp.concatenate([vals[n].reshape(-1) for n in SHARDED] + [vals[n].reshape(-1) for n in REPLICATED]
                                     + [jnp.zeros((1,), F32)]), 32)


def kernel(x, norm_mix, norm_mlp, norm_final, w_up, w_down, swa_w_qkv, swa_b_qkv, swa_sinks, swa_w_o, hgrn_w_in, hgrn_lb_logits, hgrn_g_norm, hgrn_w_o, fox_w_in, fox_b_in, fox_w_o, loss_target, m_norm_mix, m_norm_mlp, m_norm_final, m_w_up, m_w_down, m_swa_w_qkv, m_swa_b_qkv, m_swa_sinks, m_swa_w_o, m_hgrn_w_in, m_hgrn_lb_logits, m_hgrn_g_norm, m_hgrn_w_o, m_fox_w_in, m_fox_b_in, m_fox_w_o, v_norm_mix, v_norm_mlp, v_norm_final, v_w_up, v_w_down, v_swa_w_qkv, v_swa_b_qkv, v_swa_sinks, v_swa_w_o, v_hgrn_w_in, v_hgrn_lb_logits, v_hgrn_g_norm, v_hgrn_w_o, v_fox_w_in, v_fox_b_in, v_fox_w_o):
    w = dict(norm_mix=norm_mix, norm_mlp=norm_mlp, norm_final=norm_final, w_up=w_up, w_down=w_down,
             swa_w_qkv=swa_w_qkv, swa_b_qkv=swa_b_qkv, swa_sinks=swa_sinks, swa_w_o=swa_w_o, hgrn_w_in=hgrn_w_in,
             hgrn_lb_logits=hgrn_lb_logits, hgrn_g_norm=hgrn_g_norm, hgrn_w_o=hgrn_w_o, fox_w_in=fox_w_in,
             fox_b_in=fox_b_in, fox_w_o=fox_w_o)
    mom = dict(norm_mix=m_norm_mix, norm_mlp=m_norm_mlp, norm_final=m_norm_final, w_up=m_w_up, w_down=m_w_down,
               swa_w_qkv=m_swa_w_qkv, swa_b_qkv=m_swa_b_qkv, swa_sinks=m_swa_sinks, swa_w_o=m_swa_w_o,
               hgrn_w_in=m_hgrn_w_in, hgrn_lb_logits=m_hgrn_lb_logits, hgrn_g_norm=m_hgrn_g_norm, hgrn_w_o=m_hgrn_w_o,
               fox_w_in=m_fox_w_in, fox_b_in=m_fox_b_in, fox_w_o=m_fox_w_o)
    var = dict(norm_mix=v_norm_mix, norm_mlp=v_norm_mlp, norm_final=v_norm_final, w_up=v_w_up, w_down=v_w_down,
               swa_w_qkv=v_swa_w_qkv, swa_b_qkv=v_swa_b_qkv, swa_sinks=v_swa_sinks, swa_w_o=v_swa_w_o,
               hgrn_w_in=v_hgrn_w_in, hgrn_lb_logits=v_hgrn_lb_logits, hgrn_g_norm=v_hgrn_g_norm, hgrn_w_o=v_hgrn_w_o,
               fox_w_in=v_fox_w_in, fox_b_in=v_fox_b_in, fox_w_o=v_fox_w_o)
    T = x.shape[1]
    x0 = x[0]
    tgt = loss_target[0]
    W = _gather_weights(w)
    zeros_b = jnp.zeros((1, 4 * D_MODEL), F32)

    def swa_layer(xin, i, j):
        qkv = norm_matmul(xin, norm_mix[i:i + 1], W['swa_w_qkv'][j], W['swa_b_qkv'][j:j + 1], BF16, f"swa_qkv_L{i}")
        dup = lambda a: jnp.broadcast_to(a.reshape(T, 4, 1, 64), (T, 4, 2, 64)).reshape(T, 4 * LANES)
        kdup, vdup = dup(qkv[:, 1024:1280]), dup(qkv[:, 1280:1536])
        sk = jnp.broadcast_to(jnp.pad(swa_sinks[j].reshape(4, 4), ((0, 0), (0, 4)))[:, :, None], (4, 8, LANES))
        ao = swa_fwd(qkv, kdup, vdup, sk, f"swa_fwd_L{i}")
        xmid = matmul(ao, W['swa_w_o'][j], F32, f"swa_out_L{i}", res=xin)
        return xmid, (qkv, kdup, vdup, sk, ao)

    def swa_layer_bwd(xin, saved, dmid, i, j, grads):
        qkv, kdup, vdup, sk, ao = saved
        dao = matmul(dmid, W['swa_w_o'][j].T, BF16, f"swa_dout_L{i}")
        grads['swa_w_o'][j] = tn_matmul(ao, dmid, f"swa_dwo_L{i}")
        dq, dk, dv, dsk = swa_bwd(qkv, kdup, vdup, sk, ao, dao, f"swa_bwd_L{i}")
        fold = lambda a: a.reshape(T, 4, LANES)[:, :, :64].reshape(T, 256).astype(BF16)
        dqkv = jnp.concatenate([dq, fold(dk), fold(dv)], axis=1)
        dx, h, dg = proj_bwd(xin, norm_mix[i:i + 1], dmid, [(dqkv, W['swa_w_qkv'][j].T)], f"swa_din_L{i}")
        grads['swa_w_qkv'][j], dbq = tn_matmul(h, dqkv, f"swa_dwqkv_L{i}", colsum=True)
        grads['swa_b_qkv'][j] = dbq[0]
        grads['swa_sinks'][j] = dsk[:, :4, 0].reshape(16)
        grads['norm_mix'][i] = dg[0]
        return dx

    lb_soft = lower_bound_fwd(hgrn_lb_logits, "hgrn_lb_fwd")
    lb = lb_soft[1:2]

    def hgrn_layer(xin, i, j):
        proj = norm_matmul(xin, norm_mix[i:i + 1], W['hgrn_w_in'][j], zeros_b, F32, f"hgrn_in_L{i}")
        ao, o_raw, states = hgrn_fwd(proj, lb, hgrn_g_norm[j:j + 1], f"hgrn_fwd_L{i}")
        xmid = matmul(ao, W['hgrn_w_o'][j], F32, f"hgrn_out_L{i}", res=xin)
        return xmid, (proj, ao, o_raw, states)

    def hgrn_layer_bwd(xin, saved, dmid, i, j, grads):
        proj, ao, o_raw, states = saved
        dao = matmul(dmid, W['hgrn_w_o'][j].T, BF16, f"hgrn_dout_L{i}")
        grads['hgrn_w_o'][j] = tn_matmul(ao, dmid, f"hgrn_dwo_L{i}")
        dq, dfl, dv, dgt, dlb, dgn = hgrn_bwd(proj, lb, hgrn_g_norm[j:j + 1], o_raw, states, dao, f"hgrn_bwd_L{i}")
        dproj = jnp.concatenate([dq, dfl, dv, dgt], axis=1)
        dx, h, dg = proj_bwd(xin, norm_mix[i:i + 1], dmid, [(dproj, W['hgrn_w_in'][j].T)], f"hgrn_din_L{i}")
        grads['hgrn_w_in'][j] = tn_matmul(h, dproj, f"hgrn_dwin_L{i}")
        grads['hgrn_g_norm'][j] = dgn[0]
        grads['hgrn_lb_logits'] = lower_bound_bwd(lb_soft, dlb, "hgrn_lb_bwd")
        grads['norm_mix'][i] = dg[0]
        return dx

    def fox_layer(xin, i, j):
        w_in = W['fox_w_in'][j]
        b_in = W['fox_b_in'][j:j + 1]
        qkv = norm_matmul(xin, norm_mix[i:i + 1], w_in[:, :3072], b_in[:, :3072], BF16, f"fox_qkv_L{i}")
        wf = jnp.pad(w_in[:, 3072:], ((0, 0), (0, LANES - 16)))
        bf = jnp.pad(b_in[:, 3072:], ((0, 0), (0, LANES - 16)))
        fl = norm_matmul(xin, norm_mix[i:i + 1], wf, bf, F32, f"fox_f_L{i}")
        qa, ka = fox_augment(qkv, fox_gate_fwd(fl, f"fox_gate_L{i}")[:, :16])
        ao, lse = fox_fwd(qa, ka, qkv, f"fox_fwd_L{i}")
        xmid = matmul(ao, W['fox_w_o'][j], F32, f"fox_out_L{i}", res=xin)
        return xmid, (qkv, fl, qa, ka, ao, lse, wf)

    def fox_layer_bwd(xin, saved, dmid, i, j, grads):
        qkv, fl, qa, ka, ao, lse, wf = saved
        dao = matmul(dmid, W['fox_w_o'][j].T, BF16, f"fox_dout_L{i}")
        grads['fox_w_o'][j] = tn_matmul(ao, dmid, f"fox_dwo_L{i}")
        dq, delta, aux_q = fox_bwd_dq(qa, ka, qkv, ao, lse, dao, f"fox_dq_L{i}")
        as_rows = lambda a: a[:, ::64].T.reshape(8, 2, T)
        dk, dv, aux_k = fox_bwd_dkv(qa, ka, qkv, as_rows(lse), as_rows(delta), dao, f"fox_dkv_L{i}")
        dcp = jnp.pad(aux_q[:, ::64] - aux_k[:, 3::64], ((0, 0), (0, LANES - 16)))
        dfl = fox_gate_bwd(fl, dcp, f"fox_dgate_L{i}")
        dqkv = jnp.concatenate([dq, dk, dv], axis=1)
        dx, h, dg = proj_bwd(xin, norm_mix[i:i + 1], dmid,
                             [(dqkv, W['fox_w_in'][j][:, :3072].T), (dfl, wf.T)], f"fox_din_L{i}")
        dwq, dbq = tn_matmul(h, dqkv, f"fox_dwqkv_L{i}", colsum=True)
        dwf, dbf = tn_matmul(h, dfl, f"fox_dwf_L{i}", colsum=True)
        grads['fox_w_in'][j] = jnp.concatenate([dwq, dwf[:, :16]], axis=1)
        grads['fox_b_in'][j] = jnp.concatenate([dbq[0], dbf[0, :16]])
        grads['norm_mix'][i] = dg[0]
        return dx

    mixers = [(swa_layer, swa_layer_bwd), (hgrn_layer, hgrn_layer_bwd), (fox_layer, fox_layer_bwd)]

    xs, mids, saves = [x0], [], []
    for i in range(DEPTH):
        xmid, saved = mixers[i % 3][0](xs[-1], i, i // 3)
        mids.append(xmid)
        saves.append(saved)
        xs.append(mlp_fwd(xmid, norm_mlp[i:i + 1], W['w_up'][i], W['w_down'][i], f"mlp_fwd_L{i}"))

    grads = {n: [None] * w[n].shape[0] for n in WEIGHTS if n not in ('norm_final', 'hgrn_lb_logits')}
    loss_part, dx, dgf = final_loss(xs[-1], norm_final.reshape(1, D_MODEL), tgt, "final_loss")
    grads['norm_final'] = dgf[0]
    for i in reversed(range(DEPTH)):
        dmid, h, a, du, dg = mlp_bwd(mids[i], norm_mlp[i:i + 1], W['w_up'][i], W['w_up'][i].T, W['w_down'][i].T, dx,
                                     f"mlp_bwd_L{i}")
        grads['w_up'][i] = tn_matmul(h, du, f"mlp_dwup_L{i}")
        grads['w_down'][i] = tn_matmul(a, dx, f"mlp_dwdown_L{i}")
        grads['norm_mlp'][i] = dg[0]
        dx = mixers[i % 3][1](xs[i], saves[i], dmid, i, i // 3, grads)
    gfull = {n: (g if not isinstance(g, list) else jnp.stack(g)) for n, g in grads.items()}

    small = jnp.concatenate([gfull[n].reshape(-1) for n in REPLICATED] + [loss_part[0, 0:1]])
    send = jnp.concatenate([_pieces(gfull[n], SHARD_AXIS[n]) for n in SHARDED]
                           + [jnp.broadcast_to(small[None], (N_DEV, small.shape[0]))], axis=1)
    recv = all_to_all_rows(_pad_rows(send, 32))
    outs = reduce_adamw(recv, _flat_local(w), _flat_local(mom), _flat_local(var))
    res = [{}, {}, {}, {}]
    off = 0
    for nme in SHARDED + REPLICATED:
        cnt = w[nme].size
        for o, r in zip(outs, res):
            r[nme] = o.reshape(-1)[off:off + cnt].reshape(w[nme].shape)
        off += cnt
    loss = outs[0].reshape(-1)[off]
    return (loss, dx[None], *[res[0][n] for n in WEIGHTS], *[res[1][n] for n in WEIGHTS],
            *[res[2][n] for n in WEIGHTS], *[res[3][n] for n in WEIGHTS])
```

```python
import functools

import jax
import jax.numpy as jnp
from jax import lax
from jax.experimental import pallas as pl
from jax.experimental.pallas import tpu as pltpu

F32 = jnp.float32
BF16 = jnp.bfloat16
HI = lax.Precision.HIGHEST

N_DEV = 8
D_MODEL = 1024
DEPTH = 4
EPS = 1e-6
SWA_WINDOW = 128
HGRN_CHUNK = 64
LANES = 128
VMEM_LIMIT = 56 << 20

ADAM_LR, ADAM_B1, ADAM_B2, ADAM_EPS, ADAM_WD, ADAM_STEP = 0.001, 0.9, 0.999, 1e-08, 0.01, 10

TM = 512
TF = 512
TK = 512
FOX_T = 1024
SWA_TQ = 512
HGRN_TG = 512
SCAN_T = 256

WEIGHTS = ['norm_mix', 'norm_mlp', 'norm_final', 'w_up', 'w_down', 'swa_w_qkv', 'swa_b_qkv', 'swa_sinks', 'swa_w_o',
           'hgrn_w_in', 'hgrn_lb_logits', 'hgrn_g_norm', 'hgrn_w_o', 'fox_w_in', 'fox_b_in', 'fox_w_o']
SHARD_AXIS = {'norm_mix': None, 'norm_mlp': None, 'norm_final': None, 'w_up': 2, 'w_down': 1, 'swa_w_qkv': 2,
              'swa_b_qkv': 1, 'swa_sinks': None, 'swa_w_o': 1, 'hgrn_w_in': 2, 'hgrn_lb_logits': None,
              'hgrn_g_norm': None, 'hgrn_w_o': 1, 'fox_w_in': 2, 'fox_b_in': 1, 'fox_w_o': 1}
SHARDED = [n for n in WEIGHTS if SHARD_AXIS[n] is not None]
REPLICATED = [n for n in WEIGHTS if SHARD_AXIS[n] is None]
BIASES = ('swa_b_qkv', 'fox_b_in')


def _cparams(*sem):
    return pltpu.CompilerParams(dimension_semantics=sem, vmem_limit_bytes=VMEM_LIMIT)


def _nt(a, b):
    return lax.dot_general(a, b, (((1,), (1,)), ((), ())), preferred_element_type=F32)


def _tn(a, b):
    return lax.dot_general(a, b, (((0,), (0,)), ((), ())), preferred_element_type=F32)


def _dot(a, b):
    return jnp.dot(a, b, preferred_element_type=F32)


def _sigmoid(x):
    return 1.0 / (1.0 + jnp.exp(-x))


def _rms(xv):
    return lax.rsqrt(jnp.mean(xv * xv, axis=-1, keepdims=True) + EPS)


def _rms_bwd(xv, g, dh):
    r = _rms(xv)
    xhat = xv * r
    dhg = dh * g
    dx = r * (dhg - xhat * jnp.mean(dhg * xhat, axis=-1, keepdims=True))
    return dx, jnp.sum(dh * xhat, axis=0, keepdims=True)


def _my_id():
    return lax.axis_index("x"), lax.axis_index("y"), lax.axis_index("c")


def _peer(x, y, c, k):
    return (lax.rem(x + ((k >> 2) & 1), 2), lax.rem(y + ((k >> 1) & 1), 2), lax.rem(c + (k & 1), 2))


def all_gather_rows(local):
    def body(x_ref, o_ref, send_sems, recv_sems, loc_sem):
        x, y, c = _my_id()
        me = 4 * x + 2 * y + c
        mine = pltpu.make_async_copy(x_ref, o_ref.at[me], loc_sem)
        mine.start()
        copies = []
        for k in range(1, N_DEV):
            px, py, pc = _peer(x, y, c, k)
            cp = pltpu.make_async_remote_copy(
                src_ref=x_ref, dst_ref=o_ref.at[me], send_sem=send_sems.at[k - 1], recv_sem=recv_sems.at[k - 1],
                device_id=(px, py, pc), device_id_type=pl.DeviceIdType.MESH)
            cp.start()
            copies.append(cp)
        for cp in copies:
            cp.wait()
        mine.wait()

    return pl.pallas_call(
        body, name="all_gather_weights",
        out_shape=jax.ShapeDtypeStruct((N_DEV,) + local.shape, local.dtype),
        in_specs=[pl.BlockSpec(memory_space=pl.ANY)],
        out_specs=pl.BlockSpec(memory_space=pl.ANY),
        scratch_shapes=[pltpu.SemaphoreType.DMA((N_DEV - 1,)), pltpu.SemaphoreType.DMA((N_DEV - 1,)),
                        pltpu.SemaphoreType.DMA],
    )(local)


def all_to_all_rows(sends):
    n = len(sends)

    def body(*refs):
        s_refs, r_refs = refs[:n], refs[n:2 * n]
        send_sems, recv_sems, loc_sems = refs[2 * n:]
        x, y, c = _my_id()
        me = 4 * x + 2 * y + c
        copies = []
        for a, (s_ref, r_ref) in enumerate(zip(s_refs, r_refs)):
            mine = pltpu.make_async_copy(s_ref.at[me], r_ref.at[me], loc_sems.at[a])
            mine.start()
            copies.append(mine)
            for k in range(1, N_DEV):
                px, py, pc = _peer(x, y, c, k)
                sem = a * (N_DEV - 1) + k - 1
                cp = pltpu.make_async_remote_copy(
                    src_ref=s_ref.at[4 * px + 2 * py + pc], dst_ref=r_ref.at[me],
                    send_sem=send_sems.at[sem], recv_sem=recv_sems.at[sem],
                    device_id=(px, py, pc), device_id_type=pl.DeviceIdType.MESH)
                cp.start()
                copies.append(cp)
        for cp in copies:
            cp.wait()

    hbm = pl.BlockSpec(memory_space=pl.ANY)
    return pl.pallas_call(
        body, name="all_to_all_grads",
        out_shape=[jax.ShapeDtypeStruct(s.shape, s.dtype) for s in sends],
        in_specs=[hbm] * n, out_specs=[hbm] * n,
        scratch_shapes=[pltpu.SemaphoreType.DMA((n * (N_DEV - 1),)), pltpu.SemaphoreType.DMA((n * (N_DEV - 1),)),
                        pltpu.SemaphoreType.DMA((n,))],
    )(*sends)


def reduce_adamw(recv, w, m, v, name):
    R = w.shape[0]
    tr = max(t for t in range(16, 257, 16) if R % t == 0)
    c1 = 1.0 / (1.0 - ADAM_B1 ** ADAM_STEP)
    c2 = 1.0 / (1.0 - ADAM_B2 ** ADAM_STEP)

    def body(r_ref, w_ref, m_ref, v_ref, g_ref, d_ref, nm_ref, nv_ref):
        g = r_ref[0].astype(F32)
        for s in range(1, N_DEV):
            g = g + r_ref[s].astype(F32)
        m2 = ADAM_B1 * m_ref[...] + (1.0 - ADAM_B1) * g
        v2 = ADAM_B2 * v_ref[...] + (1.0 - ADAM_B2) * (g * g)
        g_ref[...] = g
        nm_ref[...] = m2
        nv_ref[...] = v2
        d_ref[...] = -ADAM_LR * ((m2 * c1) / (jnp.sqrt(v2 * c2) + ADAM_EPS) + ADAM_WD * w_ref[...])

    row = pl.BlockSpec((tr, D_MODEL), lambda i: (i, 0))
    shp = jax.ShapeDtypeStruct((R, D_MODEL), F32)
    return pl.pallas_call(
        body, name=name, grid=(R // tr,),
        in_specs=[pl.BlockSpec((N_DEV, tr, D_MODEL), lambda i: (0, i, 0)), row, row, row],
        out_specs=[row, row, row, row], out_shape=[shp, shp, shp, shp],
        compiler_params=_cparams("parallel"),
    )(recv, w, m, v)


def norm_matmul(x, g, w, b, out_dtype, name):
    T, N = x.shape[0], w.shape[1]
    tm, tn = min(TM, T), min(512, N)

    def body(x_ref, g_ref, w_ref, b_ref, o_ref, h_sc):
        @pl.when(pl.program_id(1) == 0)
        def _():
            xv = x_ref[...]
            h_sc[...] = (xv * _rms(xv) * g_ref[...]).astype(BF16)
        o_ref[...] = (_dot(h_sc[...], w_ref[...]) + b_ref[...]).astype(o_ref.dtype)

    return pl.pallas_call(
        body, name=name, grid=(T // tm, N // tn),
        in_specs=[pl.BlockSpec((tm, D_MODEL), lambda i, j: (i, 0)), pl.BlockSpec((1, D_MODEL), lambda i, j: (0, 0)),
                  pl.BlockSpec((D_MODEL, tn), lambda i, j: (0, j)), pl.BlockSpec((1, tn), lambda i, j: (0, j))],
        out_specs=pl.BlockSpec((tm, tn), lambda i, j: (i, j)),
        out_shape=jax.ShapeDtypeStruct((T, N), out_dtype),
        scratch_shapes=[pltpu.VMEM((tm, D_MODEL), BF16)],
        compiler_params=_cparams("parallel", "arbitrary"),
    )(x, g, w, b)


def matmul(a, w, out_dtype, name, res=None):
    T, K = a.shape
    N = w.shape[1]
    tm = min(TM, T)

    def body(*refs):
        if res is None:
            a_ref, w_ref, o_ref = refs
            acc = _dot(a_ref[...].astype(BF16), w_ref[...])
        else:
            a_ref, w_ref, r_ref, o_ref = refs
            acc = r_ref[...] + _dot(a_ref[...].astype(BF16), w_ref[...])
        o_ref[...] = acc.astype(o_ref.dtype)

    in_specs = [pl.BlockSpec((tm, K), lambda i: (i, 0)), pl.BlockSpec((K, N), lambda i: (0, 0))]
    ops = [a, w]
    if res is not None:
        in_specs.append(pl.BlockSpec((tm, N), lambda i: (i, 0)))
        ops.append(res)
    return pl.pallas_call(
        body, name=name, grid=(T // tm,), in_specs=in_specs,
        out_specs=pl.BlockSpec((tm, N), lambda i: (i, 0)),
        out_shape=jax.ShapeDtypeStruct((T, N), out_dtype),
        compiler_params=_cparams("parallel"),
    )(*ops)


def tn_matmul(a, b, name, colsum=False):
    T, M = a.shape
    N = b.shape[1]
    tk = min(TK, T)
    tmm = min(1024, M)
    tn = N if N <= 1024 else (1024 if N % 1024 == 0 else N)

    def body(a_ref, b_ref, o_ref, *rest):
        k = pl.program_id(2)
        bv = b_ref[...]

        @pl.when(k == 0)
        def _():
            o_ref[...] = jnp.zeros(o_ref.shape, F32)
            if colsum:
                rest[0][...] = jnp.zeros(rest[0].shape, F32)

        o_ref[...] += _tn(a_ref[...].astype(BF16), bv.astype(BF16))
        if colsum:
            rest[0][...] += jnp.sum(bv.astype(F32), axis=0, keepdims=True)

    out_specs = [pl.BlockSpec((tmm, tn), lambda i, j, k: (i, j))]
    out_shape = [jax.ShapeDtypeStruct((M, N), F32)]
    if colsum:
        assert M == tmm
        out_specs.append(pl.BlockSpec((1, tn), lambda i, j, k: (0, j)))
        out_shape.append(jax.ShapeDtypeStruct((1, N), F32))
    out = pl.pallas_call(
        body, name=name, grid=(M // tmm, N // tn, T // tk),
        in_specs=[pl.BlockSpec((tk, tmm), lambda i, j, k: (k, i)), pl.BlockSpec((tk, tn), lambda i, j, k: (k, j))],
        out_specs=out_specs, out_shape=out_shape,
        compiler_params=_cparams("parallel", "parallel", "arbitrary"),
    )(a, b)
    return out if colsum else out[0]


def mlp_fwd(x, g, w_up, w_down, name):
    T, F = x.shape[0], w_up.shape[1]
    tm, tf = min(TM, T), min(TF, F)
    nf = F // tf

    def body(x_ref, g_ref, wu_ref, wd_ref, o_ref, h_sc, acc_sc):
        f = pl.program_id(1)

        @pl.when(f == 0)
        def _():
            xv = x_ref[...]
            h_sc[...] = (xv * _rms(xv) * g_ref[...]).astype(BF16)
            acc_sc[...] = xv

        u = jnp.maximum(_dot(h_sc[...], wu_ref[...]), 0.0)
        acc_sc[...] += _dot((u * u).astype(BF16), wd_ref[...])

        @pl.when(f == nf - 1)
        def _():
            o_ref[...] = acc_sc[...]

    return pl.pallas_call(
        body, name=name, grid=(T // tm, nf),
        in_specs=[pl.BlockSpec((tm, D_MODEL), lambda i, f: (i, 0)), pl.BlockSpec((1, D_MODEL), lambda i, f: (0, 0)),
                  pl.BlockSpec((D_MODEL, tf), lambda i, f: (0, f)), pl.BlockSpec((tf, D_MODEL), lambda i, f: (f, 0))],
        out_specs=pl.BlockSpec((tm, D_MODEL), lambda i, f: (i, 0)),
        out_shape=jax.ShapeDtypeStruct((T, D_MODEL), F32),
        scratch_shapes=[pltpu.VMEM((tm, D_MODEL), BF16), pltpu.VMEM((tm, D_MODEL), F32)],
        compiler_params=_cparams("parallel", "arbitrary"),
    )(x, g, w_up, w_down)


def mlp_bwd(x, g, w_up, w_up_t, w_down_t, dy, name):
    T, F = x.shape[0], w_up.shape[1]
    tm, tf = min(TM, T), min(TF, F)
    nf = F // tf

    def body(x_ref, g_ref, wu_ref, wut_ref, wdt_ref, dy_ref, dx_ref, h_ref, a_ref, du_ref, dg_ref, h_sc, dyb_sc, dh_sc):
        i, f = pl.program_id(0), pl.program_id(1)

        @pl.when(f == 0)
        def _():
            xv = x_ref[...]
            h = (xv * _rms(xv) * g_ref[...]).astype(BF16)
            h_sc[...] = h
            h_ref[...] = h
            dyb_sc[...] = dy_ref[...].astype(BF16)
            dh_sc[...] = jnp.zeros(dh_sc.shape, F32)

        @pl.when((i == 0) & (f == 0))
        def _():
            dg_ref[...] = jnp.zeros(dg_ref.shape, F32)

        u = jnp.maximum(_dot(h_sc[...], wu_ref[...]), 0.0)
        a_ref[...] = (u * u).astype(BF16)
        du = (_dot(dyb_sc[...], wdt_ref[...]) * (2.0 * u)).astype(BF16)
        du_ref[...] = du
        dh_sc[...] += _dot(du, wut_ref[...])

        @pl.when(f == nf - 1)
        def _():
            dx, dg = _rms_bwd(x_ref[...], g_ref[...], dh_sc[...])
            dx_ref[...] = dy_ref[...] + dx
            dg_ref[...] += dg

    row = pl.BlockSpec((tm, D_MODEL), lambda i, f: (i, 0))
    hid = pl.BlockSpec((tm, tf), lambda i, f: (i, f))
    return pl.pallas_call(
        body, name=name, grid=(T // tm, nf),
        in_specs=[row, pl.BlockSpec((1, D_MODEL), lambda i, f: (0, 0)),
                  pl.BlockSpec((D_MODEL, tf), lambda i, f: (0, f)), pl.BlockSpec((tf, D_MODEL), lambda i, f: (f, 0)),
                  pl.BlockSpec((D_MODEL, tf), lambda i, f: (0, f)), row],
        out_specs=[row, row, hid, hid, pl.BlockSpec((1, D_MODEL), lambda i, f: (0, 0))],
        out_shape=[jax.ShapeDtypeStruct((T, D_MODEL), F32), jax.ShapeDtypeStruct((T, D_MODEL), BF16),
                   jax.ShapeDtypeStruct((T, F), BF16), jax.ShapeDtypeStruct((T, F), BF16),
                   jax.ShapeDtypeStruct((1, D_MODEL), F32)],
        scratch_shapes=[pltpu.VMEM((tm, D_MODEL), BF16), pltpu.VMEM((tm, D_MODEL), BF16),
                        pltpu.VMEM((tm, D_MODEL), F32)],
        compiler_params=_cparams("arbitrary", "arbitrary"),
    )(x, g, w_up, w_up_t, w_down_t, dy)


def proj_bwd(x, g, dres, parts, name):
    T = x.shape[0]
    tm = min(TM, T)
    n = len(parts)

    def body(*refs):
        x_ref, g_ref, dr_ref = refs[:3]
        da_refs, wt_refs = refs[3:3 + n], refs[3 + n:3 + 2 * n]
        dx_ref, h_ref, dg_ref = refs[3 + 2 * n:]

        @pl.when(pl.program_id(0) == 0)
        def _():
            dg_ref[...] = jnp.zeros(dg_ref.shape, F32)

        xv = x_ref[...]
        dh = _dot(da_refs[0][...].astype(BF16), wt_refs[0][...])
        for a_ref, w_ref in zip(da_refs[1:], wt_refs[1:]):
            dh = dh + _dot(a_ref[...].astype(BF16), w_ref[...])
        h_ref[...] = (xv * _rms(xv) * g_ref[...]).astype(BF16)
        dx, dg = _rms_bwd(xv, g_ref[...], dh)
        dx_ref[...] = dr_ref[...] + dx
        dg_ref[...] += dg

    row = pl.BlockSpec((tm, D_MODEL), lambda i: (i, 0))
    one = pl.BlockSpec((1, D_MODEL), lambda i: (0, 0))
    in_specs = [row, one, row]
    in_specs += [pl.BlockSpec((tm, da.shape[1]), lambda i: (i, 0)) for da, _ in parts]
    in_specs += [pl.BlockSpec(wt.shape, lambda i: (0, 0)) for _, wt in parts]
    return pl.pallas_call(
        body, name=name, grid=(T // tm,), in_specs=in_specs,
        out_specs=[row, row, one],
        out_shape=[jax.ShapeDtypeStruct((T, D_MODEL), F32), jax.ShapeDtypeStruct((T, D_MODEL), BF16),
                   jax.ShapeDtypeStruct((1, D_MODEL), F32)],
        compiler_params=_cparams("arbitrary"),
    )(x, g, dres, *[da for da, _ in parts], *[wt for _, wt in parts])


def final_loss(x, g, tgt, name):
    T = x.shape[0]
    tm = min(TM, T)

    def body(x_ref, g_ref, t_ref, l_ref, dx_ref, dg_ref):
        @pl.when(pl.program_id(0) == 0)
        def _():
            l_ref[...] = jnp.zeros(l_ref.shape, F32)
            dg_ref[...] = jnp.zeros(dg_ref.shape, F32)

        xv = x_ref[...]
        gv = g_ref[...]
        err = xv * _rms(xv) * gv - t_ref[...]
        l_ref[...] += 0.5 * jnp.sum(jnp.mean(err * err, axis=-1, keepdims=True), axis=0, keepdims=True)
        dx, dg = _rms_bwd(xv, gv, err * (1.0 / D_MODEL))
        dx_ref[...] = dx
        dg_ref[...] += dg

    row = pl.BlockSpec((tm, D_MODEL), lambda i: (i, 0))
    one = pl.BlockSpec((1, D_MODEL), lambda i: (0, 0))
    return pl.pallas_call(
        body, name=name, grid=(T // tm,), in_specs=[row, one, row],
        out_specs=[pl.BlockSpec((8, LANES), lambda i: (0, 0)), row, one],
        out_shape=[jax.ShapeDtypeStruct((8, LANES), F32), jax.ShapeDtypeStruct((T, D_MODEL), F32),
                   jax.ShapeDtypeStruct((1, D_MODEL), F32)],
        compiler_params=_cparams("arbitrary"),
    )(x, g, tgt)


def _swa_specs(tq):
    r = tq // SWA_WINDOW
    cur = lambda ix: pl.BlockSpec((tq, LANES), lambda kv, i: (ix(i), kv))
    prev = lambda ix: pl.BlockSpec((SWA_WINDOW, LANES), lambda kv, i: (jnp.maximum(ix(i) * r - 1, 0), kv))
    return cur, prev


def _swa_visible(tq, tile):
    r = lax.broadcasted_iota(jnp.int32, (tq, tq + SWA_WINDOW), 0)
    c = lax.broadcasted_iota(jnp.int32, (tq, tq + SWA_WINDOW), 1)
    rel = r + SWA_WINDOW - c
    return (rel >= 0) & (rel < SWA_WINDOW) & ((c >= SWA_WINDOW) | (tile > 0))


def _swa_probs(qm, kcat, vis, sk):
    s = jnp.where(vis, _nt(qm, kcat) * 0.125, -1e30)
    m = jnp.maximum(jnp.max(s, axis=1, keepdims=True), sk)
    e = jnp.exp(s - m)
    esk = jnp.exp(sk - m)
    inv = 1.0 / (jnp.sum(e, axis=1, keepdims=True) + esk)
    return e * inv, esk * inv


def swa_fwd(qkv, kdup, vdup, sinks_b, name):
    T = qkv.shape[0]
    tq = min(SWA_TQ, T)
    cur, prev = _swa_specs(tq)
    ident = lambda i: i

    def body(q_ref, kc_ref, kp_ref, vc_ref, vp_ref, sk_ref, o_ref):
        i = pl.program_id(1)
        kcat = jnp.concatenate([kp_ref[...], kc_ref[...]], axis=0)
        vcat = jnp.concatenate([vp_ref[...], vc_ref[...]], axis=0)
        vis = _swa_visible(tq, i)
        lane = lax.broadcasted_iota(jnp.int32, (1, LANES), 1)
        for pp in range(2):
            q2 = q_ref[:, pp * LANES:(pp + 1) * LANES]
            outs = []
            for hf in range(2):
                lm = (lane < 64) if hf == 0 else (lane >= 64)
                qm = jnp.where(lm, q2, jnp.zeros_like(q2))
                p, _ = _swa_probs(qm, kcat, vis, sk_ref[2 * pp + hf:2 * pp + hf + 1, 0:1])
                outs.append(_dot(p.astype(BF16), vcat))
            o_ref[:, pp * LANES:(pp + 1) * LANES] = jnp.where(lane < 64, outs[0], outs[1]).astype(BF16)

    return pl.pallas_call(
        body, name=name, grid=(4, T // tq),
        in_specs=[pl.BlockSpec((tq, 2 * LANES), lambda kv, i: (i, kv)), cur(ident), prev(ident), cur(ident), prev(ident),
                  pl.BlockSpec((None, 8, LANES), lambda kv, i: (kv, 0, 0))],
        out_specs=pl.BlockSpec((tq, 2 * LANES), lambda kv, i: (i, kv)),
        out_shape=jax.ShapeDtypeStruct((T, D_MODEL), BF16),
        compiler_params=_cparams("parallel", "arbitrary"),
    )(qkv, kdup, kdup, vdup, vdup, sinks_b)


def swa_bwd(qkv, kdup, vdup, sinks_b, o, do, name):
    T = qkv.shape[0]
    tq = min(SWA_TQ, T)
    n = T // tq
    cur, prev = _swa_specs(tq)
    rev = lambda i: n - 1 - i

    def body(q_ref, kc_ref, kp_ref, vc_ref, vp_ref, sk_ref, o_ref, do_ref, dq_ref, dk_ref, dv_ref, dsk_ref, ck_sc, cv_sc):
        i = pl.program_id(1)

        @pl.when(i == 0)
        def _():
            ck_sc[...] = jnp.zeros(ck_sc.shape, F32)
            cv_sc[...] = jnp.zeros(cv_sc.shape, F32)
            dsk_ref[...] = jnp.zeros(dsk_ref.shape, F32)

        kcat = jnp.concatenate([kp_ref[...], kc_ref[...]], axis=0)
        vcat = jnp.concatenate([vp_ref[...], vc_ref[...]], axis=0)
        vis = _swa_visible(tq, n - 1 - i)
        lane = lax.broadcasted_iota(jnp.int32, (1, LANES), 1)
        dkc = jnp.zeros((tq + SWA_WINDOW, LANES), F32)
        dvc = jnp.zeros((tq + SWA_WINDOW, LANES), F32)
        for pp in range(2):
            sl = slice(pp * LANES, (pp + 1) * LANES)
            q2, do2, o2 = q_ref[:, sl], do_ref[:, sl], o_ref[:, sl]
            dqs = []
            for hf in range(2):
                g = 2 * pp + hf
                lm = (lane < 64) if hf == 0 else (lane >= 64)
                qm = jnp.where(lm, q2, jnp.zeros_like(q2))
                dom = jnp.where(lm, do2, jnp.zeros_like(do2))
                p, psk = _swa_probs(qm, kcat, vis, sk_ref[g:g + 1, 0:1])
                delta = jnp.sum(dom.astype(F32) * o2.astype(F32), axis=1, keepdims=True)
                ds = p * (_nt(dom, vcat) - delta)
                dsk_ref[g:g + 1, :] += jnp.zeros((1, LANES), F32) - jnp.sum(psk * delta, axis=0, keepdims=True)
                dsb = (ds * 0.125).astype(BF16)
                dqs.append(_dot(dsb, kcat))
                dkc = dkc + _tn(dsb, qm)
                dvc = dvc + _tn(p.astype(BF16), dom)
            dq_ref[:, sl] = jnp.where(lane < 64, dqs[0], dqs[1]).astype(BF16)
        dkc = dkc + pltpu.roll(dkc, 64, 1)
        dvc = dvc + pltpu.roll(dvc, 64, 1)
        for full, ref, carry in ((dkc, dk_ref, ck_sc), (dvc, dv_ref, cv_sc)):
            if tq > SWA_WINDOW:
                ref[0:tq - SWA_WINDOW, :] = full[SWA_WINDOW:tq, :]
            ref[tq - SWA_WINDOW:tq, :] = full[tq:tq + SWA_WINDOW, :] + carry[...]
            carry[...] = full[0:SWA_WINDOW, :]

    wide = pl.BlockSpec((tq, 2 * LANES), lambda kv, i: (rev(i), kv))
    return pl.pallas_call(
        body, name=name, grid=(4, n),
        in_specs=[wide, cur(rev), prev(rev), cur(rev), prev(rev),
                  pl.BlockSpec((None, 8, LANES), lambda kv, i: (kv, 0, 0)), wide, wide],
        out_specs=[wide, cur(rev), cur(rev), pl.BlockSpec((None, 8, LANES), lambda kv, i: (kv, 0, 0))],
        out_shape=[jax.ShapeDtypeStruct((T, D_MODEL), BF16), jax.ShapeDtypeStruct((T, 4 * LANES), F32),
                   jax.ShapeDtypeStruct((T, 4 * LANES), F32), jax.ShapeDtypeStruct((4, 8, LANES), F32)],
        scratch_shapes=[pltpu.VMEM((SWA_WINDOW, LANES), F32), pltpu.VMEM((SWA_WINDOW, LANES), F32)],
        compiler_params=_cparams("arbitrary", "arbitrary"),
    )(qkv, kdup, kdup, vdup, vdup, sinks_b, o, do)


def fox_gate_fwd(fl, qkv, name):
    T = fl.shape[0]
    ts = min(SCAN_T, T)

    def body(fl_ref, q_ref, k_ref, qa_ref, ka_ref, carry):
        @pl.when(pl.program_id(0) == 0)
        def _():
            carry[...] = jnp.zeros(carry.shape, F32)

        xv = fl_ref[...]
        ls = jnp.minimum(xv, 0.0) - jnp.log(1.0 + jnp.exp(-jnp.abs(xv)))
        tri = (lax.broadcasted_iota(jnp.int32, (ts, ts), 0) >= lax.broadcasted_iota(jnp.int32, (ts, ts), 1)).astype(F32)
        cs = jnp.dot(tri, ls, precision=HI, preferred_element_type=F32) + carry[...]
        carry[...] = cs[ts - 1:ts, :]
        c1 = cs.astype(BF16).astype(F32)
        c2 = (cs - c1).astype(BF16).astype(F32)
        c3 = (cs - c1 - c2).astype(BF16).astype(F32)
        lane = lax.broadcasted_iota(jnp.int32, (1, LANES), 1)
        ones_q = jnp.where((lane >= 67) & (lane < 70), 1.0, 0.0)
        ones_k = jnp.where((lane >= 64) & (lane < 67), 1.0, 0.0)
        for b in range(8):
            qf = q_ref[:, b * LANES:(b + 1) * LANES].astype(F32) * 0.125
            kf = k_ref[:, b * LANES:(b + 1) * LANES].astype(F32)
            for hf in range(2):
                h = 2 * b + hf
                a1, a2, a3 = c1[:, h:h + 1], c2[:, h:h + 1], c3[:, h:h + 1]
                aux_q = jnp.where(lane == 64, a1, jnp.where(lane == 65, a2, jnp.where(lane == 66, a3, ones_q)))
                aux_k = jnp.where(lane == 67, -a1, jnp.where(lane == 68, -a2, jnp.where(lane == 69, -a3, ones_k)))
                qs = qf if hf == 0 else pltpu.roll(qf, 64, 1)
                ks = kf if hf == 0 else pltpu.roll(kf, 64, 1)
                qa_ref[:, h * LANES:(h + 1) * LANES] = jnp.where(lane < 64, qs, aux_q).astype(BF16)
                ka_ref[:, h * LANES:(h + 1) * LANES] = jnp.where(lane < 64, ks, aux_k).astype(BF16)

    out = pl.BlockSpec((ts, 16 * LANES), lambda i: (i, 0))
    return pl.pallas_call(
        body, name=name, grid=(T // ts,),
        in_specs=[pl.BlockSpec((ts, LANES), lambda i: (i, 0)), pl.BlockSpec((ts, D_MODEL), lambda i: (i, 0)),
                  pl.BlockSpec((ts, D_MODEL), lambda i: (i, 1))],
        out_specs=[out, out],
        out_shape=[jax.ShapeDtypeStruct((T, 16 * LANES), BF16), jax.ShapeDtypeStruct((T, 16 * LANES), BF16)],
        scratch_shapes=[pltpu.VMEM((1, LANES), F32)],
        compiler_params=_cparams("arbitrary"),
    )(fl, qkv, qkv)


def fox_gate_bwd(fl, dc, name):
    T = fl.shape[0]
    ts = min(SCAN_T, T)
    n = T // ts

    def body(fl_ref, dc_ref, o_ref, carry):
        @pl.when(pl.program_id(0) == 0)
        def _():
            carry[...] = jnp.zeros(carry.shape, F32)

        tri = (lax.broadcasted_iota(jnp.int32, (ts, ts), 0) <= lax.broadcasted_iota(jnp.int32, (ts, ts), 1)).astype(F32)
        rs = jnp.dot(tri, dc_ref[...], precision=HI, preferred_element_type=F32) + carry[...]
        carry[...] = rs[0:1, :]
        o_ref[...] = rs * (1.0 / (1.0 + jnp.exp(fl_ref[...])))

    blk = pl.BlockSpec((ts, LANES), lambda i: (n - 1 - i, 0))
    return pl.pallas_call(
        body, name=name, grid=(n,), in_specs=[blk, blk], out_specs=blk,
        out_shape=jax.ShapeDtypeStruct((T, LANES), F32), scratch_shapes=[pltpu.VMEM((1, LANES), F32)],
        compiler_params=_cparams("arbitrary"),
    )(fl, dc)


FOX_RB = 32


def _half(lane, hf):
    return (lane < 64) if hf == 0 else (lane >= 64)


def _pair(lane, a, b):
    return jnp.where(lane < 64, a, pltpu.roll(b, 64, 1)), jnp.where(lane < 64, pltpu.roll(a, 64, 1), b)


def fox_fwd(qa, ka, qkv, name):
    T = qa.shape[0]
    t = min(FOX_T, T)
    n = T // t

    def body(qa_ref, ka_ref, v_ref, o_ref, lse_ref, m_sc, l_sc, acc_sc, ls_sc, s_sc, p_sc):
        i, j = pl.program_id(1), pl.program_id(2)
        lane = lax.broadcasted_iota(jnp.int32, (1, LANES), 1)

        @pl.when(j == 0)
        def _():
            m_sc[...] = jnp.full(m_sc.shape, -1e30, F32)
            l_sc[...] = jnp.zeros(l_sc.shape, F32)
            acc_sc[...] = jnp.zeros(acc_sc.shape, F32)

        def tile(diag):
            v2 = v_ref[...]
            for hf in range(2):
                hs = slice(hf * LANES, (hf + 1) * LANES)
                sv = _nt(qa_ref[:, hs], ka_ref[:, hs])
                if diag:
                    vis = lax.broadcasted_iota(jnp.int32, (t, t), 0) >= lax.broadcasted_iota(jnp.int32, (t, t), 1)
                    sv = jnp.where(vis, sv, -1e30)
                s_sc[...] = sv
                m_old = m_sc[hf]
                m_new = jnp.maximum(m_old, jnp.max(s_sc[...], axis=1, keepdims=True))
                al = jnp.exp(m_old - m_new)
                m_sc[hf] = m_new
                for r0 in range(0, t, FOX_RB):
                    rs = slice(r0, r0 + FOX_RB)
                    mrow = m_new[rs, :]
                    part, pieces = None, []
                    for cb in range(0, t, LANES):
                        pc = jnp.exp(s_sc[rs, cb:cb + LANES] - mrow)
                        part = pc if part is None else part + pc
                        pieces.append(pc.astype(BF16))
                    p_sc[rs, :] = jnp.concatenate(pieces, axis=1)
                    ls_sc[rs, :] = part
                l_sc[hf] = al * l_sc[hf] + ls_sc[...]
                acc_sc[hf] = al * acc_sc[hf] + _dot(p_sc[...], v2)

        @pl.when(j < i)
        def _():
            tile(False)

        @pl.when(j == i)
        def _():
            tile(True)
            l0 = jnp.sum(l_sc[0], axis=1, keepdims=True)
            l1 = jnp.sum(l_sc[1], axis=1, keepdims=True)
            o_ref[...] = jnp.where(lane < 64, acc_sc[0] / l0, acc_sc[1] / l1).astype(BF16)
            lse_ref[...] = jnp.where(lane < 64, m_sc[0] + jnp.log(l0), m_sc[1] + jnp.log(l1))

    oblk = pl.BlockSpec((t, LANES), lambda p, i, j: (i, p))
    return pl.pallas_call(
        body, name=name, grid=(8, n, n),
        in_specs=[pl.BlockSpec((t, 2 * LANES), lambda p, i, j: (i, p)),
                  pl.BlockSpec((t, 2 * LANES), lambda p, i, j: (jnp.minimum(j, i), p)),
                  pl.BlockSpec((t, LANES), lambda p, i, j: (jnp.minimum(j, i), 16 + p))],
        out_specs=[oblk, oblk],
        out_shape=[jax.ShapeDtypeStruct((T, D_MODEL), BF16), jax.ShapeDtypeStruct((T, D_MODEL), F32)],
        scratch_shapes=[pltpu.VMEM((2, t, LANES), F32), pltpu.VMEM((2, t, LANES), F32), pltpu.VMEM((2, t, LANES), F32),
                        pltpu.VMEM((t, LANES), F32), pltpu.VMEM((t, t), F32), pltpu.VMEM((t, t), BF16)],
        compiler_params=_cparams("parallel", "parallel", "arbitrary"),
    )(qa, ka, qkv)


def fox_bwd_dq(qa, ka, qkv, o, lse, do, name):
    T = qa.shape[0]
    t = min(FOX_T, T)
    n = T // t

    def body(qa_ref, ka_ref, v_ref, o_ref, lse_ref, do_ref, dq_ref, dl_ref, aux_ref, acc_sc, dl_sc, s_sc, dp_sc, ds_sc):
        i, j = pl.program_id(1), pl.program_id(2)
        lane = lax.broadcasted_iota(jnp.int32, (1, LANES), 1)

        @pl.when(j == 0)
        def _():
            acc_sc[...] = jnp.zeros(acc_sc.shape, F32)
            d = do_ref[...].astype(F32) * o_ref[...].astype(F32)
            for hf in range(2):
                dl_sc[hf] = jnp.sum(jnp.where(_half(lane, hf), d, 0.0), axis=1, keepdims=True)
            dl_ref[...] = jnp.where(lane < 64, dl_sc[0], dl_sc[1])

        def tile(diag):
            v2, do2 = v_ref[...], do_ref[...]
            for hf in range(2):
                hs = slice(hf * LANES, (hf + 1) * LANES)
                kh = ka_ref[:, hs]
                s_sc[...] = _nt(qa_ref[:, hs], kh)
                dp_sc[...] = _nt(jnp.where(_half(lane, hf), do2, jnp.zeros_like(do2)), v2)
                for r0 in range(0, t, FOX_RB):
                    rs = slice(r0, r0 + FOX_RB)
                    sv = s_sc[rs, :]
                    if diag:
                        vis = (r0 + lax.broadcasted_iota(jnp.int32, (FOX_RB, t), 0)) >= lax.broadcasted_iota(jnp.int32, (FOX_RB, t), 1)
                        sv = jnp.where(vis, sv, -1e30)
                    p = jnp.exp(sv - lse_ref[rs, 64 * hf:64 * hf + 1])
                    ds_sc[rs, :] = (p * (dp_sc[rs, :] - dl_sc[hf, rs, :])).astype(BF16)
                acc_sc[hf] += _dot(ds_sc[...], kh)

        @pl.when(j < i)
        def _():
            tile(False)

        @pl.when(j == i)
        def _():
            tile(True)
            dq, aux = _pair(lane, acc_sc[0], acc_sc[1])
            dq_ref[...] = (dq * 0.125).astype(BF16)
            aux_ref[...] = aux

    oblk = pl.BlockSpec((t, LANES), lambda p, i, j: (i, p))
    return pl.pallas_call(
        body, name=name, grid=(8, n, n),
        in_specs=[pl.BlockSpec((t, 2 * LANES), lambda p, i, j: (i, p)),
                  pl.BlockSpec((t, 2 * LANES), lambda p, i, j: (jnp.minimum(j, i), p)),
                  pl.BlockSpec((t, LANES), lambda p, i, j: (jnp.minimum(j, i), 16 + p)), oblk, oblk, oblk],
        out_specs=[oblk, oblk, oblk],
        out_shape=[jax.ShapeDtypeStruct((T, D_MODEL), BF16), jax.ShapeDtypeStruct((T, D_MODEL), F32),
                   jax.ShapeDtypeStruct((T, D_MODEL), F32)],
        scratch_shapes=[pltpu.VMEM((2, t, LANES), F32), pltpu.VMEM((2, t, 1), F32), pltpu.VMEM((t, t), F32),
                        pltpu.VMEM((t, t), F32), pltpu.VMEM((t, t), BF16)],
        compiler_params=_cparams("parallel", "parallel", "arbitrary"),
    )(qa, ka, qkv, o, lse, do)


def fox_bwd_dkv(qa, ka, qkv, lse_row, delta_row, do, name):
    T = qa.shape[0]
    t = min(FOX_T, T)
    n = T // t

    def body(qa_ref, ka_ref, v_ref, lr_ref, dr_ref, do_ref, dk_ref, dv_ref, aux_ref, dk_sc, dv_sc, s_sc, dp_sc, p_sc, ds_sc):
        j, i = pl.program_id(1), pl.program_id(2)
        lane = lax.broadcasted_iota(jnp.int32, (1, LANES), 1)

        @pl.when(i == 0)
        def _():
            dk_sc[...] = jnp.zeros(dk_sc.shape, F32)
            dv_sc[...] = jnp.zeros(dv_sc.shape, F32)

        def tile(diag):
            v2, do2 = v_ref[...], do_ref[...]
            for hf in range(2):
                hs = slice(hf * LANES, (hf + 1) * LANES)
                lm = _half(lane, hf)
                qh = qa_ref[:, hs]
                s_sc[...] = _nt(ka_ref[:, hs], qh)
                dp_sc[...] = _nt(jnp.where(lm, v2, jnp.zeros_like(v2)), do2)
                lrow, drow = lr_ref[hf:hf + 1, :], dr_ref[hf:hf + 1, :]
                for r0 in range(0, t, FOX_RB):
                    rs = slice(r0, r0 + FOX_RB)
                    sv = s_sc[rs, :]
                    if diag:
                        vis = lax.broadcasted_iota(jnp.int32, (FOX_RB, t), 1) >= (r0 + lax.broadcasted_iota(jnp.int32, (FOX_RB, t), 0))
                        sv = jnp.where(vis, sv, -1e30)
                    p = jnp.exp(sv - lrow)
                    p_sc[rs, :] = p.astype(BF16)
                    ds_sc[rs, :] = (p * (dp_sc[rs, :] - drow)).astype(BF16)
                dv_sc[...] += _dot(p_sc[...], jnp.where(lm, do2, jnp.zeros_like(do2)))
                dk_sc[hf] += _dot(ds_sc[...], qh)

        @pl.when(i > j)
        def _():
            tile(False)

        @pl.when(i == j)
        def _():
            tile(True)

        @pl.when(i == n - 1)
        def _():
            dk, aux = _pair(lane, dk_sc[0], dk_sc[1])
            dk_ref[...] = dk.astype(BF16)
            aux_ref[...] = aux
            dv_ref[...] = dv_sc[...].astype(BF16)

    qblk = pl.BlockSpec((t, LANES), lambda p, j, i: (jnp.maximum(i, j), p))
    kblk = pl.BlockSpec((t, LANES), lambda p, j, i: (j, p))
    rblk = pl.BlockSpec((None, 2, t), lambda p, j, i: (p, 0, jnp.maximum(i, j)))
    return pl.pallas_call(
        body, name=name, grid=(8, n, n),
        in_specs=[pl.BlockSpec((t, 2 * LANES), lambda p, j, i: (jnp.maximum(i, j), p)),
                  pl.BlockSpec((t, 2 * LANES), lambda p, j, i: (j, p)),
                  pl.BlockSpec((t, LANES), lambda p, j, i: (j, 16 + p)), rblk, rblk, qblk],
        out_specs=[kblk, kblk, kblk],
        out_shape=[jax.ShapeDtypeStruct((T, D_MODEL), BF16), jax.ShapeDtypeStruct((T, D_MODEL), BF16),
                   jax.ShapeDtypeStruct((T, D_MODEL), F32)],
        scratch_shapes=[pltpu.VMEM((2, t, LANES), F32), pltpu.VMEM((t, LANES), F32), pltpu.VMEM((t, t), F32),
                        pltpu.VMEM((t, t), F32), pltpu.VMEM((t, t), BF16), pltpu.VMEM((t, t), BF16)],
        compiler_params=_cparams("parallel", "parallel", "arbitrary"),
    )(qa, ka, qkv, lse_row, delta_row, do)


C = HGRN_CHUNK
LEVELS = (64, 32, 16, 8, 4, 2)


def _pivot(b, B, row):
    if B == C:
        return jnp.broadcast_to(b[C // 2 - 1:C // 2, :], b.shape)
    if B >= 8:
        b3 = b.reshape(C // B, B, LANES)
        return jnp.broadcast_to(b3[:, B // 2 - 1:B // 2, :], b3.shape).reshape(C, LANES)
    if B == 4:
        y = jnp.where((row & 3) == 1, b, 0.0)
        return y + pltpu.roll(y, 1, 0) + pltpu.roll(y, 2, 0) + pltpu.roll(y, C - 1, 0)
    y = jnp.where((row & 1) == 0, b, 0.0)
    return y + pltpu.roll(y, 1, 0)


def _level_factors(bcum):
    row = lax.broadcasted_iota(jnp.int32, (C, 1), 0)
    out = []
    for B in LEVELS:
        upper = (row & (B - 1)) >= B // 2
        e = jnp.exp(-jnp.abs(bcum - _pivot(bcum, B, row)))
        out.append((B, jnp.where(upper, e, 0.0), jnp.where(upper, 0.0, e)))
    return out


def _same_block(B):
    sh = B.bit_length() - 1
    r = lax.broadcasted_iota(jnp.int32, (C, C), 0)
    c = lax.broadcasted_iota(jnp.int32, (C, C), 1)
    return (r >> sh) == (c >> sh)


def _hgrn_gates(q, fl, lb):
    sg = _sigmoid(fl)
    f = lb + (1.0 - lb) * sg
    sq = _sigmoid(q)
    return sg, f, jnp.log(f), 1.0 - f, sq, q * sq


def _cumsum_rows(x, reverse=False):
    r = lax.broadcasted_iota(jnp.int32, (C, C), 0)
    c = lax.broadcasted_iota(jnp.int32, (C, C), 1)
    tri = ((r <= c) if reverse else (r >= c)).astype(F32)
    return jnp.dot(tri, x, precision=HI, preferred_element_type=F32)


def _intra(qs, k, factors):
    r = lax.broadcasted_iota(jnp.int32, (C, C), 0)
    c = lax.broadcasted_iota(jnp.int32, (C, C), 1)
    a = jnp.where(r == c, jnp.sum(qs * k, axis=1, keepdims=True), 0.0)
    ops = []
    for B, eq, ek in factors:
        ql, kl = (qs * eq).astype(BF16), (k * ek).astype(BF16)
        al = _nt(ql, kl)
        a = a + (al if B == C else jnp.where(_same_block(B), al, 0.0))
        ops.append((ql, kl))
    return a, ops


def hgrn_fwd(proj, lb, gn, name):
    T = proj.shape[0]
    tg = min(HGRN_TG, T)
    nch = tg // C

    def body(q_ref, fl_ref, v_ref, g_ref, lb_ref, gn_ref, ao_ref, o_ref, st_ref, st_sc):
        @pl.when(pl.program_id(1) == 0)
        def _():
            st_sc[...] = jnp.zeros(st_sc.shape, F32)

        lb_v, gn_v = lb_ref[...], gn_ref[...]

        def chunk(ci, carry):
            rows = pl.ds(pl.multiple_of(ci * C, C), C)
            _, f, lf, k, _, qs = _hgrn_gates(q_ref[rows, :], fl_ref[rows, :], lb_v)
            vb = v_ref[rows, :].astype(BF16)
            gv = g_ref[rows, :]
            bcum = _cumsum_rows(lf)
            blast = bcum[C - 1:C, :]
            a, _ = _intra(qs, k, _level_factors(bcum))
            st = st_sc[...]
            st_ref[ci] = st
            o = _dot(a.astype(BF16), vb) + _nt((qs * jnp.exp(bcum)).astype(BF16), st.astype(BF16))
            st_sc[...] = st * jnp.exp(blast) + _tn(vb, (k * jnp.exp(blast - bcum)).astype(BF16))
            o_ref[rows, :] = o
            ao_ref[rows, :] = (o * _rms(o) * gn_v * (gv * _sigmoid(gv))).astype(BF16)
            return carry

        lax.fori_loop(0, nch, chunk, 0)

    col = lambda off: pl.BlockSpec((tg, LANES), lambda h, i: (i, off + h))
    one = pl.BlockSpec((1, LANES), lambda h, i: (0, h))
    return pl.pallas_call(
        body, name=name, grid=(8, T // tg),
        in_specs=[col(0), col(8), col(16), col(24), one, one],
        out_specs=[col(0), col(0), pl.BlockSpec((None, nch, LANES, LANES), lambda h, i: (h, i, 0, 0))],
        out_shape=[jax.ShapeDtypeStruct((T, D_MODEL), BF16), jax.ShapeDtypeStruct((T, D_MODEL), F32),
                   jax.ShapeDtypeStruct((8, T // C, LANES, LANES), F32)],
        scratch_shapes=[pltpu.VMEM((LANES, LANES), F32)],
        compiler_params=_cparams("parallel", "arbitrary"),
    )(proj, proj, proj, proj, lb, gn)


def hgrn_bwd(proj, lb, gn, o_raw, states, dao, name):
    T = proj.shape[0]
    tg = min(HGRN_TG, T)
    nch = tg // C
    n = T // tg

    def body(q_ref, fl_ref, v_ref, g_ref, lb_ref, gn_ref, o_ref, st_ref, dao_ref,
             dq_ref, dfl_ref, dv_ref, dg_ref, dlb_ref, dgn_ref, dst_sc):
        @pl.when(pl.program_id(1) == 0)
        def _():
            dst_sc[...] = jnp.zeros(dst_sc.shape, F32)
            dlb_ref[...] = jnp.zeros(dlb_ref.shape, F32)
            dgn_ref[...] = jnp.zeros(dgn_ref.shape, F32)

        lb_v, gn_v = lb_ref[...], gn_ref[...]
        r64 = lax.broadcasted_iota(jnp.int32, (C, C), 0)
        c64 = lax.broadcasted_iota(jnp.int32, (C, C), 1)
        row = lax.broadcasted_iota(jnp.int32, (C, 1), 0)

        def chunk(cr, carry):
            ci = nch - 1 - cr
            rows = pl.ds(pl.multiple_of(ci * C, C), C)
            q, fl, gv = q_ref[rows, :], fl_ref[rows, :], g_ref[rows, :]
            sg, f, lf, k, sq, qs = _hgrn_gates(q, fl, lb_v)
            vb = v_ref[rows, :].astype(BF16)
            o = o_ref[rows, :]
            ro = _rms(o)
            on = o * ro
            sgg = _sigmoid(gv)
            gate = gv * sgg
            dao_v = dao_ref[rows, :].astype(F32)
            dg_ref[rows, :] = (dao_v * on * gn_v * (sgg * (1.0 + gv * (1.0 - sgg)))).astype(BF16)
            dgn_ref[...] += jnp.sum(dao_v * on * gate, axis=0, keepdims=True)
            don = dao_v * gn_v * gate
            do = ro * (don - on * jnp.mean(don * on, axis=-1, keepdims=True))
            dob = do.astype(BF16)
            bcum = _cumsum_rows(lf)
            blast = bcum[C - 1:C, :]
            factors = _level_factors(bcum)
            a, ops = _intra(qs, k, factors)
            eb = jnp.exp(bcum)
            ekb = jnp.exp(blast - bcum)
            qb = qs * eb
            kb = k * ekb
            st = st_ref[ci]
            dst = dst_sc[...]
            dstb = dst.astype(BF16)
            da = jnp.where(r64 >= c64, _nt(dob, vb), 0.0)
            dv_ref[rows, :] = (_tn(a.astype(BF16), dob) + _nt(kb.astype(BF16), dstb)).astype(BF16)
            dqb = _dot(dob, st.astype(BF16))
            dkb = _dot(vb, dstb)
            eblast = jnp.exp(blast)
            dst_sc[...] = dst * eblast + _tn(dob, qb.astype(BF16))
            dblast = eblast * jnp.sum(dst * st, axis=0, keepdims=True) + jnp.sum(dkb * kb, axis=0, keepdims=True)
            dad = jnp.sum(jnp.where(r64 == c64, da, 0.0), axis=1, keepdims=True)
            dqs = dqb * eb + dad * k
            dk = dkb * ekb + dad * qs
            dbcum = dqb * qb - dkb * kb + jnp.where(row == C - 1, dblast, 0.0)
            for (B, eq, ek), (ql, kl) in zip(factors, ops):
                dal = (da if B == C else jnp.where(_same_block(B), da, 0.0)).astype(BF16)
                dql, dkl = _dot(dal, kl), _tn(dal, ql)
                dqs = dqs + dql * eq
                dk = dk + dkl * ek
                dbcum = dbcum + (dql * ql.astype(F32) - dkl * kl.astype(F32))
            df = _cumsum_rows(dbcum, reverse=True) / f - dk
            dfl_ref[rows, :] = (df * (1.0 - lb_v) * sg * (1.0 - sg)).astype(BF16)
            dlb_ref[...] += jnp.sum(df * (1.0 - sg), axis=0, keepdims=True)
            dq_ref[rows, :] = (dqs * (sq * (1.0 + q * (1.0 - sq)))).astype(BF16)
            return carry

        lax.fori_loop(0, nch, chunk, 0)

    col = lambda off: pl.BlockSpec((tg, LANES), lambda h, i: (n - 1 - i, off + h))
    one = pl.BlockSpec((1, LANES), lambda h, i: (0, h))
    big = jax.ShapeDtypeStruct((T, D_MODEL), BF16)
    small = jax.ShapeDtypeStruct((1, D_MODEL), F32)
    return pl.pallas_call(
        body, name=name, grid=(8, n),
        in_specs=[col(0), col(8), col(16), col(24), one, one, col(0),
                  pl.BlockSpec((None, nch, LANES, LANES), lambda h, i: (h, n - 1 - i, 0, 0)), col(0)],
        out_specs=[col(0), col(0), col(0), col(0), one, one],
        out_shape=[big, big, big, big, small, small],
        scratch_shapes=[pltpu.VMEM((LANES, LANES), F32)],
        compiler_params=_cparams("arbitrary", "arbitrary"),
    )(proj, proj, proj, proj, lb, gn, o_raw, states, dao)


def lower_bound_fwd(logits, name):
    def body(l_ref, s_ref):
        lv = l_ref[...]
        e = jnp.exp(lv - jnp.max(lv, axis=0, keepdims=True))
        s_ref[...] = e / jnp.sum(e, axis=0, keepdims=True)

    return pl.pallas_call(body, name=name, out_shape=jax.ShapeDtypeStruct(logits.shape, F32))(logits)


def lower_bound_bwd(sm, dlb, name):
    def body(s_ref, d_ref, o_ref):
        s = s_ref[...]
        row = lax.broadcasted_iota(jnp.int32, s.shape, 0)
        o_ref[...] = d_ref[...] * s[1:2, :] * (jnp.where(row == 1, 1.0, 0.0) - s)

    return pl.pallas_call(body, name=name, out_shape=jax.ShapeDtypeStruct(sm.shape, F32))(sm, dlb)


def _pad_rows(flat, mult):
    rows = -(-flat.shape[-1] // D_MODEL)
    rows = -(-rows // mult) * mult
    pad = rows * D_MODEL - flat.shape[-1]
    flat = jnp.pad(flat, [(0, 0)] * (flat.ndim - 1) + [(0, pad)])
    return flat.reshape(flat.shape[:-1] + (rows, D_MODEL))


def _gather_weights(w):
    pieces = []
    for nme in SHARDED:
        a = w[nme]
        if nme in BIASES:
            pieces.append(lax.bitcast_convert_type(a, BF16).reshape(-1))
        else:
            pieces.append(a.astype(BF16).reshape(-1))
    flat = _pad_rows(jnp.concatenate(pieces), 16)
    got = all_gather_rows(flat).reshape(N_DEV, -1)
    full, off = {}, 0
    for nme in SHARDED:
        shp = w[nme].shape
        cnt = 1
        for s in shp:
            cnt *= s
        if nme in BIASES:
            seg = got[:, off:off + 2 * cnt].reshape((N_DEV,) + shp + (2,))
            seg = lax.bitcast_convert_type(seg, F32)
            off += 2 * cnt
        else:
            seg = got[:, off:off + cnt].reshape((N_DEV,) + shp)
            off += cnt
        full[nme] = jnp.concatenate([seg[d] for d in range(N_DEV)], axis=SHARD_AXIS[nme])
    return full


def _pieces(gfull, axis):
    shp = gfull.shape
    a = gfull.reshape(shp[:axis] + (N_DEV, shp[axis] // N_DEV) + shp[axis + 1:])
    return jnp.moveaxis(a, axis, 0).reshape(N_DEV, -1)


def _flat_local(vals):
    return (_pad_rows(jnp.concatenate([vals[n].reshape(-1) for n in SHARDED]), 32),
            _pad_rows(jnp.concatenate([vals[n].reshape(-1) for n in REPLICATED] + [jnp.zeros((1,), F32)]), 16))


def kernel(x, norm_mix, norm_mlp, norm_final, w_up, w_down, swa_w_qkv, swa_b_qkv, swa_sinks, swa_w_o, hgrn_w_in, hgrn_lb_logits, hgrn_g_norm, hgrn_w_o, fox_w_in, fox_b_in, fox_w_o, loss_target, m_norm_mix, m_norm_mlp, m_norm_final, m_w_up, m_w_down, m_swa_w_qkv, m_swa_b_qkv, m_swa_sinks, m_swa_w_o, m_hgrn_w_in, m_hgrn_lb_logits, m_hgrn_g_norm, m_hgrn_w_o, m_fox_w_in, m_fox_b_in, m_fox_w_o, v_norm_mix, v_norm_mlp, v_norm_final, v_w_up, v_w_down, v_swa_w_qkv, v_swa_b_qkv, v_swa_sinks, v_swa_w_o, v_hgrn_w_in, v_hgrn_lb_logits, v_hgrn_g_norm, v_hgrn_w_o, v_fox_w_in, v_fox_b_in, v_fox_w_o):
    w = dict(norm_mix=norm_mix, norm_mlp=norm_mlp, norm_final=norm_final, w_up=w_up, w_down=w_down,
             swa_w_qkv=swa_w_qkv, swa_b_qkv=swa_b_qkv, swa_sinks=swa_sinks, swa_w_o=swa_w_o, hgrn_w_in=hgrn_w_in,
             hgrn_lb_logits=hgrn_lb_logits, hgrn_g_norm=hgrn_g_norm, hgrn_w_o=hgrn_w_o, fox_w_in=fox_w_in,
             fox_b_in=fox_b_in, fox_w_o=fox_w_o)
    mom = dict(norm_mix=m_norm_mix, norm_mlp=m_norm_mlp, norm_final=m_norm_final, w_up=m_w_up, w_down=m_w_down,
               swa_w_qkv=m_swa_w_qkv, swa_b_qkv=m_swa_b_qkv, swa_sinks=m_swa_sinks, swa_w_o=m_swa_w_o,
               hgrn_w_in=m_hgrn_w_in, hgrn_lb_logits=m_hgrn_lb_logits, hgrn_g_norm=m_hgrn_g_norm, hgrn_w_o=m_hgrn_w_o,
               fox_w_in=m_fox_w_in, fox_b_in=m_fox_b_in, fox_w_o=m_fox_w_o)
    var = dict(norm_mix=v_norm_mix, norm_mlp=v_norm_mlp, norm_final=v_norm_final, w_up=v_w_up, w_down=v_w_down,
               swa_w_qkv=v_swa_w_qkv, swa_b_qkv=v_swa_b_qkv, swa_sinks=v_swa_sinks, swa_w_o=v_swa_w_o,
               hgrn_w_in=v_hgrn_w_in, hgrn_lb_logits=v_hgrn_lb_logits, hgrn_g_norm=v_hgrn_g_norm, hgrn_w_o=v_hgrn_w_o,
               fox_w_in=v_fox_w_in, fox_b_in=v_fox_b_in, fox_w_o=v_fox_w_o)
    T = x.shape[1]
    x0 = x[0]
    tgt = loss_target[0]
    W = _gather_weights(w)
    zeros_b = jnp.zeros((1, 4 * D_MODEL), F32)

    def swa_layer(xin, i, j):
        qkv = norm_matmul(xin, norm_mix[i:i + 1], W['swa_w_qkv'][j], W['swa_b_qkv'][j:j + 1], BF16, f"swa_qkv_L{i}")
        dup = lambda a: jnp.broadcast_to(a.reshape(T, 4, 1, 64), (T, 4, 2, 64)).reshape(T, 4 * LANES)
        kdup, vdup = dup(qkv[:, 1024:1280]), dup(qkv[:, 1280:1536])
        sk = jnp.broadcast_to(jnp.pad(swa_sinks[j].reshape(4, 4), ((0, 0), (0, 4)))[:, :, None], (4, 8, LANES))
        ao = swa_fwd(qkv, kdup, vdup, sk, f"swa_fwd_L{i}")
        xmid = matmul(ao, W['swa_w_o'][j], F32, f"swa_out_L{i}", res=xin)
        return xmid, (qkv, kdup, vdup, sk, ao)

    def swa_layer_bwd(xin, saved, dmid, i, j, grads):
        qkv, kdup, vdup, sk, ao = saved
        dao = matmul(dmid, W['swa_w_o'][j].T, BF16, f"swa_dout_L{i}")
        grads['swa_w_o'][j] = tn_matmul(ao, dmid, f"swa_dwo_L{i}")
        dq, dk, dv, dsk = swa_bwd(qkv, kdup, vdup, sk, ao, dao, f"swa_bwd_L{i}")
        wt = W['swa_w_qkv'][j].T
        spread = lambda a: jnp.pad(a.reshape(4, 64, D_MODEL), ((0, 0), (0, 64), (0, 0))).reshape(4 * LANES, D_MODEL)
        gather = lambda a: a.reshape(a.shape[0], 4, LANES)[:, :, :64].reshape(a.shape[0], 256)
        dx, h, dg = proj_bwd(xin, norm_mix[i:i + 1], dmid,
                             [(dq, wt[:1024]), (dk, spread(wt[1024:1280])), (dv, spread(wt[1280:]))], f"swa_din_L{i}")
        gq, bq = tn_matmul(h, dq, f"swa_dwq_L{i}", colsum=True)
        gk, bk = tn_matmul(h, dk, f"swa_dwk_L{i}", colsum=True)
        gv, bv = tn_matmul(h, dv, f"swa_dwv_L{i}", colsum=True)
        grads['swa_w_qkv'][j] = jnp.concatenate([gq, gather(gk), gather(gv)], axis=1)
        grads['swa_b_qkv'][j] = jnp.concatenate([bq, gather(bk), gather(bv)], axis=1)[0]
        grads['swa_sinks'][j] = dsk[:, :4, 0].reshape(16)
        grads['norm_mix'][i] = dg[0]
        return dx

    lb_soft = lower_bound_fwd(hgrn_lb_logits, "hgrn_lb_fwd")
    lb = lb_soft[1:2]

    def hgrn_layer(xin, i, j):
        proj = norm_matmul(xin, norm_mix[i:i + 1], W['hgrn_w_in'][j], zeros_b, F32, f"hgrn_in_L{i}")
        ao, o_raw, states = hgrn_fwd(proj, lb, hgrn_g_norm[j:j + 1], f"hgrn_fwd_L{i}")
        xmid = matmul(ao, W['hgrn_w_o'][j], F32, f"hgrn_out_L{i}", res=xin)
        return xmid, (proj, ao, o_raw, states)

    def hgrn_layer_bwd(xin, saved, dmid, i, j, grads):
        proj, ao, o_raw, states = saved
        dao = matmul(dmid, W['hgrn_w_o'][j].T, BF16, f"hgrn_dout_L{i}")
        grads['hgrn_w_o'][j] = tn_matmul(ao, dmid, f"hgrn_dwo_L{i}")
        dq, dfl, dv, dgt, dlb, dgn = hgrn_bwd(proj, lb, hgrn_g_norm[j:j + 1], o_raw, states, dao, f"hgrn_bwd_L{i}")
        wt = W['hgrn_w_in'][j].T
        parts = [dq, dfl, dv, dgt]
        dx, h, dg = proj_bwd(xin, norm_mix[i:i + 1], dmid,
                             [(d, wt[n * D_MODEL:(n + 1) * D_MODEL]) for n, d in enumerate(parts)], f"hgrn_din_L{i}")
        grads['hgrn_w_in'][j] = jnp.concatenate(
            [tn_matmul(h, d, f"hgrn_dwin{n}_L{i}") for n, d in enumerate(parts)], axis=1)
        grads['hgrn_g_norm'][j] = dgn[0]
        grads['hgrn_lb_logits'] = lower_bound_bwd(lb_soft, dlb, "hgrn_lb_bwd")
        grads['norm_mix'][i] = dg[0]
        return dx

    def fox_layer(xin, i, j):
        w_in = W['fox_w_in'][j]
        b_in = W['fox_b_in'][j:j + 1]
        qkv = norm_matmul(xin, norm_mix[i:i + 1], w_in[:, :3072], b_in[:, :3072], BF16, f"fox_qkv_L{i}")
        wf = jnp.pad(w_in[:, 3072:], ((0, 0), (0, LANES - 16)))
        bf = jnp.pad(b_in[:, 3072:], ((0, 0), (0, LANES - 16)))
        fl = norm_matmul(xin, norm_mix[i:i + 1], wf, bf, F32, f"fox_f_L{i}")
        qa, ka = fox_gate_fwd(fl, qkv, f"fox_gate_L{i}")
        ao, lse = fox_fwd(qa, ka, qkv, f"fox_fwd_L{i}")
        xmid = matmul(ao, W['fox_w_o'][j], F32, f"fox_out_L{i}", res=xin)
        return xmid, (qkv, fl, qa, ka, ao, lse, wf)

    def fox_layer_bwd(xin, saved, dmid, i, j, grads):
        qkv, fl, qa, ka, ao, lse, wf = saved
        dao = matmul(dmid, W['fox_w_o'][j].T, BF16, f"fox_dout_L{i}")
        grads['fox_w_o'][j] = tn_matmul(ao, dmid, f"fox_dwo_L{i}")
        dq, delta, aux_q = fox_bwd_dq(qa, ka, qkv, ao, lse, dao, f"fox_dq_L{i}")
        as_rows = lambda a: a[:, ::64].T.reshape(8, 2, T)
        dk, dv, aux_k = fox_bwd_dkv(qa, ka, qkv, as_rows(lse), as_rows(delta), dao, f"fox_dkv_L{i}")
        dcp = jnp.pad(aux_q[:, ::64] - aux_k[:, 3::64], ((0, 0), (0, LANES - 16)))
        dfl = fox_gate_bwd(fl, dcp, f"fox_dgate_L{i}")
        wt = W['fox_w_in'][j][:, :3072].T
        parts = [dq, dk, dv]
        dx, h, dg = proj_bwd(xin, norm_mix[i:i + 1], dmid,
                             [(d, wt[n * D_MODEL:(n + 1) * D_MODEL]) for n, d in enumerate(parts)] + [(dfl, wf.T)],
                             f"fox_din_L{i}")
        gw = [tn_matmul(h, d, f"fox_dw{n}_L{i}", colsum=True) for n, d in enumerate(parts + [dfl])]
        grads['fox_w_in'][j] = jnp.concatenate([g for g, _ in gw[:3]] + [gw[3][0][:, :16]], axis=1)
        grads['fox_b_in'][j] = jnp.concatenate([b for _, b in gw[:3]] + [gw[3][1][:, :16]], axis=1)[0]
        grads['norm_mix'][i] = dg[0]
        return dx

    mixers = [(swa_layer, swa_layer_bwd), (hgrn_layer, hgrn_layer_bwd), (fox_layer, fox_layer_bwd)]

    xs, mids, saves = [x0], [], []
    for i in range(DEPTH):
        xmid, saved = mixers[i % 3][0](xs[-1], i, i // 3)
        mids.append(xmid)
        saves.append(saved)
        xs.append(mlp_fwd(xmid, norm_mlp[i:i + 1], W['w_up'][i], W['w_down'][i], f"mlp_fwd_L{i}"))

    grads = {n: [None] * w[n].shape[0] for n in WEIGHTS if n not in ('norm_final', 'hgrn_lb_logits')}
    loss_part, dx, dgf = final_loss(xs[-1], norm_final.reshape(1, D_MODEL), tgt, "final_loss")
    grads['norm_final'] = dgf[0]
    for i in reversed(range(DEPTH)):
        dmid, h, a, du, dg = mlp_bwd(mids[i], norm_mlp[i:i + 1], W['w_up'][i], W['w_up'][i].T, W['w_down'][i].T, dx,
                                     f"mlp_bwd_L{i}")
        grads['w_up'][i] = tn_matmul(h, du, f"mlp_dwup_L{i}")
        grads['w_down'][i] = tn_matmul(a, dx, f"mlp_dwdown_L{i}")
        grads['norm_mlp'][i] = dg[0]
        dx = mixers[i % 3][1](xs[i], saves[i], dmid, i, i // 3, grads)
    gfull = {n: (g if not isinstance(g, list) else jnp.stack(g)) for n, g in grads.items()}

    small = jnp.concatenate([gfull[n].reshape(-1) for n in REPLICATED] + [loss_part[0, 0:1]])
    big = jnp.concatenate([_pieces(gfull[n], SHARD_AXIS[n]) for n in SHARDED], axis=1).astype(BF16)
    recv_big, recv_small = all_to_all_rows(
        [_pad_rows(big, 32), _pad_rows(jnp.broadcast_to(small[None], (N_DEV, small.shape[0])), 16)])
    (w_big, w_small), (m_big, m_small), (v_big, v_small) = _flat_local(w), _flat_local(mom), _flat_local(var)
    outs_big = reduce_adamw(recv_big, w_big, m_big, v_big, "reduce_adamw")
    outs_small = reduce_adamw(recv_small, w_small, m_small, v_small, "reduce_adamw_replicated")
    res = [{}, {}, {}, {}]
    for names, outs in ((SHARDED, outs_big), (REPLICATED, outs_small)):
        off = 0
        for nme in names:
            cnt = w[nme].size
            for o, r in zip(outs, res):
                r[nme] = o.reshape(-1)[off:off + cnt].reshape(w[nme].shape)
            off += cnt
    loss = outs_small[0].reshape(-1)[off]
    return (loss, dx[None], *[res[0][n] for n in WEIGHTS], *[res[1][n] for n in WEIGHTS],
            *[res[2][n] for n in WEIGHTS], *[res[3][n] for n in WEIGHTS])
```

```python
import functools

import jax
import jax.numpy as jnp
from jax import lax
from jax.experimental import pallas as pl
from jax.experimental.pallas import tpu as pltpu

F32 = jnp.float32
BF16 = jnp.bfloat16
HI = lax.Precision.HIGHEST

N_DEV = 8
D_MODEL = 1024
DEPTH = 4
EPS = 1e-6
SWA_WINDOW = 128
HGRN_CHUNK = 64
LANES = 128
VMEM_LIMIT = 56 << 20

ADAM_LR, ADAM_B1, ADAM_B2, ADAM_EPS, ADAM_WD, ADAM_STEP = 0.001, 0.9, 0.999, 1e-08, 0.01, 10

TM = 512
TF = 512
TK = 512
FOX_T = 1024
SWA_TQ = 512
HGRN_TG = 512
SCAN_T = 256

WEIGHTS = ['norm_mix', 'norm_mlp', 'norm_final', 'w_up', 'w_down', 'swa_w_qkv', 'swa_b_qkv', 'swa_sinks', 'swa_w_o',
           'hgrn_w_in', 'hgrn_lb_logits', 'hgrn_g_norm', 'hgrn_w_o', 'fox_w_in', 'fox_b_in', 'fox_w_o']
SHARD_AXIS = {'norm_mix': None, 'norm_mlp': None, 'norm_final': None, 'w_up': 2, 'w_down': 1, 'swa_w_qkv': 2,
              'swa_b_qkv': 1, 'swa_sinks': None, 'swa_w_o': 1, 'hgrn_w_in': 2, 'hgrn_lb_logits': None,
              'hgrn_g_norm': None, 'hgrn_w_o': 1, 'fox_w_in': 2, 'fox_b_in': 1, 'fox_w_o': 1}
SHARDED = [n for n in WEIGHTS if SHARD_AXIS[n] is not None]
REPLICATED = [n for n in WEIGHTS if SHARD_AXIS[n] is None]
BIASES = ('swa_b_qkv', 'fox_b_in')


def _cparams(*sem):
    return pltpu.CompilerParams(dimension_semantics=sem, vmem_limit_bytes=VMEM_LIMIT)


def _nt(a, b):
    return lax.dot_general(a, b, (((1,), (1,)), ((), ())), preferred_element_type=F32)


def _tn(a, b):
    return lax.dot_general(a, b, (((0,), (0,)), ((), ())), preferred_element_type=F32)


def _dot(a, b):
    return jnp.dot(a, b, preferred_element_type=F32)


def _sigmoid(x):
    return 1.0 / (1.0 + jnp.exp(-x))


def _rms(xv):
    return lax.rsqrt(jnp.mean(xv * xv, axis=-1, keepdims=True) + EPS)


def _rms_bwd(xv, g, dh):
    r = _rms(xv)
    xhat = xv * r
    dhg = dh * g
    dx = r * (dhg - xhat * jnp.mean(dhg * xhat, axis=-1, keepdims=True))
    return dx, jnp.sum(dh * xhat, axis=0, keepdims=True)


def _my_id():
    return lax.axis_index("x"), lax.axis_index("y"), lax.axis_index("c")


def _peer(x, y, c, k):
    return (lax.rem(x + ((k >> 2) & 1), 2), lax.rem(y + ((k >> 1) & 1), 2), lax.rem(c + (k & 1), 2))


def all_gather_rows(local):
    def body(x_ref, o_ref, send_sems, recv_sems, loc_sem):
        x, y, c = _my_id()
        me = 4 * x + 2 * y + c
        mine = pltpu.make_async_copy(x_ref, o_ref.at[me], loc_sem)
        mine.start()
        copies = []
        for k in range(1, N_DEV):
            px, py, pc = _peer(x, y, c, k)
            cp = pltpu.make_async_remote_copy(
                src_ref=x_ref, dst_ref=o_ref.at[me], send_sem=send_sems.at[k - 1], recv_sem=recv_sems.at[k - 1],
                device_id=(px, py, pc), device_id_type=pl.DeviceIdType.MESH)
            cp.start()
            copies.append(cp)
        for cp in copies:
            cp.wait()
        mine.wait()

    return pl.pallas_call(
        body, name="all_gather_weights",
        out_shape=jax.ShapeDtypeStruct((N_DEV,) + local.shape, local.dtype),
        in_specs=[pl.BlockSpec(memory_space=pl.ANY)],
        out_specs=pl.BlockSpec(memory_space=pl.ANY),
        scratch_shapes=[pltpu.SemaphoreType.DMA((N_DEV - 1,)), pltpu.SemaphoreType.DMA((N_DEV - 1,)),
                        pltpu.SemaphoreType.DMA],
    )(local)


def all_to_all_rows(sends):
    n = len(sends)

    def body(*refs):
        s_refs, r_refs = refs[:n], refs[n:2 * n]
        send_sems, recv_sems, loc_sems = refs[2 * n:]
        x, y, c = _my_id()
        me = 4 * x + 2 * y + c
        copies = []
        for a, (s_ref, r_ref) in enumerate(zip(s_refs, r_refs)):
            mine = pltpu.make_async_copy(s_ref.at[me], r_ref.at[me], loc_sems.at[a])
            mine.start()
            copies.append(mine)
            for k in range(1, N_DEV):
                px, py, pc = _peer(x, y, c, k)
                sem = a * (N_DEV - 1) + k - 1
                cp = pltpu.make_async_remote_copy(
                    src_ref=s_ref.at[4 * px + 2 * py + pc], dst_ref=r_ref.at[me],
                    send_sem=send_sems.at[sem], recv_sem=recv_sems.at[sem],
                    device_id=(px, py, pc), device_id_type=pl.DeviceIdType.MESH)
                cp.start()
                copies.append(cp)
        for cp in copies:
            cp.wait()

    hbm = pl.BlockSpec(memory_space=pl.ANY)
    return pl.pallas_call(
        body, name="all_to_all_grads",
        out_shape=[jax.ShapeDtypeStruct(s.shape, s.dtype) for s in sends],
        in_specs=[hbm] * n, out_specs=[hbm] * n,
        scratch_shapes=[pltpu.SemaphoreType.DMA((n * (N_DEV - 1),)), pltpu.SemaphoreType.DMA((n * (N_DEV - 1),)),
                        pltpu.SemaphoreType.DMA((n,))],
    )(*sends)


def reduce_adamw(recv, w, m, v, name):
    R = w.shape[0]
    tr = max(t for t in range(16, 257, 16) if R % t == 0)
    c1 = 1.0 / (1.0 - ADAM_B1 ** ADAM_STEP)
    c2 = 1.0 / (1.0 - ADAM_B2 ** ADAM_STEP)

    def body(r_ref, w_ref, m_ref, v_ref, g_ref, d_ref, nm_ref, nv_ref):
        g = r_ref[0].astype(F32)
        for s in range(1, N_DEV):
            g = g + r_ref[s].astype(F32)
        m2 = ADAM_B1 * m_ref[...] + (1.0 - ADAM_B1) * g
        v2 = ADAM_B2 * v_ref[...] + (1.0 - ADAM_B2) * (g * g)
        g_ref[...] = g
        nm_ref[...] = m2
        nv_ref[...] = v2
        d_ref[...] = -ADAM_LR * ((m2 * c1) / (jnp.sqrt(v2 * c2) + ADAM_EPS) + ADAM_WD * w_ref[...])

    row = pl.BlockSpec((tr, D_MODEL), lambda i: (i, 0))
    shp = jax.ShapeDtypeStruct((R, D_MODEL), F32)
    return pl.pallas_call(
        body, name=name, grid=(R // tr,),
        in_specs=[pl.BlockSpec((N_DEV, tr, D_MODEL), lambda i: (0, i, 0)), row, row, row],
        out_specs=[row, row, row, row], out_shape=[shp, shp, shp, shp],
        compiler_params=_cparams("parallel"),
    )(recv, w, m, v)


def norm_matmul(x, g, w, b, out_dtype, name):
    T, N = x.shape[0], w.shape[1]
    tm, tn = min(TM, T), min(512, N)

    def body(x_ref, g_ref, w_ref, b_ref, o_ref, h_sc):
        @pl.when(pl.program_id(1) == 0)
        def _():
            xv = x_ref[...]
            h_sc[...] = (xv * _rms(xv) * g_ref[...]).astype(BF16)
        o_ref[...] = (_dot(h_sc[...], w_ref[...]) + b_ref[...]).astype(o_ref.dtype)

    return pl.pallas_call(
        body, name=name, grid=(T // tm, N // tn),
        in_specs=[pl.BlockSpec((tm, D_MODEL), lambda i, j: (i, 0)), pl.BlockSpec((1, D_MODEL), lambda i, j: (0, 0)),
                  pl.BlockSpec((D_MODEL, tn), lambda i, j: (0, j)), pl.BlockSpec((1, tn), lambda i, j: (0, j))],
        out_specs=pl.BlockSpec((tm, tn), lambda i, j: (i, j)),
        out_shape=jax.ShapeDtypeStruct((T, N), out_dtype),
        scratch_shapes=[pltpu.VMEM((tm, D_MODEL), BF16)],
        compiler_params=_cparams("parallel", "arbitrary"),
    )(x, g, w, b)


def matmul(a, w, out_dtype, name, res=None):
    T, K = a.shape
    N = w.shape[1]
    tm = min(TM, T)

    def body(*refs):
        if res is None:
            a_ref, w_ref, o_ref = refs
            acc = _dot(a_ref[...].astype(BF16), w_ref[...])
        else:
            a_ref, w_ref, r_ref, o_ref = refs
            acc = r_ref[...] + _dot(a_ref[...].astype(BF16), w_ref[...])
        o_ref[...] = acc.astype(o_ref.dtype)

    in_specs = [pl.BlockSpec((tm, K), lambda i: (i, 0)), pl.BlockSpec((K, N), lambda i: (0, 0))]
    ops = [a, w]
    if res is not None:
        in_specs.append(pl.BlockSpec((tm, N), lambda i: (i, 0)))
        ops.append(res)
    return pl.pallas_call(
        body, name=name, grid=(T // tm,), in_specs=in_specs,
        out_specs=pl.BlockSpec((tm, N), lambda i: (i, 0)),
        out_shape=jax.ShapeDtypeStruct((T, N), out_dtype),
        compiler_params=_cparams("parallel"),
    )(*ops)


def tn_matmul(a, b, name, colsum=False):
    T, M = a.shape
    N = b.shape[1]
    tk = min(TK, T)
    tmm = min(1024, M)
    tn = N if N <= 1024 else (1024 if N % 1024 == 0 else N)

    def body(a_ref, b_ref, o_ref, *rest):
        k = pl.program_id(2)
        bv = b_ref[...]

        @pl.when(k == 0)
        def _():
            o_ref[...] = jnp.zeros(o_ref.shape, F32)
            if colsum:
                rest[0][...] = jnp.zeros(rest[0].shape, F32)

        o_ref[...] += _tn(a_ref[...].astype(BF16), bv.astype(BF16))
        if colsum:
            rest[0][...] += jnp.sum(bv.astype(F32), axis=0, keepdims=True)

    out_specs = [pl.BlockSpec((tmm, tn), lambda i, j, k: (i, j))]
    out_shape = [jax.ShapeDtypeStruct((M, N), F32)]
    if colsum:
        assert M == tmm
        out_specs.append(pl.BlockSpec((1, tn), lambda i, j, k: (0, j)))
        out_shape.append(jax.ShapeDtypeStruct((1, N), F32))
    out = pl.pallas_call(
        body, name=name, grid=(M // tmm, N // tn, T // tk),
        in_specs=[pl.BlockSpec((tk, tmm), lambda i, j, k: (k, i)), pl.BlockSpec((tk, tn), lambda i, j, k: (k, j))],
        out_specs=out_specs, out_shape=out_shape,
        compiler_params=_cparams("parallel", "parallel", "arbitrary"),
    )(a, b)
    return out if colsum else out[0]


def mlp_fwd(x, g, w_up, w_down, name):
    T, F = x.shape[0], w_up.shape[1]
    tm, tf = min(TM, T), min(TF, F)
    nf = F // tf

    def body(x_ref, g_ref, wu_ref, wd_ref, o_ref, h_sc, acc_sc):
        f = pl.program_id(1)

        @pl.when(f == 0)
        def _():
            xv = x_ref[...]
            h_sc[...] = (xv * _rms(xv) * g_ref[...]).astype(BF16)
            acc_sc[...] = xv

        u = jnp.maximum(_dot(h_sc[...], wu_ref[...]), 0.0)
        acc_sc[...] += _dot((u * u).astype(BF16), wd_ref[...])

        @pl.when(f == nf - 1)
        def _():
            o_ref[...] = acc_sc[...]

    return pl.pallas_call(
        body, name=name, grid=(T // tm, nf),
        in_specs=[pl.BlockSpec((tm, D_MODEL), lambda i, f: (i, 0)), pl.BlockSpec((1, D_MODEL), lambda i, f: (0, 0)),
                  pl.BlockSpec((D_MODEL, tf), lambda i, f: (0, f)), pl.BlockSpec((tf, D_MODEL), lambda i, f: (f, 0))],
        out_specs=pl.BlockSpec((tm, D_MODEL), lambda i, f: (i, 0)),
        out_shape=jax.ShapeDtypeStruct((T, D_MODEL), F32),
        scratch_shapes=[pltpu.VMEM((tm, D_MODEL), BF16), pltpu.VMEM((tm, D_MODEL), F32)],
        compiler_params=_cparams("parallel", "arbitrary"),
    )(x, g, w_up, w_down)


def mlp_bwd(x, g, w_up, w_up_t, w_down_t, dy, name):
    T, F = x.shape[0], w_up.shape[1]
    tm, tf = min(TM, T), min(TF, F)
    nf = F // tf

    def body(x_ref, g_ref, wu_ref, wut_ref, wdt_ref, dy_ref, dx_ref, h_ref, a_ref, du_ref, dg_ref, h_sc, dyb_sc, dh_sc):
        i, f = pl.program_id(0), pl.program_id(1)

        @pl.when(f == 0)
        def _():
            xv = x_ref[...]
            h = (xv * _rms(xv) * g_ref[...]).astype(BF16)
            h_sc[...] = h
            h_ref[...] = h
            dyb_sc[...] = dy_ref[...].astype(BF16)
            dh_sc[...] = jnp.zeros(dh_sc.shape, F32)

        @pl.when((i == 0) & (f == 0))
        def _():
            dg_ref[...] = jnp.zeros(dg_ref.shape, F32)

        u = jnp.maximum(_dot(h_sc[...], wu_ref[...]), 0.0)
        a_ref[...] = (u * u).astype(BF16)
        du = (_dot(dyb_sc[...], wdt_ref[...]) * (2.0 * u)).astype(BF16)
        du_ref[...] = du
        dh_sc[...] += _dot(du, wut_ref[...])

        @pl.when(f == nf - 1)
        def _():
            dx, dg = _rms_bwd(x_ref[...], g_ref[...], dh_sc[...])
            dx_ref[...] = dy_ref[...] + dx
            dg_ref[...] += dg

    row = pl.BlockSpec((tm, D_MODEL), lambda i, f: (i, 0))
    hid = pl.BlockSpec((tm, tf), lambda i, f: (i, f))
    return pl.pallas_call(
        body, name=name, grid=(T // tm, nf),
        in_specs=[row, pl.BlockSpec((1, D_MODEL), lambda i, f: (0, 0)),
                  pl.BlockSpec((D_MODEL, tf), lambda i, f: (0, f)), pl.BlockSpec((tf, D_MODEL), lambda i, f: (f, 0)),
                  pl.BlockSpec((D_MODEL, tf), lambda i, f: (0, f)), row],
        out_specs=[row, row, hid, hid, pl.BlockSpec((1, D_MODEL), lambda i, f: (0, 0))],
        out_shape=[jax.ShapeDtypeStruct((T, D_MODEL), F32), jax.ShapeDtypeStruct((T, D_MODEL), BF16),
                   jax.ShapeDtypeStruct((T, F), BF16), jax.ShapeDtypeStruct((T, F), BF16),
                   jax.ShapeDtypeStruct((1, D_MODEL), F32)],
        scratch_shapes=[pltpu.VMEM((tm, D_MODEL), BF16), pltpu.VMEM((tm, D_MODEL), BF16),
                        pltpu.VMEM((tm, D_MODEL), F32)],
        compiler_params=_cparams("arbitrary", "arbitrary"),
    )(x, g, w_up, w_up_t, w_down_t, dy)


def proj_bwd(x, g, dres, parts, name):
    T = x.shape[0]
    tm = min(TM, T)
    n = len(parts)

    def body(*refs):
        x_ref, g_ref, dr_ref = refs[:3]
        da_refs, wt_refs = refs[3:3 + n], refs[3 + n:3 + 2 * n]
        dx_ref, h_ref, dg_ref = refs[3 + 2 * n:]

        @pl.when(pl.program_id(0) == 0)
        def _():
            dg_ref[...] = jnp.zeros(dg_ref.shape, F32)

        xv = x_ref[...]
        dh = _dot(da_refs[0][...].astype(BF16), wt_refs[0][...])
        for a_ref, w_ref in zip(da_refs[1:], wt_refs[1:]):
            dh = dh + _dot(a_ref[...].astype(BF16), w_ref[...])
        h_ref[...] = (xv * _rms(xv) * g_ref[...]).astype(BF16)
        dx, dg = _rms_bwd(xv, g_ref[...], dh)
        dx_ref[...] = dr_ref[...] + dx
        dg_ref[...] += dg

    row = pl.BlockSpec((tm, D_MODEL), lambda i: (i, 0))
    one = pl.BlockSpec((1, D_MODEL), lambda i: (0, 0))
    in_specs = [row, one, row]
    in_specs += [pl.BlockSpec((tm, da.shape[1]), lambda i: (i, 0)) for da, _ in parts]
    in_specs += [pl.BlockSpec(wt.shape, lambda i: (0, 0)) for _, wt in parts]
    return pl.pallas_call(
        body, name=name, grid=(T // tm,), in_specs=in_specs,
        out_specs=[row, row, one],
        out_shape=[jax.ShapeDtypeStruct((T, D_MODEL), F32), jax.ShapeDtypeStruct((T, D_MODEL), BF16),
                   jax.ShapeDtypeStruct((1, D_MODEL), F32)],
        compiler_params=_cparams("arbitrary"),
    )(x, g, dres, *[da for da, _ in parts], *[wt for _, wt in parts])


def final_loss(x, g, tgt, name):
    T = x.shape[0]
    tm = min(TM, T)

    def body(x_ref, g_ref, t_ref, l_ref, dx_ref, dg_ref):
        @pl.when(pl.program_id(0) == 0)
        def _():
            l_ref[...] = jnp.zeros(l_ref.shape, F32)
            dg_ref[...] = jnp.zeros(dg_ref.shape, F32)

        xv = x_ref[...]
        gv = g_ref[...]
        err = xv * _rms(xv) * gv - t_ref[...]
        l_ref[...] += 0.5 * jnp.sum(jnp.mean(err * err, axis=-1, keepdims=True), axis=0, keepdims=True)
        dx, dg = _rms_bwd(xv, gv, err * (1.0 / D_MODEL))
        dx_ref[...] = dx
        dg_ref[...] += dg

    row = pl.BlockSpec((tm, D_MODEL), lambda i: (i, 0))
    one = pl.BlockSpec((1, D_MODEL), lambda i: (0, 0))
    return pl.pallas_call(
        body, name=name, grid=(T // tm,), in_specs=[row, one, row],
        out_specs=[pl.BlockSpec((8, LANES), lambda i: (0, 0)), row, one],
        out_shape=[jax.ShapeDtypeStruct((8, LANES), F32), jax.ShapeDtypeStruct((T, D_MODEL), F32),
                   jax.ShapeDtypeStruct((1, D_MODEL), F32)],
        compiler_params=_cparams("arbitrary"),
    )(x, g, tgt)


def _swa_specs(tq):
    r = tq // SWA_WINDOW
    cur = lambda ix: pl.BlockSpec((tq, LANES), lambda kv, i: (ix(i), kv))
    prev = lambda ix: pl.BlockSpec((SWA_WINDOW, LANES), lambda kv, i: (jnp.maximum(ix(i) * r - 1, 0), kv))
    return cur, prev


def _swa_visible(tq, tile):
    r = lax.broadcasted_iota(jnp.int32, (tq, tq + SWA_WINDOW), 0)
    c = lax.broadcasted_iota(jnp.int32, (tq, tq + SWA_WINDOW), 1)
    rel = r + SWA_WINDOW - c
    return (rel >= 0) & (rel < SWA_WINDOW) & ((c >= SWA_WINDOW) | (tile > 0))


def _swa_probs(qm, kcat, vis, sk):
    s = jnp.where(vis, _nt(qm, kcat) * 0.125, -1e30)
    m = jnp.maximum(jnp.max(s, axis=1, keepdims=True), sk)
    e = jnp.exp(s - m)
    esk = jnp.exp(sk - m)
    inv = 1.0 / (jnp.sum(e, axis=1, keepdims=True) + esk)
    return e * inv, esk * inv


def swa_fwd(qkv, kdup, vdup, sinks_b, name):
    T = qkv.shape[0]
    tq = min(SWA_TQ, T)
    cur, prev = _swa_specs(tq)
    ident = lambda i: i

    def body(q_ref, kc_ref, kp_ref, vc_ref, vp_ref, sk_ref, o_ref):
        i = pl.program_id(1)
        kcat = jnp.concatenate([kp_ref[...], kc_ref[...]], axis=0)
        vcat = jnp.concatenate([vp_ref[...], vc_ref[...]], axis=0)
        vis = _swa_visible(tq, i)
        lane = lax.broadcasted_iota(jnp.int32, (1, LANES), 1)
        for pp in range(2):
            q2 = q_ref[:, pp * LANES:(pp + 1) * LANES]
            outs = []
            for hf in range(2):
                lm = (lane < 64) if hf == 0 else (lane >= 64)
                qm = jnp.where(lm, q2, jnp.zeros_like(q2))
                p, _ = _swa_probs(qm, kcat, vis, sk_ref[2 * pp + hf:2 * pp + hf + 1, 0:1])
                outs.append(_dot(p.astype(BF16), vcat))
            o_ref[:, pp * LANES:(pp + 1) * LANES] = jnp.where(lane < 64, outs[0], outs[1]).astype(BF16)

    return pl.pallas_call(
        body, name=name, grid=(4, T // tq),
        in_specs=[pl.BlockSpec((tq, 2 * LANES), lambda kv, i: (i, kv)), cur(ident), prev(ident), cur(ident), prev(ident),
                  pl.BlockSpec((None, 8, LANES), lambda kv, i: (kv, 0, 0))],
        out_specs=pl.BlockSpec((tq, 2 * LANES), lambda kv, i: (i, kv)),
        out_shape=jax.ShapeDtypeStruct((T, D_MODEL), BF16),
        compiler_params=_cparams("parallel", "arbitrary"),
    )(qkv, kdup, kdup, vdup, vdup, sinks_b)


def swa_bwd(qkv, kdup, vdup, sinks_b, o, do, name):
    T = qkv.shape[0]
    tq = min(SWA_TQ, T)
    n = T // tq
    cur, prev = _swa_specs(tq)
    rev = lambda i: n - 1 - i

    def body(q_ref, kc_ref, kp_ref, vc_ref, vp_ref, sk_ref, o_ref, do_ref, dq_ref, dk_ref, dv_ref, dsk_ref, ck_sc, cv_sc):
        i = pl.program_id(1)

        @pl.when(i == 0)
        def _():
            ck_sc[...] = jnp.zeros(ck_sc.shape, F32)
            cv_sc[...] = jnp.zeros(cv_sc.shape, F32)
            dsk_ref[...] = jnp.zeros(dsk_ref.shape, F32)

        kcat = jnp.concatenate([kp_ref[...], kc_ref[...]], axis=0)
        vcat = jnp.concatenate([vp_ref[...], vc_ref[...]], axis=0)
        vis = _swa_visible(tq, n - 1 - i)
        lane = lax.broadcasted_iota(jnp.int32, (1, LANES), 1)
        dkc = jnp.zeros((tq + SWA_WINDOW, LANES), F32)
        dvc = jnp.zeros((tq + SWA_WINDOW, LANES), F32)
        for pp in range(2):
            sl = slice(pp * LANES, (pp + 1) * LANES)
            q2, do2, o2 = q_ref[:, sl], do_ref[:, sl], o_ref[:, sl]
            dqs = []
            for hf in range(2):
                g = 2 * pp + hf
                lm = (lane < 64) if hf == 0 else (lane >= 64)
                qm = jnp.where(lm, q2, jnp.zeros_like(q2))
                dom = jnp.where(lm, do2, jnp.zeros_like(do2))
                p, psk = _swa_probs(qm, kcat, vis, sk_ref[g:g + 1, 0:1])
                delta = jnp.sum(dom.astype(F32) * o2.astype(F32), axis=1, keepdims=True)
                ds = p * (_nt(dom, vcat) - delta)
                dsk_ref[g:g + 1, :] += jnp.zeros((1, LANES), F32) - jnp.sum(psk * delta, axis=0, keepdims=True)
                dsb = (ds * 0.125).astype(BF16)
                dqs.append(_dot(dsb, kcat))
                dkc = dkc + _tn(dsb, qm)
                dvc = dvc + _tn(p.astype(BF16), dom)
            dq_ref[:, sl] = jnp.where(lane < 64, dqs[0], dqs[1]).astype(BF16)
        dkc = dkc + pltpu.roll(dkc, 64, 1)
        dvc = dvc + pltpu.roll(dvc, 64, 1)
        for full, ref, carry in ((dkc, dk_ref, ck_sc), (dvc, dv_ref, cv_sc)):
            if tq > SWA_WINDOW:
                ref[0:tq - SWA_WINDOW, :] = full[SWA_WINDOW:tq, :]
            ref[tq - SWA_WINDOW:tq, :] = full[tq:tq + SWA_WINDOW, :] + carry[...]
            carry[...] = full[0:SWA_WINDOW, :]

    wide = pl.BlockSpec((tq, 2 * LANES), lambda kv, i: (rev(i), kv))
    return pl.pallas_call(
        body, name=name, grid=(4, n),
        in_specs=[wide, cur(rev), prev(rev), cur(rev), prev(rev),
                  pl.BlockSpec((None, 8, LANES), lambda kv, i: (kv, 0, 0)), wide, wide],
        out_specs=[wide, cur(rev), cur(rev), pl.BlockSpec((None, 8, LANES), lambda kv, i: (kv, 0, 0))],
        out_shape=[jax.ShapeDtypeStruct((T, D_MODEL), BF16), jax.ShapeDtypeStruct((T, 4 * LANES), F32),
                   jax.ShapeDtypeStruct((T, 4 * LANES), F32), jax.ShapeDtypeStruct((4, 8, LANES), F32)],
        scratch_shapes=[pltpu.VMEM((SWA_WINDOW, LANES), F32), pltpu.VMEM((SWA_WINDOW, LANES), F32)],
        compiler_params=_cparams("arbitrary", "arbitrary"),
    )(qkv, kdup, kdup, vdup, vdup, sinks_b, o, do)


def fox_gate_fwd(fl, qkv, name):
    T = fl.shape[0]
    ts = min(SCAN_T, T)

    def body(fl_ref, q_ref, k_ref, qa_ref, ka_ref, carry):
        @pl.when(pl.program_id(0) == 0)
        def _():
            carry[...] = jnp.zeros(carry.shape, F32)

        xv = fl_ref[...]
        ls = jnp.minimum(xv, 0.0) - jnp.log(1.0 + jnp.exp(-jnp.abs(xv)))
        tri = (lax.broadcasted_iota(jnp.int32, (ts, ts), 0) >= lax.broadcasted_iota(jnp.int32, (ts, ts), 1)).astype(F32)
        cs = jnp.dot(tri, ls, precision=HI, preferred_element_type=F32) + carry[...]
        carry[...] = cs[ts - 1:ts, :]
        c1 = cs.astype(BF16).astype(F32)
        c2 = (cs - c1).astype(BF16).astype(F32)
        c3 = (cs - c1 - c2).astype(BF16).astype(F32)
        lane = lax.broadcasted_iota(jnp.int32, (1, LANES), 1)
        ones_q = jnp.where((lane >= 67) & (lane < 70), 1.0, 0.0)
        ones_k = jnp.where((lane >= 64) & (lane < 67), 1.0, 0.0)
        for b in range(8):
            qf = q_ref[:, b * LANES:(b + 1) * LANES].astype(F32) * 0.125
            kf = k_ref[:, b * LANES:(b + 1) * LANES].astype(F32)
            for hf in range(2):
                h = 2 * b + hf
                a1, a2, a3 = c1[:, h:h + 1], c2[:, h:h + 1], c3[:, h:h + 1]
                aux_q = jnp.where(lane == 64, a1, jnp.where(lane == 65, a2, jnp.where(lane == 66, a3, ones_q)))
                aux_k = jnp.where(lane == 67, -a1, jnp.where(lane == 68, -a2, jnp.where(lane == 69, -a3, ones_k)))
                qs = qf if hf == 0 else pltpu.roll(qf, 64, 1)
                ks = kf if hf == 0 else pltpu.roll(kf, 64, 1)
                qa_ref[:, h * LANES:(h + 1) * LANES] = jnp.where(lane < 64, qs, aux_q).astype(BF16)
                ka_ref[:, h * LANES:(h + 1) * LANES] = jnp.where(lane < 64, ks, aux_k).astype(BF16)

    out = pl.BlockSpec((ts, 16 * LANES), lambda i: (i, 0))
    return pl.pallas_call(
        body, name=name, grid=(T // ts,),
        in_specs=[pl.BlockSpec((ts, LANES), lambda i: (i, 0)), pl.BlockSpec((ts, D_MODEL), lambda i: (i, 0)),
                  pl.BlockSpec((ts, D_MODEL), lambda i: (i, 1))],
        out_specs=[out, out],
        out_shape=[jax.ShapeDtypeStruct((T, 16 * LANES), BF16), jax.ShapeDtypeStruct((T, 16 * LANES), BF16)],
        scratch_shapes=[pltpu.VMEM((1, LANES), F32)],
        compiler_params=_cparams("arbitrary"),
    )(fl, qkv, qkv)


def fox_gate_bwd(fl, dc, name):
    T = fl.shape[0]
    ts = min(SCAN_T, T)
    n = T // ts

    def body(fl_ref, dc_ref, o_ref, carry):
        @pl.when(pl.program_id(0) == 0)
        def _():
            carry[...] = jnp.zeros(carry.shape, F32)

        tri = (lax.broadcasted_iota(jnp.int32, (ts, ts), 0) <= lax.broadcasted_iota(jnp.int32, (ts, ts), 1)).astype(F32)
        rs = jnp.dot(tri, dc_ref[...], precision=HI, preferred_element_type=F32) + carry[...]
        carry[...] = rs[0:1, :]
        o_ref[...] = rs * (1.0 / (1.0 + jnp.exp(fl_ref[...])))

    blk = pl.BlockSpec((ts, LANES), lambda i: (n - 1 - i, 0))
    return pl.pallas_call(
        body, name=name, grid=(n,), in_specs=[blk, blk], out_specs=blk,
        out_shape=jax.ShapeDtypeStruct((T, LANES), F32), scratch_shapes=[pltpu.VMEM((1, LANES), F32)],
        compiler_params=_cparams("arbitrary"),
    )(fl, dc)


FOX_RB = 32


def _half(lane, hf):
    return (lane < 64) if hf == 0 else (lane >= 64)


def _pair(lane, a, b):
    return jnp.where(lane < 64, a, pltpu.roll(b, 64, 1)), jnp.where(lane < 64, pltpu.roll(a, 64, 1), b)


def fox_fwd(qa, ka, qkv, name):
    T = qa.shape[0]
    t = min(FOX_T, T)
    n = T // t

    def body(qa_ref, ka_ref, v_ref, o_ref, lse_ref, m_sc, l_sc, acc_sc, ls_sc, s_sc, p_sc):
        i, j = pl.program_id(1), pl.program_id(2)
        lane = lax.broadcasted_iota(jnp.int32, (1, LANES), 1)

        @pl.when(j == 0)
        def _():
            m_sc[...] = jnp.full(m_sc.shape, -1e30, F32)
            l_sc[...] = jnp.zeros(l_sc.shape, F32)
            acc_sc[...] = jnp.zeros(acc_sc.shape, F32)

        def tile(diag):
            v2 = v_ref[...]
            for hf in range(2):
                hs = slice(hf * LANES, (hf + 1) * LANES)
                sv = _nt(qa_ref[:, hs], ka_ref[:, hs])
                if diag:
                    vis = lax.broadcasted_iota(jnp.int32, (t, t), 0) >= lax.broadcasted_iota(jnp.int32, (t, t), 1)
                    sv = jnp.where(vis, sv, -1e30)
                s_sc[...] = sv
                m_old = m_sc[hf]
                m_new = jnp.maximum(m_old, jnp.max(s_sc[...], axis=1, keepdims=True))
                al = jnp.exp(m_old - m_new)
                m_sc[hf] = m_new
                for r0 in range(0, t, FOX_RB):
                    rs = slice(r0, r0 + FOX_RB)
                    mrow = m_new[rs, :]
                    part, pieces = None, []
                    for cb in range(0, t, LANES):
                        pc = jnp.exp(s_sc[rs, cb:cb + LANES] - mrow)
                        part = pc if part is None else part + pc
                        pieces.append(pc.astype(BF16))
                    p_sc[rs, :] = jnp.concatenate(pieces, axis=1)
                    ls_sc[rs, :] = part
                l_sc[hf] = al * l_sc[hf] + ls_sc[...]
                acc_sc[hf] = al * acc_sc[hf] + _dot(p_sc[...], v2)

        @pl.when(j < i)
        def _():
            tile(False)

        @pl.when(j == i)
        def _():
            tile(True)
            l0 = jnp.sum(l_sc[0], axis=1, keepdims=True)
            l1 = jnp.sum(l_sc[1], axis=1, keepdims=True)
            o_ref[...] = jnp.where(lane < 64, acc_sc[0] / l0, acc_sc[1] / l1).astype(BF16)
            lse_ref[...] = jnp.where(lane < 64, m_sc[0] + jnp.log(l0), m_sc[1] + jnp.log(l1))

    oblk = pl.BlockSpec((t, LANES), lambda p, i, j: (i, p))
    return pl.pallas_call(
        body, name=name, grid=(8, n, n),
        in_specs=[pl.BlockSpec((t, 2 * LANES), lambda p, i, j: (i, p)),
                  pl.BlockSpec((t, 2 * LANES), lambda p, i, j: (jnp.minimum(j, i), p)),
                  pl.BlockSpec((t, LANES), lambda p, i, j: (jnp.minimum(j, i), 16 + p))],
        out_specs=[oblk, oblk],
        out_shape=[jax.ShapeDtypeStruct((T, D_MODEL), BF16), jax.ShapeDtypeStruct((T, D_MODEL), F32)],
        scratch_shapes=[pltpu.VMEM((2, t, LANES), F32), pltpu.VMEM((2, t, LANES), F32), pltpu.VMEM((2, t, LANES), F32),
                        pltpu.VMEM((t, LANES), F32), pltpu.VMEM((t, t), F32), pltpu.VMEM((t, t), BF16)],
        compiler_params=_cparams("parallel", "parallel", "arbitrary"),
    )(qa, ka, qkv)


def fox_delta(do, o, name):
    T = do.shape[0]
    tm = min(TM, T)

    def body(do_ref, o_ref, d_ref):
        lane = lax.broadcasted_iota(jnp.int32, (1, LANES), 1)
        out = jnp.zeros((tm, LANES), F32)
        for b in range(8):
            d = do_ref[:, b * LANES:(b + 1) * LANES].astype(F32) * o_ref[:, b * LANES:(b + 1) * LANES].astype(F32)
            for hf in range(2):
                out = jnp.where(lane == 2 * b + hf, jnp.sum(jnp.where(_half(lane, hf), d, 0.0), axis=1, keepdims=True), out)
        d_ref[...] = out

    row = pl.BlockSpec((tm, D_MODEL), lambda i: (i, 0))
    return pl.pallas_call(
        body, name=name, grid=(T // tm,), in_specs=[row, row],
        out_specs=pl.BlockSpec((tm, LANES), lambda i: (i, 0)),
        out_shape=jax.ShapeDtypeStruct((T, LANES), F32),
        compiler_params=_cparams("parallel"),
    )(do, o)


def fox_bwd(qa, ka, qkv, lse_row, delta_row, do, name):
    T = qa.shape[0]
    t = min(FOX_T, T)
    n = T // t

    def body(qa_ref, ka_ref, v_ref, lr_ref, dr_ref, do_ref, dq_ref, auxq_ref, dk_ref, dv_ref, aux_ref,
             dq_sc, dk_sc, dv_sc, s_sc, dp_sc, p_sc, ds_sc, dqo_sc, auxo_sc, out_sems):
        j, i = pl.program_id(1), pl.program_id(2)
        lane = lax.broadcasted_iota(jnp.int32, (1, LANES), 1)
        qrows = pl.ds(pl.multiple_of(i * t, t), t)

        @pl.when(i == 0)
        def _():
            dk_sc[...] = jnp.zeros(dk_sc.shape, F32)
            dv_sc[...] = jnp.zeros(dv_sc.shape, F32)

        @pl.when(j == 0)
        def _():
            dq_sc[:, qrows, :] = jnp.zeros((2, t, LANES), F32)

        def tile(diag):
            v2, do2 = v_ref[...], do_ref[...]
            for hf in range(2):
                hs = slice(hf * LANES, (hf + 1) * LANES)
                lm = _half(lane, hf)
                qh = qa_ref[:, hs]
                s_sc[...] = _nt(ka_ref[:, hs], qh)
                dp_sc[...] = _nt(jnp.where(lm, v2, jnp.zeros_like(v2)), do2)
                lrow, drow = lr_ref[hf:hf + 1, :], dr_ref[hf:hf + 1, :]
                for r0 in range(0, t, FOX_RB):
                    rs = slice(r0, r0 + FOX_RB)
                    sv = s_sc[rs, :]
                    if diag:
                        vis = lax.broadcasted_iota(jnp.int32, (FOX_RB, t), 1) >= (r0 + lax.broadcasted_iota(jnp.int32, (FOX_RB, t), 0))
                        sv = jnp.where(vis, sv, -1e30)
                    p = jnp.exp(sv - lrow)
                    p_sc[rs, :] = p.astype(BF16)
                    ds_sc[rs, :] = (p * (dp_sc[rs, :] - drow)).astype(BF16)
                dv_sc[...] += _dot(p_sc[...], jnp.where(lm, do2, jnp.zeros_like(do2)))
                dk_sc[hf] += _dot(ds_sc[...], qh)
                dq_sc[hf, qrows, :] += _tn(ds_sc[...], ka_ref[:, hs])

        @pl.when(i > j)
        def _():
            tile(False)

        @pl.when(i == j)
        def _():
            tile(True)
            dq, aux = _pair(lane, dq_sc[0, qrows, :], dq_sc[1, qrows, :])
            dqo_sc[...] = (dq * 0.125).astype(BF16)
            auxo_sc[...] = aux
            cols = pl.ds(pl.multiple_of(pl.program_id(0) * LANES, LANES), LANES)
            c1 = pltpu.make_async_copy(dqo_sc, dq_ref.at[qrows, cols], out_sems.at[0])
            c2 = pltpu.make_async_copy(auxo_sc, auxq_ref.at[qrows, cols], out_sems.at[1])
            c1.start()
            c2.start()
            c1.wait()
            c2.wait()

        @pl.when(i == n - 1)
        def _():
            dk, aux = _pair(lane, dk_sc[0], dk_sc[1])
            dk_ref[...] = dk.astype(BF16)
            aux_ref[...] = aux
            dv_ref[...] = dv_sc[...].astype(BF16)

    qblk = pl.BlockSpec((t, LANES), lambda p, j, i: (jnp.maximum(i, j), p))
    kblk = pl.BlockSpec((t, LANES), lambda p, j, i: (j, p))
    rblk = pl.BlockSpec((None, 2, t), lambda p, j, i: (p, 0, jnp.maximum(i, j)))
    bf, f32 = jax.ShapeDtypeStruct((T, D_MODEL), BF16), jax.ShapeDtypeStruct((T, D_MODEL), F32)
    return pl.pallas_call(
        body, name=name, grid=(8, n, n),
        in_specs=[pl.BlockSpec((t, 2 * LANES), lambda p, j, i: (jnp.maximum(i, j), p)),
                  pl.BlockSpec((t, 2 * LANES), lambda p, j, i: (j, p)),
                  pl.BlockSpec((t, LANES), lambda p, j, i: (j, 16 + p)), rblk, rblk, qblk],
        out_specs=[pl.BlockSpec(memory_space=pl.ANY), pl.BlockSpec(memory_space=pl.ANY), kblk, kblk, kblk],
        out_shape=[bf, f32, bf, bf, f32],
        scratch_shapes=[pltpu.VMEM((2, T, LANES), F32), pltpu.VMEM((2, t, LANES), F32), pltpu.VMEM((t, LANES), F32),
                        pltpu.VMEM((t, t), F32), pltpu.VMEM((t, t), F32), pltpu.VMEM((t, t), BF16),
                        pltpu.VMEM((t, t), BF16), pltpu.VMEM((t, LANES), BF16), pltpu.VMEM((t, LANES), F32),
                        pltpu.SemaphoreType.DMA((2,))],
        compiler_params=_cparams("arbitrary", "arbitrary", "arbitrary"),
    )(qa, ka, qkv, lse_row, delta_row, do)


C = HGRN_CHUNK
LEVELS = (64, 32, 16, 8, 4, 2)


def _pivot(b, B, row):
    if B == C:
        return jnp.broadcast_to(b[C // 2 - 1:C // 2, :], b.shape)
    if B >= 8:
        b3 = b.reshape(C // B, B, LANES)
        return jnp.broadcast_to(b3[:, B // 2 - 1:B // 2, :], b3.shape).reshape(C, LANES)
    if B == 4:
        y = jnp.where((row & 3) == 1, b, 0.0)
        return y + pltpu.roll(y, 1, 0) + pltpu.roll(y, 2, 0) + pltpu.roll(y, C - 1, 0)
    y = jnp.where((row & 1) == 0, b, 0.0)
    return y + pltpu.roll(y, 1, 0)


def _level_factors(bcum):
    row = lax.broadcasted_iota(jnp.int32, (C, 1), 0)
    out = []
    for B in LEVELS:
        upper = (row & (B - 1)) >= B // 2
        e = jnp.exp(-jnp.abs(bcum - _pivot(bcum, B, row)))
        out.append((B, jnp.where(upper, e, 0.0), jnp.where(upper, 0.0, e)))
    return out


def _same_block(B):
    sh = B.bit_length() - 1
    r = lax.broadcasted_iota(jnp.int32, (C, C), 0)
    c = lax.broadcasted_iota(jnp.int32, (C, C), 1)
    return (r >> sh) == (c >> sh)


def _hgrn_gates(q, fl, lb):
    sg = _sigmoid(fl)
    f = lb + (1.0 - lb) * sg
    sq = _sigmoid(q)
    return sg, f, jnp.log(f), 1.0 - f, sq, q * sq


def _cumsum_rows(x, reverse=False):
    r = lax.broadcasted_iota(jnp.int32, (C, C), 0)
    c = lax.broadcasted_iota(jnp.int32, (C, C), 1)
    tri = ((r <= c) if reverse else (r >= c)).astype(F32)
    return jnp.dot(tri, x, precision=HI, preferred_element_type=F32)


def _intra(qs, k, factors):
    r = lax.broadcasted_iota(jnp.int32, (C, C), 0)
    c = lax.broadcasted_iota(jnp.int32, (C, C), 1)
    a = jnp.where(r == c, jnp.sum(qs * k, axis=1, keepdims=True), 0.0)
    ops = []
    for B, eq, ek in factors:
        ql, kl = (qs * eq).astype(BF16), (k * ek).astype(BF16)
        al = _nt(ql, kl)
        a = a + (al if B == C else jnp.where(_same_block(B), al, 0.0))
        ops.append((ql, kl))
    return a, ops


def hgrn_fwd(proj, lb, gn, name):
    T = proj.shape[0]
    tg = min(HGRN_TG, T)
    nch = tg // C

    def body(q_ref, fl_ref, v_ref, g_ref, lb_ref, gn_ref, ao_ref, o_ref, st_ref, st_sc):
        @pl.when(pl.program_id(1) == 0)
        def _():
            st_sc[...] = jnp.zeros(st_sc.shape, F32)

        lb_v, gn_v = lb_ref[...], gn_ref[...]

        def chunk(ci, carry):
            rows = pl.ds(pl.multiple_of(ci * C, C), C)
            _, f, lf, k, _, qs = _hgrn_gates(q_ref[rows, :], fl_ref[rows, :], lb_v)
            vb = v_ref[rows, :].astype(BF16)
            gv = g_ref[rows, :]
            bcum = _cumsum_rows(lf)
            blast = bcum[C - 1:C, :]
            a, _ = _intra(qs, k, _level_factors(bcum))
            st = st_sc[...]
            st_ref[ci] = st
            o = _dot(a.astype(BF16), vb) + _nt((qs * jnp.exp(bcum)).astype(BF16), st.astype(BF16))
            st_sc[...] = st * jnp.exp(blast) + _tn(vb, (k * jnp.exp(blast - bcum)).astype(BF16))
            o_ref[rows, :] = o
            ao_ref[rows, :] = (o * _rms(o) * gn_v * (gv * _sigmoid(gv))).astype(BF16)
            return carry

        lax.fori_loop(0, nch, chunk, 0)

    col = lambda off: pl.BlockSpec((tg, LANES), lambda h, i: (i, off + h))
    one = pl.BlockSpec((1, LANES), lambda h, i: (0, h))
    return pl.pallas_call(
        body, name=name, grid=(8, T // tg),
        in_specs=[col(0), col(8), col(16), col(24), one, one],
        out_specs=[col(0), col(0), pl.BlockSpec((None, nch, LANES, LANES), lambda h, i: (h, i, 0, 0))],
        out_shape=[jax.ShapeDtypeStruct((T, D_MODEL), BF16), jax.ShapeDtypeStruct((T, D_MODEL), F32),
                   jax.ShapeDtypeStruct((8, T // C, LANES, LANES), F32)],
        scratch_shapes=[pltpu.VMEM((LANES, LANES), F32)],
        compiler_params=_cparams("parallel", "arbitrary"),
    )(proj, proj, proj, proj, lb, gn)


def hgrn_bwd(proj, lb, gn, o_raw, states, dao, name):
    T = proj.shape[0]
    tg = min(HGRN_TG, T)
    nch = tg // C
    n = T // tg

    def body(q_ref, fl_ref, v_ref, g_ref, lb_ref, gn_ref, o_ref, st_ref, dao_ref,
             dq_ref, dfl_ref, dv_ref, dg_ref, dlb_ref, dgn_ref, dst_sc):
        @pl.when(pl.program_id(1) == 0)
        def _():
            dst_sc[...] = jnp.zeros(dst_sc.shape, F32)
            dlb_ref[...] = jnp.zeros(dlb_ref.shape, F32)
            dgn_ref[...] = jnp.zeros(dgn_ref.shape, F32)

        lb_v, gn_v = lb_ref[...], gn_ref[...]
        r64 = lax.broadcasted_iota(jnp.int32, (C, C), 0)
        c64 = lax.broadcasted_iota(jnp.int32, (C, C), 1)
        row = lax.broadcasted_iota(jnp.int32, (C, 1), 0)

        def chunk(cr, carry):
            ci = nch - 1 - cr
            rows = pl.ds(pl.multiple_of(ci * C, C), C)
            q, fl, gv = q_ref[rows, :], fl_ref[rows, :], g_ref[rows, :]
            sg, f, lf, k, sq, qs = _hgrn_gates(q, fl, lb_v)
            vb = v_ref[rows, :].astype(BF16)
            o = o_ref[rows, :]
            ro = _rms(o)
            on = o * ro
            sgg = _sigmoid(gv)
            gate = gv * sgg
            dao_v = dao_ref[rows, :].astype(F32)
            dg_ref[rows, :] = (dao_v * on * gn_v * (sgg * (1.0 + gv * (1.0 - sgg)))).astype(BF16)
            dgn_ref[...] += jnp.sum(dao_v * on * gate, axis=0, keepdims=True)
            don = dao_v * gn_v * gate
            do = ro * (don - on * jnp.mean(don * on, axis=-1, keepdims=True))
            dob = do.astype(BF16)
            bcum = _cumsum_rows(lf)
            blast = bcum[C - 1:C, :]
            factors = _level_factors(bcum)
            a, ops = _intra(qs, k, factors)
            eb = jnp.exp(bcum)
            ekb = jnp.exp(blast - bcum)
            qb = qs * eb
            kb = k * ekb
            st = st_ref[ci]
            dst = dst_sc[...]
            dstb = dst.astype(BF16)
            da = jnp.where(r64 >= c64, _nt(dob, vb), 0.0)
            dv_ref[rows, :] = (_tn(a.astype(BF16), dob) + _nt(kb.astype(BF16), dstb)).astype(BF16)
            dqb = _dot(dob, st.astype(BF16))
            dkb = _dot(vb, dstb)
            eblast = jnp.exp(blast)
            dst_sc[...] = dst * eblast + _tn(dob, qb.astype(BF16))
            dblast = eblast * jnp.sum(dst * st, axis=0, keepdims=True) + jnp.sum(dkb * kb, axis=0, keepdims=True)
            dad = jnp.sum(jnp.where(r64 == c64, da, 0.0), axis=1, keepdims=True)
            dqs = dqb * eb + dad * k
            dk = dkb * ekb + dad * qs
            dbcum = dqb * qb - dkb * kb + jnp.where(row == C - 1, dblast, 0.0)
            for (B, eq, ek), (ql, kl) in zip(factors, ops):
                dal = (da if B == C else jnp.where(_same_block(B), da, 0.0)).astype(BF16)
                dql, dkl = _dot(dal, kl), _tn(dal, ql)
                dqs = dqs + dql * eq
                dk = dk + dkl * ek
                dbcum = dbcum + (dql * ql.astype(F32) - dkl * kl.astype(F32))
            df = _cumsum_rows(dbcum, reverse=True) / f - dk
            dfl_ref[rows, :] = (df * (1.0 - lb_v) * sg * (1.0 - sg)).astype(BF16)
            dlb_ref[...] += jnp.sum(df * (1.0 - sg), axis=0, keepdims=True)
            dq_ref[rows, :] = (dqs * (sq * (1.0 + q * (1.0 - sq)))).astype(BF16)
            return carry

        lax.fori_loop(0, nch, chunk, 0)

    col = lambda off: pl.BlockSpec((tg, LANES), lambda h, i: (n - 1 - i, off + h))
    one = pl.BlockSpec((1, LANES), lambda h, i: (0, h))
    big = jax.ShapeDtypeStruct((T, D_MODEL), BF16)
    small = jax.ShapeDtypeStruct((1, D_MODEL), F32)
    return pl.pallas_call(
        body, name=name, grid=(8, n),
        in_specs=[col(0), col(8), col(16), col(24), one, one, col(0),
                  pl.BlockSpec((None, nch, LANES, LANES), lambda h, i: (h, n - 1 - i, 0, 0)), col(0)],
        out_specs=[col(0), col(0), col(0), col(0), one, one],
        out_shape=[big, big, big, big, small, small],
        scratch_shapes=[pltpu.VMEM((LANES, LANES), F32)],
        compiler_params=_cparams("arbitrary", "arbitrary"),
    )(proj, proj, proj, proj, lb, gn, o_raw, states, dao)


def lower_bound_fwd(logits, name):
    def body(l_ref, s_ref):
        lv = l_ref[...]
        e = jnp.exp(lv - jnp.max(lv, axis=0, keepdims=True))
        s_ref[...] = e / jnp.sum(e, axis=0, keepdims=True)

    return pl.pallas_call(body, name=name, out_shape=jax.ShapeDtypeStruct(logits.shape, F32))(logits)


def lower_bound_bwd(sm, dlb, name):
    def body(s_ref, d_ref, o_ref):
        s = s_ref[...]
        row = lax.broadcasted_iota(jnp.int32, s.shape, 0)
        o_ref[...] = d_ref[...] * s[1:2, :] * (jnp.where(row == 1, 1.0, 0.0) - s)

    return pl.pallas_call(body, name=name, out_shape=jax.ShapeDtypeStruct(sm.shape, F32))(sm, dlb)


def _pad_rows(flat, mult):
    rows = -(-flat.shape[-1] // D_MODEL)
    rows = -(-rows // mult) * mult
    pad = rows * D_MODEL - flat.shape[-1]
    flat = jnp.pad(flat, [(0, 0)] * (flat.ndim - 1) + [(0, pad)])
    return flat.reshape(flat.shape[:-1] + (rows, D_MODEL))


def _gather_weights(w):
    pieces = []
    for nme in SHARDED:
        a = w[nme]
        if nme in BIASES:
            pieces.append(lax.bitcast_convert_type(a, BF16).reshape(-1))
        else:
            pieces.append(a.astype(BF16).reshape(-1))
    flat = _pad_rows(jnp.concatenate(pieces), 16)
    got = all_gather_rows(flat).reshape(N_DEV, -1)
    full, off = {}, 0
    for nme in SHARDED:
        shp = w[nme].shape
        cnt = 1
        for s in shp:
            cnt *= s
        if nme in BIASES:
            seg = got[:, off:off + 2 * cnt].reshape((N_DEV,) + shp + (2,))
            seg = lax.bitcast_convert_type(seg, F32)
            off += 2 * cnt
        else:
            seg = got[:, off:off + cnt].reshape((N_DEV,) + shp)
            off += cnt
        full[nme] = jnp.concatenate([seg[d] for d in range(N_DEV)], axis=SHARD_AXIS[nme])
    return full


def _pieces(gfull, axis):
    shp = gfull.shape
    a = gfull.reshape(shp[:axis] + (N_DEV, shp[axis] // N_DEV) + shp[axis + 1:])
    return jnp.moveaxis(a, axis, 0).reshape(N_DEV, -1)


def _flat_local(vals):
    return (_pad_rows(jnp.concatenate([vals[n].reshape(-1) for n in SHARDED]), 32),
            _pad_rows(jnp.concatenate([vals[n].reshape(-1) for n in REPLICATED] + [jnp.zeros((1,), F32)]), 16))


def kernel(x, norm_mix, norm_mlp, norm_final, w_up, w_down, swa_w_qkv, swa_b_qkv, swa_sinks, swa_w_o, hgrn_w_in, hgrn_lb_logits, hgrn_g_norm, hgrn_w_o, fox_w_in, fox_b_in, fox_w_o, loss_target, m_norm_mix, m_norm_mlp, m_norm_final, m_w_up, m_w_down, m_swa_w_qkv, m_swa_b_qkv, m_swa_sinks, m_swa_w_o, m_hgrn_w_in, m_hgrn_lb_logits, m_hgrn_g_norm, m_hgrn_w_o, m_fox_w_in, m_fox_b_in, m_fox_w_o, v_norm_mix, v_norm_mlp, v_norm_final, v_w_up, v_w_down, v_swa_w_qkv, v_swa_b_qkv, v_swa_sinks, v_swa_w_o, v_hgrn_w_in, v_hgrn_lb_logits, v_hgrn_g_norm, v_hgrn_w_o, v_fox_w_in, v_fox_b_in, v_fox_w_o):
    w = dict(norm_mix=norm_mix, norm_mlp=norm_mlp, norm_final=norm_final, w_up=w_up, w_down=w_down,
             swa_w_qkv=swa_w_qkv, swa_b_qkv=swa_b_qkv, swa_sinks=swa_sinks, swa_w_o=swa_w_o, hgrn_w_in=hgrn_w_in,
             hgrn_lb_logits=hgrn_lb_logits, hgrn_g_norm=hgrn_g_norm, hgrn_w_o=hgrn_w_o, fox_w_in=fox_w_in,
             fox_b_in=fox_b_in, fox_w_o=fox_w_o)
    mom = dict(norm_mix=m_norm_mix, norm_mlp=m_norm_mlp, norm_final=m_norm_final, w_up=m_w_up, w_down=m_w_down,
               swa_w_qkv=m_swa_w_qkv, swa_b_qkv=m_swa_b_qkv, swa_sinks=m_swa_sinks, swa_w_o=m_swa_w_o,
               hgrn_w_in=m_hgrn_w_in, hgrn_lb_logits=m_hgrn_lb_logits, hgrn_g_norm=m_hgrn_g_norm, hgrn_w_o=m_hgrn_w_o,
               fox_w_in=m_fox_w_in, fox_b_in=m_fox_b_in, fox_w_o=m_fox_w_o)
    var = dict(norm_mix=v_norm_mix, norm_mlp=v_norm_mlp, norm_final=v_norm_final, w_up=v_w_up, w_down=v_w_down,
               swa_w_qkv=v_swa_w_qkv, swa_b_qkv=v_swa_b_qkv, swa_sinks=v_swa_sinks, swa_w_o=v_swa_w_o,
               hgrn_w_in=v_hgrn_w_in, hgrn_lb_logits=v_hgrn_lb_logits, hgrn_g_norm=v_hgrn_g_norm, hgrn_w_o=v_hgrn_w_o,
               fox_w_in=v_fox_w_in, fox_b_in=v_fox_b_in, fox_w_o=v_fox_w_o)
    T = x.shape[1]
    x0 = x[0]
    tgt = loss_target[0]
    W = _gather_weights(w)
    zeros_b = jnp.zeros((1, 4 * D_MODEL), F32)

    def swa_layer(xin, i, j):
        qkv = norm_matmul(xin, norm_mix[i:i + 1], W['swa_w_qkv'][j], W['swa_b_qkv'][j:j + 1], BF16, f"swa_qkv_L{i}")
        dup = lambda a: jnp.broadcast_to(a.reshape(T, 4, 1, 64), (T, 4, 2, 64)).reshape(T, 4 * LANES)
        kdup, vdup = dup(qkv[:, 1024:1280]), dup(qkv[:, 1280:1536])
        sk = jnp.broadcast_to(jnp.pad(swa_sinks[j].reshape(4, 4), ((0, 0), (0, 4)))[:, :, None], (4, 8, LANES))
        ao = swa_fwd(qkv, kdup, vdup, sk, f"swa_fwd_L{i}")
        xmid = matmul(ao, W['swa_w_o'][j], F32, f"swa_out_L{i}", res=xin)
        return xmid, (qkv, kdup, vdup, sk, ao)

    def swa_layer_bwd(xin, saved, dmid, i, j, grads):
        qkv, kdup, vdup, sk, ao = saved
        dao = matmul(dmid, W['swa_w_o'][j].T, BF16, f"swa_dout_L{i}")
        grads['swa_w_o'][j] = tn_matmul(ao, dmid, f"swa_dwo_L{i}")
        dq, dk, dv, dsk = swa_bwd(qkv, kdup, vdup, sk, ao, dao, f"swa_bwd_L{i}")
        wt = W['swa_w_qkv'][j].T
        spread = lambda a: jnp.pad(a.reshape(4, 64, D_MODEL), ((0, 0), (0, 64), (0, 0))).reshape(4 * LANES, D_MODEL)
        gather = lambda a: a.reshape(a.shape[0], 4, LANES)[:, :, :64].reshape(a.shape[0], 256)
        dx, h, dg = proj_bwd(xin, norm_mix[i:i + 1], dmid,
                             [(dq, wt[:1024]), (dk, spread(wt[1024:1280])), (dv, spread(wt[1280:]))], f"swa_din_L{i}")
        gq, bq = tn_matmul(h, dq, f"swa_dwq_L{i}", colsum=True)
        gk, bk = tn_matmul(h, dk, f"swa_dwk_L{i}", colsum=True)
        gv, bv = tn_matmul(h, dv, f"swa_dwv_L{i}", colsum=True)
        grads['swa_w_qkv'][j] = jnp.concatenate([gq, gather(gk), gather(gv)], axis=1)
        grads['swa_b_qkv'][j] = jnp.concatenate([bq, gather(bk), gather(bv)], axis=1)[0]
        grads['swa_sinks'][j] = dsk[:, :4, 0].reshape(16)
        grads['norm_mix'][i] = dg[0]
        return dx

    lb_soft = lower_bound_fwd(hgrn_lb_logits, "hgrn_lb_fwd")
    lb = lb_soft[1:2]

    def hgrn_layer(xin, i, j):
        proj = norm_matmul(xin, norm_mix[i:i + 1], W['hgrn_w_in'][j], zeros_b, F32, f"hgrn_in_L{i}")
        ao, o_raw, states = hgrn_fwd(proj, lb, hgrn_g_norm[j:j + 1], f"hgrn_fwd_L{i}")
        xmid = matmul(ao, W['hgrn_w_o'][j], F32, f"hgrn_out_L{i}", res=xin)
        return xmid, (proj, ao, o_raw, states)

    def hgrn_layer_bwd(xin, saved, dmid, i, j, grads):
        proj, ao, o_raw, states = saved
        dao = matmul(dmid, W['hgrn_w_o'][j].T, BF16, f"hgrn_dout_L{i}")
        grads['hgrn_w_o'][j] = tn_matmul(ao, dmid, f"hgrn_dwo_L{i}")
        dq, dfl, dv, dgt, dlb, dgn = hgrn_bwd(proj, lb, hgrn_g_norm[j:j + 1], o_raw, states, dao, f"hgrn_bwd_L{i}")
        wt = W['hgrn_w_in'][j].T
        parts = [dq, dfl, dv, dgt]
        dx, h, dg = proj_bwd(xin, norm_mix[i:i + 1], dmid,
                             [(d, wt[n * D_MODEL:(n + 1) * D_MODEL]) for n, d in enumerate(parts)], f"hgrn_din_L{i}")
        grads['hgrn_w_in'][j] = jnp.concatenate(
            [tn_matmul(h, d, f"hgrn_dwin{n}_L{i}") for n, d in enumerate(parts)], axis=1)
        grads['hgrn_g_norm'][j] = dgn[0]
        grads['hgrn_lb_logits'] = lower_bound_bwd(lb_soft, dlb, "hgrn_lb_bwd")
        grads['norm_mix'][i] = dg[0]
        return dx

    def fox_layer(xin, i, j):
        w_in = W['fox_w_in'][j]
        b_in = W['fox_b_in'][j:j + 1]
        qkv = norm_matmul(xin, norm_mix[i:i + 1], w_in[:, :3072], b_in[:, :3072], BF16, f"fox_qkv_L{i}")
        wf = jnp.pad(w_in[:, 3072:], ((0, 0), (0, LANES - 16)))
        bf = jnp.pad(b_in[:, 3072:], ((0, 0), (0, LANES - 16)))
        fl = norm_matmul(xin, norm_mix[i:i + 1], wf, bf, F32, f"fox_f_L{i}")
        qa, ka = fox_gate_fwd(fl, qkv, f"fox_gate_L{i}")
        ao, lse = fox_fwd(qa, ka, qkv, f"fox_fwd_L{i}")
        xmid = matmul(ao, W['fox_w_o'][j], F32, f"fox_out_L{i}", res=xin)
        return xmid, (qkv, fl, qa, ka, ao, lse, wf)

    def fox_layer_bwd(xin, saved, dmid, i, j, grads):
        qkv, fl, qa, ka, ao, lse, wf = saved
        dao = matmul(dmid, W['fox_w_o'][j].T, BF16, f"fox_dout_L{i}")
        grads['fox_w_o'][j] = tn_matmul(ao, dmid, f"fox_dwo_L{i}")
        delta = fox_delta(dao, ao, f"fox_delta_L{i}")
        dq, aux_q, dk, dv, aux_k = fox_bwd(qa, ka, qkv, lse[:, ::64].T.reshape(8, 2, T),
                                           delta[:, :16].T.reshape(8, 2, T), dao, f"fox_bwd_L{i}")
        dcp = jnp.pad(aux_q[:, ::64] - aux_k[:, 3::64], ((0, 0), (0, LANES - 16)))
        dfl = fox_gate_bwd(fl, dcp, f"fox_dgate_L{i}")
        wt = W['fox_w_in'][j][:, :3072].T
        parts = [dq, dk, dv]
        dx, h, dg = proj_bwd(xin, norm_mix[i:i + 1], dmid,
                             [(d, wt[n * D_MODEL:(n + 1) * D_MODEL]) for n, d in enumerate(parts)] + [(dfl, wf.T)],
                             f"fox_din_L{i}")
        gw = [tn_matmul(h, d, f"fox_dw{n}_L{i}", colsum=True) for n, d in enumerate(parts + [dfl])]
        grads['fox_w_in'][j] = jnp.concatenate([g for g, _ in gw[:3]] + [gw[3][0][:, :16]], axis=1)
        grads['fox_b_in'][j] = jnp.concatenate([b for _, b in gw[:3]] + [gw[3][1][:, :16]], axis=1)[0]
        grads['norm_mix'][i] = dg[0]
        return dx

    mixers = [(swa_layer, swa_layer_bwd), (hgrn_layer, hgrn_layer_bwd), (fox_layer, fox_layer_bwd)]

    xs, mids, saves = [x0], [], []
    for i in range(DEPTH):
        xmid, saved = mixers[i % 3][0](xs[-1], i, i // 3)
        mids.append(xmid)
        saves.append(saved)
        xs.append(mlp_fwd(xmid, norm_mlp[i:i + 1], W['w_up'][i], W['w_down'][i], f"mlp_fwd_L{i}"))

    grads = {n: [None] * w[n].shape[0] for n in WEIGHTS if n not in ('norm_final', 'hgrn_lb_logits')}
    loss_part, dx, dgf = final_loss(xs[-1], norm_final.reshape(1, D_MODEL), tgt, "final_loss")
    grads['norm_final'] = dgf[0]
    for i in reversed(range(DEPTH)):
        dmid, h, a, du, dg = mlp_bwd(mids[i], norm_mlp[i:i + 1], W['w_up'][i], W['w_up'][i].T, W['w_down'][i].T, dx,
                                     f"mlp_bwd_L{i}")
        grads['w_up'][i] = tn_matmul(h, du, f"mlp_dwup_L{i}")
        grads['w_down'][i] = tn_matmul(a, dx, f"mlp_dwdown_L{i}")
        grads['norm_mlp'][i] = dg[0]
        dx = mixers[i % 3][1](xs[i], saves[i], dmid, i, i // 3, grads)
    gfull = {n: (g if not isinstance(g, list) else jnp.stack(g)) for n, g in grads.items()}

    small = jnp.concatenate([gfull[n].reshape(-1) for n in REPLICATED] + [loss_part[0, 0:1]])
    big = jnp.concatenate([_pieces(gfull[n], SHARD_AXIS[n]) for n in SHARDED], axis=1).astype(BF16)
    recv_big, recv_small = all_to_all_rows(
        [_pad_rows(big, 32), _pad_rows(jnp.broadcast_to(small[None], (N_DEV, small.shape[0])), 16)])
    (w_big, w_small), (m_big, m_small), (v_big, v_small) = _flat_local(w), _flat_local(mom), _flat_local(var)
    outs_big = reduce_adamw(recv_big, w_big, m_big, v_big, "reduce_adamw")
    outs_small = reduce_adamw(recv_small, w_small, m_small, v_small, "reduce_adamw_replicated")
    res = [{}, {}, {}, {}]
    for names, outs in ((SHARDED, outs_big), (REPLICATED, outs_small)):
        off = 0
        for nme in names:
            cnt = w[nme].size
            for o, r in zip(outs, res):
                r[nme] = o.reshape(-1)[off:off + cnt].reshape(w[nme].shape)
            off += cnt
    loss = outs_small[0].reshape(-1)[off]
    return (loss, dx[None], *[res[0][n] for n in WEIGHTS], *[res[1][n] for n in WEIGHTS],
            *[res[2][n] for n in WEIGHTS], *[res[3][n] for n in WEIGHTS])
```

```python
import functools

import jax
import jax.numpy as jnp
from jax import lax
from jax.experimental import pallas as pl
from jax.experimental.pallas import tpu as pltpu

F32 = jnp.float32
BF16 = jnp.bfloat16
HI = lax.Precision.HIGHEST

N_DEV = 8
D_MODEL = 1024
DEPTH = 4
EPS = 1e-6
SWA_WINDOW = 128
HGRN_CHUNK = 64
LANES = 128
VMEM_LIMIT = 56 << 20

ADAM_LR, ADAM_B1, ADAM_B2, ADAM_EPS, ADAM_WD, ADAM_STEP = 0.001, 0.9, 0.999, 1e-08, 0.01, 10

TM = 512
TF = 512
TK = 512
FOX_T = 1024
SWA_TQ = 512
HGRN_TG = 512
SCAN_T = 256

WEIGHTS = ['norm_mix', 'norm_mlp', 'norm_final', 'w_up', 'w_down', 'swa_w_qkv', 'swa_b_qkv', 'swa_sinks', 'swa_w_o',
           'hgrn_w_in', 'hgrn_lb_logits', 'hgrn_g_norm', 'hgrn_w_o', 'fox_w_in', 'fox_b_in', 'fox_w_o']
SHARD_AXIS = {'norm_mix': None, 'norm_mlp': None, 'norm_final': None, 'w_up': 2, 'w_down': 1, 'swa_w_qkv': 2,
              'swa_b_qkv': 1, 'swa_sinks': None, 'swa_w_o': 1, 'hgrn_w_in': 2, 'hgrn_lb_logits': None,
              'hgrn_g_norm': None, 'hgrn_w_o': 1, 'fox_w_in': 2, 'fox_b_in': 1, 'fox_w_o': 1}
SHARDED = [n for n in WEIGHTS if SHARD_AXIS[n] is not None]
REPLICATED = [n for n in WEIGHTS if SHARD_AXIS[n] is None]
BIASES = ('swa_b_qkv', 'fox_b_in')


def _cparams(*sem):
    return pltpu.CompilerParams(dimension_semantics=sem, vmem_limit_bytes=VMEM_LIMIT)


def _nt(a, b):
    return lax.dot_general(a, b, (((1,), (1,)), ((), ())), preferred_element_type=F32)


def _tn(a, b):
    return lax.dot_general(a, b, (((0,), (0,)), ((), ())), preferred_element_type=F32)


def _dot(a, b):
    return jnp.dot(a, b, preferred_element_type=F32)


def _sigmoid(x):
    return 1.0 / (1.0 + jnp.exp(-x))


def _rms(xv):
    return lax.rsqrt(jnp.mean(xv * xv, axis=-1, keepdims=True) + EPS)


def _rms_bwd(xv, g, dh):
    r = _rms(xv)
    xhat = xv * r
    dhg = dh * g
    dx = r * (dhg - xhat * jnp.mean(dhg * xhat, axis=-1, keepdims=True))
    return dx, jnp.sum(dh * xhat, axis=0, keepdims=True)


def _my_id():
    return lax.axis_index("x"), lax.axis_index("y"), lax.axis_index("c")


def _peer(x, y, c, k):
    return (lax.rem(x + ((k >> 2) & 1), 2), lax.rem(y + ((k >> 1) & 1), 2), lax.rem(c + (k & 1), 2))


def all_gather_rows(local):
    def body(x_ref, o_ref, send_sems, recv_sems, loc_sem):
        x, y, c = _my_id()
        me = 4 * x + 2 * y + c
        mine = pltpu.make_async_copy(x_ref, o_ref.at[me], loc_sem)
        mine.start()
        copies = []
        for k in range(1, N_DEV):
            px, py, pc = _peer(x, y, c, k)
            cp = pltpu.make_async_remote_copy(
                src_ref=x_ref, dst_ref=o_ref.at[me], send_sem=send_sems.at[k - 1], recv_sem=recv_sems.at[k - 1],
                device_id=(px, py, pc), device_id_type=pl.DeviceIdType.MESH)
            cp.start()
            copies.append(cp)
        for cp in copies:
            cp.wait()
        mine.wait()

    return pl.pallas_call(
        body, name="all_gather_weights",
        out_shape=jax.ShapeDtypeStruct((N_DEV,) + local.shape, local.dtype),
        in_specs=[pl.BlockSpec(memory_space=pl.ANY)],
        out_specs=pl.BlockSpec(memory_space=pl.ANY),
        scratch_shapes=[pltpu.SemaphoreType.DMA((N_DEV - 1,)), pltpu.SemaphoreType.DMA((N_DEV - 1,)),
                        pltpu.SemaphoreType.DMA],
    )(local)


def all_to_all_rows(sends):
    n = len(sends)

    def body(*refs):
        s_refs, r_refs = refs[:n], refs[n:2 * n]
        send_sems, recv_sems, loc_sems = refs[2 * n:]
        x, y, c = _my_id()
        me = 4 * x + 2 * y + c
        copies = []
        for a, (s_ref, r_ref) in enumerate(zip(s_refs, r_refs)):
            mine = pltpu.make_async_copy(s_ref.at[me], r_ref.at[me], loc_sems.at[a])
            mine.start()
            copies.append(mine)
            for k in range(1, N_DEV):
                px, py, pc = _peer(x, y, c, k)
                sem = a * (N_DEV - 1) + k - 1
                cp = pltpu.make_async_remote_copy(
                    src_ref=s_ref.at[4 * px + 2 * py + pc], dst_ref=r_ref.at[me],
                    send_sem=send_sems.at[sem], recv_sem=recv_sems.at[sem],
                    device_id=(px, py, pc), device_id_type=pl.DeviceIdType.MESH)
                cp.start()
                copies.append(cp)
        for cp in copies:
            cp.wait()

    hbm = pl.BlockSpec(memory_space=pl.ANY)
    return pl.pallas_call(
        body, name="all_to_all_grads",
        out_shape=[jax.ShapeDtypeStruct(s.shape, s.dtype) for s in sends],
        in_specs=[hbm] * n, out_specs=[hbm] * n,
        scratch_shapes=[pltpu.SemaphoreType.DMA((n * (N_DEV - 1),)), pltpu.SemaphoreType.DMA((n * (N_DEV - 1),)),
                        pltpu.SemaphoreType.DMA((n,))],
    )(*sends)


def reduce_adamw(recv, w, m, v, name):
    R = w.shape[0]
    tr = max(t for t in range(16, 257, 16) if R % t == 0)
    c1 = 1.0 / (1.0 - ADAM_B1 ** ADAM_STEP)
    c2 = 1.0 / (1.0 - ADAM_B2 ** ADAM_STEP)

    def body(r_ref, w_ref, m_ref, v_ref, g_ref, d_ref, nm_ref, nv_ref):
        g = r_ref[0].astype(F32)
        for s in range(1, N_DEV):
            g = g + r_ref[s].astype(F32)
        m2 = ADAM_B1 * m_ref[...] + (1.0 - ADAM_B1) * g
        v2 = ADAM_B2 * v_ref[...] + (1.0 - ADAM_B2) * (g * g)
        g_ref[...] = g
        nm_ref[...] = m2
        nv_ref[...] = v2
        d_ref[...] = -ADAM_LR * ((m2 * c1) / (jnp.sqrt(v2 * c2) + ADAM_EPS) + ADAM_WD * w_ref[...])

    row = pl.BlockSpec((tr, D_MODEL), lambda i: (i, 0))
    shp = jax.ShapeDtypeStruct((R, D_MODEL), F32)
    return pl.pallas_call(
        body, name=name, grid=(R // tr,),
        in_specs=[pl.BlockSpec((N_DEV, tr, D_MODEL), lambda i: (0, i, 0)), row, row, row],
        out_specs=[row, row, row, row], out_shape=[shp, shp, shp, shp],
        compiler_params=_cparams("parallel"),
    )(recv, w, m, v)


def norm_matmul(x, g, w, b, out_dtype, name):
    T, N = x.shape[0], w.shape[1]
    tm, tn = min(TM, T), min(512, N)

    def body(x_ref, g_ref, w_ref, b_ref, o_ref, h_sc):
        @pl.when(pl.program_id(1) == 0)
        def _():
            xv = x_ref[...]
            h_sc[...] = (xv * _rms(xv) * g_ref[...]).astype(BF16)
        o_ref[...] = (_dot(h_sc[...], w_ref[...]) + b_ref[...]).astype(o_ref.dtype)

    return pl.pallas_call(
        body, name=name, grid=(T // tm, N // tn),
        in_specs=[pl.BlockSpec((tm, D_MODEL), lambda i, j: (i, 0)), pl.BlockSpec((1, D_MODEL), lambda i, j: (0, 0)),
                  pl.BlockSpec((D_MODEL, tn), lambda i, j: (0, j)), pl.BlockSpec((1, tn), lambda i, j: (0, j))],
        out_specs=pl.BlockSpec((tm, tn), lambda i, j: (i, j)),
        out_shape=jax.ShapeDtypeStruct((T, N), out_dtype),
        scratch_shapes=[pltpu.VMEM((tm, D_MODEL), BF16)],
        compiler_params=_cparams("parallel", "arbitrary"),
    )(x, g, w, b)


def matmul(a, w, out_dtype, name, res=None):
    T, K = a.shape
    N = w.shape[1]
    tm = min(TM, T)

    def body(*refs):
        if res is None:
            a_ref, w_ref, o_ref = refs
            acc = _dot(a_ref[...].astype(BF16), w_ref[...])
        else:
            a_ref, w_ref, r_ref, o_ref = refs
            acc = r_ref[...] + _dot(a_ref[...].astype(BF16), w_ref[...])
        o_ref[...] = acc.astype(o_ref.dtype)

    in_specs = [pl.BlockSpec((tm, K), lambda i: (i, 0)), pl.BlockSpec((K, N), lambda i: (0, 0))]
    ops = [a, w]
    if res is not None:
        in_specs.append(pl.BlockSpec((tm, N), lambda i: (i, 0)))
        ops.append(res)
    return pl.pallas_call(
        body, name=name, grid=(T // tm,), in_specs=in_specs,
        out_specs=pl.BlockSpec((tm, N), lambda i: (i, 0)),
        out_shape=jax.ShapeDtypeStruct((T, N), out_dtype),
        compiler_params=_cparams("parallel"),
    )(*ops)


def tn_matmul(a, b, name, colsum=False):
    T, M = a.shape
    N = b.shape[1]
    tk = min(TK, T)
    tmm = min(1024, M)
    tn = N if N <= 1024 else (1024 if N % 1024 == 0 else N)

    def body(a_ref, b_ref, o_ref, *rest):
        k = pl.program_id(2)
        bv = b_ref[...]

        @pl.when(k == 0)
        def _():
            o_ref[...] = jnp.zeros(o_ref.shape, F32)
            if colsum:
                rest[0][...] = jnp.zeros(rest[0].shape, F32)

        o_ref[...] += _tn(a_ref[...].astype(BF16), bv.astype(BF16))
        if colsum:
            rest[0][...] += jnp.sum(bv.astype(F32), axis=0, keepdims=True)

    out_specs = [pl.BlockSpec((tmm, tn), lambda i, j, k: (i, j))]
    out_shape = [jax.ShapeDtypeStruct((M, N), F32)]
    if colsum:
        assert M == tmm
        out_specs.append(pl.BlockSpec((1, tn), lambda i, j, k: (0, j)))
        out_shape.append(jax.ShapeDtypeStruct((1, N), F32))
    out = pl.pallas_call(
        body, name=name, grid=(M // tmm, N // tn, T // tk),
        in_specs=[pl.BlockSpec((tk, tmm), lambda i, j, k: (k, i)), pl.BlockSpec((tk, tn), lambda i, j, k: (k, j))],
        out_specs=out_specs, out_shape=out_shape,
        compiler_params=_cparams("parallel", "parallel", "arbitrary"),
    )(a, b)
    return out if colsum else out[0]


def mlp_fwd(x, g, w_up, w_down, name):
    T, F = x.shape[0], w_up.shape[1]
    tm, tf = min(TM, T), min(TF, F)
    nf = F // tf

    def body(x_ref, g_ref, wu_ref, wd_ref, o_ref, h_sc, acc_sc):
        f = pl.program_id(1)

        @pl.when(f == 0)
        def _():
            xv = x_ref[...]
            h_sc[...] = (xv * _rms(xv) * g_ref[...]).astype(BF16)
            acc_sc[...] = xv

        u = jnp.maximum(_dot(h_sc[...], wu_ref[...]), 0.0)
        acc_sc[...] += _dot((u * u).astype(BF16), wd_ref[...])

        @pl.when(f == nf - 1)
        def _():
            o_ref[...] = acc_sc[...]

    return pl.pallas_call(
        body, name=name, grid=(T // tm, nf),
        in_specs=[pl.BlockSpec((tm, D_MODEL), lambda i, f: (i, 0)), pl.BlockSpec((1, D_MODEL), lambda i, f: (0, 0)),
                  pl.BlockSpec((D_MODEL, tf), lambda i, f: (0, f)), pl.BlockSpec((tf, D_MODEL), lambda i, f: (f, 0))],
        out_specs=pl.BlockSpec((tm, D_MODEL), lambda i, f: (i, 0)),
        out_shape=jax.ShapeDtypeStruct((T, D_MODEL), F32),
        scratch_shapes=[pltpu.VMEM((tm, D_MODEL), BF16), pltpu.VMEM((tm, D_MODEL), F32)],
        compiler_params=_cparams("parallel", "arbitrary"),
    )(x, g, w_up, w_down)


def mlp_bwd(x, g, w_up, w_up_t, w_down_t, dy, name):
    T, F = x.shape[0], w_up.shape[1]
    tm, tf = min(TM, T), min(TF, F)
    nf = F // tf

    def body(x_ref, g_ref, wu_ref, wut_ref, wdt_ref, dy_ref, dx_ref, h_ref, a_ref, du_ref, dg_ref, h_sc, dyb_sc, dh_sc):
        i, f = pl.program_id(0), pl.program_id(1)

        @pl.when(f == 0)
        def _():
            xv = x_ref[...]
            h = (xv * _rms(xv) * g_ref[...]).astype(BF16)
            h_sc[...] = h
            h_ref[...] = h
            dyb_sc[...] = dy_ref[...].astype(BF16)
            dh_sc[...] = jnp.zeros(dh_sc.shape, F32)

        @pl.when((i == 0) & (f == 0))
        def _():
            dg_ref[...] = jnp.zeros(dg_ref.shape, F32)

        u = jnp.maximum(_dot(h_sc[...], wu_ref[...]), 0.0)
        a_ref[...] = (u * u).astype(BF16)
        du = (_dot(dyb_sc[...], wdt_ref[...]) * (2.0 * u)).astype(BF16)
        du_ref[...] = du
        dh_sc[...] += _dot(du, wut_ref[...])

        @pl.when(f == nf - 1)
        def _():
            dx, dg = _rms_bwd(x_ref[...], g_ref[...], dh_sc[...])
            dx_ref[...] = dy_ref[...] + dx
            dg_ref[...] += dg

    row = pl.BlockSpec((tm, D_MODEL), lambda i, f: (i, 0))
    hid = pl.BlockSpec((tm, tf), lambda i, f: (i, f))
    return pl.pallas_call(
        body, name=name, grid=(T // tm, nf),
        in_specs=[row, pl.BlockSpec((1, D_MODEL), lambda i, f: (0, 0)),
                  pl.BlockSpec((D_MODEL, tf), lambda i, f: (0, f)), pl.BlockSpec((tf, D_MODEL), lambda i, f: (f, 0)),
                  pl.BlockSpec((D_MODEL, tf), lambda i, f: (0, f)), row],
        out_specs=[row, row, hid, hid, pl.BlockSpec((1, D_MODEL), lambda i, f: (0, 0))],
        out_shape=[jax.ShapeDtypeStruct((T, D_MODEL), F32), jax.ShapeDtypeStruct((T, D_MODEL), BF16),
                   jax.ShapeDtypeStruct((T, F), BF16), jax.ShapeDtypeStruct((T, F), BF16),
                   jax.ShapeDtypeStruct((1, D_MODEL), F32)],
        scratch_shapes=[pltpu.VMEM((tm, D_MODEL), BF16), pltpu.VMEM((tm, D_MODEL), BF16),
                        pltpu.VMEM((tm, D_MODEL), F32)],
        compiler_params=_cparams("arbitrary", "arbitrary"),
    )(x, g, w_up, w_up_t, w_down_t, dy)


def proj_bwd(x, g, dres, parts, name):
    T = x.shape[0]
    tm = min(TM, T)
    n = len(parts)

    def body(*refs):
        x_ref, g_ref, dr_ref = refs[:3]
        da_refs, wt_refs = refs[3:3 + n], refs[3 + n:3 + 2 * n]
        dx_ref, h_ref, dg_ref = refs[3 + 2 * n:]

        @pl.when(pl.program_id(0) == 0)
        def _():
            dg_ref[...] = jnp.zeros(dg_ref.shape, F32)

        xv = x_ref[...]
        dh = _dot(da_refs[0][...].astype(BF16), wt_refs[0][...])
        for a_ref, w_ref in zip(da_refs[1:], wt_refs[1:]):
            dh = dh + _dot(a_ref[...].astype(BF16), w_ref[...])
        h_ref[...] = (xv * _rms(xv) * g_ref[...]).astype(BF16)
        dx, dg = _rms_bwd(xv, g_ref[...], dh)
        dx_ref[...] = dr_ref[...] + dx
        dg_ref[...] += dg

    row = pl.BlockSpec((tm, D_MODEL), lambda i: (i, 0))
    one = pl.BlockSpec((1, D_MODEL), lambda i: (0, 0))
    in_specs = [row, one, row]
    in_specs += [pl.BlockSpec((tm, da.shape[1]), lambda i: (i, 0)) for da, _ in parts]
    in_specs += [pl.BlockSpec(wt.shape, lambda i: (0, 0)) for _, wt in parts]
    return pl.pallas_call(
        body, name=name, grid=(T // tm,), in_specs=in_specs,
        out_specs=[row, row, one],
        out_shape=[jax.ShapeDtypeStruct((T, D_MODEL), F32), jax.ShapeDtypeStruct((T, D_MODEL), BF16),
                   jax.ShapeDtypeStruct((1, D_MODEL), F32)],
        compiler_params=_cparams("arbitrary"),
    )(x, g, dres, *[da for da, _ in parts], *[wt for _, wt in parts])


def final_loss(x, g, tgt, name):
    T = x.shape[0]
    tm = min(TM, T)

    def body(x_ref, g_ref, t_ref, l_ref, dx_ref, dg_ref):
        @pl.when(pl.program_id(0) == 0)
        def _():
            l_ref[...] = jnp.zeros(l_ref.shape, F32)
            dg_ref[...] = jnp.zeros(dg_ref.shape, F32)

        xv = x_ref[...]
        gv = g_ref[...]
        err = xv * _rms(xv) * gv - t_ref[...]
        l_ref[...] += 0.5 * jnp.sum(jnp.mean(err * err, axis=-1, keepdims=True), axis=0, keepdims=True)
        dx, dg = _rms_bwd(xv, gv, err * (1.0 / D_MODEL))
        dx_ref[...] = dx
        dg_ref[...] += dg

    row = pl.BlockSpec((tm, D_MODEL), lambda i: (i, 0))
    one = pl.BlockSpec((1, D_MODEL), lambda i: (0, 0))
    return pl.pallas_call(
        body, name=name, grid=(T // tm,), in_specs=[row, one, row],
        out_specs=[pl.BlockSpec((8, LANES), lambda i: (0, 0)), row, one],
        out_shape=[jax.ShapeDtypeStruct((8, LANES), F32), jax.ShapeDtypeStruct((T, D_MODEL), F32),
                   jax.ShapeDtypeStruct((1, D_MODEL), F32)],
        compiler_params=_cparams("arbitrary"),
    )(x, g, tgt)


def _swa_specs(tq):
    r = tq // SWA_WINDOW
    cur = lambda ix: pl.BlockSpec((tq, LANES), lambda kv, i: (ix(i), kv))
    prev = lambda ix: pl.BlockSpec((SWA_WINDOW, LANES), lambda kv, i: (jnp.maximum(ix(i) * r - 1, 0), kv))
    return cur, prev


W2 = 2 * SWA_WINDOW
SWA_RB = 32


def _swa_visible(tile):
    r = lax.broadcasted_iota(jnp.int32, (SWA_WINDOW, W2), 0)
    c = lax.broadcasted_iota(jnp.int32, (SWA_WINDOW, W2), 1)
    inside = (c > r) & (c <= r + SWA_WINDOW)
    return inside & ((c >= SWA_WINDOW) | (tile > 0)), inside


def swa_fwd(qkv, kdup, vdup, sinks_b, name):
    T = qkv.shape[0]
    tq = min(SWA_TQ, T)
    nsub = tq // SWA_WINDOW
    cur, prev = _swa_specs(tq)
    ident = lambda i: i

    def body(q_ref, kc_ref, kp_ref, vc_ref, vp_ref, sk_ref, o_ref, lse_ref, s_sc, e_sc):
        i = pl.program_id(1)
        kcat = jnp.concatenate([kp_ref[...], kc_ref[...]], axis=0)
        vcat = jnp.concatenate([vp_ref[...], vc_ref[...]], axis=0)
        vis_first, vis_in = _swa_visible(i)
        lane = lax.broadcasted_iota(jnp.int32, (1, LANES), 1)
        lse_all = jnp.zeros((tq, LANES), F32)
        for pp in range(2):
            q2 = q_ref[:, pp * LANES:(pp + 1) * LANES]
            outs = []
            for hf in range(2):
                g = 2 * pp + hf
                qm = jnp.where(_half(lane, hf), q2, jnp.zeros_like(q2))
                for nb in range(nsub):
                    rows = slice(nb * SWA_WINDOW, (nb + 1) * SWA_WINDOW)
                    s = _nt(qm[rows], kcat[nb * SWA_WINDOW:nb * SWA_WINDOW + W2]) * 0.125
                    s_sc[rows, :] = jnp.where(vis_first if nb == 0 else vis_in, s, -1e30)
                sk = sk_ref[g:g + 1, 0:1]
                m = jnp.maximum(jnp.max(s_sc[...], axis=1, keepdims=True), sk)
                m_rep = jnp.broadcast_to(m, (tq, LANES))
                parts = []
                for r0 in range(0, tq, SWA_RB):
                    rs = slice(r0, r0 + SWA_RB)
                    e0 = jnp.exp(s_sc[rs, 0:LANES] - m_rep[rs])
                    e1 = jnp.exp(s_sc[rs, LANES:W2] - m_rep[rs])
                    e_sc[rs, :] = jnp.concatenate([e0.astype(BF16), e1.astype(BF16)], axis=1)
                    parts.append(e0 + e1)
                den = jnp.sum(jnp.concatenate(parts, axis=0), axis=1, keepdims=True) + jnp.exp(sk - m)
                pv = [_dot(e_sc[nb * SWA_WINDOW:(nb + 1) * SWA_WINDOW, :], vcat[nb * SWA_WINDOW:nb * SWA_WINDOW + W2])
                      for nb in range(nsub)]
                outs.append(jnp.concatenate(pv, axis=0) * (1.0 / den))
                lse_all = jnp.where(lane == g, m + jnp.log(den), lse_all)
            o_ref[:, pp * LANES:(pp + 1) * LANES] = jnp.where(lane < 64, outs[0], outs[1]).astype(BF16)
        lse_ref[...] = lse_all

    return pl.pallas_call(
        body, name=name, grid=(4, T // tq),
        in_specs=[pl.BlockSpec((tq, 2 * LANES), lambda kv, i: (i, kv)), cur(ident), prev(ident), cur(ident), prev(ident),
                  pl.BlockSpec((None, 8, LANES), lambda kv, i: (kv, 0, 0))],
        out_specs=[pl.BlockSpec((tq, 2 * LANES), lambda kv, i: (i, kv)), cur(ident)],
        out_shape=[jax.ShapeDtypeStruct((T, D_MODEL), BF16), jax.ShapeDtypeStruct((T, 4 * LANES), F32)],
        scratch_shapes=[pltpu.VMEM((tq, W2), F32), pltpu.VMEM((tq, W2), BF16)],
        compiler_params=_cparams("parallel", "arbitrary"),
    )(qkv, kdup, kdup, vdup, vdup, sinks_b)


def swa_bwd(qkv, kdup, vdup, sinks_b, o, lse, do, name):
    T = qkv.shape[0]
    tq = min(SWA_TQ, T)
    n = T // tq
    nsub = tq // SWA_WINDOW
    cur, prev = _swa_specs(tq)
    rev = lambda i: n - 1 - i

    def body(q_ref, kc_ref, kp_ref, vc_ref, vp_ref, sk_ref, o_ref, lse_ref, do_ref, dq_ref, dk_ref, dv_ref, dsk_ref,
             ck_sc, cv_sc, dkc_sc, dvc_sc):
        i = pl.program_id(1)

        @pl.when(i == 0)
        def _():
            ck_sc[...] = jnp.zeros(ck_sc.shape, F32)
            cv_sc[...] = jnp.zeros(cv_sc.shape, F32)
            dsk_ref[...] = jnp.zeros(dsk_ref.shape, F32)

        kcat = jnp.concatenate([kp_ref[...], kc_ref[...]], axis=0)
        vcat = jnp.concatenate([vp_ref[...], vc_ref[...]], axis=0)
        vis_first, vis_in = _swa_visible(n - 1 - i)
        lane = lax.broadcasted_iota(jnp.int32, (1, LANES), 1)
        dkc_sc[...] = jnp.zeros(dkc_sc.shape, F32)
        dvc_sc[...] = jnp.zeros(dvc_sc.shape, F32)
        for pp in range(2):
            sl = slice(pp * LANES, (pp + 1) * LANES)
            q2, do2, o2 = q_ref[:, sl], do_ref[:, sl], o_ref[:, sl]
            dqs = []
            for hf in range(2):
                g = 2 * pp + hf
                lm = _half(lane, hf)
                qm = jnp.where(lm, q2, jnp.zeros_like(q2))
                dom = jnp.where(lm, do2, jnp.zeros_like(do2))
                delta = jnp.sum(dom.astype(F32) * o2.astype(F32), axis=1, keepdims=True)
                lse_g = lse_ref[:, g:g + 1]
                psk = jnp.exp(sk_ref[g:g + 1, 0:1] - lse_g)
                dsk_ref[g:g + 1, :] += jnp.zeros((1, LANES), F32) - jnp.sum(psk * delta, axis=0, keepdims=True)
                dq_parts = []
                for nb in range(nsub):
                    rows = slice(nb * SWA_WINDOW, (nb + 1) * SWA_WINDOW)
                    band = slice(nb * SWA_WINDOW, nb * SWA_WINDOW + W2)
                    s = jnp.where(vis_first if nb == 0 else vis_in, _nt(qm[rows], kcat[band]) * 0.125, -1e30)
                    p = jnp.exp(s - lse_g[rows])
                    dsb = (p * (_nt(dom[rows], vcat[band]) - delta[rows]) * 0.125).astype(BF16)
                    dq_parts.append(_dot(dsb, kcat[band]))
                    dkc_sc[band, :] += _tn(dsb, qm[rows])
                    dvc_sc[band, :] += _tn(p.astype(BF16), dom[rows])
                dqs.append(jnp.concatenate(dq_parts, axis=0))
            dq_ref[:, sl] = jnp.where(lane < 64, dqs[0], dqs[1]).astype(BF16)
        dkc = dkc_sc[...]
        dvc = dvc_sc[...]
        dkc = dkc + pltpu.roll(dkc, 64, 1)
        dvc = dvc + pltpu.roll(dvc, 64, 1)
        for full, ref, carry in ((dkc, dk_ref, ck_sc), (dvc, dv_ref, cv_sc)):
            if tq > SWA_WINDOW:
                ref[0:tq - SWA_WINDOW, :] = full[SWA_WINDOW:tq, :]
            ref[tq - SWA_WINDOW:tq, :] = full[tq:tq + SWA_WINDOW, :] + carry[...]
            carry[...] = full[0:SWA_WINDOW, :]

    wide = pl.BlockSpec((tq, 2 * LANES), lambda kv, i: (rev(i), kv))
    return pl.pallas_call(
        body, name=name, grid=(4, n),
        in_specs=[wide, cur(rev), prev(rev), cur(rev), prev(rev),
                  pl.BlockSpec((None, 8, LANES), lambda kv, i: (kv, 0, 0)), wide, cur(rev), wide],
        out_specs=[wide, cur(rev), cur(rev), pl.BlockSpec((None, 8, LANES), lambda kv, i: (kv, 0, 0))],
        out_shape=[jax.ShapeDtypeStruct((T, D_MODEL), BF16), jax.ShapeDtypeStruct((T, 4 * LANES), F32),
                   jax.ShapeDtypeStruct((T, 4 * LANES), F32), jax.ShapeDtypeStruct((4, 8, LANES), F32)],
        scratch_shapes=[pltpu.VMEM((SWA_WINDOW, LANES), F32), pltpu.VMEM((SWA_WINDOW, LANES), F32),
                        pltpu.VMEM((tq + SWA_WINDOW, LANES), F32), pltpu.VMEM((tq + SWA_WINDOW, LANES), F32)],
        compiler_params=_cparams("arbitrary", "arbitrary"),
    )(qkv, kdup, kdup, vdup, vdup, sinks_b, o, lse, do)


def fox_gate_fwd(fl, qkv, name):
    T = fl.shape[0]
    ts = min(SCAN_T, T)

    def body(fl_ref, q_ref, k_ref, qa_ref, ka_ref, carry):
        @pl.when(pl.program_id(0) == 0)
        def _():
            carry[...] = jnp.zeros(carry.shape, F32)

        xv = fl_ref[...]
        ls = jnp.minimum(xv, 0.0) - jnp.log(1.0 + jnp.exp(-jnp.abs(xv)))
        tri = (lax.broadcasted_iota(jnp.int32, (ts, ts), 0) >= lax.broadcasted_iota(jnp.int32, (ts, ts), 1)).astype(F32)
        cs = jnp.dot(tri, ls, precision=HI, preferred_element_type=F32) + carry[...]
        carry[...] = cs[ts - 1:ts, :]
        c1 = cs.astype(BF16).astype(F32)
        c2 = (cs - c1).astype(BF16).astype(F32)
        c3 = (cs - c1 - c2).astype(BF16).astype(F32)
        lane = lax.broadcasted_iota(jnp.int32, (1, LANES), 1)
        ones_q = jnp.where((lane >= 67) & (lane < 70), 1.0, 0.0)
        ones_k = jnp.where((lane >= 64) & (lane < 67), 1.0, 0.0)
        for b in range(8):
            qf = q_ref[:, b * LANES:(b + 1) * LANES].astype(F32) * 0.125
            kf = k_ref[:, b * LANES:(b + 1) * LANES].astype(F32)
            for hf in range(2):
                h = 2 * b + hf
                a1, a2, a3 = c1[:, h:h + 1], c2[:, h:h + 1], c3[:, h:h + 1]
                aux_q = jnp.where(lane == 64, a1, jnp.where(lane == 65, a2, jnp.where(lane == 66, a3, ones_q)))
                aux_k = jnp.where(lane == 67, -a1, jnp.where(lane == 68, -a2, jnp.where(lane == 69, -a3, ones_k)))
                qs = qf if hf == 0 else pltpu.roll(qf, 64, 1)
                ks = kf if hf == 0 else pltpu.roll(kf, 64, 1)
                qa_ref[:, h * LANES:(h + 1) * LANES] = jnp.where(lane < 64, qs, aux_q).astype(BF16)
                ka_ref[:, h * LANES:(h + 1) * LANES] = jnp.where(lane < 64, ks, aux_k).astype(BF16)

    out = pl.BlockSpec((ts, 16 * LANES), lambda i: (i, 0))
    return pl.pallas_call(
        body, name=name, grid=(T // ts,),
        in_specs=[pl.BlockSpec((ts, LANES), lambda i: (i, 0)), pl.BlockSpec((ts, D_MODEL), lambda i: (i, 0)),
                  pl.BlockSpec((ts, D_MODEL), lambda i: (i, 1))],
        out_specs=[out, out],
        out_shape=[jax.ShapeDtypeStruct((T, 16 * LANES), BF16), jax.ShapeDtypeStruct((T, 16 * LANES), BF16)],
        scratch_shapes=[pltpu.VMEM((1, LANES), F32)],
        compiler_params=_cparams("arbitrary"),
    )(fl, qkv, qkv)


def fox_gate_bwd(fl, dc, name):
    T = fl.shape[0]
    ts = min(SCAN_T, T)
    n = T // ts

    def body(fl_ref, dc_ref, o_ref, carry):
        @pl.when(pl.program_id(0) == 0)
        def _():
            carry[...] = jnp.zeros(carry.shape, F32)

        tri = (lax.broadcasted_iota(jnp.int32, (ts, ts), 0) <= lax.broadcasted_iota(jnp.int32, (ts, ts), 1)).astype(F32)
        rs = jnp.dot(tri, dc_ref[...], precision=HI, preferred_element_type=F32) + carry[...]
        carry[...] = rs[0:1, :]
        o_ref[...] = rs * (1.0 / (1.0 + jnp.exp(fl_ref[...])))

    blk = pl.BlockSpec((ts, LANES), lambda i: (n - 1 - i, 0))
    return pl.pallas_call(
        body, name=name, grid=(n,), in_specs=[blk, blk], out_specs=blk,
        out_shape=jax.ShapeDtypeStruct((T, LANES), F32), scratch_shapes=[pltpu.VMEM((1, LANES), F32)],
        compiler_params=_cparams("arbitrary"),
    )(fl, dc)


FOX_RB = 32


def _half(lane, hf):
    return (lane < 64) if hf == 0 else (lane >= 64)


def _pair(lane, a, b):
    return jnp.where(lane < 64, a, pltpu.roll(b, 64, 1)), jnp.where(lane < 64, pltpu.roll(a, 64, 1), b)


def fox_fwd(qa, ka, qkv, name):
    T = qa.shape[0]
    t = min(FOX_T, T)
    n = T // t

    def body(qa_ref, ka_ref, v_ref, o_ref, lse_ref, m_sc, l_sc, acc_sc, ls_sc, s_sc, p_sc):
        i, j = pl.program_id(1), pl.program_id(2)
        lane = lax.broadcasted_iota(jnp.int32, (1, LANES), 1)

        @pl.when(j == 0)
        def _():
            m_sc[...] = jnp.full(m_sc.shape, -1e30, F32)
            l_sc[...] = jnp.zeros(l_sc.shape, F32)
            acc_sc[...] = jnp.zeros(acc_sc.shape, F32)

        def tile(diag):
            v2 = v_ref[...]
            for hf in range(2):
                hs = slice(hf * LANES, (hf + 1) * LANES)
                sv = _nt(qa_ref[:, hs], ka_ref[:, hs])
                if diag:
                    vis = lax.broadcasted_iota(jnp.int32, (t, t), 0) >= lax.broadcasted_iota(jnp.int32, (t, t), 1)
                    sv = jnp.where(vis, sv, -1e30)
                s_sc[...] = sv
                m_old = m_sc[hf]
                m_new = jnp.maximum(m_old, jnp.max(s_sc[...], axis=1, keepdims=True))
                al = jnp.exp(m_old - m_new)
                m_sc[hf] = m_new
                for r0 in range(0, t, FOX_RB):
                    rs = slice(r0, r0 + FOX_RB)
                    mrow = m_new[rs, :]
                    part, pieces = None, []
                    for cb in range(0, t, LANES):
                        pc = jnp.exp(s_sc[rs, cb:cb + LANES] - mrow)
                        part = pc if part is None else part + pc
                        pieces.append(pc.astype(BF16))
                    p_sc[rs, :] = jnp.concatenate(pieces, axis=1)
                    ls_sc[rs, :] = part
                l_sc[hf] = al * l_sc[hf] + ls_sc[...]
                acc_sc[hf] = al * acc_sc[hf] + _dot(p_sc[...], v2)

        @pl.when(j < i)
        def _():
            tile(False)

        @pl.when(j == i)
        def _():
            tile(True)
            l0 = jnp.sum(l_sc[0], axis=1, keepdims=True)
            l1 = jnp.sum(l_sc[1], axis=1, keepdims=True)
            o_ref[...] = jnp.where(lane < 64, acc_sc[0] / l0, acc_sc[1] / l1).astype(BF16)
            lse_ref[...] = jnp.where(lane < 64, m_sc[0] + jnp.log(l0), m_sc[1] + jnp.log(l1))

    oblk = pl.BlockSpec((t, LANES), lambda p, i, j: (i, p))
    return pl.pallas_call(
        body, name=name, grid=(8, n, n),
        in_specs=[pl.BlockSpec((t, 2 * LANES), lambda p, i, j: (i, p)),
                  pl.BlockSpec((t, 2 * LANES), lambda p, i, j: (jnp.minimum(j, i), p)),
                  pl.BlockSpec((t, LANES), lambda p, i, j: (jnp.minimum(j, i), 16 + p))],
        out_specs=[oblk, oblk],
        out_shape=[jax.ShapeDtypeStruct((T, D_MODEL), BF16), jax.ShapeDtypeStruct((T, D_MODEL), F32)],
        scratch_shapes=[pltpu.VMEM((2, t, LANES), F32), pltpu.VMEM((2, t, LANES), F32), pltpu.VMEM((2, t, LANES), F32),
                        pltpu.VMEM((t, LANES), F32), pltpu.VMEM((t, t), F32), pltpu.VMEM((t, t), BF16)],
        compiler_params=_cparams("parallel", "parallel", "arbitrary"),
    )(qa, ka, qkv)


def fox_delta(do, o, name):
    T = do.shape[0]
    tm = min(TM, T)

    def body(do_ref, o_ref, d_ref):
        lane = lax.broadcasted_iota(jnp.int32, (1, LANES), 1)
        out = jnp.zeros((tm, LANES), F32)
        for b in range(8):
            d = do_ref[:, b * LANES:(b + 1) * LANES].astype(F32) * o_ref[:, b * LANES:(b + 1) * LANES].astype(F32)
            for hf in range(2):
                out = jnp.where(lane == 2 * b + hf, jnp.sum(jnp.where(_half(lane, hf), d, 0.0), axis=1, keepdims=True), out)
        d_ref[...] = out

    row = pl.BlockSpec((tm, D_MODEL), lambda i: (i, 0))
    return pl.pallas_call(
        body, name=name, grid=(T // tm,), in_specs=[row, row],
        out_specs=pl.BlockSpec((tm, LANES), lambda i: (i, 0)),
        out_shape=jax.ShapeDtypeStruct((T, LANES), F32),
        compiler_params=_cparams("parallel"),
    )(do, o)


def fox_bwd(qa, ka, qkv, lse_row, delta_row, do, name):
    T = qa.shape[0]
    t = min(FOX_T, T)
    n = T // t

    def body(qa_ref, ka_ref, v_ref, lr_ref, dr_ref, do_ref, dq_ref, auxq_ref, dk_ref, dv_ref, aux_ref,
             dq_sc, dk_sc, dv_sc, s_sc, dp_sc, p_sc, ds_sc, dqo_sc, auxo_sc, out_sems):
        j, i = pl.program_id(1), pl.program_id(2)
        lane = lax.broadcasted_iota(jnp.int32, (1, LANES), 1)
        qrows = pl.ds(pl.multiple_of(i * t, t), t)

        @pl.when(i == 0)
        def _():
            dk_sc[...] = jnp.zeros(dk_sc.shape, F32)
            dv_sc[...] = jnp.zeros(dv_sc.shape, F32)

        @pl.when(j == 0)
        def _():
            dq_sc[:, qrows, :] = jnp.zeros((2, t, LANES), F32)

        def tile(diag):
            v2, do2 = v_ref[...], do_ref[...]
            for hf in range(2):
                hs = slice(hf * LANES, (hf + 1) * LANES)
                lm = _half(lane, hf)
                qh = qa_ref[:, hs]
                s_sc[...] = _nt(ka_ref[:, hs], qh)
                dp_sc[...] = _nt(jnp.where(lm, v2, jnp.zeros_like(v2)), do2)
                lrow, drow = lr_ref[hf:hf + 1, :], dr_ref[hf:hf + 1, :]
                for r0 in range(0, t, FOX_RB):
                    rs = slice(r0, r0 + FOX_RB)
                    sv = s_sc[rs, :]
                    if diag:
                        vis = lax.broadcasted_iota(jnp.int32, (FOX_RB, t), 1) >= (r0 + lax.broadcasted_iota(jnp.int32, (FOX_RB, t), 0))
                        sv = jnp.where(vis, sv, -1e30)
                    p = jnp.exp(sv - lrow)
                    p_sc[rs, :] = p.astype(BF16)
                    ds_sc[rs, :] = (p * (dp_sc[rs, :] - drow)).astype(BF16)
                dv_sc[...] += _dot(p_sc[...], jnp.where(lm, do2, jnp.zeros_like(do2)))
                dk_sc[hf] += _dot(ds_sc[...], qh)
                dq_sc[hf, qrows, :] += _tn(ds_sc[...], ka_ref[:, hs])

        @pl.when(i > j)
        def _():
            tile(False)

        @pl.when(i == j)
        def _():
            tile(True)
            dq, aux = _pair(lane, dq_sc[0, qrows, :], dq_sc[1, qrows, :])
            dqo_sc[...] = (dq * 0.125).astype(BF16)
            auxo_sc[...] = aux
            cols = pl.ds(pl.multiple_of(pl.program_id(0) * LANES, LANES), LANES)
            c1 = pltpu.make_async_copy(dqo_sc, dq_ref.at[qrows, cols], out_sems.at[0])
            c2 = pltpu.make_async_copy(auxo_sc, auxq_ref.at[qrows, cols], out_sems.at[1])
            c1.start()
            c2.start()
            c1.wait()
            c2.wait()

        @pl.when(i == n - 1)
        def _():
            dk, aux = _pair(lane, dk_sc[0], dk_sc[1])
            dk_ref[...] = dk.astype(BF16)
            aux_ref[...] = aux
            dv_ref[...] = dv_sc[...].astype(BF16)

    qblk = pl.BlockSpec((t, LANES), lambda p, j, i: (jnp.maximum(i, j), p))
    kblk = pl.BlockSpec((t, LANES), lambda p, j, i: (j, p))
    rblk = pl.BlockSpec((None, 2, t), lambda p, j, i: (p, 0, jnp.maximum(i, j)))
    bf, f32 = jax.ShapeDtypeStruct((T, D_MODEL), BF16), jax.ShapeDtypeStruct((T, D_MODEL), F32)
    return pl.pallas_call(
        body, name=name, grid=(8, n, n),
        in_specs=[pl.BlockSpec((t, 2 * LANES), lambda p, j, i: (jnp.maximum(i, j), p)),
                  pl.BlockSpec((t, 2 * LANES), lambda p, j, i: (j, p)),
                  pl.BlockSpec((t, LANES), lambda p, j, i: (j, 16 + p)), rblk, rblk, qblk],
        out_specs=[pl.BlockSpec(memory_space=pl.ANY), pl.BlockSpec(memory_space=pl.ANY), kblk, kblk, kblk],
        out_shape=[bf, f32, bf, bf, f32],
        scratch_shapes=[pltpu.VMEM((2, T, LANES), F32), pltpu.VMEM((2, t, LANES), F32), pltpu.VMEM((t, LANES), F32),
                        pltpu.VMEM((t, t), F32), pltpu.VMEM((t, t), F32), pltpu.VMEM((t, t), BF16),
                        pltpu.VMEM((t, t), BF16), pltpu.VMEM((t, LANES), BF16), pltpu.VMEM((t, LANES), F32),
                        pltpu.SemaphoreType.DMA((2,))],
        compiler_params=_cparams("arbitrary", "arbitrary", "arbitrary"),
    )(qa, ka, qkv, lse_row, delta_row, do)


C = HGRN_CHUNK
LEVELS = (64, 32, 16, 8, 4, 2)


def _pivot(b, B, row):
    if B == C:
        return jnp.broadcast_to(b[C // 2 - 1:C // 2, :], b.shape)
    if B >= 8:
        b3 = b.reshape(C // B, B, LANES)
        return jnp.broadcast_to(b3[:, B // 2 - 1:B // 2, :], b3.shape).reshape(C, LANES)
    if B == 4:
        y = jnp.where((row & 3) == 1, b, 0.0)
        return y + pltpu.roll(y, 1, 0) + pltpu.roll(y, 2, 0) + pltpu.roll(y, C - 1, 0)
    y = jnp.where((row & 1) == 0, b, 0.0)
    return y + pltpu.roll(y, 1, 0)


def _level_factors(bcum):
    row = lax.broadcasted_iota(jnp.int32, (C, 1), 0)
    out = []
    for B in LEVELS:
        upper = (row & (B - 1)) >= B // 2
        e = jnp.exp(-jnp.abs(bcum - _pivot(bcum, B, row)))
        out.append((B, jnp.where(upper, e, 0.0), jnp.where(upper, 0.0, e)))
    return out


def _same_block(B):
    sh = B.bit_length() - 1
    r = lax.broadcasted_iota(jnp.int32, (C, C), 0)
    c = lax.broadcasted_iota(jnp.int32, (C, C), 1)
    return (r >> sh) == (c >> sh)


def _hgrn_gates(q, fl, lb):
    sg = _sigmoid(fl)
    f = lb + (1.0 - lb) * sg
    sq = _sigmoid(q)
    return sg, f, jnp.log(f), 1.0 - f, sq, q * sq


def _cumsum_rows(x, reverse=False):
    r = lax.broadcasted_iota(jnp.int32, (C, C), 0)
    c = lax.broadcasted_iota(jnp.int32, (C, C), 1)
    tri = ((r <= c) if reverse else (r >= c)).astype(F32)
    return jnp.dot(tri, x, precision=HI, preferred_element_type=F32)


def _intra(qs, k, factors):
    r = lax.broadcasted_iota(jnp.int32, (C, C), 0)
    c = lax.broadcasted_iota(jnp.int32, (C, C), 1)
    a = jnp.where(r == c, jnp.sum(qs * k, axis=1, keepdims=True), 0.0)
    ops = []
    for B, eq, ek in factors:
        ql, kl = (qs * eq).astype(BF16), (k * ek).astype(BF16)
        al = _nt(ql, kl)
        a = a + (al if B == C else jnp.where(_same_block(B), al, 0.0))
        ops.append((ql, kl))
    return a, ops


def hgrn_fwd(proj, lb, gn, name):
    T = proj.shape[0]
    tg = min(HGRN_TG, T)
    nch = tg // C

    def body(q_ref, fl_ref, v_ref, g_ref, lb_ref, gn_ref, ao_ref, o_ref, st_ref, st_sc):
        @pl.when(pl.program_id(1) == 0)
        def _():
            st_sc[...] = jnp.zeros(st_sc.shape, F32)

        lb_v, gn_v = lb_ref[...], gn_ref[...]

        def chunk(ci, carry):
            rows = pl.ds(pl.multiple_of(ci * C, C), C)
            _, f, lf, k, _, qs = _hgrn_gates(q_ref[rows, :], fl_ref[rows, :], lb_v)
            vb = v_ref[rows, :].astype(BF16)
            gv = g_ref[rows, :]
            bcum = _cumsum_rows(lf)
            blast = bcum[C - 1:C, :]
            a, _ = _intra(qs, k, _level_factors(bcum))
            st = st_sc[...]
            st_ref[ci] = st
            o = _dot(a.astype(BF16), vb) + _nt((qs * jnp.exp(bcum)).astype(BF16), st.astype(BF16))
            st_sc[...] = st * jnp.exp(blast) + _tn(vb, (k * jnp.exp(blast - bcum)).astype(BF16))
            o_ref[rows, :] = o
            ao_ref[rows, :] = (o * _rms(o) * gn_v * (gv * _sigmoid(gv))).astype(BF16)
            return carry

        lax.fori_loop(0, nch, chunk, 0, unroll=8)

    col = lambda off: pl.BlockSpec((tg, LANES), lambda h, i: (i, off + h))
    one = pl.BlockSpec((1, LANES), lambda h, i: (0, h))
    return pl.pallas_call(
        body, name=name, grid=(8, T // tg),
        in_specs=[col(0), col(8), col(16), col(24), one, one],
        out_specs=[col(0), col(0), pl.BlockSpec((None, nch, LANES, LANES), lambda h, i: (h, i, 0, 0))],
        out_shape=[jax.ShapeDtypeStruct((T, D_MODEL), BF16), jax.ShapeDtypeStruct((T, D_MODEL), F32),
                   jax.ShapeDtypeStruct((8, T // C, LANES, LANES), F32)],
        scratch_shapes=[pltpu.VMEM((LANES, LANES), F32)],
        compiler_params=_cparams("parallel", "arbitrary"),
    )(proj, proj, proj, proj, lb, gn)


def hgrn_bwd(proj, lb, gn, o_raw, states, dao, name):
    T = proj.shape[0]
    tg = min(HGRN_TG, T)
    nch = tg // C
    n = T // tg

    def body(q_ref, fl_ref, v_ref, g_ref, lb_ref, gn_ref, o_ref, st_ref, dao_ref,
             dq_ref, dfl_ref, dv_ref, dg_ref, dlb_ref, dgn_ref, dst_sc):
        @pl.when(pl.program_id(1) == 0)
        def _():
            dst_sc[...] = jnp.zeros(dst_sc.shape, F32)
            dlb_ref[...] = jnp.zeros(dlb_ref.shape, F32)
            dgn_ref[...] = jnp.zeros(dgn_ref.shape, F32)

        lb_v, gn_v = lb_ref[...], gn_ref[...]
        r64 = lax.broadcasted_iota(jnp.int32, (C, C), 0)
        c64 = lax.broadcasted_iota(jnp.int32, (C, C), 1)
        row = lax.broadcasted_iota(jnp.int32, (C, 1), 0)

        def chunk(cr, carry):
            ci = nch - 1 - cr
            rows = pl.ds(pl.multiple_of(ci * C, C), C)
            q, fl, gv = q_ref[rows, :], fl_ref[rows, :], g_ref[rows, :]
            sg, f, lf, k, sq, qs = _hgrn_gates(q, fl, lb_v)
            vb = v_ref[rows, :].astype(BF16)
            o = o_ref[rows, :]
            ro = _rms(o)
            on = o * ro
            sgg = _sigmoid(gv)
            gate = gv * sgg
            dao_v = dao_ref[rows, :].astype(F32)
            dg_ref[rows, :] = (dao_v * on * gn_v * (sgg * (1.0 + gv * (1.0 - sgg)))).astype(BF16)
            dgn_ref[...] += jnp.sum(dao_v * on * gate, axis=0, keepdims=True)
            don = dao_v * gn_v * gate
            do = ro * (don - on * jnp.mean(don * on, axis=-1, keepdims=True))
            dob = do.astype(BF16)
            bcum = _cumsum_rows(lf)
            blast = bcum[C - 1:C, :]
            factors = _level_factors(bcum)
            a, ops = _intra(qs, k, factors)
            eb = jnp.exp(bcum)
            ekb = jnp.exp(blast - bcum)
            qb = qs * eb
            kb = k * ekb
            st = st_ref[ci]
            dst = dst_sc[...]
            dstb = dst.astype(BF16)
            da = jnp.where(r64 >= c64, _nt(dob, vb), 0.0)
            dv_ref[rows, :] = (_tn(a.astype(BF16), dob) + _nt(kb.astype(BF16), dstb)).astype(BF16)
            dqb = _dot(dob, st.astype(BF16))
            dkb = _dot(vb, dstb)
            eblast = jnp.exp(blast)
            dst_sc[...] = dst * eblast + _tn(dob, qb.astype(BF16))
            dblast = eblast * jnp.sum(dst * st, axis=0, keepdims=True) + jnp.sum(dkb * kb, axis=0, keepdims=True)
            dad = jnp.sum(jnp.where(r64 == c64, da, 0.0), axis=1, keepdims=True)
            dqs = dqb * eb + dad * k
            dk = dkb * ekb + dad * qs
            dbcum = dqb * qb - dkb * kb + jnp.where(row == C - 1, dblast, 0.0)
            for (B, eq, ek), (ql, kl) in zip(factors, ops):
                dal = (da if B == C else jnp.where(_same_block(B), da, 0.0)).astype(BF16)
                dql, dkl = _dot(dal, kl), _tn(dal, ql)
                dqs = dqs + dql * eq
                dk = dk + dkl * ek
                dbcum = dbcum + (dql * ql.astype(F32) - dkl * kl.astype(F32))
            df = _cumsum_rows(dbcum, reverse=True) / f - dk
            dfl_ref[rows, :] = (df * (1.0 - lb_v) * sg * (1.0 - sg)).astype(BF16)
            dlb_ref[...] += jnp.sum(df * (1.0 - sg), axis=0, keepdims=True)
            dq_ref[rows, :] = (dqs * (sq * (1.0 + q * (1.0 - sq)))).astype(BF16)
            return carry

        lax.fori_loop(0, nch, chunk, 0, unroll=8)

    col = lambda off: pl.BlockSpec((tg, LANES), lambda h, i: (n - 1 - i, off + h))
    one = pl.BlockSpec((1, LANES), lambda h, i: (0, h))
    big = jax.ShapeDtypeStruct((T, D_MODEL), BF16)
    small = jax.ShapeDtypeStruct((1, D_MODEL), F32)
    return pl.pallas_call(
        body, name=name, grid=(8, n),
        in_specs=[col(0), col(8), col(16), col(24), one, one, col(0),
                  pl.BlockSpec((None, nch, LANES, LANES), lambda h, i: (h, n - 1 - i, 0, 0)), col(0)],
        out_specs=[col(0), col(0), col(0), col(0), one, one],
        out_shape=[big, big, big, big, small, small],
        scratch_shapes=[pltpu.VMEM((LANES, LANES), F32)],
        compiler_params=_cparams("arbitrary", "arbitrary"),
    )(proj, proj, proj, proj, lb, gn, o_raw, states, dao)


def lower_bound_fwd(logits, name):
    def body(l_ref, s_ref):
        lv = l_ref[...]
        e = jnp.exp(lv - jnp.max(lv, axis=0, keepdims=True))
        s_ref[...] = e / jnp.sum(e, axis=0, keepdims=True)

    return pl.pallas_call(body, name=name, out_shape=jax.ShapeDtypeStruct(logits.shape, F32))(logits)


def lower_bound_bwd(sm, dlb, name):
    def body(s_ref, d_ref, o_ref):
        s = s_ref[...]
        row = lax.broadcasted_iota(jnp.int32, s.shape, 0)
        o_ref[...] = d_ref[...] * s[1:2, :] * (jnp.where(row == 1, 1.0, 0.0) - s)

    return pl.pallas_call(body, name=name, out_shape=jax.ShapeDtypeStruct(sm.shape, F32))(sm, dlb)


def _pad_rows(flat, mult):
    rows = -(-flat.shape[-1] // D_MODEL)
    rows = -(-rows // mult) * mult
    pad = rows * D_MODEL - flat.shape[-1]
    flat = jnp.pad(flat, [(0, 0)] * (flat.ndim - 1) + [(0, pad)])
    return flat.reshape(flat.shape[:-1] + (rows, D_MODEL))


def _gather_weights(w):
    pieces = []
    for nme in SHARDED:
        a = w[nme]
        if nme in BIASES:
            pieces.append(lax.bitcast_convert_type(a, BF16).reshape(-1))
        else:
            pieces.append(a.astype(BF16).reshape(-1))
    flat = _pad_rows(jnp.concatenate(pieces), 16)
    got = all_gather_rows(flat).reshape(N_DEV, -1)
    full, off = {}, 0
    for nme in SHARDED:
        shp = w[nme].shape
        cnt = 1
        for s in shp:
            cnt *= s
        if nme in BIASES:
            seg = got[:, off:off + 2 * cnt].reshape((N_DEV,) + shp + (2,))
            seg = lax.bitcast_convert_type(seg, F32)
            off += 2 * cnt
        else:
            seg = got[:, off:off + cnt].reshape((N_DEV,) + shp)
            off += cnt
        full[nme] = jnp.concatenate([seg[d] for d in range(N_DEV)], axis=SHARD_AXIS[nme])
    return full


def _pieces(gfull, axis):
    shp = gfull.shape
    a = gfull.reshape(shp[:axis] + (N_DEV, shp[axis] // N_DEV) + shp[axis + 1:])
    return jnp.moveaxis(a, axis, 0).reshape(N_DEV, -1)


def _flat_local(vals):
    return (_pad_rows(jnp.concatenate([vals[n].reshape(-1) for n in SHARDED]), 32),
            _pad_rows(jnp.concatenate([vals[n].reshape(-1) for n in REPLICATED] + [jnp.zeros((1,), F32)]), 16))


def kernel(x, norm_mix, norm_mlp, norm_final, w_up, w_down, swa_w_qkv, swa_b_qkv, swa_sinks, swa_w_o, hgrn_w_in, hgrn_lb_logits, hgrn_g_norm, hgrn_w_o, fox_w_in, fox_b_in, fox_w_o, loss_target, m_norm_mix, m_norm_mlp, m_norm_final, m_w_up, m_w_down, m_swa_w_qkv, m_swa_b_qkv, m_swa_sinks, m_swa_w_o, m_hgrn_w_in, m_hgrn_lb_logits, m_hgrn_g_norm, m_hgrn_w_o, m_fox_w_in, m_fox_b_in, m_fox_w_o, v_norm_mix, v_norm_mlp, v_norm_final, v_w_up, v_w_down, v_swa_w_qkv, v_swa_b_qkv, v_swa_sinks, v_swa_w_o, v_hgrn_w_in, v_hgrn_lb_logits, v_hgrn_g_norm, v_hgrn_w_o, v_fox_w_in, v_fox_b_in, v_fox_w_o):
    w = dict(norm_mix=norm_mix, norm_mlp=norm_mlp, norm_final=norm_final, w_up=w_up, w_down=w_down,
             swa_w_qkv=swa_w_qkv, swa_b_qkv=swa_b_qkv, swa_sinks=swa_sinks, swa_w_o=swa_w_o, hgrn_w_in=hgrn_w_in,
             hgrn_lb_logits=hgrn_lb_logits, hgrn_g_norm=hgrn_g_norm, hgrn_w_o=hgrn_w_o, fox_w_in=fox_w_in,
             fox_b_in=fox_b_in, fox_w_o=fox_w_o)
    mom = dict(norm_mix=m_norm_mix, norm_mlp=m_norm_mlp, norm_final=m_norm_final, w_up=m_w_up, w_down=m_w_down,
               swa_w_qkv=m_swa_w_qkv, swa_b_qkv=m_swa_b_qkv, swa_sinks=m_swa_sinks, swa_w_o=m_swa_w_o,
               hgrn_w_in=m_hgrn_w_in, hgrn_lb_logits=m_hgrn_lb_logits, hgrn_g_norm=m_hgrn_g_norm, hgrn_w_o=m_hgrn_w_o,
               fox_w_in=m_fox_w_in, fox_b_in=m_fox_b_in, fox_w_o=m_fox_w_o)
    var = dict(norm_mix=v_norm_mix, norm_mlp=v_norm_mlp, norm_final=v_norm_final, w_up=v_w_up, w_down=v_w_down,
               swa_w_qkv=v_swa_w_qkv, swa_b_qkv=v_swa_b_qkv, swa_sinks=v_swa_sinks, swa_w_o=v_swa_w_o,
               hgrn_w_in=v_hgrn_w_in, hgrn_lb_logits=v_hgrn_lb_logits, hgrn_g_norm=v_hgrn_g_norm, hgrn_w_o=v_hgrn_w_o,
               fox_w_in=v_fox_w_in, fox_b_in=v_fox_b_in, fox_w_o=v_fox_w_o)
    T = x.shape[1]
    x0 = x[0]
    tgt = loss_target[0]
    W = _gather_weights(w)
    zeros_b = jnp.zeros((1, 4 * D_MODEL), F32)

    def swa_layer(xin, i, j):
        qkv = norm_matmul(xin, norm_mix[i:i + 1], W['swa_w_qkv'][j], W['swa_b_qkv'][j:j + 1], BF16, f"swa_qkv_L{i}")
        dup = lambda a: jnp.broadcast_to(a.reshape(T, 4, 1, 64), (T, 4, 2, 64)).reshape(T, 4 * LANES)
        kdup, vdup = dup(qkv[:, 1024:1280]), dup(qkv[:, 1280:1536])
        sk = jnp.broadcast_to(jnp.pad(swa_sinks[j].reshape(4, 4), ((0, 0), (0, 4)))[:, :, None], (4, 8, LANES))
        ao, lse = swa_fwd(qkv, kdup, vdup, sk, f"swa_fwd_L{i}")
        xmid = matmul(ao, W['swa_w_o'][j], F32, f"swa_out_L{i}", res=xin)
        return xmid, (qkv, kdup, vdup, sk, ao, lse)

    def swa_layer_bwd(xin, saved, dmid, i, j, grads):
        qkv, kdup, vdup, sk, ao, lse = saved
        dao = matmul(dmid, W['swa_w_o'][j].T, BF16, f"swa_dout_L{i}")
        grads['swa_w_o'][j] = tn_matmul(ao, dmid, f"swa_dwo_L{i}")
        dq, dk, dv, dsk = swa_bwd(qkv, kdup, vdup, sk, ao, lse, dao, f"swa_bwd_L{i}")
        wt = W['swa_w_qkv'][j].T
        spread = lambda a: jnp.pad(a.reshape(4, 64, D_MODEL), ((0, 0), (0, 64), (0, 0))).reshape(4 * LANES, D_MODEL)
        gather = lambda a: a.reshape(a.shape[0], 4, LANES)[:, :, :64].reshape(a.shape[0], 256)
        dx, h, dg = proj_bwd(xin, norm_mix[i:i + 1], dmid,
                             [(dq, wt[:1024]), (dk, spread(wt[1024:1280])), (dv, spread(wt[1280:]))], f"swa_din_L{i}")
        gq, bq = tn_matmul(h, dq, f"swa_dwq_L{i}", colsum=True)
        gk, bk = tn_matmul(h, dk, f"swa_dwk_L{i}", colsum=True)
        gv, bv = tn_matmul(h, dv, f"swa_dwv_L{i}", colsum=True)
        grads['swa_w_qkv'][j] = jnp.concatenate([gq, gather(gk), gather(gv)], axis=1)
        grads['swa_b_qkv'][j] = jnp.concatenate([bq, gather(bk), gather(bv)], axis=1)[0]
        grads['swa_sinks'][j] = dsk[:, :4, 0].reshape(16)
        grads['norm_mix'][i] = dg[0]
        return dx

    lb_soft = lower_bound_fwd(hgrn_lb_logits, "hgrn_lb_fwd")
    lb = lb_soft[1:2]

    def hgrn_layer(xin, i, j):
        proj = norm_matmul(xin, norm_mix[i:i + 1], W['hgrn_w_in'][j], zeros_b, F32, f"hgrn_in_L{i}")
        ao, o_raw, states = hgrn_fwd(proj, lb, hgrn_g_norm[j:j + 1], f"hgrn_fwd_L{i}")
        xmid = matmul(ao, W['hgrn_w_o'][j], F32, f"hgrn_out_L{i}", res=xin)
        return xmid, (proj, ao, o_raw, states)

    def hgrn_layer_bwd(xin, saved, dmid, i, j, grads):
        proj, ao, o_raw, states = saved
        dao = matmul(dmid, W['hgrn_w_o'][j].T, BF16, f"hgrn_dout_L{i}")
        grads['hgrn_w_o'][j] = tn_matmul(ao, dmid, f"hgrn_dwo_L{i}")
        dq, dfl, dv, dgt, dlb, dgn = hgrn_bwd(proj, lb, hgrn_g_norm[j:j + 1], o_raw, states, dao, f"hgrn_bwd_L{i}")
        wt = W['hgrn_w_in'][j].T
        parts = [dq, dfl, dv, dgt]
        dx, h, dg = proj_bwd(xin, norm_mix[i:i + 1], dmid,
                             [(d, wt[n * D_MODEL:(n + 1) * D_MODEL]) for n, d in enumerate(parts)], f"hgrn_din_L{i}")
        grads['hgrn_w_in'][j] = jnp.concatenate(
            [tn_matmul(h, d, f"hgrn_dwin{n}_L{i}") for n, d in enumerate(parts)], axis=1)
        grads['hgrn_g_norm'][j] = dgn[0]
        grads['hgrn_lb_logits'] = lower_bound_bwd(lb_soft, dlb, "hgrn_lb_bwd")
        grads['norm_mix'][i] = dg[0]
        return dx

    def fox_layer(xin, i, j):
        w_in = W['fox_w_in'][j]
        b_in = W['fox_b_in'][j:j + 1]
        qkv = norm_matmul(xin, norm_mix[i:i + 1], w_in[:, :3072], b_in[:, :3072], BF16, f"fox_qkv_L{i}")
        wf = jnp.pad(w_in[:, 3072:], ((0, 0), (0, LANES - 16)))
        bf = jnp.pad(b_in[:, 3072:], ((0, 0), (0, LANES - 16)))
        fl = norm_matmul(xin, norm_mix[i:i + 1], wf, bf, F32, f"fox_f_L{i}")
        qa, ka = fox_gate_fwd(fl, qkv, f"fox_gate_L{i}")
        ao, lse = fox_fwd(qa, ka, qkv, f"fox_fwd_L{i}")
        xmid = matmul(ao, W['fox_w_o'][j], F32, f"fox_out_L{i}", res=xin)
        return xmid, (qkv, fl, qa, ka, ao, lse, wf)

    def fox_layer_bwd(xin, saved, dmid, i, j, grads):
        qkv, fl, qa, ka, ao, lse, wf = saved
        dao = matmul(dmid, W['fox_w_o'][j].T, BF16, f"fox_dout_L{i}")
        grads['fox_w_o'][j] = tn_matmul(ao, dmid, f"fox_dwo_L{i}")
        delta = fox_delta(dao, ao, f"fox_delta_L{i}")
        dq, aux_q, dk, dv, aux_k = fox_bwd(qa, ka, qkv, lse[:, ::64].T.reshape(8, 2, T),
                                           delta[:, :16].T.reshape(8, 2, T), dao, f"fox_bwd_L{i}")
        dcp = jnp.pad(aux_q[:, ::64] - aux_k[:, 3::64], ((0, 0), (0, LANES - 16)))
        dfl = fox_gate_bwd(fl, dcp, f"fox_dgate_L{i}")
        wt = W['fox_w_in'][j][:, :3072].T
        parts = [dq, dk, dv]
        dx, h, dg = proj_bwd(xin, norm_mix[i:i + 1], dmid,
                             [(d, wt[n * D_MODEL:(n + 1) * D_MODEL]) for n, d in enumerate(parts)] + [(dfl, wf.T)],
                             f"fox_din_L{i}")
        gw = [tn_matmul(h, d, f"fox_dw{n}_L{i}", colsum=True) for n, d in enumerate(parts + [dfl])]
        grads['fox_w_in'][j] = jnp.concatenate([g for g, _ in gw[:3]] + [gw[3][0][:, :16]], axis=1)
        grads['fox_b_in'][j] = jnp.concatenate([b for _, b in gw[:3]] + [gw[3][1][:, :16]], axis=1)[0]
        grads['norm_mix'][i] = dg[0]
        return dx

    mixers = [(swa_layer, swa_layer_bwd), (hgrn_layer, hgrn_layer_bwd), (fox_layer, fox_layer_bwd)]

    xs, mids, saves = [x0], [], []
    for i in range(DEPTH):
        xmid, saved = mixers[i % 3][0](xs[-1], i, i // 3)
        mids.append(xmid)
        saves.append(saved)
        xs.append(mlp_fwd(xmid, norm_mlp[i:i + 1], W['w_up'][i], W['w_down'][i], f"mlp_fwd_L{i}"))

    grads = {n: [None] * w[n].shape[0] for n in WEIGHTS if n not in ('norm_final', 'hgrn_lb_logits')}
    loss_part, dx, dgf = final_loss(xs[-1], norm_final.reshape(1, D_MODEL), tgt, "final_loss")
    grads['norm_final'] = dgf[0]
    for i in reversed(range(DEPTH)):
        dmid, h, a, du, dg = mlp_bwd(mids[i], norm_mlp[i:i + 1], W['w_up'][i], W['w_up'][i].T, W['w_down'][i].T, dx,
                                     f"mlp_bwd_L{i}")
        grads['w_up'][i] = tn_matmul(h, du, f"mlp_dwup_L{i}")
        grads['w_down'][i] = tn_matmul(a, dx, f"mlp_dwdown_L{i}")
        grads['norm_mlp'][i] = dg[0]
        dx = mixers[i % 3][1](xs[i], saves[i], dmid, i, i // 3, grads)
    gfull = {n: (g if not isinstance(g, list) else jnp.stack(g)) for n, g in grads.items()}

    small = jnp.concatenate([gfull[n].reshape(-1) for n in REPLICATED] + [loss_part[0, 0:1]])
    big = jnp.concatenate([_pieces(gfull[n], SHARD_AXIS[n]) for n in SHARDED], axis=1).astype(BF16)
    recv_big, recv_small = all_to_all_rows(
        [_pad_rows(big, 32), _pad_rows(jnp.broadcast_to(small[None], (N_DEV, small.shape[0])), 16)])
    (w_big, w_small), (m_big, m_small), (v_big, v_small) = _flat_local(w), _flat_local(mom), _flat_local(var)
    outs_big = reduce_adamw(recv_big, w_big, m_big, v_big, "reduce_adamw")
    outs_small = reduce_adamw(recv_small, w_small, m_small, v_small, "reduce_adamw_replicated")
    res = [{}, {}, {}, {}]
    for names, outs in ((SHARDED, outs_big), (REPLICATED, outs_small)):
        off = 0
        for nme in names:
            cnt = w[nme].size
            for o, r in zip(outs, res):
                r[nme] = o.reshape(-1)[off:off + cnt].reshape(w[nme].shape)
            off += cnt
    loss = outs_small[0].reshape(-1)[off]
    return (loss, dx[None], *[res[0][n] for n in WEIGHTS], *[res[1][n] for n in WEIGHTS],
            *[res[2][n] for n in WEIGHTS], *[res[3][n] for n in WEIGHTS])
```

```python
import functools

import jax
import jax.numpy as jnp
from jax import lax
from jax.experimental import pallas as pl
from jax.experimental.pallas import tpu as pltpu

F32 = jnp.float32
BF16 = jnp.bfloat16
HI = lax.Precision.HIGHEST

N_DEV = 8
D_MODEL = 1024
DEPTH = 4
EPS = 1e-6
SWA_WINDOW = 128
HGRN_CHUNK = 64
LANES = 128
VMEM_LIMIT = 56 << 20

ADAM_LR, ADAM_B1, ADAM_B2, ADAM_EPS, ADAM_WD, ADAM_STEP = 0.001, 0.9, 0.999, 1e-08, 0.01, 10

TM = 512
TF = 512
TK = 512
FOX_T = 1024
SWA_TQ = 512
HGRN_TG = 512
SCAN_T = 256

WEIGHTS = ['norm_mix', 'norm_mlp', 'norm_final', 'w_up', 'w_down', 'swa_w_qkv', 'swa_b_qkv', 'swa_sinks', 'swa_w_o',
           'hgrn_w_in', 'hgrn_lb_logits', 'hgrn_g_norm', 'hgrn_w_o', 'fox_w_in', 'fox_b_in', 'fox_w_o']
SHARD_AXIS = {'norm_mix': None, 'norm_mlp': None, 'norm_final': None, 'w_up': 2, 'w_down': 1, 'swa_w_qkv': 2,
              'swa_b_qkv': 1, 'swa_sinks': None, 'swa_w_o': 1, 'hgrn_w_in': 2, 'hgrn_lb_logits': None,
              'hgrn_g_norm': None, 'hgrn_w_o': 1, 'fox_w_in': 2, 'fox_b_in': 1, 'fox_w_o': 1}
SHARDED = [n for n in WEIGHTS if SHARD_AXIS[n] is not None]
REPLICATED = [n for n in WEIGHTS if SHARD_AXIS[n] is None]
BIASES = ('swa_b_qkv', 'fox_b_in')


def _cparams(*sem):
    return pltpu.CompilerParams(dimension_semantics=sem, vmem_limit_bytes=VMEM_LIMIT)


def _nt(a, b):
    return lax.dot_general(a, b, (((1,), (1,)), ((), ())), preferred_element_type=F32)


def _tn(a, b):
    return lax.dot_general(a, b, (((0,), (0,)), ((), ())), preferred_element_type=F32)


def _dot(a, b):
    return jnp.dot(a, b, preferred_element_type=F32)


def _sigmoid(x):
    return 1.0 / (1.0 + jnp.exp(-x))


def _rms(xv):
    return lax.rsqrt(jnp.mean(xv * xv, axis=-1, keepdims=True) + EPS)


def _rms_bwd(xv, g, dh):
    r = _rms(xv)
    xhat = xv * r
    dhg = dh * g
    dx = r * (dhg - xhat * jnp.mean(dhg * xhat, axis=-1, keepdims=True))
    return dx, jnp.sum(dh * xhat, axis=0, keepdims=True)


def _my_id():
    return lax.axis_index("x"), lax.axis_index("y"), lax.axis_index("c")


def _peer(x, y, c, k):
    return (lax.rem(x + ((k >> 2) & 1), 2), lax.rem(y + ((k >> 1) & 1), 2), lax.rem(c + (k & 1), 2))


def all_gather_shards(shards):
    n = len(shards)

    def place(o_ref, local, axis, me):
        if axis is None:
            return o_ref.at[me]
        idx = [slice(None)] * local.ndim
        idx[axis] = pl.ds(pl.multiple_of(me * local.shape[axis], local.shape[axis]), local.shape[axis])
        return o_ref.at[tuple(idx)]

    def body(*refs):
        x_refs, o_refs = refs[:n], refs[n:2 * n]
        send_sems, recv_sems, loc_sems = refs[2 * n:]
        x, y, c = _my_id()
        me = 4 * x + 2 * y + c
        copies = []
        for a, (x_ref, o_ref, (local, axis)) in enumerate(zip(x_refs, o_refs, shards)):
            dst = place(o_ref, local, axis, me)
            mine = pltpu.make_async_copy(x_ref, dst, loc_sems.at[a])
            mine.start()
            copies.append(mine)
            for k in range(1, N_DEV):
                px, py, pc = _peer(x, y, c, k)
                sem = a * (N_DEV - 1) + k - 1
                cp = pltpu.make_async_remote_copy(
                    src_ref=x_ref, dst_ref=dst, send_sem=send_sems.at[sem], recv_sem=recv_sems.at[sem],
                    device_id=(px, py, pc), device_id_type=pl.DeviceIdType.MESH)
                cp.start()
                copies.append(cp)
        for cp in copies:
            cp.wait()

    def full_shape(local, axis):
        if axis is None:
            return (N_DEV,) + local.shape
        return local.shape[:axis] + (N_DEV * local.shape[axis],) + local.shape[axis + 1:]

    hbm = pl.BlockSpec(memory_space=pl.ANY)
    return pl.pallas_call(
        body, name="all_gather_weights",
        out_shape=[jax.ShapeDtypeStruct(full_shape(l, ax), l.dtype) for l, ax in shards],
        in_specs=[hbm] * n, out_specs=[hbm] * n,
        scratch_shapes=[pltpu.SemaphoreType.DMA((n * (N_DEV - 1),)), pltpu.SemaphoreType.DMA((n * (N_DEV - 1),)),
                        pltpu.SemaphoreType.DMA((n,))],
    )(*[l for l, _ in shards])


def all_to_all_rows(sends):
    n = len(sends)

    def body(*refs):
        s_refs, r_refs = refs[:n], refs[n:2 * n]
        send_sems, recv_sems, loc_sems = refs[2 * n:]
        x, y, c = _my_id()
        me = 4 * x + 2 * y + c
        copies = []
        for a, (s_ref, r_ref) in enumerate(zip(s_refs, r_refs)):
            mine = pltpu.make_async_copy(s_ref.at[me], r_ref.at[me], loc_sems.at[a])
            mine.start()
            copies.append(mine)
            for k in range(1, N_DEV):
                px, py, pc = _peer(x, y, c, k)
                sem = a * (N_DEV - 1) + k - 1
                cp = pltpu.make_async_remote_copy(
                    src_ref=s_ref.at[4 * px + 2 * py + pc], dst_ref=r_ref.at[me],
                    send_sem=send_sems.at[sem], recv_sem=recv_sems.at[sem],
                    device_id=(px, py, pc), device_id_type=pl.DeviceIdType.MESH)
                cp.start()
                copies.append(cp)
        for cp in copies:
            cp.wait()

    hbm = pl.BlockSpec(memory_space=pl.ANY)
    return pl.pallas_call(
        body, name="all_to_all_grads",
        out_shape=[jax.ShapeDtypeStruct(s.shape, s.dtype) for s in sends],
        in_specs=[hbm] * n, out_specs=[hbm] * n,
        scratch_shapes=[pltpu.SemaphoreType.DMA((n * (N_DEV - 1),)), pltpu.SemaphoreType.DMA((n * (N_DEV - 1),)),
                        pltpu.SemaphoreType.DMA((n,))],
    )(*sends)


def reduce_adamw(recv, w, m, v, name):
    R, C = w.shape
    tr = max(t for t in range(16, (1 << 18) // C + 1, 16) if R % t == 0)
    c1 = 1.0 / (1.0 - ADAM_B1 ** ADAM_STEP)
    c2 = 1.0 / (1.0 - ADAM_B2 ** ADAM_STEP)

    def body(r_ref, w_ref, m_ref, v_ref, g_ref, d_ref, nm_ref, nv_ref):
        g = r_ref[0].astype(F32)
        for s in range(1, N_DEV):
            g = g + r_ref[s].astype(F32)
        m2 = ADAM_B1 * m_ref[...] + (1.0 - ADAM_B1) * g
        v2 = ADAM_B2 * v_ref[...] + (1.0 - ADAM_B2) * (g * g)
        g_ref[...] = g
        nm_ref[...] = m2
        nv_ref[...] = v2
        d_ref[...] = -ADAM_LR * ((m2 * c1) / (jnp.sqrt(v2 * c2) + ADAM_EPS) + ADAM_WD * w_ref[...])

    row = pl.BlockSpec((tr, C), lambda i: (i, 0))
    shp = jax.ShapeDtypeStruct((R, C), F32)
    return pl.pallas_call(
        body, name=name, grid=(R // tr,),
        in_specs=[pl.BlockSpec((N_DEV, tr, C), lambda i: (0, i, 0)), row, row, row],
        out_specs=[row, row, row, row], out_shape=[shp, shp, shp, shp],
        compiler_params=_cparams("parallel"),
    )(recv, w, m, v)


def norm_matmul(x, g, w, b, out_dtype, name):
    T, N = x.shape[0], w.shape[1]
    tm, tn = min(TM, T), min(512, N)

    def body(x_ref, g_ref, w_ref, b_ref, o_ref, h_sc):
        @pl.when(pl.program_id(1) == 0)
        def _():
            xv = x_ref[...]
            h_sc[...] = (xv * _rms(xv) * g_ref[...]).astype(BF16)
        o_ref[...] = (_dot(h_sc[...], w_ref[...]) + b_ref[...]).astype(o_ref.dtype)

    return pl.pallas_call(
        body, name=name, grid=(T // tm, N // tn),
        in_specs=[pl.BlockSpec((tm, D_MODEL), lambda i, j: (i, 0)), pl.BlockSpec((1, D_MODEL), lambda i, j: (0, 0)),
                  pl.BlockSpec((D_MODEL, tn), lambda i, j: (0, j)), pl.BlockSpec((1, tn), lambda i, j: (0, j))],
        out_specs=pl.BlockSpec((tm, tn), lambda i, j: (i, j)),
        out_shape=jax.ShapeDtypeStruct((T, N), out_dtype),
        scratch_shapes=[pltpu.VMEM((tm, D_MODEL), BF16)],
        compiler_params=_cparams("parallel", "arbitrary"),
    )(x, g, w, b)


def matmul(a, w, out_dtype, name, res=None):
    T, K = a.shape
    N = w.shape[1]
    tm = min(TM, T)

    def body(*refs):
        if res is None:
            a_ref, w_ref, o_ref = refs
            acc = _dot(a_ref[...].astype(BF16), w_ref[...])
        else:
            a_ref, w_ref, r_ref, o_ref = refs
            acc = r_ref[...] + _dot(a_ref[...].astype(BF16), w_ref[...])
        o_ref[...] = acc.astype(o_ref.dtype)

    in_specs = [pl.BlockSpec((tm, K), lambda i: (i, 0)), pl.BlockSpec((K, N), lambda i: (0, 0))]
    ops = [a, w]
    if res is not None:
        in_specs.append(pl.BlockSpec((tm, N), lambda i: (i, 0)))
        ops.append(res)
    return pl.pallas_call(
        body, name=name, grid=(T // tm,), in_specs=in_specs,
        out_specs=pl.BlockSpec((tm, N), lambda i: (i, 0)),
        out_shape=jax.ShapeDtypeStruct((T, N), out_dtype),
        compiler_params=_cparams("parallel"),
    )(*ops)


def tn_matmul(a, b, name, colsum=False):
    T, M = a.shape
    N = b.shape[1]
    tk = min(TK, T)
    tmm = min(1024, M)
    tn = N if N <= 1024 else (1024 if N % 1024 == 0 else N)

    def body(a_ref, b_ref, o_ref, *rest):
        k = pl.program_id(2)
        bv = b_ref[...]

        @pl.when(k == 0)
        def _():
            o_ref[...] = jnp.zeros(o_ref.shape, F32)
            if colsum:
                rest[0][...] = jnp.zeros(rest[0].shape, F32)

        o_ref[...] += _tn(a_ref[...].astype(BF16), bv.astype(BF16))
        if colsum:
            rest[0][...] += jnp.sum(bv.astype(F32), axis=0, keepdims=True)

    out_specs = [pl.BlockSpec((tmm, tn), lambda i, j, k: (i, j))]
    out_shape = [jax.ShapeDtypeStruct((M, N), F32)]
    if colsum:
        assert M == tmm
        out_specs.append(pl.BlockSpec((1, tn), lambda i, j, k: (0, j)))
        out_shape.append(jax.ShapeDtypeStruct((1, N), F32))
    out = pl.pallas_call(
        body, name=name, grid=(M // tmm, N // tn, T // tk),
        in_specs=[pl.BlockSpec((tk, tmm), lambda i, j, k: (k, i)), pl.BlockSpec((tk, tn), lambda i, j, k: (k, j))],
        out_specs=out_specs, out_shape=out_shape,
        compiler_params=_cparams("parallel", "parallel", "arbitrary"),
    )(a, b)
    return out if colsum else out[0]


def mlp_fwd(x, g, w_up, w_down, name):
    T, F = x.shape[0], w_up.shape[1]
    tm, tf = min(TM, T), min(TF, F)
    nf = F // tf

    def body(x_ref, g_ref, wu_ref, wd_ref, o_ref, h_sc, acc_sc):
        f = pl.program_id(1)

        @pl.when(f == 0)
        def _():
            xv = x_ref[...]
            h_sc[...] = (xv * _rms(xv) * g_ref[...]).astype(BF16)
            acc_sc[...] = xv

        u = jnp.maximum(_dot(h_sc[...], wu_ref[...]), 0.0)
        acc_sc[...] += _dot((u * u).astype(BF16), wd_ref[...])

        @pl.when(f == nf - 1)
        def _():
            o_ref[...] = acc_sc[...]

    return pl.pallas_call(
        body, name=name, grid=(T // tm, nf),
        in_specs=[pl.BlockSpec((tm, D_MODEL), lambda i, f: (i, 0)), pl.BlockSpec((1, D_MODEL), lambda i, f: (0, 0)),
                  pl.BlockSpec((D_MODEL, tf), lambda i, f: (0, f)), pl.BlockSpec((tf, D_MODEL), lambda i, f: (f, 0))],
        out_specs=pl.BlockSpec((tm, D_MODEL), lambda i, f: (i, 0)),
        out_shape=jax.ShapeDtypeStruct((T, D_MODEL), F32),
        scratch_shapes=[pltpu.VMEM((tm, D_MODEL), BF16), pltpu.VMEM((tm, D_MODEL), F32)],
        compiler_params=_cparams("parallel", "arbitrary"),
    )(x, g, w_up, w_down)


def mlp_bwd(x, g, w_up, w_up_t, w_down_t, dy, name):
    T, F = x.shape[0], w_up.shape[1]
    tm, tf = min(TM, T), min(TF, F)
    nf = F // tf

    def body(x_ref, g_ref, wu_ref, wut_ref, wdt_ref, dy_ref, dx_ref, h_ref, a_ref, du_ref, dg_ref, h_sc, dyb_sc, dh_sc):
        i, f = pl.program_id(0), pl.program_id(1)

        @pl.when(f == 0)
        def _():
            xv = x_ref[...]
            h = (xv * _rms(xv) * g_ref[...]).astype(BF16)
            h_sc[...] = h
            h_ref[...] = h
            dyb_sc[...] = dy_ref[...].astype(BF16)
            dh_sc[...] = jnp.zeros(dh_sc.shape, F32)

        @pl.when((i == 0) & (f == 0))
        def _():
            dg_ref[...] = jnp.zeros(dg_ref.shape, F32)

        u = jnp.maximum(_dot(h_sc[...], wu_ref[...]), 0.0)
        a_ref[...] = (u * u).astype(BF16)
        du = (_dot(dyb_sc[...], wdt_ref[...]) * (2.0 * u)).astype(BF16)
        du_ref[...] = du
        dh_sc[...] += _dot(du, wut_ref[...])

        @pl.when(f == nf - 1)
        def _():
            dx, dg = _rms_bwd(x_ref[...], g_ref[...], dh_sc[...])
            dx_ref[...] = dy_ref[...] + dx
            dg_ref[...] += dg

    row = pl.BlockSpec((tm, D_MODEL), lambda i, f: (i, 0))
    hid = pl.BlockSpec((tm, tf), lambda i, f: (i, f))
    return pl.pallas_call(
        body, name=name, grid=(T // tm, nf),
        in_specs=[row, pl.BlockSpec((1, D_MODEL), lambda i, f: (0, 0)),
                  pl.BlockSpec((D_MODEL, tf), lambda i, f: (0, f)), pl.BlockSpec((tf, D_MODEL), lambda i, f: (f, 0)),
                  pl.BlockSpec((D_MODEL, tf), lambda i, f: (0, f)), row],
        out_specs=[row, row, hid, hid, pl.BlockSpec((1, D_MODEL), lambda i, f: (0, 0))],
        out_shape=[jax.ShapeDtypeStruct((T, D_MODEL), F32), jax.ShapeDtypeStruct((T, D_MODEL), BF16),
                   jax.ShapeDtypeStruct((T, F), BF16), jax.ShapeDtypeStruct((T, F), BF16),
                   jax.ShapeDtypeStruct((1, D_MODEL), F32)],
        scratch_shapes=[pltpu.VMEM((tm, D_MODEL), BF16), pltpu.VMEM((tm, D_MODEL), BF16),
                        pltpu.VMEM((tm, D_MODEL), F32)],
        compiler_params=_cparams("arbitrary", "arbitrary"),
    )(x, g, w_up, w_up_t, w_down_t, dy)


def proj_bwd(x, g, dres, parts, name):
    T = x.shape[0]
    tm = min(TM, T)
    n = len(parts)

    def body(*refs):
        x_ref, g_ref, dr_ref = refs[:3]
        da_refs, wt_refs = refs[3:3 + n], refs[3 + n:3 + 2 * n]
        dx_ref, h_ref, dg_ref = refs[3 + 2 * n:]

        @pl.when(pl.program_id(0) == 0)
        def _():
            dg_ref[...] = jnp.zeros(dg_ref.shape, F32)

        xv = x_ref[...]
        dh = _dot(da_refs[0][...].astype(BF16), wt_refs[0][...])
        for a_ref, w_ref in zip(da_refs[1:], wt_refs[1:]):
            dh = dh + _dot(a_ref[...].astype(BF16), w_ref[...])
        h_ref[...] = (xv * _rms(xv) * g_ref[...]).astype(BF16)
        dx, dg = _rms_bwd(xv, g_ref[...], dh)
        dx_ref[...] = dr_ref[...] + dx
        dg_ref[...] += dg

    row = pl.BlockSpec((tm, D_MODEL), lambda i: (i, 0))
    one = pl.BlockSpec((1, D_MODEL), lambda i: (0, 0))
    in_specs = [row, one, row]
    in_specs += [pl.BlockSpec((tm, da.shape[1]), lambda i: (i, 0)) for da, _ in parts]
    in_specs += [pl.BlockSpec(wt.shape, lambda i: (0, 0)) for _, wt in parts]
    return pl.pallas_call(
        body, name=name, grid=(T // tm,), in_specs=in_specs,
        out_specs=[row, row, one],
        out_shape=[jax.ShapeDtypeStruct((T, D_MODEL), F32), jax.ShapeDtypeStruct((T, D_MODEL), BF16),
                   jax.ShapeDtypeStruct((1, D_MODEL), F32)],
        compiler_params=_cparams("arbitrary"),
    )(x, g, dres, *[da for da, _ in parts], *[wt for _, wt in parts])


def final_loss(x, g, tgt, name):
    T = x.shape[0]
    tm = min(TM, T)

    def body(x_ref, g_ref, t_ref, l_ref, dx_ref, dg_ref):
        @pl.when(pl.program_id(0) == 0)
        def _():
            l_ref[...] = jnp.zeros(l_ref.shape, F32)
            dg_ref[...] = jnp.zeros(dg_ref.shape, F32)

        xv = x_ref[...]
        gv = g_ref[...]
        err = xv * _rms(xv) * gv - t_ref[...]
        l_ref[...] += 0.5 * jnp.sum(jnp.mean(err * err, axis=-1, keepdims=True), axis=0, keepdims=True)
        dx, dg = _rms_bwd(xv, gv, err * (1.0 / D_MODEL))
        dx_ref[...] = dx
        dg_ref[...] += dg

    row = pl.BlockSpec((tm, D_MODEL), lambda i: (i, 0))
    one = pl.BlockSpec((1, D_MODEL), lambda i: (0, 0))
    return pl.pallas_call(
        body, name=name, grid=(T // tm,), in_specs=[row, one, row],
        out_specs=[pl.BlockSpec((8, LANES), lambda i: (0, 0)), row, one],
        out_shape=[jax.ShapeDtypeStruct((8, LANES), F32), jax.ShapeDtypeStruct((T, D_MODEL), F32),
                   jax.ShapeDtypeStruct((1, D_MODEL), F32)],
        compiler_params=_cparams("arbitrary"),
    )(x, g, tgt)


def _swa_specs(tq):
    r = tq // SWA_WINDOW
    cur = lambda ix: pl.BlockSpec((tq, LANES), lambda kv, i: (ix(i), kv))
    prev = lambda ix: pl.BlockSpec((SWA_WINDOW, LANES), lambda kv, i: (jnp.maximum(ix(i) * r - 1, 0), kv))
    return cur, prev


W2 = 2 * SWA_WINDOW
SWA_RB = 32


def _swa_visible(tile):
    r = lax.broadcasted_iota(jnp.int32, (SWA_WINDOW, W2), 0)
    c = lax.broadcasted_iota(jnp.int32, (SWA_WINDOW, W2), 1)
    inside = (c > r) & (c <= r + SWA_WINDOW)
    return inside & ((c >= SWA_WINDOW) | (tile > 0)), inside


def swa_fwd(qkv, kdup, vdup, sinks_b, name):
    T = qkv.shape[0]
    tq = min(SWA_TQ, T)
    nsub = tq // SWA_WINDOW
    cur, prev = _swa_specs(tq)
    ident = lambda i: i

    def body(q_ref, kc_ref, kp_ref, vc_ref, vp_ref, sk_ref, o_ref, lse_ref, s_sc, e_sc):
        i = pl.program_id(1)
        kcat = jnp.concatenate([kp_ref[...], kc_ref[...]], axis=0)
        vcat = jnp.concatenate([vp_ref[...], vc_ref[...]], axis=0)
        vis_first, vis_in = _swa_visible(i)
        lane = lax.broadcasted_iota(jnp.int32, (1, LANES), 1)
        lse_all = jnp.zeros((tq, LANES), F32)
        for pp in range(2):
            q2 = q_ref[:, pp * LANES:(pp + 1) * LANES]
            outs = []
            for hf in range(2):
                g = 2 * pp + hf
                qm = jnp.where(_half(lane, hf), q2, jnp.zeros_like(q2))
                for nb in range(nsub):
                    rows = slice(nb * SWA_WINDOW, (nb + 1) * SWA_WINDOW)
                    s = _nt(qm[rows], kcat[nb * SWA_WINDOW:nb * SWA_WINDOW + W2]) * 0.125
                    s_sc[rows, :] = jnp.where(vis_first if nb == 0 else vis_in, s, -1e30)
                sk = sk_ref[g:g + 1, 0:1]
                m = jnp.maximum(jnp.max(s_sc[...], axis=1, keepdims=True), sk)
                m_rep = jnp.broadcast_to(m, (tq, LANES))
                parts = []
                for r0 in range(0, tq, SWA_RB):
                    rs = slice(r0, r0 + SWA_RB)
                    e0 = jnp.exp(s_sc[rs, 0:LANES] - m_rep[rs])
                    e1 = jnp.exp(s_sc[rs, LANES:W2] - m_rep[rs])
                    e_sc[rs, :] = jnp.concatenate([e0.astype(BF16), e1.astype(BF16)], axis=1)
                    parts.append(e0 + e1)
                den = jnp.sum(jnp.concatenate(parts, axis=0), axis=1, keepdims=True) + jnp.exp(sk - m)
                pv = [_dot(e_sc[nb * SWA_WINDOW:(nb + 1) * SWA_WINDOW, :], vcat[nb * SWA_WINDOW:nb * SWA_WINDOW + W2])
                      for nb in range(nsub)]
                outs.append(jnp.concatenate(pv, axis=0) * (1.0 / den))
                lse_all = jnp.where(lane == g, m + jnp.log(den), lse_all)
            o_ref[:, pp * LANES:(pp + 1) * LANES] = jnp.where(lane < 64, outs[0], outs[1]).astype(BF16)
        lse_ref[...] = lse_all

    return pl.pallas_call(
        body, name=name, grid=(4, T // tq),
        in_specs=[pl.BlockSpec((tq, 2 * LANES), lambda kv, i: (i, kv)), cur(ident), prev(ident), cur(ident), prev(ident),
                  pl.BlockSpec((None, 8, LANES), lambda kv, i: (kv, 0, 0))],
        out_specs=[pl.BlockSpec((tq, 2 * LANES), lambda kv, i: (i, kv)), cur(ident)],
        out_shape=[jax.ShapeDtypeStruct((T, D_MODEL), BF16), jax.ShapeDtypeStruct((T, 4 * LANES), F32)],
        scratch_shapes=[pltpu.VMEM((tq, W2), F32), pltpu.VMEM((tq, W2), BF16)],
        compiler_params=_cparams("parallel", "arbitrary"),
    )(qkv, kdup, kdup, vdup, vdup, sinks_b)


def swa_bwd(qkv, kdup, vdup, sinks_b, o, lse, do, name):
    T = qkv.shape[0]
    tq = min(SWA_TQ, T)
    n = T // tq
    nsub = tq // SWA_WINDOW
    cur, prev = _swa_specs(tq)
    rev = lambda i: n - 1 - i

    def body(q_ref, kc_ref, kp_ref, vc_ref, vp_ref, sk_ref, o_ref, lse_ref, do_ref, dq_ref, dk_ref, dv_ref, dsk_ref,
             ck_sc, cv_sc, dkc_sc, dvc_sc):
        i = pl.program_id(1)

        @pl.when(i == 0)
        def _():
            ck_sc[...] = jnp.zeros(ck_sc.shape, F32)
            cv_sc[...] = jnp.zeros(cv_sc.shape, F32)
            dsk_ref[...] = jnp.zeros(dsk_ref.shape, F32)

        kcat = jnp.concatenate([kp_ref[...], kc_ref[...]], axis=0)
        vcat = jnp.concatenate([vp_ref[...], vc_ref[...]], axis=0)
        vis_first, vis_in = _swa_visible(n - 1 - i)
        lane = lax.broadcasted_iota(jnp.int32, (1, LANES), 1)
        dkc_sc[...] = jnp.zeros(dkc_sc.shape, F32)
        dvc_sc[...] = jnp.zeros(dvc_sc.shape, F32)
        for pp in range(2):
            sl = slice(pp * LANES, (pp + 1) * LANES)
            q2, do2, o2 = q_ref[:, sl], do_ref[:, sl], o_ref[:, sl]
            dqs = []
            for hf in range(2):
                g = 2 * pp + hf
                lm = _half(lane, hf)
                qm = jnp.where(lm, q2, jnp.zeros_like(q2))
                dom = jnp.where(lm, do2, jnp.zeros_like(do2))
                delta = jnp.sum(dom.astype(F32) * o2.astype(F32), axis=1, keepdims=True)
                lse_g = lse_ref[:, g:g + 1]
                psk = jnp.exp(sk_ref[g:g + 1, 0:1] - lse_g)
                dsk_ref[g:g + 1, :] += jnp.zeros((1, LANES), F32) - jnp.sum(psk * delta, axis=0, keepdims=True)
                dq_parts = []
                for nb in range(nsub):
                    rows = slice(nb * SWA_WINDOW, (nb + 1) * SWA_WINDOW)
                    band = slice(nb * SWA_WINDOW, nb * SWA_WINDOW + W2)
                    s = jnp.where(vis_first if nb == 0 else vis_in, _nt(qm[rows], kcat[band]) * 0.125, -1e30)
                    p = jnp.exp(s - lse_g[rows])
                    dsb = (p * (_nt(dom[rows], vcat[band]) - delta[rows]) * 0.125).astype(BF16)
                    dq_parts.append(_dot(dsb, kcat[band]))
                    dkc_sc[band, :] += _tn(dsb, qm[rows])
                    dvc_sc[band, :] += _tn(p.astype(BF16), dom[rows])
                dqs.append(jnp.concatenate(dq_parts, axis=0))
            dq_ref[:, sl] = jnp.where(lane < 64, dqs[0], dqs[1]).astype(BF16)
        dkc = dkc_sc[...]
        dvc = dvc_sc[...]
        dkc = dkc + pltpu.roll(dkc, 64, 1)
        dvc = dvc + pltpu.roll(dvc, 64, 1)
        for full, ref, carry in ((dkc, dk_ref, ck_sc), (dvc, dv_ref, cv_sc)):
            if tq > SWA_WINDOW:
                ref[0:tq - SWA_WINDOW, :] = full[SWA_WINDOW:tq, :]
            ref[tq - SWA_WINDOW:tq, :] = full[tq:tq + SWA_WINDOW, :] + carry[...]
            carry[...] = full[0:SWA_WINDOW, :]

    wide = pl.BlockSpec((tq, 2 * LANES), lambda kv, i: (rev(i), kv))
    return pl.pallas_call(
        body, name=name, grid=(4, n),
        in_specs=[wide, cur(rev), prev(rev), cur(rev), prev(rev),
                  pl.BlockSpec((None, 8, LANES), lambda kv, i: (kv, 0, 0)), wide, cur(rev), wide],
        out_specs=[wide, cur(rev), cur(rev), pl.BlockSpec((None, 8, LANES), lambda kv, i: (kv, 0, 0))],
        out_shape=[jax.ShapeDtypeStruct((T, D_MODEL), BF16), jax.ShapeDtypeStruct((T, 4 * LANES), F32),
                   jax.ShapeDtypeStruct((T, 4 * LANES), F32), jax.ShapeDtypeStruct((4, 8, LANES), F32)],
        scratch_shapes=[pltpu.VMEM((SWA_WINDOW, LANES), F32), pltpu.VMEM((SWA_WINDOW, LANES), F32),
                        pltpu.VMEM((tq + SWA_WINDOW, LANES), F32), pltpu.VMEM((tq + SWA_WINDOW, LANES), F32)],
        compiler_params=_cparams("arbitrary", "arbitrary"),
    )(qkv, kdup, kdup, vdup, vdup, sinks_b, o, lse, do)


def fox_gate_fwd(fl, qkv, name):
    T = fl.shape[0]
    ts = min(SCAN_T, T)

    def body(fl_ref, q_ref, k_ref, qa_ref, ka_ref, carry):
        @pl.when(pl.program_id(0) == 0)
        def _():
            carry[...] = jnp.zeros(carry.shape, F32)

        xv = fl_ref[...]
        ls = jnp.minimum(xv, 0.0) - jnp.log(1.0 + jnp.exp(-jnp.abs(xv)))
        tri = (lax.broadcasted_iota(jnp.int32, (ts, ts), 0) >= lax.broadcasted_iota(jnp.int32, (ts, ts), 1)).astype(F32)
        cs = jnp.dot(tri, ls, precision=HI, preferred_element_type=F32) + carry[...]
        carry[...] = cs[ts - 1:ts, :]
        c1 = cs.astype(BF16).astype(F32)
        c2 = (cs - c1).astype(BF16).astype(F32)
        c3 = (cs - c1 - c2).astype(BF16).astype(F32)
        lane = lax.broadcasted_iota(jnp.int32, (1, LANES), 1)
        ones_q = jnp.where((lane >= 67) & (lane < 70), 1.0, 0.0)
        ones_k = jnp.where((lane >= 64) & (lane < 67), 1.0, 0.0)
        for b in range(8):
            qf = q_ref[:, b * LANES:(b + 1) * LANES].astype(F32) * 0.125
            kf = k_ref[:, b * LANES:(b + 1) * LANES].astype(F32)
            for hf in range(2):
                h = 2 * b + hf
                a1, a2, a3 = c1[:, h:h + 1], c2[:, h:h + 1], c3[:, h:h + 1]
                aux_q = jnp.where(lane == 64, a1, jnp.where(lane == 65, a2, jnp.where(lane == 66, a3, ones_q)))
                aux_k = jnp.where(lane == 67, -a1, jnp.where(lane == 68, -a2, jnp.where(lane == 69, -a3, ones_k)))
                qs = qf if hf == 0 else pltpu.roll(qf, 64, 1)
                ks = kf if hf == 0 else pltpu.roll(kf, 64, 1)
                qa_ref[:, h * LANES:(h + 1) * LANES] = jnp.where(lane < 64, qs, aux_q).astype(BF16)
                ka_ref[:, h * LANES:(h + 1) * LANES] = jnp.where(lane < 64, ks, aux_k).astype(BF16)

    out = pl.BlockSpec((ts, 16 * LANES), lambda i: (i, 0))
    return pl.pallas_call(
        body, name=name, grid=(T // ts,),
        in_specs=[pl.BlockSpec((ts, LANES), lambda i: (i, 0)), pl.BlockSpec((ts, D_MODEL), lambda i: (i, 0)),
                  pl.BlockSpec((ts, D_MODEL), lambda i: (i, 1))],
        out_specs=[out, out],
        out_shape=[jax.ShapeDtypeStruct((T, 16 * LANES), BF16), jax.ShapeDtypeStruct((T, 16 * LANES), BF16)],
        scratch_shapes=[pltpu.VMEM((1, LANES), F32)],
        compiler_params=_cparams("arbitrary"),
    )(fl, qkv, qkv)


def fox_gate_bwd(fl, dc, name):
    T = fl.shape[0]
    ts = min(SCAN_T, T)
    n = T // ts

    def body(fl_ref, dc_ref, o_ref, carry):
        @pl.when(pl.program_id(0) == 0)
        def _():
            carry[...] = jnp.zeros(carry.shape, F32)

        tri = (lax.broadcasted_iota(jnp.int32, (ts, ts), 0) <= lax.broadcasted_iota(jnp.int32, (ts, ts), 1)).astype(F32)
        rs = jnp.dot(tri, dc_ref[...], precision=HI, preferred_element_type=F32) + carry[...]
        carry[...] = rs[0:1, :]
        o_ref[...] = rs * (1.0 / (1.0 + jnp.exp(fl_ref[...])))

    blk = pl.BlockSpec((ts, LANES), lambda i: (n - 1 - i, 0))
    return pl.pallas_call(
        body, name=name, grid=(n,), in_specs=[blk, blk], out_specs=blk,
        out_shape=jax.ShapeDtypeStruct((T, LANES), F32), scratch_shapes=[pltpu.VMEM((1, LANES), F32)],
        compiler_params=_cparams("arbitrary"),
    )(fl, dc)


FOX_RB = 32


def _half(lane, hf):
    return (lane < 64) if hf == 0 else (lane >= 64)


def _pair(lane, a, b):
    return jnp.where(lane < 64, a, pltpu.roll(b, 64, 1)), jnp.where(lane < 64, pltpu.roll(a, 64, 1), b)


def fox_fwd(qa, ka, qkv, name):
    T = qa.shape[0]
    t = min(FOX_T, T)
    n = T // t

    def body(qa_ref, ka_ref, v_ref, o_ref, lse_ref, m_sc, l_sc, acc_sc, ls_sc, s_sc, p_sc):
        i, j = pl.program_id(1), pl.program_id(2)
        lane = lax.broadcasted_iota(jnp.int32, (1, LANES), 1)

        @pl.when(j == 0)
        def _():
            m_sc[...] = jnp.full(m_sc.shape, -1e30, F32)
            l_sc[...] = jnp.zeros(l_sc.shape, F32)
            acc_sc[...] = jnp.zeros(acc_sc.shape, F32)

        def tile(diag):
            v2 = v_ref[...]
            for hf in range(2):
                hs = slice(hf * LANES, (hf + 1) * LANES)
                sv = _nt(qa_ref[:, hs], ka_ref[:, hs])
                if diag:
                    vis = lax.broadcasted_iota(jnp.int32, (t, t), 0) >= lax.broadcasted_iota(jnp.int32, (t, t), 1)
                    sv = jnp.where(vis, sv, -1e30)
                s_sc[...] = sv
                m_old = m_sc[hf]
                m_new = jnp.maximum(m_old, jnp.max(s_sc[...], axis=1, keepdims=True))
                al = jnp.exp(m_old - m_new)
                m_sc[hf] = m_new
                for r0 in range(0, t, FOX_RB):
                    rs = slice(r0, r0 + FOX_RB)
                    mrow = m_new[rs, :]
                    part, pieces = None, []
                    for cb in range(0, t, LANES):
                        pc = jnp.exp(s_sc[rs, cb:cb + LANES] - mrow)
                        part = pc if part is None else part + pc
                        pieces.append(pc.astype(BF16))
                    p_sc[rs, :] = jnp.concatenate(pieces, axis=1)
                    ls_sc[rs, :] = part
                l_sc[hf] = al * l_sc[hf] + ls_sc[...]
                acc_sc[hf] = al * acc_sc[hf] + _dot(p_sc[...], v2)

        @pl.when(j < i)
        def _():
            tile(False)

        @pl.when(j == i)
        def _():
            tile(True)
            l0 = jnp.sum(l_sc[0], axis=1, keepdims=True)
            l1 = jnp.sum(l_sc[1], axis=1, keepdims=True)
            o_ref[...] = jnp.where(lane < 64, acc_sc[0] / l0, acc_sc[1] / l1).astype(BF16)
            lse_ref[...] = jnp.where(lane < 64, m_sc[0] + jnp.log(l0), m_sc[1] + jnp.log(l1))

    oblk = pl.BlockSpec((t, LANES), lambda p, i, j: (i, p))
    return pl.pallas_call(
        body, name=name, grid=(8, n, n),
        in_specs=[pl.BlockSpec((t, 2 * LANES), lambda p, i, j: (i, p)),
                  pl.BlockSpec((t, 2 * LANES), lambda p, i, j: (jnp.minimum(j, i), p)),
                  pl.BlockSpec((t, LANES), lambda p, i, j: (jnp.minimum(j, i), 16 + p))],
        out_specs=[oblk, oblk],
        out_shape=[jax.ShapeDtypeStruct((T, D_MODEL), BF16), jax.ShapeDtypeStruct((T, D_MODEL), F32)],
        scratch_shapes=[pltpu.VMEM((2, t, LANES), F32), pltpu.VMEM((2, t, LANES), F32), pltpu.VMEM((2, t, LANES), F32),
                        pltpu.VMEM((t, LANES), F32), pltpu.VMEM((t, t), F32), pltpu.VMEM((t, t), BF16)],
        compiler_params=_cparams("parallel", "parallel", "arbitrary"),
    )(qa, ka, qkv)


def fox_delta(do, o, name):
    T = do.shape[0]
    tm = min(TM, T)

    def body(do_ref, o_ref, d_ref):
        lane = lax.broadcasted_iota(jnp.int32, (1, LANES), 1)
        out = jnp.zeros((tm, LANES), F32)
        for b in range(8):
            d = do_ref[:, b * LANES:(b + 1) * LANES].astype(F32) * o_ref[:, b * LANES:(b + 1) * LANES].astype(F32)
            for hf in range(2):
                out = jnp.where(lane == 2 * b + hf, jnp.sum(jnp.where(_half(lane, hf), d, 0.0), axis=1, keepdims=True), out)
        d_ref[...] = out

    row = pl.BlockSpec((tm, D_MODEL), lambda i: (i, 0))
    return pl.pallas_call(
        body, name=name, grid=(T // tm,), in_specs=[row, row],
        out_specs=pl.BlockSpec((tm, LANES), lambda i: (i, 0)),
        out_shape=jax.ShapeDtypeStruct((T, LANES), F32),
        compiler_params=_cparams("parallel"),
    )(do, o)


def fox_bwd(qa, ka, qkv, lse_row, delta_row, do, name):
    T = qa.shape[0]
    t = min(FOX_T, T)
    n = T // t

    def body(qa_ref, ka_ref, v_ref, lr_ref, dr_ref, do_ref, dq_ref, auxq_ref, dk_ref, dv_ref, aux_ref,
             dq_sc, dk_sc, dv_sc, s_sc, dp_sc, p_sc, ds_sc, dqo_sc, auxo_sc, out_sems):
        j, i = pl.program_id(1), pl.program_id(2)
        lane = lax.broadcasted_iota(jnp.int32, (1, LANES), 1)
        qrows = pl.ds(pl.multiple_of(i * t, t), t)

        @pl.when(i == 0)
        def _():
            dk_sc[...] = jnp.zeros(dk_sc.shape, F32)
            dv_sc[...] = jnp.zeros(dv_sc.shape, F32)

        @pl.when(j == 0)
        def _():
            dq_sc[:, qrows, :] = jnp.zeros((2, t, LANES), F32)

        def tile(diag):
            v2, do2 = v_ref[...], do_ref[...]
            for hf in range(2):
                hs = slice(hf * LANES, (hf + 1) * LANES)
                lm = _half(lane, hf)
                qh = qa_ref[:, hs]
                s_sc[...] = _nt(ka_ref[:, hs], qh)
                dp_sc[...] = _nt(jnp.where(lm, v2, jnp.zeros_like(v2)), do2)
                lrow, drow = lr_ref[hf:hf + 1, :], dr_ref[hf:hf + 1, :]
                for r0 in range(0, t, FOX_RB):
                    rs = slice(r0, r0 + FOX_RB)
                    sv = s_sc[rs, :]
                    if diag:
                        vis = lax.broadcasted_iota(jnp.int32, (FOX_RB, t), 1) >= (r0 + lax.broadcasted_iota(jnp.int32, (FOX_RB, t), 0))
                        sv = jnp.where(vis, sv, -1e30)
                    p = jnp.exp(sv - lrow)
                    p_sc[rs, :] = p.astype(BF16)
                    ds_sc[rs, :] = (p * (dp_sc[rs, :] - drow)).astype(BF16)
                dv_sc[...] += _dot(p_sc[...], jnp.where(lm, do2, jnp.zeros_like(do2)))
                dk_sc[hf] += _dot(ds_sc[...], qh)
                dq_sc[hf, qrows, :] += _tn(ds_sc[...], ka_ref[:, hs])

        @pl.when(i > j)
        def _():
            tile(False)

        @pl.when(i == j)
        def _():
            tile(True)
            dq, aux = _pair(lane, dq_sc[0, qrows, :], dq_sc[1, qrows, :])
            dqo_sc[...] = (dq * 0.125).astype(BF16)
            auxo_sc[...] = aux
            cols = pl.ds(pl.multiple_of(pl.program_id(0) * LANES, LANES), LANES)
            c1 = pltpu.make_async_copy(dqo_sc, dq_ref.at[qrows, cols], out_sems.at[0])
            c2 = pltpu.make_async_copy(auxo_sc, auxq_ref.at[qrows, cols], out_sems.at[1])
            c1.start()
            c2.start()
            c1.wait()
            c2.wait()

        @pl.when(i == n - 1)
        def _():
            dk, aux = _pair(lane, dk_sc[0], dk_sc[1])
            dk_ref[...] = dk.astype(BF16)
            aux_ref[...] = aux
            dv_ref[...] = dv_sc[...].astype(BF16)

    qblk = pl.BlockSpec((t, LANES), lambda p, j, i: (jnp.maximum(i, j), p))
    kblk = pl.BlockSpec((t, LANES), lambda p, j, i: (j, p))
    rblk = pl.BlockSpec((None, 2, t), lambda p, j, i: (p, 0, jnp.maximum(i, j)))
    bf, f32 = jax.ShapeDtypeStruct((T, D_MODEL), BF16), jax.ShapeDtypeStruct((T, D_MODEL), F32)
    return pl.pallas_call(
        body, name=name, grid=(8, n, n),
        in_specs=[pl.BlockSpec((t, 2 * LANES), lambda p, j, i: (jnp.maximum(i, j), p)),
                  pl.BlockSpec((t, 2 * LANES), lambda p, j, i: (j, p)),
                  pl.BlockSpec((t, LANES), lambda p, j, i: (j, 16 + p)), rblk, rblk, qblk],
        out_specs=[pl.BlockSpec(memory_space=pl.ANY), pl.BlockSpec(memory_space=pl.ANY), kblk, kblk, kblk],
        out_shape=[bf, f32, bf, bf, f32],
        scratch_shapes=[pltpu.VMEM((2, T, LANES), F32), pltpu.VMEM((2, t, LANES), F32), pltpu.VMEM((t, LANES), F32),
                        pltpu.VMEM((t, t), F32), pltpu.VMEM((t, t), F32), pltpu.VMEM((t, t), BF16),
                        pltpu.VMEM((t, t), BF16), pltpu.VMEM((t, LANES), BF16), pltpu.VMEM((t, LANES), F32),
                        pltpu.SemaphoreType.DMA((2,))],
        compiler_params=_cparams("arbitrary", "arbitrary", "arbitrary"),
    )(qa, ka, qkv, lse_row, delta_row, do)


C = HGRN_CHUNK
LEVELS = (64, 32, 16, 8, 4, 2)


def _pivot(b, B, row):
    if B == C:
        return jnp.broadcast_to(b[C // 2 - 1:C // 2, :], b.shape)
    if B >= 8:
        b3 = b.reshape(C // B, B, LANES)
        return jnp.broadcast_to(b3[:, B // 2 - 1:B // 2, :], b3.shape).reshape(C, LANES)
    if B == 4:
        y = jnp.where((row & 3) == 1, b, 0.0)
        return y + pltpu.roll(y, 1, 0) + pltpu.roll(y, 2, 0) + pltpu.roll(y, C - 1, 0)
    y = jnp.where((row & 1) == 0, b, 0.0)
    return y + pltpu.roll(y, 1, 0)


def _level_factors(bcum):
    row = lax.broadcasted_iota(jnp.int32, (C, 1), 0)
    out = []
    for B in LEVELS:
        upper = (row & (B - 1)) >= B // 2
        e = jnp.exp(-jnp.abs(bcum - _pivot(bcum, B, row)))
        out.append((B, jnp.where(upper, e, 0.0), jnp.where(upper, 0.0, e)))
    return out


def _same_block(B):
    sh = B.bit_length() - 1
    r = lax.broadcasted_iota(jnp.int32, (C, C), 0)
    c = lax.broadcasted_iota(jnp.int32, (C, C), 1)
    return (r >> sh) == (c >> sh)


def _hgrn_gates(q, fl, lb):
    sg = _sigmoid(fl)
    f = lb + (1.0 - lb) * sg
    sq = _sigmoid(q)
    return sg, f, jnp.log(f), 1.0 - f, sq, q * sq


def _cumsum_rows(x, reverse=False):
    r = lax.broadcasted_iota(jnp.int32, (C, C), 0)
    c = lax.broadcasted_iota(jnp.int32, (C, C), 1)
    tri = ((r <= c) if reverse else (r >= c)).astype(F32)
    return jnp.dot(tri, x, precision=HI, preferred_element_type=F32)


def _intra(qs, k, factors):
    r = lax.broadcasted_iota(jnp.int32, (C, C), 0)
    c = lax.broadcasted_iota(jnp.int32, (C, C), 1)
    a = jnp.where(r == c, jnp.sum(qs * k, axis=1, keepdims=True), 0.0)
    ops = []
    for B, eq, ek in factors:
        ql, kl = (qs * eq).astype(BF16), (k * ek).astype(BF16)
        al = _nt(ql, kl)
        a = a + (al if B == C else jnp.where(_same_block(B), al, 0.0))
        ops.append((ql, kl))
    return a, ops


def hgrn_fwd(proj, lb, gn, name):
    T = proj.shape[0]
    tg = min(HGRN_TG, T)
    nch = tg // C

    def body(q_ref, fl_ref, v_ref, g_ref, lb_ref, gn_ref, ao_ref, o_ref, st_ref, st_sc):
        @pl.when(pl.program_id(1) == 0)
        def _():
            st_sc[...] = jnp.zeros(st_sc.shape, F32)

        lb_v, gn_v = lb_ref[...], gn_ref[...]

        def chunk(ci, carry):
            rows = pl.ds(pl.multiple_of(ci * C, C), C)
            _, f, lf, k, _, qs = _hgrn_gates(q_ref[rows, :], fl_ref[rows, :], lb_v)
            vb = v_ref[rows, :].astype(BF16)
            gv = g_ref[rows, :]
            bcum = _cumsum_rows(lf)
            blast = bcum[C - 1:C, :]
            a, _ = _intra(qs, k, _level_factors(bcum))
            st = st_sc[...]
            st_ref[ci] = st
            o = _dot(a.astype(BF16), vb) + _nt((qs * jnp.exp(bcum)).astype(BF16), st.astype(BF16))
            st_sc[...] = st * jnp.exp(blast) + _tn(vb, (k * jnp.exp(blast - bcum)).astype(BF16))
            o_ref[rows, :] = o
            ao_ref[rows, :] = (o * _rms(o) * gn_v * (gv * _sigmoid(gv))).astype(BF16)
            return carry

        lax.fori_loop(0, nch, chunk, 0, unroll=8)

    col = lambda off: pl.BlockSpec((tg, LANES), lambda h, i: (i, off + h))
    one = pl.BlockSpec((1, LANES), lambda h, i: (0, h))
    return pl.pallas_call(
        body, name=name, grid=(8, T // tg),
        in_specs=[col(0), col(8), col(16), col(24), one, one],
        out_specs=[col(0), col(0), pl.BlockSpec((None, nch, LANES, LANES), lambda h, i: (h, i, 0, 0))],
        out_shape=[jax.ShapeDtypeStruct((T, D_MODEL), BF16), jax.ShapeDtypeStruct((T, D_MODEL), F32),
                   jax.ShapeDtypeStruct((8, T // C, LANES, LANES), F32)],
        scratch_shapes=[pltpu.VMEM((LANES, LANES), F32)],
        compiler_params=_cparams("parallel", "arbitrary"),
    )(proj, proj, proj, proj, lb, gn)


def hgrn_bwd(proj, lb, gn, o_raw, states, dao, name):
    T = proj.shape[0]
    tg = min(HGRN_TG, T)
    nch = tg // C
    n = T // tg

    def body(q_ref, fl_ref, v_ref, g_ref, lb_ref, gn_ref, o_ref, st_ref, dao_ref,
             dq_ref, dfl_ref, dv_ref, dg_ref, dlb_ref, dgn_ref, dst_sc):
        @pl.when(pl.program_id(1) == 0)
        def _():
            dst_sc[...] = jnp.zeros(dst_sc.shape, F32)
            dlb_ref[...] = jnp.zeros(dlb_ref.shape, F32)
            dgn_ref[...] = jnp.zeros(dgn_ref.shape, F32)

        lb_v, gn_v = lb_ref[...], gn_ref[...]
        r64 = lax.broadcasted_iota(jnp.int32, (C, C), 0)
        c64 = lax.broadcasted_iota(jnp.int32, (C, C), 1)
        row = lax.broadcasted_iota(jnp.int32, (C, 1), 0)

        def chunk(cr, carry):
            ci = nch - 1 - cr
            rows = pl.ds(pl.multiple_of(ci * C, C), C)
            q, fl, gv = q_ref[rows, :], fl_ref[rows, :], g_ref[rows, :]
            sg, f, lf, k, sq, qs = _hgrn_gates(q, fl, lb_v)
            vb = v_ref[rows, :].astype(BF16)
            o = o_ref[rows, :]
            ro = _rms(o)
            on = o * ro
            sgg = _sigmoid(gv)
            gate = gv * sgg
            dao_v = dao_ref[rows, :].astype(F32)
            dg_ref[rows, :] = (dao_v * on * gn_v * (sgg * (1.0 + gv * (1.0 - sgg)))).astype(BF16)
            dgn_ref[...] += jnp.sum(dao_v * on * gate, axis=0, keepdims=True)
            don = dao_v * gn_v * gate
            do = ro * (don - on * jnp.mean(don * on, axis=-1, keepdims=True))
            dob = do.astype(BF16)
            bcum = _cumsum_rows(lf)
            blast = bcum[C - 1:C, :]
            factors = _level_factors(bcum)
            a, ops = _intra(qs, k, factors)
            eb = jnp.exp(bcum)
            ekb = jnp.exp(blast - bcum)
            qb = qs * eb
            kb = k * ekb
            st = st_ref[ci]
            dst = dst_sc[...]
            dstb = dst.astype(BF16)
            da = jnp.where(r64 >= c64, _nt(dob, vb), 0.0)
            dv_ref[rows, :] = (_tn(a.astype(BF16), dob) + _nt(kb.astype(BF16), dstb)).astype(BF16)
            dqb = _dot(dob, st.astype(BF16))
            dkb = _dot(vb, dstb)
            eblast = jnp.exp(blast)
            dst_sc[...] = dst * eblast + _tn(dob, qb.astype(BF16))
            dblast = eblast * jnp.sum(dst * st, axis=0, keepdims=True) + jnp.sum(dkb * kb, axis=0, keepdims=True)
            dad = jnp.sum(jnp.where(r64 == c64, da, 0.0), axis=1, keepdims=True)
            dqs = dqb * eb + dad * k
            dk = dkb * ekb + dad * qs
            dbcum = dqb * qb - dkb * kb + jnp.where(row == C - 1, dblast, 0.0)
            for (B, eq, ek), (ql, kl) in zip(factors, ops):
                dal = (da if B == C else jnp.where(_same_block(B), da, 0.0)).astype(BF16)
                dql, dkl = _dot(dal, kl), _tn(dal, ql)
                dqs = dqs + dql * eq
                dk = dk + dkl * ek
                dbcum = dbcum + (dql * ql.astype(F32) - dkl * kl.astype(F32))
            df = _cumsum_rows(dbcum, reverse=True) / f - dk
            dfl_ref[rows, :] = (df * (1.0 - lb_v) * sg * (1.0 - sg)).astype(BF16)
            dlb_ref[...] += jnp.sum(df * (1.0 - sg), axis=0, keepdims=True)
            dq_ref[rows, :] = (dqs * (sq * (1.0 + q * (1.0 - sq)))).astype(BF16)
            return carry

        lax.fori_loop(0, nch, chunk, 0, unroll=8)

    col = lambda off: pl.BlockSpec((tg, LANES), lambda h, i: (n - 1 - i, off + h))
    one = pl.BlockSpec((1, LANES), lambda h, i: (0, h))
    big = jax.ShapeDtypeStruct((T, D_MODEL), BF16)
    small = jax.ShapeDtypeStruct((1, D_MODEL), F32)
    return pl.pallas_call(
        body, name=name, grid=(8, n),
        in_specs=[col(0), col(8), col(16), col(24), one, one, col(0),
                  pl.BlockSpec((None, nch, LANES, LANES), lambda h, i: (h, n - 1 - i, 0, 0)), col(0)],
        out_specs=[col(0), col(0), col(0), col(0), one, one],
        out_shape=[big, big, big, big, small, small],
        scratch_shapes=[pltpu.VMEM((LANES, LANES), F32)],
        compiler_params=_cparams("arbitrary", "arbitrary"),
    )(proj, proj, proj, proj, lb, gn, o_raw, states, dao)


def lower_bound_fwd(logits, name):
    def body(l_ref, s_ref):
        lv = l_ref[...]
        e = jnp.exp(lv - jnp.max(lv, axis=0, keepdims=True))
        s_ref[...] = e / jnp.sum(e, axis=0, keepdims=True)

    return pl.pallas_call(body, name=name, out_shape=jax.ShapeDtypeStruct(logits.shape, F32))(logits)


def lower_bound_bwd(sm, dlb, name):
    def body(s_ref, d_ref, o_ref):
        s = s_ref[...]
        row = lax.broadcasted_iota(jnp.int32, s.shape, 0)
        o_ref[...] = d_ref[...] * s[1:2, :] * (jnp.where(row == 1, 1.0, 0.0) - s)

    return pl.pallas_call(body, name=name, out_shape=jax.ShapeDtypeStruct(sm.shape, F32))(sm, dlb)


def _pad_rows(flat, mult):
    rows = -(-flat.shape[-1] // D_MODEL)
    rows = -(-rows // mult) * mult
    pad = rows * D_MODEL - flat.shape[-1]
    flat = jnp.pad(flat, [(0, 0)] * (flat.ndim - 1) + [(0, pad)])
    return flat.reshape(flat.shape[:-1] + (rows, D_MODEL))


def _gather_weights(w):
    direct = [n for n in SHARDED if n not in BIASES and w[n].shape[SHARD_AXIS[n]] % LANES == 0]
    packed = [n for n in SHARDED if n not in direct]
    pieces = []
    for nme in packed:
        a = w[nme]
        if nme in BIASES:
            pieces.append(lax.bitcast_convert_type(a, BF16).reshape(-1))
        else:
            pieces.append(a.astype(BF16).reshape(-1))
    flat = _pad_rows(jnp.concatenate(pieces), 16)
    out = all_gather_shards([(w[n].astype(BF16), SHARD_AXIS[n]) for n in direct] + [(flat, None)])
    full = dict(zip(direct, out[:-1]))
    got, off = out[-1].reshape(N_DEV, -1), 0
    for nme in packed:
        shp = w[nme].shape
        cnt = 1
        for s in shp:
            cnt *= s
        if nme in BIASES:
            seg = got[:, off:off + 2 * cnt].reshape((N_DEV,) + shp + (2,))
            seg = lax.bitcast_convert_type(seg, F32)
            off += 2 * cnt
        else:
            seg = got[:, off:off + cnt].reshape((N_DEV,) + shp)
            off += cnt
        full[nme] = jnp.concatenate([seg[d] for d in range(N_DEV)], axis=SHARD_AXIS[nme])
    return full


def _pieces(gfull, axis):
    shp = gfull.shape
    a = gfull.reshape(shp[:axis] + (N_DEV, shp[axis] // N_DEV) + shp[axis + 1:])
    return jnp.moveaxis(a, axis, 0).reshape(N_DEV, -1)


def kernel(x, norm_mix, norm_mlp, norm_final, w_up, w_down, swa_w_qkv, swa_b_qkv, swa_sinks, swa_w_o, hgrn_w_in, hgrn_lb_logits, hgrn_g_norm, hgrn_w_o, fox_w_in, fox_b_in, fox_w_o, loss_target, m_norm_mix, m_norm_mlp, m_norm_final, m_w_up, m_w_down, m_swa_w_qkv, m_swa_b_qkv, m_swa_sinks, m_swa_w_o, m_hgrn_w_in, m_hgrn_lb_logits, m_hgrn_g_norm, m_hgrn_w_o, m_fox_w_in, m_fox_b_in, m_fox_w_o, v_norm_mix, v_norm_mlp, v_norm_final, v_w_up, v_w_down, v_swa_w_qkv, v_swa_b_qkv, v_swa_sinks, v_swa_w_o, v_hgrn_w_in, v_hgrn_lb_logits, v_hgrn_g_norm, v_hgrn_w_o, v_fox_w_in, v_fox_b_in, v_fox_w_o):
    w = dict(norm_mix=norm_mix, norm_mlp=norm_mlp, norm_final=norm_final, w_up=w_up, w_down=w_down,
             swa_w_qkv=swa_w_qkv, swa_b_qkv=swa_b_qkv, swa_sinks=swa_sinks, swa_w_o=swa_w_o, hgrn_w_in=hgrn_w_in,
             hgrn_lb_logits=hgrn_lb_logits, hgrn_g_norm=hgrn_g_norm, hgrn_w_o=hgrn_w_o, fox_w_in=fox_w_in,
             fox_b_in=fox_b_in, fox_w_o=fox_w_o)
    mom = dict(norm_mix=m_norm_mix, norm_mlp=m_norm_mlp, norm_final=m_norm_final, w_up=m_w_up, w_down=m_w_down,
               swa_w_qkv=m_swa_w_qkv, swa_b_qkv=m_swa_b_qkv, swa_sinks=m_swa_sinks, swa_w_o=m_swa_w_o,
               hgrn_w_in=m_hgrn_w_in, hgrn_lb_logits=m_hgrn_lb_logits, hgrn_g_norm=m_hgrn_g_norm, hgrn_w_o=m_hgrn_w_o,
               fox_w_in=m_fox_w_in, fox_b_in=m_fox_b_in, fox_w_o=m_fox_w_o)
    var = dict(norm_mix=v_norm_mix, norm_mlp=v_norm_mlp, norm_final=v_norm_final, w_up=v_w_up, w_down=v_w_down,
               swa_w_qkv=v_swa_w_qkv, swa_b_qkv=v_swa_b_qkv, swa_sinks=v_swa_sinks, swa_w_o=v_swa_w_o,
               hgrn_w_in=v_hgrn_w_in, hgrn_lb_logits=v_hgrn_lb_logits, hgrn_g_norm=v_hgrn_g_norm, hgrn_w_o=v_hgrn_w_o,
               fox_w_in=v_fox_w_in, fox_b_in=v_fox_b_in, fox_w_o=v_fox_w_o)
    T = x.shape[1]
    x0 = x[0]
    tgt = loss_target[0]
    W = _gather_weights(w)
    zeros_b = jnp.zeros((1, 4 * D_MODEL), F32)

    def swa_layer(xin, i, j):
        qkv = norm_matmul(xin, norm_mix[i:i + 1], W['swa_w_qkv'][j], W['swa_b_qkv'][j:j + 1], BF16, f"swa_qkv_L{i}")
        dup = lambda a: jnp.broadcast_to(a.reshape(T, 4, 1, 64), (T, 4, 2, 64)).reshape(T, 4 * LANES)
        kdup, vdup = dup(qkv[:, 1024:1280]), dup(qkv[:, 1280:1536])
        sk = jnp.broadcast_to(jnp.pad(swa_sinks[j].reshape(4, 4), ((0, 0), (0, 4)))[:, :, None], (4, 8, LANES))
        ao, lse = swa_fwd(qkv, kdup, vdup, sk, f"swa_fwd_L{i}")
        xmid = matmul(ao, W['swa_w_o'][j], F32, f"swa_out_L{i}", res=xin)
        return xmid, (qkv, kdup, vdup, sk, ao, lse)

    def swa_layer_bwd(xin, saved, dmid, i, j, grads):
        qkv, kdup, vdup, sk, ao, lse = saved
        dao = matmul(dmid, W['swa_w_o'][j].T, BF16, f"swa_dout_L{i}")
        grads['swa_w_o'][j] = tn_matmul(ao, dmid, f"swa_dwo_L{i}")
        dq, dk, dv, dsk = swa_bwd(qkv, kdup, vdup, sk, ao, lse, dao, f"swa_bwd_L{i}")
        wt = W['swa_w_qkv'][j].T
        spread = lambda a: jnp.pad(a.reshape(4, 64, D_MODEL), ((0, 0), (0, 64), (0, 0))).reshape(4 * LANES, D_MODEL)
        gather = lambda a: a.reshape(a.shape[0], 4, LANES)[:, :, :64].reshape(a.shape[0], 256)
        dx, h, dg = proj_bwd(xin, norm_mix[i:i + 1], dmid,
                             [(dq, wt[:1024]), (dk, spread(wt[1024:1280])), (dv, spread(wt[1280:]))], f"swa_din_L{i}")
        gq, bq = tn_matmul(h, dq, f"swa_dwq_L{i}", colsum=True)
        gk, bk = tn_matmul(h, dk, f"swa_dwk_L{i}", colsum=True)
        gv, bv = tn_matmul(h, dv, f"swa_dwv_L{i}", colsum=True)
        grads['swa_w_qkv'][j] = jnp.concatenate([gq, gather(gk), gather(gv)], axis=1)
        grads['swa_b_qkv'][j] = jnp.concatenate([bq, gather(bk), gather(bv)], axis=1)[0]
        grads['swa_sinks'][j] = dsk[:, :4, 0].reshape(16)
        grads['norm_mix'][i] = dg[0]
        return dx

    lb_soft = lower_bound_fwd(hgrn_lb_logits, "hgrn_lb_fwd")
    lb = lb_soft[1:2]

    def hgrn_layer(xin, i, j):
        proj = norm_matmul(xin, norm_mix[i:i + 1], W['hgrn_w_in'][j], zeros_b, F32, f"hgrn_in_L{i}")
        ao, o_raw, states = hgrn_fwd(proj, lb, hgrn_g_norm[j:j + 1], f"hgrn_fwd_L{i}")
        xmid = matmul(ao, W['hgrn_w_o'][j], F32, f"hgrn_out_L{i}", res=xin)
        return xmid, (proj, ao, o_raw, states)

    def hgrn_layer_bwd(xin, saved, dmid, i, j, grads):
        proj, ao, o_raw, states = saved
        dao = matmul(dmid, W['hgrn_w_o'][j].T, BF16, f"hgrn_dout_L{i}")
        grads['hgrn_w_o'][j] = tn_matmul(ao, dmid, f"hgrn_dwo_L{i}")
        dq, dfl, dv, dgt, dlb, dgn = hgrn_bwd(proj, lb, hgrn_g_norm[j:j + 1], o_raw, states, dao, f"hgrn_bwd_L{i}")
        wt = W['hgrn_w_in'][j].T
        parts = [dq, dfl, dv, dgt]
        dx, h, dg = proj_bwd(xin, norm_mix[i:i + 1], dmid,
                             [(d, wt[n * D_MODEL:(n + 1) * D_MODEL]) for n, d in enumerate(parts)], f"hgrn_din_L{i}")
        grads['hgrn_w_in'][j] = jnp.concatenate(
            [tn_matmul(h, d, f"hgrn_dwin{n}_L{i}") for n, d in enumerate(parts)], axis=1)
        grads['hgrn_g_norm'][j] = dgn[0]
        grads['hgrn_lb_logits'] = lower_bound_bwd(lb_soft, dlb, "hgrn_lb_bwd")
        grads['norm_mix'][i] = dg[0]
        return dx

    def fox_layer(xin, i, j):
        w_in = W['fox_w_in'][j]
        b_in = W['fox_b_in'][j:j + 1]
        qkv = norm_matmul(xin, norm_mix[i:i + 1], w_in[:, :3072], b_in[:, :3072], BF16, f"fox_qkv_L{i}")
        wf = jnp.pad(w_in[:, 3072:], ((0, 0), (0, LANES - 16)))
        bf = jnp.pad(b_in[:, 3072:], ((0, 0), (0, LANES - 16)))
        fl = norm_matmul(xin, norm_mix[i:i + 1], wf, bf, F32, f"fox_f_L{i}")
        qa, ka = fox_gate_fwd(fl, qkv, f"fox_gate_L{i}")
        ao, lse = fox_fwd(qa, ka, qkv, f"fox_fwd_L{i}")
        xmid = matmul(ao, W['fox_w_o'][j], F32, f"fox_out_L{i}", res=xin)
        return xmid, (qkv, fl, qa, ka, ao, lse, wf)

    def fox_layer_bwd(xin, saved, dmid, i, j, grads):
        qkv, fl, qa, ka, ao, lse, wf = saved
        dao = matmul(dmid, W['fox_w_o'][j].T, BF16, f"fox_dout_L{i}")
        grads['fox_w_o'][j] = tn_matmul(ao, dmid, f"fox_dwo_L{i}")
        delta = fox_delta(dao, ao, f"fox_delta_L{i}")
        dq, aux_q, dk, dv, aux_k = fox_bwd(qa, ka, qkv, lse[:, ::64].T.reshape(8, 2, T),
                                           delta[:, :16].T.reshape(8, 2, T), dao, f"fox_bwd_L{i}")
        dcp = jnp.pad(aux_q[:, ::64] - aux_k[:, 3::64], ((0, 0), (0, LANES - 16)))
        dfl = fox_gate_bwd(fl, dcp, f"fox_dgate_L{i}")
        wt = W['fox_w_in'][j][:, :3072].T
        parts = [dq, dk, dv]
        dx, h, dg = proj_bwd(xin, norm_mix[i:i + 1], dmid,
                             [(d, wt[n * D_MODEL:(n + 1) * D_MODEL]) for n, d in enumerate(parts)] + [(dfl, wf.T)],
                             f"fox_din_L{i}")
        gw = [tn_matmul(h, d, f"fox_dw{n}_L{i}", colsum=True) for n, d in enumerate(parts + [dfl])]
        grads['fox_w_in'][j] = jnp.concatenate([g for g, _ in gw[:3]] + [gw[3][0][:, :16]], axis=1)
        grads['fox_b_in'][j] = jnp.concatenate([b for _, b in gw[:3]] + [gw[3][1][:, :16]], axis=1)[0]
        grads['norm_mix'][i] = dg[0]
        return dx

    mixers = [(swa_layer, swa_layer_bwd), (hgrn_layer, hgrn_layer_bwd), (fox_layer, fox_layer_bwd)]

    xs, mids, saves = [x0], [], []
    for i in range(DEPTH):
        xmid, saved = mixers[i % 3][0](xs[-1], i, i // 3)
        mids.append(xmid)
        saves.append(saved)
        xs.append(mlp_fwd(xmid, norm_mlp[i:i + 1], W['w_up'][i], W['w_down'][i], f"mlp_fwd_L{i}"))

    grads = {n: [None] * w[n].shape[0] for n in WEIGHTS if n not in ('norm_final', 'hgrn_lb_logits')}
    loss_part, dx, dgf = final_loss(xs[-1], norm_final.reshape(1, D_MODEL), tgt, "final_loss")
    grads['norm_final'] = dgf[0]
    for i in reversed(range(DEPTH)):
        dmid, h, a, du, dg = mlp_bwd(mids[i], norm_mlp[i:i + 1], W['w_up'][i], W['w_up'][i].T, W['w_down'][i].T, dx,
                                     f"mlp_bwd_L{i}")
        grads['w_up'][i] = tn_matmul(h, du, f"mlp_dwup_L{i}")
        grads['w_down'][i] = tn_matmul(a, dx, f"mlp_dwdown_L{i}")
        grads['norm_mlp'][i] = dg[0]
        dx = mixers[i % 3][1](xs[i], saves[i], dmid, i, i // 3, grads)
    gfull = {n: (g if not isinstance(g, list) else jnp.stack(g)) for n, g in grads.items()}

    mats = [n for n in SHARDED if n not in BIASES]
    view = lambda a: a.reshape(-1, a.shape[-1])
    sends = [_pieces(gfull[n], SHARD_AXIS[n]).astype(BF16).reshape((N_DEV,) + view(w[n]).shape) for n in mats]
    common = jnp.concatenate([gfull[n].reshape(-1) for n in REPLICATED] + [loss_part[0, 0:1]])
    small = jnp.concatenate([jnp.broadcast_to(common[None], (N_DEV, common.shape[0]))]
                            + [_pieces(gfull[n], SHARD_AXIS[n]) for n in BIASES], axis=1)
    recvs = all_to_all_rows(sends + [_pad_rows(small, 16)])
    tail = lambda vals: _pad_rows(jnp.concatenate([vals[n].reshape(-1) for n in REPLICATED] + [jnp.zeros((1,), F32)]
                                                  + [vals[n].reshape(-1) for n in BIASES]), 16)
    res = [{}, {}, {}, {}]
    for nme, rv in zip(mats, recvs):
        outs = reduce_adamw(rv, view(w[nme]), view(mom[nme]), view(var[nme]), f"adamw_{nme}")
        for o, r in zip(outs, res):
            r[nme] = o.reshape(w[nme].shape)
    outs = reduce_adamw(recvs[-1], tail(w), tail(mom), tail(var), "adamw_small")
    off = 0
    for nme in REPLICATED + ['loss'] + list(BIASES):
        cnt = 1 if nme == 'loss' else w[nme].size
        if nme == 'loss':
            loss = outs[0].reshape(-1)[off]
        else:
            for o, r in zip(outs, res):
                r[nme] = o.reshape(-1)[off:off + cnt].reshape(w[nme].shape)
        off += cnt
    return (loss, dx[None], *[res[0][n] for n in WEIGHTS], *[res[1][n] for n in WEIGHTS],
            *[res[2][n] for n in WEIGHTS], *[res[3][n] for n in WEIGHTS])
```

```python
import functools

import jax
import jax.numpy as jnp
from jax import lax
from jax.experimental import pallas as pl
from jax.experimental.pallas import tpu as pltpu

F32 = jnp.float32
BF16 = jnp.bfloat16
HI = lax.Precision.HIGHEST

N_DEV = 8
D_MODEL = 1024
DEPTH = 4
EPS = 1e-6
SWA_WINDOW = 128
HGRN_CHUNK = 64
LANES = 128
VMEM_LIMIT = 56 << 20

ADAM_LR, ADAM_B1, ADAM_B2, ADAM_EPS, ADAM_WD, ADAM_STEP = 0.001, 0.9, 0.999, 1e-08, 0.01, 10

TM = 512
TF = 512
TK = 512
FOX_T = 1024
SWA_TQ = 512
HGRN_TG = 512
SCAN_T = 256

WEIGHTS = ['norm_mix', 'norm_mlp', 'norm_final', 'w_up', 'w_down', 'swa_w_qkv', 'swa_b_qkv', 'swa_sinks', 'swa_w_o',
           'hgrn_w_in', 'hgrn_lb_logits', 'hgrn_g_norm', 'hgrn_w_o', 'fox_w_in', 'fox_b_in', 'fox_w_o']
SHARD_AXIS = {'norm_mix': None, 'norm_mlp': None, 'norm_final': None, 'w_up': 2, 'w_down': 1, 'swa_w_qkv': 2,
              'swa_b_qkv': 1, 'swa_sinks': None, 'swa_w_o': 1, 'hgrn_w_in': 2, 'hgrn_lb_logits': None,
              'hgrn_g_norm': None, 'hgrn_w_o': 1, 'fox_w_in': 2, 'fox_b_in': 1, 'fox_w_o': 1}
SHARDED = [n for n in WEIGHTS if SHARD_AXIS[n] is not None]
REPLICATED = [n for n in WEIGHTS if SHARD_AXIS[n] is None]
BIASES = ('swa_b_qkv', 'fox_b_in')


def _cparams(*sem):
    return pltpu.CompilerParams(dimension_semantics=sem, vmem_limit_bytes=VMEM_LIMIT)


def _nt(a, b):
    return lax.dot_general(a, b, (((1,), (1,)), ((), ())), preferred_element_type=F32)


def _tn(a, b):
    return lax.dot_general(a, b, (((0,), (0,)), ((), ())), preferred_element_type=F32)


def _dot(a, b):
    return jnp.dot(a, b, preferred_element_type=F32)


def _sigmoid(x):
    return 1.0 / (1.0 + jnp.exp(-x))


def _rms(xv):
    return lax.rsqrt(jnp.mean(xv * xv, axis=-1, keepdims=True) + EPS)


def _rms_bwd(xv, g, dh):
    r = _rms(xv)
    xhat = xv * r
    dhg = dh * g
    dx = r * (dhg - xhat * jnp.mean(dhg * xhat, axis=-1, keepdims=True))
    return dx, jnp.sum(dh * xhat, axis=0, keepdims=True)


def _my_id():
    return lax.axis_index("x"), lax.axis_index("y"), lax.axis_index("c")


def _peer(x, y, c, k):
    return (lax.rem(x + ((k >> 2) & 1), 2), lax.rem(y + ((k >> 1) & 1), 2), lax.rem(c + (k & 1), 2))


def all_gather_shards(shards):
    n = len(shards)

    def place(o_ref, local, axis, me):
        if axis is None:
            return o_ref.at[me]
        idx = [slice(None)] * local.ndim
        idx[axis] = pl.ds(pl.multiple_of(me * local.shape[axis], local.shape[axis]), local.shape[axis])
        return o_ref.at[tuple(idx)]

    def body(*refs):
        x_refs, o_refs = refs[:n], refs[n:2 * n]
        send_sems, recv_sems, loc_sems = refs[2 * n:]
        x, y, c = _my_id()
        me = 4 * x + 2 * y + c
        copies = []
        for a, (x_ref, o_ref, (local, axis)) in enumerate(zip(x_refs, o_refs, shards)):
            dst = place(o_ref, local, axis, me)
            mine = pltpu.make_async_copy(x_ref, dst, loc_sems.at[a])
            mine.start()
            copies.append(mine)
            for k in range(1, N_DEV):
                px, py, pc = _peer(x, y, c, k)
                sem = a * (N_DEV - 1) + k - 1
                cp = pltpu.make_async_remote_copy(
                    src_ref=x_ref, dst_ref=dst, send_sem=send_sems.at[sem], recv_sem=recv_sems.at[sem],
                    device_id=(px, py, pc), device_id_type=pl.DeviceIdType.MESH)
                cp.start()
                copies.append(cp)
        for cp in copies:
            cp.wait()

    def full_shape(local, axis):
        if axis is None:
            return (N_DEV,) + local.shape
        return local.shape[:axis] + (N_DEV * local.shape[axis],) + local.shape[axis + 1:]

    hbm = pl.BlockSpec(memory_space=pl.ANY)
    return pl.pallas_call(
        body, name="all_gather_weights",
        out_shape=[jax.ShapeDtypeStruct(full_shape(l, ax), l.dtype) for l, ax in shards],
        in_specs=[hbm] * n, out_specs=[hbm] * n,
        scratch_shapes=[pltpu.SemaphoreType.DMA((n * (N_DEV - 1),)), pltpu.SemaphoreType.DMA((n * (N_DEV - 1),)),
                        pltpu.SemaphoreType.DMA((n,))],
    )(*[l for l, _ in shards])


def all_to_all_rows(sends):
    n = len(sends)

    def body(*refs):
        s_refs, r_refs = refs[:n], refs[n:2 * n]
        send_sems, recv_sems, loc_sems = refs[2 * n:]
        x, y, c = _my_id()
        me = 4 * x + 2 * y + c
        copies = []
        for a, (s_ref, r_ref) in enumerate(zip(s_refs, r_refs)):
            mine = pltpu.make_async_copy(s_ref.at[me], r_ref.at[me], loc_sems.at[a])
            mine.start()
            copies.append(mine)
            for k in range(1, N_DEV):
                px, py, pc = _peer(x, y, c, k)
                sem = a * (N_DEV - 1) + k - 1
                cp = pltpu.make_async_remote_copy(
                    src_ref=s_ref.at[4 * px + 2 * py + pc], dst_ref=r_ref.at[me],
                    send_sem=send_sems.at[sem], recv_sem=recv_sems.at[sem],
                    device_id=(px, py, pc), device_id_type=pl.DeviceIdType.MESH)
                cp.start()
                copies.append(cp)
        for cp in copies:
            cp.wait()

    hbm = pl.BlockSpec(memory_space=pl.ANY)
    return pl.pallas_call(
        body, name="all_to_all_grads",
        out_shape=[jax.ShapeDtypeStruct(s.shape, s.dtype) for s in sends],
        in_specs=[hbm] * n, out_specs=[hbm] * n,
        scratch_shapes=[pltpu.SemaphoreType.DMA((n * (N_DEV - 1),)), pltpu.SemaphoreType.DMA((n * (N_DEV - 1),)),
                        pltpu.SemaphoreType.DMA((n,))],
    )(*sends)


def reduce_adamw(recv, w, m, v, name):
    R, C = w.shape
    tr = max(t for t in range(16, (1 << 18) // C + 1, 16) if R % t == 0)
    c1 = 1.0 / (1.0 - ADAM_B1 ** ADAM_STEP)
    c2 = 1.0 / (1.0 - ADAM_B2 ** ADAM_STEP)

    def body(r_ref, w_ref, m_ref, v_ref, g_ref, d_ref, nm_ref, nv_ref):
        g = r_ref[0].astype(F32)
        for s in range(1, N_DEV):
            g = g + r_ref[s].astype(F32)
        m2 = ADAM_B1 * m_ref[...] + (1.0 - ADAM_B1) * g
        v2 = ADAM_B2 * v_ref[...] + (1.0 - ADAM_B2) * (g * g)
        g_ref[...] = g
        nm_ref[...] = m2
        nv_ref[...] = v2
        d_ref[...] = -ADAM_LR * ((m2 * c1) / (jnp.sqrt(v2 * c2) + ADAM_EPS) + ADAM_WD * w_ref[...])

    row = pl.BlockSpec((tr, C), lambda i: (i, 0))
    shp = jax.ShapeDtypeStruct((R, C), F32)
    return pl.pallas_call(
        body, name=name, grid=(R // tr,),
        in_specs=[pl.BlockSpec((N_DEV, tr, C), lambda i: (0, i, 0)), row, row, row],
        out_specs=[row, row, row, row], out_shape=[shp, shp, shp, shp],
        compiler_params=_cparams("parallel"),
    )(recv, w, m, v)


def norm_matmul(x, g, w, b, out_dtype, name):
    T, N = x.shape[0], w.shape[1]
    tm, tn = min(TM, T), min(512, N)

    def body(x_ref, g_ref, w_ref, b_ref, o_ref, h_sc):
        @pl.when(pl.program_id(1) == 0)
        def _():
            xv = x_ref[...]
            h_sc[...] = (xv * _rms(xv) * g_ref[...]).astype(BF16)
        o_ref[...] = (_dot(h_sc[...], w_ref[...]) + b_ref[...]).astype(o_ref.dtype)

    return pl.pallas_call(
        body, name=name, grid=(T // tm, N // tn),
        in_specs=[pl.BlockSpec((tm, D_MODEL), lambda i, j: (i, 0)), pl.BlockSpec((1, D_MODEL), lambda i, j: (0, 0)),
                  pl.BlockSpec((D_MODEL, tn), lambda i, j: (0, j)), pl.BlockSpec((1, tn), lambda i, j: (0, j))],
        out_specs=pl.BlockSpec((tm, tn), lambda i, j: (i, j)),
        out_shape=jax.ShapeDtypeStruct((T, N), out_dtype),
        scratch_shapes=[pltpu.VMEM((tm, D_MODEL), BF16)],
        compiler_params=_cparams("parallel", "arbitrary"),
    )(x, g, w, b)


def matmul(a, w, out_dtype, name, res=None):
    T, K = a.shape
    N = w.shape[1]
    tm = min(TM, T)

    def body(*refs):
        if res is None:
            a_ref, w_ref, o_ref = refs
            acc = _dot(a_ref[...].astype(BF16), w_ref[...])
        else:
            a_ref, w_ref, r_ref, o_ref = refs
            acc = r_ref[...] + _dot(a_ref[...].astype(BF16), w_ref[...])
        o_ref[...] = acc.astype(o_ref.dtype)

    in_specs = [pl.BlockSpec((tm, K), lambda i: (i, 0)), pl.BlockSpec((K, N), lambda i: (0, 0))]
    ops = [a, w]
    if res is not None:
        in_specs.append(pl.BlockSpec((tm, N), lambda i: (i, 0)))
        ops.append(res)
    return pl.pallas_call(
        body, name=name, grid=(T // tm,), in_specs=in_specs,
        out_specs=pl.BlockSpec((tm, N), lambda i: (i, 0)),
        out_shape=jax.ShapeDtypeStruct((T, N), out_dtype),
        compiler_params=_cparams("parallel"),
    )(*ops)


def tn_matmul(a, b, name, colsum=False):
    T, M = a.shape
    N = b.shape[1]
    tk = min(TK, T)
    tmm = min(1024, M)
    tn = N if N <= 1024 else (1024 if N % 1024 == 0 else N)

    def body(a_ref, b_ref, o_ref, *rest):
        k = pl.program_id(2)
        bv = b_ref[...]

        @pl.when(k == 0)
        def _():
            o_ref[...] = jnp.zeros(o_ref.shape, F32)
            if colsum:
                rest[0][...] = jnp.zeros(rest[0].shape, F32)

        o_ref[...] += _tn(a_ref[...].astype(BF16), bv.astype(BF16))
        if colsum:
            rest[0][...] += jnp.sum(bv.astype(F32), axis=0, keepdims=True)

    out_specs = [pl.BlockSpec((tmm, tn), lambda i, j, k: (i, j))]
    out_shape = [jax.ShapeDtypeStruct((M, N), F32)]
    if colsum:
        assert M == tmm
        out_specs.append(pl.BlockSpec((1, tn), lambda i, j, k: (0, j)))
        out_shape.append(jax.ShapeDtypeStruct((1, N), F32))
    out = pl.pallas_call(
        body, name=name, grid=(M // tmm, N // tn, T // tk),
        in_specs=[pl.BlockSpec((tk, tmm), lambda i, j, k: (k, i)), pl.BlockSpec((tk, tn), lambda i, j, k: (k, j))],
        out_specs=out_specs, out_shape=out_shape,
        compiler_params=_cparams("parallel", "parallel", "arbitrary"),
    )(a, b)
    return out if colsum else out[0]


def mlp_fwd(x, g, w_up, w_down, name):
    T, F = x.shape[0], w_up.shape[1]
    tm, tf = min(TM, T), min(TF, F)
    nf = F // tf

    def body(x_ref, g_ref, wu_ref, wd_ref, o_ref, h_sc, acc_sc):
        f = pl.program_id(1)

        @pl.when(f == 0)
        def _():
            xv = x_ref[...]
            h_sc[...] = (xv * _rms(xv) * g_ref[...]).astype(BF16)
            acc_sc[...] = xv

        u = jnp.maximum(_dot(h_sc[...], wu_ref[...]), 0.0)
        acc_sc[...] += _dot((u * u).astype(BF16), wd_ref[...])

        @pl.when(f == nf - 1)
        def _():
            o_ref[...] = acc_sc[...]

    return pl.pallas_call(
        body, name=name, grid=(T // tm, nf),
        in_specs=[pl.BlockSpec((tm, D_MODEL), lambda i, f: (i, 0)), pl.BlockSpec((1, D_MODEL), lambda i, f: (0, 0)),
                  pl.BlockSpec((D_MODEL, tf), lambda i, f: (0, f)), pl.BlockSpec((tf, D_MODEL), lambda i, f: (f, 0))],
        out_specs=pl.BlockSpec((tm, D_MODEL), lambda i, f: (i, 0)),
        out_shape=jax.ShapeDtypeStruct((T, D_MODEL), F32),
        scratch_shapes=[pltpu.VMEM((tm, D_MODEL), BF16), pltpu.VMEM((tm, D_MODEL), F32)],
        compiler_params=_cparams("parallel", "arbitrary"),
    )(x, g, w_up, w_down)


def mlp_bwd(x, g, w_up, w_up_t, w_down_t, dy, name):
    T, F = x.shape[0], w_up.shape[1]
    tm, tf = min(TM, T), min(TF, F)
    nf = F // tf

    def body(x_ref, g_ref, wu_ref, wut_ref, wdt_ref, dy_ref, dx_ref, h_ref, a_ref, du_ref, dg_ref, h_sc, dyb_sc, dh_sc):
        i, f = pl.program_id(0), pl.program_id(1)

        @pl.when(f == 0)
        def _():
            xv = x_ref[...]
            h = (xv * _rms(xv) * g_ref[...]).astype(BF16)
            h_sc[...] = h
            h_ref[...] = h
            dyb_sc[...] = dy_ref[...].astype(BF16)
            dh_sc[...] = jnp.zeros(dh_sc.shape, F32)

        @pl.when((i == 0) & (f == 0))
        def _():
            dg_ref[...] = jnp.zeros(dg_ref.shape, F32)

        u = jnp.maximum(_dot(h_sc[...], wu_ref[...]), 0.0)
        a_ref[...] = (u * u).astype(BF16)
        du = (_dot(dyb_sc[...], wdt_ref[...]) * (2.0 * u)).astype(BF16)
        du_ref[...] = du
        dh_sc[...] += _dot(du, wut_ref[...])

        @pl.when(f == nf - 1)
        def _():
            dx, dg = _rms_bwd(x_ref[...], g_ref[...], dh_sc[...])
            dx_ref[...] = dy_ref[...] + dx
            dg_ref[...] += dg

    row = pl.BlockSpec((tm, D_MODEL), lambda i, f: (i, 0))
    hid = pl.BlockSpec((tm, tf), lambda i, f: (i, f))
    return pl.pallas_call(
        body, name=name, grid=(T // tm, nf),
        in_specs=[row, pl.BlockSpec((1, D_MODEL), lambda i, f: (0, 0)),
                  pl.BlockSpec((D_MODEL, tf), lambda i, f: (0, f)), pl.BlockSpec((tf, D_MODEL), lambda i, f: (f, 0)),
                  pl.BlockSpec((D_MODEL, tf), lambda i, f: (0, f)), row],
        out_specs=[row, row, hid, hid, pl.BlockSpec((1, D_MODEL), lambda i, f: (0, 0))],
        out_shape=[jax.ShapeDtypeStruct((T, D_MODEL), F32), jax.ShapeDtypeStruct((T, D_MODEL), BF16),
                   jax.ShapeDtypeStruct((T, F), BF16), jax.ShapeDtypeStruct((T, F), BF16),
                   jax.ShapeDtypeStruct((1, D_MODEL), F32)],
        scratch_shapes=[pltpu.VMEM((tm, D_MODEL), BF16), pltpu.VMEM((tm, D_MODEL), BF16),
                        pltpu.VMEM((tm, D_MODEL), F32)],
        compiler_params=_cparams("arbitrary", "arbitrary"),
    )(x, g, w_up, w_up_t, w_down_t, dy)


def proj_bwd(x, g, dres, parts, name):
    T = x.shape[0]
    tm = min(TM, T)
    n = len(parts)

    def body(*refs):
        x_ref, g_ref, dr_ref = refs[:3]
        da_refs, wt_refs = refs[3:3 + n], refs[3 + n:3 + 2 * n]
        dx_ref, h_ref, dg_ref = refs[3 + 2 * n:]

        @pl.when(pl.program_id(0) == 0)
        def _():
            dg_ref[...] = jnp.zeros(dg_ref.shape, F32)

        xv = x_ref[...]
        dh = _dot(da_refs[0][...].astype(BF16), wt_refs[0][...])
        for a_ref, w_ref in zip(da_refs[1:], wt_refs[1:]):
            dh = dh + _dot(a_ref[...].astype(BF16), w_ref[...])
        h_ref[...] = (xv * _rms(xv) * g_ref[...]).astype(BF16)
        dx, dg = _rms_bwd(xv, g_ref[...], dh)
        dx_ref[...] = dr_ref[...] + dx
        dg_ref[...] += dg

    row = pl.BlockSpec((tm, D_MODEL), lambda i: (i, 0))
    one = pl.BlockSpec((1, D_MODEL), lambda i: (0, 0))
    in_specs = [row, one, row]
    in_specs += [pl.BlockSpec((tm, da.shape[1]), lambda i: (i, 0)) for da, _ in parts]
    in_specs += [pl.BlockSpec(wt.shape, lambda i: (0, 0)) for _, wt in parts]
    return pl.pallas_call(
        body, name=name, grid=(T // tm,), in_specs=in_specs,
        out_specs=[row, row, one],
        out_shape=[jax.ShapeDtypeStruct((T, D_MODEL), F32), jax.ShapeDtypeStruct((T, D_MODEL), BF16),
                   jax.ShapeDtypeStruct((1, D_MODEL), F32)],
        compiler_params=_cparams("arbitrary"),
    )(x, g, dres, *[da for da, _ in parts], *[wt for _, wt in parts])


def final_loss(x, g, tgt, name):
    T = x.shape[0]
    tm = min(TM, T)

    def body(x_ref, g_ref, t_ref, l_ref, dx_ref, dg_ref):
        @pl.when(pl.program_id(0) == 0)
        def _():
            l_ref[...] = jnp.zeros(l_ref.shape, F32)
            dg_ref[...] = jnp.zeros(dg_ref.shape, F32)

        xv = x_ref[...]
        gv = g_ref[...]
        err = xv * _rms(xv) * gv - t_ref[...]
        l_ref[...] += 0.5 * jnp.sum(jnp.mean(err * err, axis=-1, keepdims=True), axis=0, keepdims=True)
        dx, dg = _rms_bwd(xv, gv, err * (1.0 / D_MODEL))
        dx_ref[...] = dx
        dg_ref[...] += dg

    row = pl.BlockSpec((tm, D_MODEL), lambda i: (i, 0))
    one = pl.BlockSpec((1, D_MODEL), lambda i: (0, 0))
    return pl.pallas_call(
        body, name=name, grid=(T // tm,), in_specs=[row, one, row],
        out_specs=[pl.BlockSpec((8, LANES), lambda i: (0, 0)), row, one],
        out_shape=[jax.ShapeDtypeStruct((8, LANES), F32), jax.ShapeDtypeStruct((T, D_MODEL), F32),
                   jax.ShapeDtypeStruct((1, D_MODEL), F32)],
        compiler_params=_cparams("arbitrary"),
    )(x, g, tgt)


def _swa_specs(tq):
    r = tq // SWA_WINDOW
    cur = lambda ix: pl.BlockSpec((tq, LANES), lambda kv, i: (ix(i), kv))
    prev = lambda ix: pl.BlockSpec((SWA_WINDOW, LANES), lambda kv, i: (jnp.maximum(ix(i) * r - 1, 0), kv))
    return cur, prev


W2 = 2 * SWA_WINDOW
SWA_RB = 32


def _swa_visible(tile):
    r = lax.broadcasted_iota(jnp.int32, (SWA_WINDOW, W2), 0)
    c = lax.broadcasted_iota(jnp.int32, (SWA_WINDOW, W2), 1)
    inside = (c > r) & (c <= r + SWA_WINDOW)
    return inside & ((c >= SWA_WINDOW) | (tile > 0)), inside


def swa_fwd(qkv, kdup, vdup, sinks_b, name):
    T = qkv.shape[0]
    tq = min(SWA_TQ, T)
    nsub = tq // SWA_WINDOW
    cur, prev = _swa_specs(tq)
    ident = lambda i: i

    def body(q_ref, kc_ref, kp_ref, vc_ref, vp_ref, sk_ref, o_ref, lse_ref, s_sc, e_sc):
        i = pl.program_id(1)
        kcat = jnp.concatenate([kp_ref[...], kc_ref[...]], axis=0)
        vcat = jnp.concatenate([vp_ref[...], vc_ref[...]], axis=0)
        vis_first, vis_in = _swa_visible(i)
        lane = lax.broadcasted_iota(jnp.int32, (1, LANES), 1)
        lse_all = jnp.zeros((tq, LANES), F32)
        for pp in range(2):
            q2 = q_ref[:, pp * LANES:(pp + 1) * LANES]
            outs = []
            for hf in range(2):
                g = 2 * pp + hf
                qm = jnp.where(_half(lane, hf), q2, jnp.zeros_like(q2))
                for nb in range(nsub):
                    rows = slice(nb * SWA_WINDOW, (nb + 1) * SWA_WINDOW)
                    s = _nt(qm[rows], kcat[nb * SWA_WINDOW:nb * SWA_WINDOW + W2]) * 0.125
                    s_sc[rows, :] = jnp.where(vis_first if nb == 0 else vis_in, s, -1e30)
                sk = sk_ref[g:g + 1, 0:1]
                m = jnp.maximum(jnp.max(s_sc[...], axis=1, keepdims=True), sk)
                m_rep = jnp.broadcast_to(m, (tq, LANES))
                parts = []
                for r0 in range(0, tq, SWA_RB):
                    rs = slice(r0, r0 + SWA_RB)
                    e0 = jnp.exp(s_sc[rs, 0:LANES] - m_rep[rs])
                    e1 = jnp.exp(s_sc[rs, LANES:W2] - m_rep[rs])
                    e_sc[rs, :] = jnp.concatenate([e0.astype(BF16), e1.astype(BF16)], axis=1)
                    parts.append(e0 + e1)
                den = jnp.sum(jnp.concatenate(parts, axis=0), axis=1, keepdims=True) + jnp.exp(sk - m)
                pv = [_dot(e_sc[nb * SWA_WINDOW:(nb + 1) * SWA_WINDOW, :], vcat[nb * SWA_WINDOW:nb * SWA_WINDOW + W2])
                      for nb in range(nsub)]
                outs.append(jnp.concatenate(pv, axis=0) * (1.0 / den))
                lse_all = jnp.where(lane == g, m + jnp.log(den), lse_all)
            o_ref[:, pp * LANES:(pp + 1) * LANES] = jnp.where(lane < 64, outs[0], outs[1]).astype(BF16)
        lse_ref[...] = lse_all

    return pl.pallas_call(
        body, name=name, grid=(4, T // tq),
        in_specs=[pl.BlockSpec((tq, 2 * LANES), lambda kv, i: (i, kv)), cur(ident), prev(ident), cur(ident), prev(ident),
                  pl.BlockSpec((None, 8, LANES), lambda kv, i: (kv, 0, 0))],
        out_specs=[pl.BlockSpec((tq, 2 * LANES), lambda kv, i: (i, kv)), cur(ident)],
        out_shape=[jax.ShapeDtypeStruct((T, D_MODEL), BF16), jax.ShapeDtypeStruct((T, 4 * LANES), F32)],
        scratch_shapes=[pltpu.VMEM((tq, W2), F32), pltpu.VMEM((tq, W2), BF16)],
        compiler_params=_cparams("parallel", "arbitrary"),
    )(qkv, kdup, kdup, vdup, vdup, sinks_b)


def swa_bwd(qkv, kdup, vdup, sinks_b, o, lse, do, name):
    T = qkv.shape[0]
    tq = min(SWA_TQ, T)
    n = T // tq
    nsub = tq // SWA_WINDOW
    cur, prev = _swa_specs(tq)
    rev = lambda i: n - 1 - i

    def body(q_ref, kc_ref, kp_ref, vc_ref, vp_ref, sk_ref, o_ref, lse_ref, do_ref, dq_ref, dk_ref, dv_ref, dsk_ref,
             ck_sc, cv_sc, dkc_sc, dvc_sc):
        i = pl.program_id(1)

        @pl.when(i == 0)
        def _():
            ck_sc[...] = jnp.zeros(ck_sc.shape, F32)
            cv_sc[...] = jnp.zeros(cv_sc.shape, F32)
            dsk_ref[...] = jnp.zeros(dsk_ref.shape, F32)

        kcat = jnp.concatenate([kp_ref[...], kc_ref[...]], axis=0)
        vcat = jnp.concatenate([vp_ref[...], vc_ref[...]], axis=0)
        vis_first, vis_in = _swa_visible(n - 1 - i)
        lane = lax.broadcasted_iota(jnp.int32, (1, LANES), 1)
        dkc_sc[...] = jnp.zeros(dkc_sc.shape, F32)
        dvc_sc[...] = jnp.zeros(dvc_sc.shape, F32)
        for pp in range(2):
            sl = slice(pp * LANES, (pp + 1) * LANES)
            q2, do2, o2 = q_ref[:, sl], do_ref[:, sl], o_ref[:, sl]
            dqs = []
            for hf in range(2):
                g = 2 * pp + hf
                lm = _half(lane, hf)
                qm = jnp.where(lm, q2, jnp.zeros_like(q2))
                dom = jnp.where(lm, do2, jnp.zeros_like(do2))
                delta = jnp.sum(dom.astype(F32) * o2.astype(F32), axis=1, keepdims=True)
                lse_g = lse_ref[:, g:g + 1]
                psk = jnp.exp(sk_ref[g:g + 1, 0:1] - lse_g)
                dsk_ref[g:g + 1, :] += jnp.zeros((1, LANES), F32) - jnp.sum(psk * delta, axis=0, keepdims=True)
                dq_parts = []
                for nb in range(nsub):
                    rows = slice(nb * SWA_WINDOW, (nb + 1) * SWA_WINDOW)
                    band = slice(nb * SWA_WINDOW, nb * SWA_WINDOW + W2)
                    s = jnp.where(vis_first if nb == 0 else vis_in, _nt(qm[rows], kcat[band]) * 0.125, -1e30)
                    p = jnp.exp(s - lse_g[rows])
                    dsb = (p * (_nt(dom[rows], vcat[band]) - delta[rows]) * 0.125).astype(BF16)
                    dq_parts.append(_dot(dsb, kcat[band]))
                    dkc_sc[band, :] += _tn(dsb, qm[rows])
                    dvc_sc[band, :] += _tn(p.astype(BF16), dom[rows])
                dqs.append(jnp.concatenate(dq_parts, axis=0))
            dq_ref[:, sl] = jnp.where(lane < 64, dqs[0], dqs[1]).astype(BF16)
        dkc = dkc_sc[...]
        dvc = dvc_sc[...]
        dkc = dkc + pltpu.roll(dkc, 64, 1)
        dvc = dvc + pltpu.roll(dvc, 64, 1)
        for full, ref, carry in ((dkc, dk_ref, ck_sc), (dvc, dv_ref, cv_sc)):
            if tq > SWA_WINDOW:
                ref[0:tq - SWA_WINDOW, :] = full[SWA_WINDOW:tq, :]
            ref[tq - SWA_WINDOW:tq, :] = full[tq:tq + SWA_WINDOW, :] + carry[...]
            carry[...] = full[0:SWA_WINDOW, :]

    wide = pl.BlockSpec((tq, 2 * LANES), lambda kv, i: (rev(i), kv))
    return pl.pallas_call(
        body, name=name, grid=(4, n),
        in_specs=[wide, cur(rev), prev(rev), cur(rev), prev(rev),
                  pl.BlockSpec((None, 8, LANES), lambda kv, i: (kv, 0, 0)), wide, cur(rev), wide],
        out_specs=[wide, cur(rev), cur(rev), pl.BlockSpec((None, 8, LANES), lambda kv, i: (kv, 0, 0))],
        out_shape=[jax.ShapeDtypeStruct((T, D_MODEL), BF16), jax.ShapeDtypeStruct((T, 4 * LANES), F32),
                   jax.ShapeDtypeStruct((T, 4 * LANES), F32), jax.ShapeDtypeStruct((4, 8, LANES), F32)],
        scratch_shapes=[pltpu.VMEM((SWA_WINDOW, LANES), F32), pltpu.VMEM((SWA_WINDOW, LANES), F32),
                        pltpu.VMEM((tq + SWA_WINDOW, LANES), F32), pltpu.VMEM((tq + SWA_WINDOW, LANES), F32)],
        compiler_params=_cparams("arbitrary", "arbitrary"),
    )(qkv, kdup, kdup, vdup, vdup, sinks_b, o, lse, do)


def fox_gate_fwd(fl, qkv, name):
    T = fl.shape[0]
    ts = min(SCAN_T, T)
    per_tile = min(FOX_T, T) // ts

    def body(fl_ref, q_ref, k_ref, qa_ref, ka_ref, st_ref, carry):
        @pl.when(pl.program_id(0) == 0)
        def _():
            carry[...] = jnp.zeros(carry.shape, F32)

        xv = fl_ref[...]
        ls = jnp.minimum(xv, 0.0) - jnp.log(1.0 + jnp.exp(-jnp.abs(xv)))
        tri = (lax.broadcasted_iota(jnp.int32, (ts, ts), 0) >= lax.broadcasted_iota(jnp.int32, (ts, ts), 1)).astype(F32)
        cs = jnp.dot(tri, ls, precision=HI, preferred_element_type=F32) + carry[...]
        carry[...] = cs[ts - 1:ts, :]
        c1 = cs.astype(BF16).astype(F32)
        c2 = (cs - c1).astype(BF16).astype(F32)
        c3 = (cs - c1 - c2).astype(BF16).astype(F32)
        lane = lax.broadcasted_iota(jnp.int32, (1, LANES), 1)
        ones_q = jnp.where((lane >= 67) & (lane < 70), 1.0, 0.0)
        ones_k = jnp.where((lane >= 64) & (lane < 67), 1.0, 0.0)
        nq = jnp.zeros((1, LANES), F32)
        nk = jnp.zeros((1, LANES), F32)
        for b in range(8):
            qf = q_ref[:, b * LANES:(b + 1) * LANES].astype(F32) * 0.125
            kf = k_ref[:, b * LANES:(b + 1) * LANES].astype(F32)
            for hf in range(2):
                h = 2 * b + hf
                a1, a2, a3 = c1[:, h:h + 1], c2[:, h:h + 1], c3[:, h:h + 1]
                aux_q = jnp.where(lane == 64, a1, jnp.where(lane == 65, a2, jnp.where(lane == 66, a3, ones_q)))
                aux_k = jnp.where(lane == 67, -a1, jnp.where(lane == 68, -a2, jnp.where(lane == 69, -a3, ones_k)))
                qs = qf if hf == 0 else pltpu.roll(qf, 64, 1)
                ks = kf if hf == 0 else pltpu.roll(kf, 64, 1)
                qa_ref[:, h * LANES:(h + 1) * LANES] = jnp.where(lane < 64, qs, aux_q).astype(BF16)
                ka_ref[:, h * LANES:(h + 1) * LANES] = jnp.where(lane < 64, ks, aux_k).astype(BF16)
                for src, is_q in ((qf, True), (kf, False)):
                    sq = jnp.sum(jnp.where(_half(lane, hf), src * src, 0.0), axis=1, keepdims=True)
                    big = jnp.sqrt(jnp.max(sq, axis=0, keepdims=True))
                    if is_q:
                        nq = jnp.where(lane == h, big, nq)
                    else:
                        nk = jnp.where(lane == h, big, nk)
        new = jnp.concatenate([nq, nk, jnp.max(cs, axis=0, keepdims=True), jnp.min(cs, axis=0, keepdims=True),
                               jnp.zeros((4, LANES), F32)], axis=0)
        first = pl.program_id(0) % per_tile == 0
        row = lax.broadcasted_iota(jnp.int32, (8, LANES), 0)

        @pl.when(first)
        def _():
            st_ref[...] = new

        @pl.when(jnp.logical_not(first))
        def _():
            old = st_ref[...]
            st_ref[...] = jnp.where(row == 3, jnp.minimum(old, new), jnp.maximum(old, new))

    out = pl.BlockSpec((ts, 16 * LANES), lambda i: (i, 0))
    return pl.pallas_call(
        body, name=name, grid=(T // ts,),
        in_specs=[pl.BlockSpec((ts, LANES), lambda i: (i, 0)), pl.BlockSpec((ts, D_MODEL), lambda i: (i, 0)),
                  pl.BlockSpec((ts, D_MODEL), lambda i: (i, 1))],
        out_specs=[out, out, pl.BlockSpec((None, 8, LANES), lambda i: (i // per_tile, 0, 0))],
        out_shape=[jax.ShapeDtypeStruct((T, 16 * LANES), BF16), jax.ShapeDtypeStruct((T, 16 * LANES), BF16),
                   jax.ShapeDtypeStruct((T // ts // per_tile, 8, LANES), F32)],
        scratch_shapes=[pltpu.VMEM((1, LANES), F32)],
        compiler_params=_cparams("arbitrary"),
    )(fl, qkv, qkv)


def fox_gate_bwd(fl, dc, name):
    T = fl.shape[0]
    ts = min(SCAN_T, T)
    n = T // ts

    def body(fl_ref, dc_ref, o_ref, carry):
        @pl.when(pl.program_id(0) == 0)
        def _():
            carry[...] = jnp.zeros(carry.shape, F32)

        tri = (lax.broadcasted_iota(jnp.int32, (ts, ts), 0) <= lax.broadcasted_iota(jnp.int32, (ts, ts), 1)).astype(F32)
        rs = jnp.dot(tri, dc_ref[...], precision=HI, preferred_element_type=F32) + carry[...]
        carry[...] = rs[0:1, :]
        o_ref[...] = rs * (1.0 / (1.0 + jnp.exp(fl_ref[...])))

    blk = pl.BlockSpec((ts, LANES), lambda i: (n - 1 - i, 0))
    return pl.pallas_call(
        body, name=name, grid=(n,), in_specs=[blk, blk], out_specs=blk,
        out_shape=jax.ShapeDtypeStruct((T, LANES), F32), scratch_shapes=[pltpu.VMEM((1, LANES), F32)],
        compiler_params=_cparams("arbitrary"),
    )(fl, dc)


FOX_RB = 32


def _half(lane, hf):
    return (lane < 64) if hf == 0 else (lane >= 64)


def _pair(lane, a, b):
    return jnp.where(lane < 64, a, pltpu.roll(b, 64, 1)), jnp.where(lane < 64, pltpu.roll(a, 64, 1), b)


FOX_SKIP = 110.0


def _tile_bound(st_ref, i, j, h):
    return st_ref[i, h] * st_ref[j, 16 + h] * 1.01 + (st_ref[i, 32 + h] - st_ref[j, 48 + h]) + 1.0


def fox_fwd(st, qa, ka, qkv, name):
    T = qa.shape[0]
    t = min(FOX_T, T)
    n = T // t

    def body(st_ref, qa_ref, ka_ref, v_ref, o_ref, lse_ref, lmin_ref, m_sc, l_sc, acc_sc, ls_sc, s_sc, p_sc):
        i, jj = pl.program_id(1), pl.program_id(2)
        j = jnp.maximum(i - jj, 0)
        lane = lax.broadcasted_iota(jnp.int32, (1, LANES), 1)

        def head_tile(hf, diag):
            hs = slice(hf * LANES, (hf + 1) * LANES)
            sv = _nt(qa_ref[:, hs], ka_ref[:, hs])
            if diag:
                vis = lax.broadcasted_iota(jnp.int32, (t, t), 0) >= lax.broadcasted_iota(jnp.int32, (t, t), 1)
                sv = jnp.where(vis, sv, -1e30)
            s_sc[...] = sv
            m_old = m_sc[hf]
            m_new = jnp.maximum(m_old, jnp.max(s_sc[...], axis=1, keepdims=True))
            al = jnp.exp(m_old - m_new)
            m_sc[hf] = m_new
            for r0 in range(0, t, FOX_RB):
                rs = slice(r0, r0 + FOX_RB)
                mrow = m_new[rs, :]
                part, pieces = None, []
                for cb in range(0, t, LANES):
                    pc = jnp.exp(s_sc[rs, cb:cb + LANES] - mrow)
                    part = pc if part is None else part + pc
                    pieces.append(pc.astype(BF16))
                p_sc[rs, :] = jnp.concatenate(pieces, axis=1)
                ls_sc[rs, :] = part
            l_sc[hf] = al * l_sc[hf] + ls_sc[...]
            acc_sc[hf] = al * acc_sc[hf] + _dot(p_sc[...], v_ref[...])

        @pl.when(jj == 0)
        def _():
            m_sc[...] = jnp.full(m_sc.shape, -1e30, F32)
            l_sc[...] = jnp.zeros(l_sc.shape, F32)
            acc_sc[...] = jnp.zeros(acc_sc.shape, F32)
            head_tile(0, True)
            head_tile(1, True)

        for hf in range(2):
            h = 2 * pl.program_id(0) + hf
            live = _tile_bound(st_ref, i, j, h) >= jnp.min(m_sc[hf]) - FOX_SKIP

            @pl.when((jj > 0) & (jj <= i) & live)
            def _():
                head_tile(hf, False)

        @pl.when(jj == i)
        def _():
            l0 = jnp.sum(l_sc[0], axis=1, keepdims=True)
            l1 = jnp.sum(l_sc[1], axis=1, keepdims=True)
            lse0, lse1 = m_sc[0] + jnp.log(l0), m_sc[1] + jnp.log(l1)
            o_ref[...] = jnp.where(lane < 64, acc_sc[0] / l0, acc_sc[1] / l1).astype(BF16)
            lse_ref[...] = jnp.where(lane < 64, lse0, lse1)
            row = lax.broadcasted_iota(jnp.int32, (8, LANES), 0)
            lmin_ref[...] = jnp.where(row == 0, jnp.min(lse0), jnp.where(row == 1, jnp.min(lse1), 0.0))

    oblk = pl.BlockSpec((t, LANES), lambda p, i, jj, st: (i, p))
    return pl.pallas_call(
        body, name=name,
        grid_spec=pltpu.PrefetchScalarGridSpec(
            num_scalar_prefetch=1, grid=(8, n, n),
            in_specs=[pl.BlockSpec((t, 2 * LANES), lambda p, i, jj, st: (i, p)),
                      pl.BlockSpec((t, 2 * LANES), lambda p, i, jj, st: (jnp.maximum(i - jj, 0), p)),
                      pl.BlockSpec((t, LANES), lambda p, i, jj, st: (jnp.maximum(i - jj, 0), 16 + p))],
            out_specs=[oblk, oblk, pl.BlockSpec((None, None, 8, LANES), lambda p, i, jj, st: (i, p, 0, 0))],
            scratch_shapes=[pltpu.VMEM((2, t, LANES), F32), pltpu.VMEM((2, t, LANES), F32),
                            pltpu.VMEM((2, t, LANES), F32), pltpu.VMEM((t, LANES), F32), pltpu.VMEM((t, t), F32),
                            pltpu.VMEM((t, t), BF16)]),
        out_shape=[jax.ShapeDtypeStruct((T, D_MODEL), BF16), jax.ShapeDtypeStruct((T, D_MODEL), F32),
                   jax.ShapeDtypeStruct((n, 8, 8, LANES), F32)],
        compiler_params=_cparams("parallel", "parallel", "arbitrary"),
    )(st, qa, ka, qkv)


def fox_delta(do, o, name):
    T = do.shape[0]
    tm = min(TM, T)

    def body(do_ref, o_ref, d_ref):
        lane = lax.broadcasted_iota(jnp.int32, (1, LANES), 1)
        out = jnp.zeros((tm, LANES), F32)
        for b in range(8):
            d = do_ref[:, b * LANES:(b + 1) * LANES].astype(F32) * o_ref[:, b * LANES:(b + 1) * LANES].astype(F32)
            for hf in range(2):
                out = jnp.where(lane == 2 * b + hf, jnp.sum(jnp.where(_half(lane, hf), d, 0.0), axis=1, keepdims=True), out)
        d_ref[...] = out

    row = pl.BlockSpec((tm, D_MODEL), lambda i: (i, 0))
    return pl.pallas_call(
        body, name=name, grid=(T // tm,), in_specs=[row, row],
        out_specs=pl.BlockSpec((tm, LANES), lambda i: (i, 0)),
        out_shape=jax.ShapeDtypeStruct((T, LANES), F32),
        compiler_params=_cparams("parallel"),
    )(do, o)


def fox_bwd(st, qa, ka, qkv, lse_row, delta_row, do, name):
    T = qa.shape[0]
    t = min(FOX_T, T)
    n = T // t

    def body(st_ref, qa_ref, ka_ref, v_ref, lr_ref, dr_ref, do_ref, dq_ref, auxq_ref, dk_ref, dv_ref, aux_ref,
             dq_sc, dk_sc, dv_sc, s_sc, dp_sc, p_sc, ds_sc, dqo_sc, auxo_sc, out_sems):
        j, i = pl.program_id(1), pl.program_id(2)
        lane = lax.broadcasted_iota(jnp.int32, (1, LANES), 1)
        qrows = pl.ds(pl.multiple_of(i * t, t), t)

        @pl.when(i == 0)
        def _():
            dk_sc[...] = jnp.zeros(dk_sc.shape, F32)
            dv_sc[...] = jnp.zeros(dv_sc.shape, F32)

        @pl.when(j == 0)
        def _():
            dq_sc[:, qrows, :] = jnp.zeros((2, t, LANES), F32)

        def head_tile(hf, diag):
            v2, do2 = v_ref[...], do_ref[...]
            hs = slice(hf * LANES, (hf + 1) * LANES)
            lm = _half(lane, hf)
            qh = qa_ref[:, hs]
            s_sc[...] = _nt(ka_ref[:, hs], qh)
            dp_sc[...] = _nt(jnp.where(lm, v2, jnp.zeros_like(v2)), do2)
            lrow, drow = lr_ref[hf:hf + 1, :], dr_ref[hf:hf + 1, :]
            for r0 in range(0, t, FOX_RB):
                rs = slice(r0, r0 + FOX_RB)
                sv = s_sc[rs, :]
                if diag:
                    vis = lax.broadcasted_iota(jnp.int32, (FOX_RB, t), 1) >= (r0 + lax.broadcasted_iota(jnp.int32, (FOX_RB, t), 0))
                    sv = jnp.where(vis, sv, -1e30)
                p = jnp.exp(sv - lrow)
                p_sc[rs, :] = p.astype(BF16)
                ds_sc[rs, :] = (p * (dp_sc[rs, :] - drow)).astype(BF16)
            dv_sc[...] += _dot(p_sc[...], jnp.where(lm, do2, jnp.zeros_like(do2)))
            dk_sc[hf] += _dot(ds_sc[...], qh)
            dq_sc[hf, qrows, :] += _tn(ds_sc[...], ka_ref[:, hs])

        for hf in range(2):
            h = 2 * pl.program_id(0) + hf
            live = _tile_bound(st_ref, i, j, h) >= st_ref[i, 64 + h] - FOX_SKIP

            @pl.when((i > j) & live)
            def _():
                head_tile(hf, False)

        @pl.when(i == j)
        def _():
            head_tile(0, True)
            head_tile(1, True)
            dq, aux = _pair(lane, dq_sc[0, qrows, :], dq_sc[1, qrows, :])
            dqo_sc[...] = (dq * 0.125).astype(BF16)
            auxo_sc[...] = aux
            cols = pl.ds(pl.multiple_of(pl.program_id(0) * LANES, LANES), LANES)
            c1 = pltpu.make_async_copy(dqo_sc, dq_ref.at[qrows, cols], out_sems.at[0])
            c2 = pltpu.make_async_copy(auxo_sc, auxq_ref.at[qrows, cols], out_sems.at[1])
            c1.start()
            c2.start()
            c1.wait()
            c2.wait()

        @pl.when(i == n - 1)
        def _():
            dk, aux = _pair(lane, dk_sc[0], dk_sc[1])
            dk_ref[...] = dk.astype(BF16)
            aux_ref[...] = aux
            dv_ref[...] = dv_sc[...].astype(BF16)

    qblk = pl.BlockSpec((t, LANES), lambda p, j, i, st: (jnp.maximum(i, j), p))
    kblk = pl.BlockSpec((t, LANES), lambda p, j, i, st: (j, p))
    rblk = pl.BlockSpec((None, 2, t), lambda p, j, i, st: (p, 0, jnp.maximum(i, j)))
    bf, f32 = jax.ShapeDtypeStruct((T, D_MODEL), BF16), jax.ShapeDtypeStruct((T, D_MODEL), F32)
    return pl.pallas_call(
        body, name=name,
        grid_spec=pltpu.PrefetchScalarGridSpec(
            num_scalar_prefetch=1, grid=(8, n, n),
            in_specs=[pl.BlockSpec((t, 2 * LANES), lambda p, j, i, st: (jnp.maximum(i, j), p)),
                      pl.BlockSpec((t, 2 * LANES), lambda p, j, i, st: (j, p)),
                      pl.BlockSpec((t, LANES), lambda p, j, i, st: (j, 16 + p)), rblk, rblk, qblk],
            out_specs=[pl.BlockSpec(memory_space=pl.ANY), pl.BlockSpec(memory_space=pl.ANY), kblk, kblk, kblk],
            scratch_shapes=[pltpu.VMEM((2, T, LANES), F32), pltpu.VMEM((2, t, LANES), F32), pltpu.VMEM((t, LANES), F32),
                            pltpu.VMEM((t, t), F32), pltpu.VMEM((t, t), F32), pltpu.VMEM((t, t), BF16),
                            pltpu.VMEM((t, t), BF16), pltpu.VMEM((t, LANES), BF16), pltpu.VMEM((t, LANES), F32),
                            pltpu.SemaphoreType.DMA((2,))]),
        out_shape=[bf, f32, bf, bf, f32],
        compiler_params=_cparams("arbitrary", "arbitrary", "arbitrary"),
    )(st, qa, ka, qkv, lse_row, delta_row, do)


C = HGRN_CHUNK
LEVELS = (64, 32, 16, 8, 4, 2)


def _pivot(b, B, row):
    if B == C:
        return jnp.broadcast_to(b[C // 2 - 1:C // 2, :], b.shape)
    if B >= 8:
        b3 = b.reshape(C // B, B, LANES)
        return jnp.broadcast_to(b3[:, B // 2 - 1:B // 2, :], b3.shape).reshape(C, LANES)
    if B == 4:
        y = jnp.where((row & 3) == 1, b, 0.0)
        return y + pltpu.roll(y, 1, 0) + pltpu.roll(y, 2, 0) + pltpu.roll(y, C - 1, 0)
    y = jnp.where((row & 1) == 0, b, 0.0)
    return y + pltpu.roll(y, 1, 0)


def _level_factors(bcum):
    row = lax.broadcasted_iota(jnp.int32, (C, 1), 0)
    out = []
    for B in LEVELS:
        upper = (row & (B - 1)) >= B // 2
        e = jnp.exp(-jnp.abs(bcum - _pivot(bcum, B, row)))
        out.append((B, jnp.where(upper, e, 0.0), jnp.where(upper, 0.0, e)))
    return out


def _same_block(B):
    sh = B.bit_length() - 1
    r = lax.broadcasted_iota(jnp.int32, (C, C), 0)
    c = lax.broadcasted_iota(jnp.int32, (C, C), 1)
    return (r >> sh) == (c >> sh)


def _hgrn_gates(q, fl, lb):
    sg = _sigmoid(fl)
    f = lb + (1.0 - lb) * sg
    sq = _sigmoid(q)
    return sg, f, jnp.log(f), 1.0 - f, sq, q * sq


def _cumsum_rows(x, reverse=False):
    r = lax.broadcasted_iota(jnp.int32, (C, C), 0)
    c = lax.broadcasted_iota(jnp.int32, (C, C), 1)
    tri = ((r <= c) if reverse else (r >= c)).astype(F32)
    return jnp.dot(tri, x, precision=HI, preferred_element_type=F32)


def _intra(qs, k, factors):
    r = lax.broadcasted_iota(jnp.int32, (C, C), 0)
    c = lax.broadcasted_iota(jnp.int32, (C, C), 1)
    a = jnp.where(r == c, jnp.sum(qs * k, axis=1, keepdims=True), 0.0)
    ops = []
    for B, eq, ek in factors:
        ql, kl = (qs * eq).astype(BF16), (k * ek).astype(BF16)
        al = _nt(ql, kl)
        a = a + (al if B == C else jnp.where(_same_block(B), al, 0.0))
        ops.append((ql, kl))
    return a, ops


def hgrn_fwd(proj, lb, gn, name):
    T = proj.shape[0]
    tg = min(HGRN_TG, T)
    nch = tg // C

    def body(q_ref, fl_ref, v_ref, g_ref, lb_ref, gn_ref, ao_ref, o_ref, st_ref, st_sc):
        @pl.when(pl.program_id(1) == 0)
        def _():
            st_sc[...] = jnp.zeros(st_sc.shape, F32)

        lb_v, gn_v = lb_ref[...], gn_ref[...]

        def chunk(ci, carry):
            rows = pl.ds(pl.multiple_of(ci * C, C), C)
            _, f, lf, k, _, qs = _hgrn_gates(q_ref[rows, :], fl_ref[rows, :], lb_v)
            vb = v_ref[rows, :].astype(BF16)
            gv = g_ref[rows, :]
            bcum = _cumsum_rows(lf)
            blast = bcum[C - 1:C, :]
            a, _ = _intra(qs, k, _level_factors(bcum))
            st = st_sc[...]
            st_ref[ci] = st
            o = _dot(a.astype(BF16), vb) + _nt((qs * jnp.exp(bcum)).astype(BF16), st.astype(BF16))
            st_sc[...] = st * jnp.exp(blast) + _tn(vb, (k * jnp.exp(blast - bcum)).astype(BF16))
            o_ref[rows, :] = o
            ao_ref[rows, :] = (o * _rms(o) * gn_v * (gv * _sigmoid(gv))).astype(BF16)
            return carry

        lax.fori_loop(0, nch, chunk, 0, unroll=8)

    col = lambda off: pl.BlockSpec((tg, LANES), lambda h, i: (i, off + h))
    one = pl.BlockSpec((1, LANES), lambda h, i: (0, h))
    return pl.pallas_call(
        body, name=name, grid=(8, T // tg),
        in_specs=[col(0), col(8), col(16), col(24), one, one],
        out_specs=[col(0), col(0), pl.BlockSpec((None, nch, LANES, LANES), lambda h, i: (h, i, 0, 0))],
        out_shape=[jax.ShapeDtypeStruct((T, D_MODEL), BF16), jax.ShapeDtypeStruct((T, D_MODEL), F32),
                   jax.ShapeDtypeStruct((8, T // C, LANES, LANES), F32)],
        scratch_shapes=[pltpu.VMEM((LANES, LANES), F32)],
        compiler_params=_cparams("parallel", "arbitrary"),
    )(proj, proj, proj, proj, lb, gn)


def hgrn_bwd(proj, lb, gn, o_raw, states, dao, name):
    T = proj.shape[0]
    tg = min(HGRN_TG, T)
    nch = tg // C
    n = T // tg

    def body(q_ref, fl_ref, v_ref, g_ref, lb_ref, gn_ref, o_ref, st_ref, dao_ref,
             dq_ref, dfl_ref, dv_ref, dg_ref, dlb_ref, dgn_ref, dst_sc):
        @pl.when(pl.program_id(1) == 0)
        def _():
            dst_sc[...] = jnp.zeros(dst_sc.shape, F32)
            dlb_ref[...] = jnp.zeros(dlb_ref.shape, F32)
            dgn_ref[...] = jnp.zeros(dgn_ref.shape, F32)

        lb_v, gn_v = lb_ref[...], gn_ref[...]
        r64 = lax.broadcasted_iota(jnp.int32, (C, C), 0)
        c64 = lax.broadcasted_iota(jnp.int32, (C, C), 1)
        row = lax.broadcasted_iota(jnp.int32, (C, 1), 0)

        def chunk(cr, carry):
            ci = nch - 1 - cr
            rows = pl.ds(pl.multiple_of(ci * C, C), C)
            q, fl, gv = q_ref[rows, :], fl_ref[rows, :], g_ref[rows, :]
            sg, f, lf, k, sq, qs = _hgrn_gates(q, fl, lb_v)
            vb = v_ref[rows, :].astype(BF16)
            o = o_ref[rows, :]
            ro = _rms(o)
            on = o * ro
            sgg = _sigmoid(gv)
            gate = gv * sgg
            dao_v = dao_ref[rows, :].astype(F32)
            dg_ref[rows, :] = (dao_v * on * gn_v * (sgg * (1.0 + gv * (1.0 - sgg)))).astype(BF16)
            dgn_ref[...] += jnp.sum(dao_v * on * gate, axis=0, keepdims=True)
            don = dao_v * gn_v * gate
            do = ro * (don - on * jnp.mean(don * on, axis=-1, keepdims=True))
            dob = do.astype(BF16)
            bcum = _cumsum_rows(lf)
            blast = bcum[C - 1:C, :]
            factors = _level_factors(bcum)
            a, ops = _intra(qs, k, factors)
            eb = jnp.exp(bcum)
            ekb = jnp.exp(blast - bcum)
            qb = qs * eb
            kb = k * ekb
            st = st_ref[ci]
            dst = dst_sc[...]
            dstb = dst.astype(BF16)
            da = jnp.where(r64 >= c64, _nt(dob, vb), 0.0)
            dv_ref[rows, :] = (_tn(a.astype(BF16), dob) + _nt(kb.astype(BF16), dstb)).astype(BF16)
            dqb = _dot(dob, st.astype(BF16))
            dkb = _dot(vb, dstb)
            eblast = jnp.exp(blast)
            dst_sc[...] = dst * eblast + _tn(dob, qb.astype(BF16))
            dblast = eblast * jnp.sum(dst * st, axis=0, keepdims=True) + jnp.sum(dkb * kb, axis=0, keepdims=True)
            dad = jnp.sum(jnp.where(r64 == c64, da, 0.0), axis=1, keepdims=True)
            dqs = dqb * eb + dad * k
            dk = dkb * ekb + dad * qs
            dbcum = dqb * qb - dkb * kb + jnp.where(row == C - 1, dblast, 0.0)
            for (B, eq, ek), (ql, kl) in zip(factors, ops):
                dal = (da if B == C else jnp.where(_same_block(B), da, 0.0)).astype(BF16)
                dql, dkl = _dot(dal, kl), _tn(dal, ql)
                dqs = dqs + dql * eq
                dk = dk + dkl * ek
                dbcum = dbcum + (dql * ql.astype(F32) - dkl * kl.astype(F32))
            df = _cumsum_rows(dbcum, reverse=True) / f - dk
            dfl_ref[rows, :] = (df * (1.0 - lb_v) * sg * (1.0 - sg)).astype(BF16)
            dlb_ref[...] += jnp.sum(df * (1.0 - sg), axis=0, keepdims=True)
            dq_ref[rows, :] = (dqs * (sq * (1.0 + q * (1.0 - sq)))).astype(BF16)
            return carry

        lax.fori_loop(0, nch, chunk, 0, unroll=8)

    col = lambda off: pl.BlockSpec((tg, LANES), lambda h, i: (n - 1 - i, off + h))
    one = pl.BlockSpec((1, LANES), lambda h, i: (0, h))
    big = jax.ShapeDtypeStruct((T, D_MODEL), BF16)
    small = jax.ShapeDtypeStruct((1, D_MODEL), F32)
    return pl.pallas_call(
        body, name=name, grid=(8, n),
        in_specs=[col(0), col(8), col(16), col(24), one, one, col(0),
                  pl.BlockSpec((None, nch, LANES, LANES), lambda h, i: (h, n - 1 - i, 0, 0)), col(0)],
        out_specs=[col(0), col(0), col(0), col(0), one, one],
        out_shape=[big, big, big, big, small, small],
        scratch_shapes=[pltpu.VMEM((LANES, LANES), F32)],
        compiler_params=_cparams("arbitrary", "arbitrary"),
    )(proj, proj, proj, proj, lb, gn, o_raw, states, dao)


def lower_bound_fwd(logits, name):
    def body(l_ref, s_ref):
        lv = l_ref[...]
        e = jnp.exp(lv - jnp.max(lv, axis=0, keepdims=True))
        s_ref[...] = e / jnp.sum(e, axis=0, keepdims=True)

    return pl.pallas_call(body, name=name, out_shape=jax.ShapeDtypeStruct(logits.shape, F32))(logits)


def lower_bound_bwd(sm, dlb, name):
    def body(s_ref, d_ref, o_ref):
        s = s_ref[...]
        row = lax.broadcasted_iota(jnp.int32, s.shape, 0)
        o_ref[...] = d_ref[...] * s[1:2, :] * (jnp.where(row == 1, 1.0, 0.0) - s)

    return pl.pallas_call(body, name=name, out_shape=jax.ShapeDtypeStruct(sm.shape, F32))(sm, dlb)


def _pad_rows(flat, mult):
    rows = -(-flat.shape[-1] // D_MODEL)
    rows = -(-rows // mult) * mult
    pad = rows * D_MODEL - flat.shape[-1]
    flat = jnp.pad(flat, [(0, 0)] * (flat.ndim - 1) + [(0, pad)])
    return flat.reshape(flat.shape[:-1] + (rows, D_MODEL))


def _gather_weights(w):
    direct = [n for n in SHARDED if n not in BIASES and w[n].shape[SHARD_AXIS[n]] % LANES == 0]
    packed = [n for n in SHARDED if n not in direct]
    pieces = []
    for nme in packed:
        a = w[nme]
        if nme in BIASES:
            pieces.append(lax.bitcast_convert_type(a, BF16).reshape(-1))
        else:
            pieces.append(a.astype(BF16).reshape(-1))
    flat = _pad_rows(jnp.concatenate(pieces), 16)
    out = all_gather_shards([(w[n].astype(BF16), SHARD_AXIS[n]) for n in direct] + [(flat, None)])
    full = dict(zip(direct, out[:-1]))
    got, off = out[-1].reshape(N_DEV, -1), 0
    for nme in packed:
        shp = w[nme].shape
        cnt = 1
        for s in shp:
            cnt *= s
        if nme in BIASES:
            seg = got[:, off:off + 2 * cnt].reshape((N_DEV,) + shp + (2,))
            seg = lax.bitcast_convert_type(seg, F32)
            off += 2 * cnt
        else:
            seg = got[:, off:off + cnt].reshape((N_DEV,) + shp)
            off += cnt
        full[nme] = jnp.concatenate([seg[d] for d in range(N_DEV)], axis=SHARD_AXIS[nme])
    return full


def _pieces(gfull, axis):
    shp = gfull.shape
    a = gfull.reshape(shp[:axis] + (N_DEV, shp[axis] // N_DEV) + shp[axis + 1:])
    return jnp.moveaxis(a, axis, 0).reshape(N_DEV, -1)


def kernel(x, norm_mix, norm_mlp, norm_final, w_up, w_down, swa_w_qkv, swa_b_qkv, swa_sinks, swa_w_o, hgrn_w_in, hgrn_lb_logits, hgrn_g_norm, hgrn_w_o, fox_w_in, fox_b_in, fox_w_o, loss_target, m_norm_mix, m_norm_mlp, m_norm_final, m_w_up, m_w_down, m_swa_w_qkv, m_swa_b_qkv, m_swa_sinks, m_swa_w_o, m_hgrn_w_in, m_hgrn_lb_logits, m_hgrn_g_norm, m_hgrn_w_o, m_fox_w_in, m_fox_b_in, m_fox_w_o, v_norm_mix, v_norm_mlp, v_norm_final, v_w_up, v_w_down, v_swa_w_qkv, v_swa_b_qkv, v_swa_sinks, v_swa_w_o, v_hgrn_w_in, v_hgrn_lb_logits, v_hgrn_g_norm, v_hgrn_w_o, v_fox_w_in, v_fox_b_in, v_fox_w_o):
    w = dict(norm_mix=norm_mix, norm_mlp=norm_mlp, norm_final=norm_final, w_up=w_up, w_down=w_down,
             swa_w_qkv=swa_w_qkv, swa_b_qkv=swa_b_qkv, swa_sinks=swa_sinks, swa_w_o=swa_w_o, hgrn_w_in=hgrn_w_in,
             hgrn_lb_logits=hgrn_lb_logits, hgrn_g_norm=hgrn_g_norm, hgrn_w_o=hgrn_w_o, fox_w_in=fox_w_in,
             fox_b_in=fox_b_in, fox_w_o=fox_w_o)
    mom = dict(norm_mix=m_norm_mix, norm_mlp=m_norm_mlp, norm_final=m_norm_final, w_up=m_w_up, w_down=m_w_down,
               swa_w_qkv=m_swa_w_qkv, swa_b_qkv=m_swa_b_qkv, swa_sinks=m_swa_sinks, swa_w_o=m_swa_w_o,
               hgrn_w_in=m_hgrn_w_in, hgrn_lb_logits=m_hgrn_lb_logits, hgrn_g_norm=m_hgrn_g_norm, hgrn_w_o=m_hgrn_w_o,
               fox_w_in=m_fox_w_in, fox_b_in=m_fox_b_in, fox_w_o=m_fox_w_o)
    var = dict(norm_mix=v_norm_mix, norm_mlp=v_norm_mlp, norm_final=v_norm_final, w_up=v_w_up, w_down=v_w_down,
               swa_w_qkv=v_swa_w_qkv, swa_b_qkv=v_swa_b_qkv, swa_sinks=v_swa_sinks, swa_w_o=v_swa_w_o,
               hgrn_w_in=v_hgrn_w_in, hgrn_lb_logits=v_hgrn_lb_logits, hgrn_g_norm=v_hgrn_g_norm, hgrn_w_o=v_hgrn_w_o,
               fox_w_in=v_fox_w_in, fox_b_in=v_fox_b_in, fox_w_o=v_fox_w_o)
    T = x.shape[1]
    x0 = x[0]
    tgt = loss_target[0]
    W = _gather_weights(w)
    zeros_b = jnp.zeros((1, 4 * D_MODEL), F32)

    def swa_layer(xin, i, j):
        qkv = norm_matmul(xin, norm_mix[i:i + 1], W['swa_w_qkv'][j], W['swa_b_qkv'][j:j + 1], BF16, f"swa_qkv_L{i}")
        dup = lambda a: jnp.broadcast_to(a.reshape(T, 4, 1, 64), (T, 4, 2, 64)).reshape(T, 4 * LANES)
        kdup, vdup = dup(qkv[:, 1024:1280]), dup(qkv[:, 1280:1536])
        sk = jnp.broadcast_to(jnp.pad(swa_sinks[j].reshape(4, 4), ((0, 0), (0, 4)))[:, :, None], (4, 8, LANES))
        ao, lse = swa_fwd(qkv, kdup, vdup, sk, f"swa_fwd_L{i}")
        xmid = matmul(ao, W['swa_w_o'][j], F32, f"swa_out_L{i}", res=xin)
        return xmid, (qkv, kdup, vdup, sk, ao, lse)

    def swa_layer_bwd(xin, saved, dmid, i, j, grads):
        qkv, kdup, vdup, sk, ao, lse = saved
        dao = matmul(dmid, W['swa_w_o'][j].T, BF16, f"swa_dout_L{i}")
        grads['swa_w_o'][j] = tn_matmul(ao, dmid, f"swa_dwo_L{i}")
        dq, dk, dv, dsk = swa_bwd(qkv, kdup, vdup, sk, ao, lse, dao, f"swa_bwd_L{i}")
        wt = W['swa_w_qkv'][j].T
        spread = lambda a: jnp.pad(a.reshape(4, 64, D_MODEL), ((0, 0), (0, 64), (0, 0))).reshape(4 * LANES, D_MODEL)
        gather = lambda a: a.reshape(a.shape[0], 4, LANES)[:, :, :64].reshape(a.shape[0], 256)
        dx, h, dg = proj_bwd(xin, norm_mix[i:i + 1], dmid,
                             [(dq, wt[:1024]), (dk, spread(wt[1024:1280])), (dv, spread(wt[1280:]))], f"swa_din_L{i}")
        gq, bq = tn_matmul(h, dq, f"swa_dwq_L{i}", colsum=True)
        gk, bk = tn_matmul(h, dk, f"swa_dwk_L{i}", colsum=True)
        gv, bv = tn_matmul(h, dv, f"swa_dwv_L{i}", colsum=True)
        grads['swa_w_qkv'][j] = jnp.concatenate([gq, gather(gk), gather(gv)], axis=1)
        grads['swa_b_qkv'][j] = jnp.concatenate([bq, gather(bk), gather(bv)], axis=1)[0]
        grads['swa_sinks'][j] = dsk[:, :4, 0].reshape(16)
        grads['norm_mix'][i] = dg[0]
        return dx

    lb_soft = lower_bound_fwd(hgrn_lb_logits, "hgrn_lb_fwd")
    lb = lb_soft[1:2]

    def hgrn_layer(xin, i, j):
        proj = norm_matmul(xin, norm_mix[i:i + 1], W['hgrn_w_in'][j], zeros_b, F32, f"hgrn_in_L{i}")
        ao, o_raw, states = hgrn_fwd(proj, lb, hgrn_g_norm[j:j + 1], f"hgrn_fwd_L{i}")
        xmid = matmul(ao, W['hgrn_w_o'][j], F32, f"hgrn_out_L{i}", res=xin)
        return xmid, (proj, ao, o_raw, states)

    def hgrn_layer_bwd(xin, saved, dmid, i, j, grads):
        proj, ao, o_raw, states = saved
        dao = matmul(dmid, W['hgrn_w_o'][j].T, BF16, f"hgrn_dout_L{i}")
        grads['hgrn_w_o'][j] = tn_matmul(ao, dmid, f"hgrn_dwo_L{i}")
        dq, dfl, dv, dgt, dlb, dgn = hgrn_bwd(proj, lb, hgrn_g_norm[j:j + 1], o_raw, states, dao, f"hgrn_bwd_L{i}")
        wt = W['hgrn_w_in'][j].T
        parts = [dq, dfl, dv, dgt]
        dx, h, dg = proj_bwd(xin, norm_mix[i:i + 1], dmid,
                             [(d, wt[n * D_MODEL:(n + 1) * D_MODEL]) for n, d in enumerate(parts)], f"hgrn_din_L{i}")
        grads['hgrn_w_in'][j] = jnp.concatenate(
            [tn_matmul(h, d, f"hgrn_dwin{n}_L{i}") for n, d in enumerate(parts)], axis=1)
        grads['hgrn_g_norm'][j] = dgn[0]
        grads['hgrn_lb_logits'] = lower_bound_bwd(lb_soft, dlb, "hgrn_lb_bwd")
        grads['norm_mix'][i] = dg[0]
        return dx

    def fox_layer(xin, i, j):
        w_in = W['fox_w_in'][j]
        b_in = W['fox_b_in'][j:j + 1]
        qkv = norm_matmul(xin, norm_mix[i:i + 1], w_in[:, :3072], b_in[:, :3072], BF16, f"fox_qkv_L{i}")
        wf = jnp.pad(w_in[:, 3072:], ((0, 0), (0, LANES - 16)))
        bf = jnp.pad(b_in[:, 3072:], ((0, 0), (0, LANES - 16)))
        fl = norm_matmul(xin, norm_mix[i:i + 1], wf, bf, F32, f"fox_f_L{i}")
        qa, ka, bounds = fox_gate_fwd(fl, qkv, f"fox_gate_L{i}")
        st = bounds[:, :4, :16].reshape(bounds.shape[0], 64)
        ao, lse, lmin = fox_fwd(st, qa, ka, qkv, f"fox_fwd_L{i}")
        st = jnp.concatenate([st, lmin[:, :, :2, 0].reshape(lmin.shape[0], 16)], axis=1)
        xmid = matmul(ao, W['fox_w_o'][j], F32, f"fox_out_L{i}", res=xin)
        return xmid, (qkv, fl, qa, ka, ao, lse, wf, st)

    def fox_layer_bwd(xin, saved, dmid, i, j, grads):
        qkv, fl, qa, ka, ao, lse, wf, st = saved
        dao = matmul(dmid, W['fox_w_o'][j].T, BF16, f"fox_dout_L{i}")
        grads['fox_w_o'][j] = tn_matmul(ao, dmid, f"fox_dwo_L{i}")
        delta = fox_delta(dao, ao, f"fox_delta_L{i}")
        dq, aux_q, dk, dv, aux_k = fox_bwd(st, qa, ka, qkv, lse[:, ::64].T.reshape(8, 2, T),
                                           delta[:, :16].T.reshape(8, 2, T), dao, f"fox_bwd_L{i}")
        dcp = jnp.pad(aux_q[:, ::64] - aux_k[:, 3::64], ((0, 0), (0, LANES - 16)))
        dfl = fox_gate_bwd(fl, dcp, f"fox_dgate_L{i}")
        wt = W['fox_w_in'][j][:, :3072].T
        parts = [dq, dk, dv]
        dx, h, dg = proj_bwd(xin, norm_mix[i:i + 1], dmid,
                             [(d, wt[n * D_MODEL:(n + 1) * D_MODEL]) for n, d in enumerate(parts)] + [(dfl, wf.T)],
                             f"fox_din_L{i}")
        gw = [tn_matmul(h, d, f"fox_dw{n}_L{i}", colsum=True) for n, d in enumerate(parts + [dfl])]
        grads['fox_w_in'][j] = jnp.concatenate([g for g, _ in gw[:3]] + [gw[3][0][:, :16]], axis=1)
        grads['fox_b_in'][j] = jnp.concatenate([b for _, b in gw[:3]] + [gw[3][1][:, :16]], axis=1)[0]
        grads['norm_mix'][i] = dg[0]
        return dx

    mixers = [(swa_layer, swa_layer_bwd), (hgrn_layer, hgrn_layer_bwd), (fox_layer, fox_layer_bwd)]

    xs, mids, saves = [x0], [], []
    for i in range(DEPTH):
        xmid, saved = mixers[i % 3][0](xs[-1], i, i // 3)
        mids.append(xmid)
        saves.append(saved)
        xs.append(mlp_fwd(xmid, norm_mlp[i:i + 1], W['w_up'][i], W['w_down'][i], f"mlp_fwd_L{i}"))

    grads = {n: [None] * w[n].shape[0] for n in WEIGHTS if n not in ('norm_final', 'hgrn_lb_logits')}
    loss_part, dx, dgf = final_loss(xs[-1], norm_final.reshape(1, D_MODEL), tgt, "final_loss")
    grads['norm_final'] = dgf[0]
    for i in reversed(range(DEPTH)):
        dmid, h, a, du, dg = mlp_bwd(mids[i], norm_mlp[i:i + 1], W['w_up'][i], W['w_up'][i].T, W['w_down'][i].T, dx,
                                     f"mlp_bwd_L{i}")
        grads['w_up'][i] = tn_matmul(h, du, f"mlp_dwup_L{i}")
        grads['w_down'][i] = tn_matmul(a, dx, f"mlp_dwdown_L{i}")
        grads['norm_mlp'][i] = dg[0]
        dx = mixers[i % 3][1](xs[i], saves[i], dmid, i, i // 3, grads)
    gfull = {n: (g if not isinstance(g, list) else jnp.stack(g)) for n, g in grads.items()}

    mats = [n for n in SHARDED if n not in BIASES]
    view = lambda a: a.reshape(-1, a.shape[-1])
    sends = [_pieces(gfull[n], SHARD_AXIS[n]).astype(BF16).reshape((N_DEV,) + view(w[n]).shape) for n in mats]
    common = jnp.concatenate([gfull[n].reshape(-1) for n in REPLICATED] + [loss_part[0, 0:1]])
    small = jnp.concatenate([jnp.broadcast_to(common[None], (N_DEV, common.shape[0]))]
                            + [_pieces(gfull[n], SHARD_AXIS[n]) for n in BIASES], axis=1)
    recvs = all_to_all_rows(sends + [_pad_rows(small, 16)])
    tail = lambda vals: _pad_rows(jnp.concatenate([vals[n].reshape(-1) for n in REPLICATED] + [jnp.zeros((1,), F32)]
                                                  + [vals[n].reshape(-1) for n in BIASES]), 16)
    res = [{}, {}, {}, {}]
    for nme, rv in zip(mats, recvs):
        outs = reduce_adamw(rv, view(w[nme]), view(mom[nme]), view(var[nme]), f"adamw_{nme}")
        for o, r in zip(outs, res):
            r[nme] = o.reshape(w[nme].shape)
    outs = reduce_adamw(recvs[-1], tail(w), tail(mom), tail(var), "adamw_small")
    off = 0
    for nme in REPLICATED + ['loss'] + list(BIASES):
        cnt = 1 if nme == 'loss' else w[nme].size
        if nme == 'loss':
            loss = outs[0].reshape(-1)[off]
        else:
            for o, r in zip(outs, res):
                r[nme] = o.reshape(-1)[off:off + cnt].reshape(w[nme].shape)
        off += cnt
    return (loss, dx[None], *[res[0][n] for n in WEIGHTS], *[res[1][n] for n in WEIGHTS],
            *[res[2][n] for n in WEIGHTS], *[res[3][n] for n in WEIGHTS])
```

```python
import functools

import jax
import jax.numpy as jnp
from jax import lax
from jax.experimental import pallas as pl
from jax.experimental.pallas import tpu as pltpu

F32 = jnp.float32
BF16 = jnp.bfloat16
HI = lax.Precision.HIGHEST

N_DEV = 8
D_MODEL = 1024
DEPTH = 4
EPS = 1e-6
SWA_WINDOW = 128
HGRN_CHUNK = 64
LANES = 128
VMEM_LIMIT = 56 << 20

ADAM_LR, ADAM_B1, ADAM_B2, ADAM_EPS, ADAM_WD, ADAM_STEP = 0.001, 0.9, 0.999, 1e-08, 0.01, 10

TM = 512
TF = 512
TK = 512
FOX_T = 1024
SWA_TQ = 512
HGRN_TG = 512
SCAN_T = 256

WEIGHTS = ['norm_mix', 'norm_mlp', 'norm_final', 'w_up', 'w_down', 'swa_w_qkv', 'swa_b_qkv', 'swa_sinks', 'swa_w_o',
           'hgrn_w_in', 'hgrn_lb_logits', 'hgrn_g_norm', 'hgrn_w_o', 'fox_w_in', 'fox_b_in', 'fox_w_o']
SHARD_AXIS = {'norm_mix': None, 'norm_mlp': None, 'norm_final': None, 'w_up': 2, 'w_down': 1, 'swa_w_qkv': 2,
              'swa_b_qkv': 1, 'swa_sinks': None, 'swa_w_o': 1, 'hgrn_w_in': 2, 'hgrn_lb_logits': None,
              'hgrn_g_norm': None, 'hgrn_w_o': 1, 'fox_w_in': 2, 'fox_b_in': 1, 'fox_w_o': 1}
SHARDED = [n for n in WEIGHTS if SHARD_AXIS[n] is not None]
REPLICATED = [n for n in WEIGHTS if SHARD_AXIS[n] is None]
BIASES = ('swa_b_qkv', 'fox_b_in')


def _cparams(*sem):
    return pltpu.CompilerParams(dimension_semantics=sem, vmem_limit_bytes=VMEM_LIMIT)


def _nt(a, b):
    return lax.dot_general(a, b, (((1,), (1,)), ((), ())), preferred_element_type=F32)


def _tn(a, b):
    return lax.dot_general(a, b, (((0,), (0,)), ((), ())), preferred_element_type=F32)


def _dot(a, b):
    return jnp.dot(a, b, preferred_element_type=F32)


def _sigmoid(x):
    return 1.0 / (1.0 + jnp.exp(-x))


def _rms(xv):
    return lax.rsqrt(jnp.mean(xv * xv, axis=-1, keepdims=True) + EPS)


def _rms_bwd(xv, g, dh):
    r = _rms(xv)
    xhat = xv * r
    dhg = dh * g
    dx = r * (dhg - xhat * jnp.mean(dhg * xhat, axis=-1, keepdims=True))
    return dx, jnp.sum(dh * xhat, axis=0, keepdims=True)


def _my_id():
    return lax.axis_index("x"), lax.axis_index("y"), lax.axis_index("c")


def _peer(x, y, c, k):
    return (lax.rem(x + ((k >> 2) & 1), 2), lax.rem(y + ((k >> 1) & 1), 2), lax.rem(c + (k & 1), 2))


def all_gather_shards(shards):
    n = len(shards)

    def place(o_ref, local, axis, dev):
        if axis is None:
            return o_ref.at[dev]
        idx = [slice(None)] * local.ndim
        idx[axis] = pl.ds(pl.multiple_of(dev * local.shape[axis], local.shape[axis]), local.shape[axis])
        return o_ref.at[tuple(idx)]

    def body(*refs):
        x_refs, o_refs = refs[:n], refs[n:2 * n]
        send_sems, recv_sems, loc_sems = refs[2 * n:]
        x, y, c = _my_id()
        sibling = (x, y, 1 - c)
        chips = [(1 - x, y), (x, 1 - y), (1 - x, 1 - y)]
        dev = lambda px, py, pc: 4 * px + 2 * py + pc

        def copy(a, k, block, to, src=None):
            local, axis = shards[a]
            spot = place(o_refs[a], local, axis, dev(*block))
            return pltpu.make_async_remote_copy(
                src_ref=spot if src is None else src, dst_ref=spot, send_sem=send_sems.at[a * 7 + k],
                recv_sem=recv_sems.at[a * 7 + k], device_id=to, device_id_type=pl.DeviceIdType.MESH)

        mines, sent = [], []
        for a in range(n):
            local, axis = shards[a]
            mine = pltpu.make_async_copy(x_refs[a], place(o_refs[a], local, axis, dev(x, y, c)), loc_sems.at[a])
            mine.start()
            mines.append(mine)
            for k, to in enumerate([sibling] + [(*chip, c) for chip in chips]):
                cp = copy(a, k, (x, y, c), to, src=x_refs[a])
                cp.start()
                sent.append(cp)
        passed = []
        for j, chip in enumerate(chips):
            for a in range(n):
                copy(a, 1 + j, (*chip, c), (x, y, c)).wait_recv()
                cp = copy(a, 4 + j, (*chip, c), sibling)
                cp.start()
                passed.append(cp)
        for a in range(n):
            copy(a, 0, sibling, (x, y, c)).wait_recv()
            for j, chip in enumerate(chips):
                copy(a, 4 + j, (*chip, 1 - c), (x, y, c)).wait_recv()
        for cp in sent + passed:
            cp.wait_send()
        for mine in mines:
            mine.wait()

    def full_shape(local, axis):
        if axis is None:
            return (N_DEV,) + local.shape
        return local.shape[:axis] + (N_DEV * local.shape[axis],) + local.shape[axis + 1:]

    hbm = pl.BlockSpec(memory_space=pl.ANY)
    return pl.pallas_call(
        body, name="all_gather_weights",
        out_shape=[jax.ShapeDtypeStruct(full_shape(l, ax), l.dtype) for l, ax in shards],
        in_specs=[hbm] * n, out_specs=[hbm] * n,
        scratch_shapes=[pltpu.SemaphoreType.DMA((n * (N_DEV - 1),)), pltpu.SemaphoreType.DMA((n * (N_DEV - 1),)),
                        pltpu.SemaphoreType.DMA((n,))],
    )(*[l for l, _ in shards])


def all_to_all_rows(sends):
    n = len(sends)

    def body(*refs):
        s_refs, r_refs = refs[:n], refs[n:2 * n]
        send_sems, recv_sems, loc_sems = refs[2 * n:]
        x, y, c = _my_id()
        me = 4 * x + 2 * y + c
        copies = []
        for a, (s_ref, r_ref) in enumerate(zip(s_refs, r_refs)):
            mine = pltpu.make_async_copy(s_ref.at[me], r_ref.at[me], loc_sems.at[a])
            mine.start()
            copies.append(mine)
            for k in range(1, N_DEV):
                px, py, pc = _peer(x, y, c, k)
                sem = a * (N_DEV - 1) + k - 1
                cp = pltpu.make_async_remote_copy(
                    src_ref=s_ref.at[4 * px + 2 * py + pc], dst_ref=r_ref.at[me],
                    send_sem=send_sems.at[sem], recv_sem=recv_sems.at[sem],
                    device_id=(px, py, pc), device_id_type=pl.DeviceIdType.MESH)
                cp.start()
                copies.append(cp)
        for cp in copies:
            cp.wait()

    hbm = pl.BlockSpec(memory_space=pl.ANY)
    return pl.pallas_call(
        body, name="all_to_all_grads",
        out_shape=[jax.ShapeDtypeStruct(s.shape, s.dtype) for s in sends],
        in_specs=[hbm] * n, out_specs=[hbm] * n,
        scratch_shapes=[pltpu.SemaphoreType.DMA((n * (N_DEV - 1),)), pltpu.SemaphoreType.DMA((n * (N_DEV - 1),)),
                        pltpu.SemaphoreType.DMA((n,))],
    )(*sends)


def reduce_adamw(recv, w, m, v, name):
    R, C = w.shape
    tr = max(t for t in range(16, (1 << 18) // C + 1, 16) if R % t == 0)
    c1 = 1.0 / (1.0 - ADAM_B1 ** ADAM_STEP)
    c2 = 1.0 / (1.0 - ADAM_B2 ** ADAM_STEP)

    def body(r_ref, w_ref, m_ref, v_ref, g_ref, d_ref, nm_ref, nv_ref):
        g = r_ref[0].astype(F32)
        for s in range(1, N_DEV):
            g = g + r_ref[s].astype(F32)
        m2 = ADAM_B1 * m_ref[...] + (1.0 - ADAM_B1) * g
        v2 = ADAM_B2 * v_ref[...] + (1.0 - ADAM_B2) * (g * g)
        g_ref[...] = g
        nm_ref[...] = m2
        nv_ref[...] = v2
        d_ref[...] = -ADAM_LR * ((m2 * c1) / (jnp.sqrt(v2 * c2) + ADAM_EPS) + ADAM_WD * w_ref[...])

    row = pl.BlockSpec((tr, C), lambda i: (i, 0))
    shp = jax.ShapeDtypeStruct((R, C), F32)
    return pl.pallas_call(
        body, name=name, grid=(R // tr,),
        in_specs=[pl.BlockSpec((N_DEV, tr, C), lambda i: (0, i, 0)), row, row, row],
        out_specs=[row, row, row, row], out_shape=[shp, shp, shp, shp],
        compiler_params=_cparams("parallel"),
    )(recv, w, m, v)


def norm_matmul(x, g, w, b, out_dtype, name):
    T, N = x.shape[0], w.shape[1]
    tm, tn = min(TM, T), min(512, N)

    def body(x_ref, g_ref, w_ref, b_ref, o_ref, h_sc):
        @pl.when(pl.program_id(1) == 0)
        def _():
            xv = x_ref[...]
            h_sc[...] = (xv * _rms(xv) * g_ref[...]).astype(BF16)
        o_ref[...] = (_dot(h_sc[...], w_ref[...]) + b_ref[...]).astype(o_ref.dtype)

    return pl.pallas_call(
        body, name=name, grid=(T // tm, N // tn),
        in_specs=[pl.BlockSpec((tm, D_MODEL), lambda i, j: (i, 0)), pl.BlockSpec((1, D_MODEL), lambda i, j: (0, 0)),
                  pl.BlockSpec((D_MODEL, tn), lambda i, j: (0, j)), pl.BlockSpec((1, tn), lambda i, j: (0, j))],
        out_specs=pl.BlockSpec((tm, tn), lambda i, j: (i, j)),
        out_shape=jax.ShapeDtypeStruct((T, N), out_dtype),
        scratch_shapes=[pltpu.VMEM((tm, D_MODEL), BF16)],
        compiler_params=_cparams("parallel", "arbitrary"),
    )(x, g, w, b)


def matmul(a, w, out_dtype, name, res=None):
    T, K = a.shape
    N = w.shape[1]
    tm = min(TM, T)

    def body(*refs):
        if res is None:
            a_ref, w_ref, o_ref = refs
            acc = _dot(a_ref[...].astype(BF16), w_ref[...])
        else:
            a_ref, w_ref, r_ref, o_ref = refs
            acc = r_ref[...] + _dot(a_ref[...].astype(BF16), w_ref[...])
        o_ref[...] = acc.astype(o_ref.dtype)

    in_specs = [pl.BlockSpec((tm, K), lambda i: (i, 0)), pl.BlockSpec((K, N), lambda i: (0, 0))]
    ops = [a, w]
    if res is not None:
        in_specs.append(pl.BlockSpec((tm, N), lambda i: (i, 0)))
        ops.append(res)
    return pl.pallas_call(
        body, name=name, grid=(T // tm,), in_specs=in_specs,
        out_specs=pl.BlockSpec((tm, N), lambda i: (i, 0)),
        out_shape=jax.ShapeDtypeStruct((T, N), out_dtype),
        compiler_params=_cparams("parallel"),
    )(*ops)


def tn_matmul(a, b, name, colsum=False):
    T, M = a.shape
    N = b.shape[1]
    tk = min(TK, T)
    tmm = min(1024, M)
    tn = N if N <= 1024 else (1024 if N % 1024 == 0 else N)

    def body(a_ref, b_ref, o_ref, *rest):
        k = pl.program_id(2)
        bv = b_ref[...]

        @pl.when(k == 0)
        def _():
            o_ref[...] = jnp.zeros(o_ref.shape, F32)
            if colsum:
                rest[0][...] = jnp.zeros(rest[0].shape, F32)

        o_ref[...] += _tn(a_ref[...].astype(BF16), bv.astype(BF16))
        if colsum:
            rest[0][...] += jnp.sum(bv.astype(F32), axis=0, keepdims=True)

    out_specs = [pl.BlockSpec((tmm, tn), lambda i, j, k: (i, j))]
    out_shape = [jax.ShapeDtypeStruct((M, N), F32)]
    if colsum:
        assert M == tmm
        out_specs.append(pl.BlockSpec((1, tn), lambda i, j, k: (0, j)))
        out_shape.append(jax.ShapeDtypeStruct((1, N), F32))
    out = pl.pallas_call(
        body, name=name, grid=(M // tmm, N // tn, T // tk),
        in_specs=[pl.BlockSpec((tk, tmm), lambda i, j, k: (k, i)), pl.BlockSpec((tk, tn), lambda i, j, k: (k, j))],
        out_specs=out_specs, out_shape=out_shape,
        compiler_params=_cparams("parallel", "parallel", "arbitrary"),
    )(a, b)
    return out if colsum else out[0]


def mlp_fwd(x, g, w_up, w_down, name):
    T, F = x.shape[0], w_up.shape[1]
    tm, tf = min(TM, T), min(TF, F)
    nf = F // tf

    def body(x_ref, g_ref, wu_ref, wd_ref, o_ref, h_sc, acc_sc):
        f = pl.program_id(1)

        @pl.when(f == 0)
        def _():
            xv = x_ref[...]
            h_sc[...] = (xv * _rms(xv) * g_ref[...]).astype(BF16)
            acc_sc[...] = xv

        u = jnp.maximum(_dot(h_sc[...], wu_ref[...]), 0.0)
        acc_sc[...] += _dot((u * u).astype(BF16), wd_ref[...])

        @pl.when(f == nf - 1)
        def _():
            o_ref[...] = acc_sc[...]

    return pl.pallas_call(
        body, name=name, grid=(T // tm, nf),
        in_specs=[pl.BlockSpec((tm, D_MODEL), lambda i, f: (i, 0)), pl.BlockSpec((1, D_MODEL), lambda i, f: (0, 0)),
                  pl.BlockSpec((D_MODEL, tf), lambda i, f: (0, f)), pl.BlockSpec((tf, D_MODEL), lambda i, f: (f, 0))],
        out_specs=pl.BlockSpec((tm, D_MODEL), lambda i, f: (i, 0)),
        out_shape=jax.ShapeDtypeStruct((T, D_MODEL), F32),
        scratch_shapes=[pltpu.VMEM((tm, D_MODEL), BF16), pltpu.VMEM((tm, D_MODEL), F32)],
        compiler_params=_cparams("parallel", "arbitrary"),
    )(x, g, w_up, w_down)


def mlp_bwd(x, g, w_up, w_up_t, w_down_t, dy, name):
    T, F = x.shape[0], w_up.shape[1]
    tm, tf = min(TM, T), min(TF, F)
    nf = F // tf

    def body(x_ref, g_ref, wu_ref, wut_ref, wdt_ref, dy_ref, dx_ref, h_ref, a_ref, du_ref, dg_ref, h_sc, dyb_sc, dh_sc):
        i, f = pl.program_id(0), pl.program_id(1)

        @pl.when(f == 0)
        def _():
            xv = x_ref[...]
            h = (xv * _rms(xv) * g_ref[...]).astype(BF16)
            h_sc[...] = h
            h_ref[...] = h
            dyb_sc[...] = dy_ref[...].astype(BF16)
            dh_sc[...] = jnp.zeros(dh_sc.shape, F32)

        @pl.when((i == 0) & (f == 0))
        def _():
            dg_ref[...] = jnp.zeros(dg_ref.shape, F32)

        u = jnp.maximum(_dot(h_sc[...], wu_ref[...]), 0.0)
        a_ref[...] = (u * u).astype(BF16)
        du = (_dot(dyb_sc[...], wdt_ref[...]) * (2.0 * u)).astype(BF16)
        du_ref[...] = du
        dh_sc[...] += _dot(du, wut_ref[...])

        @pl.when(f == nf - 1)
        def _():
            dx, dg = _rms_bwd(x_ref[...], g_ref[...], dh_sc[...])
            dx_ref[...] = dy_ref[...] + dx
            dg_ref[...] += dg

    row = pl.BlockSpec((tm, D_MODEL), lambda i, f: (i, 0))
    hid = pl.BlockSpec((tm, tf), lambda i, f: (i, f))
    return pl.pallas_call(
        body, name=name, grid=(T // tm, nf),
        in_specs=[row, pl.BlockSpec((1, D_MODEL), lambda i, f: (0, 0)),
                  pl.BlockSpec((D_MODEL, tf), lambda i, f: (0, f)), pl.BlockSpec((tf, D_MODEL), lambda i, f: (f, 0)),
                  pl.BlockSpec((D_MODEL, tf), lambda i, f: (0, f)), row],
        out_specs=[row, row, hid, hid, pl.BlockSpec((1, D_MODEL), lambda i, f: (0, 0))],
        out_shape=[jax.ShapeDtypeStruct((T, D_MODEL), F32), jax.ShapeDtypeStruct((T, D_MODEL), BF16),
                   jax.ShapeDtypeStruct((T, F), BF16), jax.ShapeDtypeStruct((T, F), BF16),
                   jax.ShapeDtypeStruct((1, D_MODEL), F32)],
        scratch_shapes=[pltpu.VMEM((tm, D_MODEL), BF16), pltpu.VMEM((tm, D_MODEL), BF16),
                        pltpu.VMEM((tm, D_MODEL), F32)],
        compiler_params=_cparams("arbitrary", "arbitrary"),
    )(x, g, w_up, w_up_t, w_down_t, dy)


def proj_bwd(x, g, dres, parts, name):
    T = x.shape[0]
    tm = min(TM, T)
    n = len(parts)

    def body(*refs):
        x_ref, g_ref, dr_ref = refs[:3]
        da_refs, wt_refs = refs[3:3 + n], refs[3 + n:3 + 2 * n]
        dx_ref, h_ref, dg_ref = refs[3 + 2 * n:]

        @pl.when(pl.program_id(0) == 0)
        def _():
            dg_ref[...] = jnp.zeros(dg_ref.shape, F32)

        xv = x_ref[...]
        dh = _dot(da_refs[0][...].astype(BF16), wt_refs[0][...])
        for a_ref, w_ref in zip(da_refs[1:], wt_refs[1:]):
            dh = dh + _dot(a_ref[...].astype(BF16), w_ref[...])
        h_ref[...] = (xv * _rms(xv) * g_ref[...]).astype(BF16)
        dx, dg = _rms_bwd(xv, g_ref[...], dh)
        dx_ref[...] = dr_ref[...] + dx
        dg_ref[...] += dg

    row = pl.BlockSpec((tm, D_MODEL), lambda i: (i, 0))
    one = pl.BlockSpec((1, D_MODEL), lambda i: (0, 0))
    in_specs = [row, one, row]
    in_specs += [pl.BlockSpec((tm, da.shape[1]), lambda i: (i, 0)) for da, _ in parts]
    in_specs += [pl.BlockSpec(wt.shape, lambda i: (0, 0)) for _, wt in parts]
    return pl.pallas_call(
        body, name=name, grid=(T // tm,), in_specs=in_specs,
        out_specs=[row, row, one],
        out_shape=[jax.ShapeDtypeStruct((T, D_MODEL), F32), jax.ShapeDtypeStruct((T, D_MODEL), BF16),
                   jax.ShapeDtypeStruct((1, D_MODEL), F32)],
        compiler_params=_cparams("arbitrary"),
    )(x, g, dres, *[da for da, _ in parts], *[wt for _, wt in parts])


def final_loss(x, g, tgt, name):
    T = x.shape[0]
    tm = min(TM, T)

    def body(x_ref, g_ref, t_ref, l_ref, dx_ref, dg_ref):
        @pl.when(pl.program_id(0) == 0)
        def _():
            l_ref[...] = jnp.zeros(l_ref.shape, F32)
            dg_ref[...] = jnp.zeros(dg_ref.shape, F32)

        xv = x_ref[...]
        gv = g_ref[...]
        err = xv * _rms(xv) * gv - t_ref[...]
        l_ref[...] += 0.5 * jnp.sum(jnp.mean(err * err, axis=-1, keepdims=True), axis=0, keepdims=True)
        dx, dg = _rms_bwd(xv, gv, err * (1.0 / D_MODEL))
        dx_ref[...] = dx
        dg_ref[...] += dg

    row = pl.BlockSpec((tm, D_MODEL), lambda i: (i, 0))
    one = pl.BlockSpec((1, D_MODEL), lambda i: (0, 0))
    return pl.pallas_call(
        body, name=name, grid=(T // tm,), in_specs=[row, one, row],
        out_specs=[pl.BlockSpec((8, LANES), lambda i: (0, 0)), row, one],
        out_shape=[jax.ShapeDtypeStruct((8, LANES), F32), jax.ShapeDtypeStruct((T, D_MODEL), F32),
                   jax.ShapeDtypeStruct((1, D_MODEL), F32)],
        compiler_params=_cparams("arbitrary"),
    )(x, g, tgt)


def _swa_specs(tq):
    r = tq // SWA_WINDOW
    cur = lambda ix: pl.BlockSpec((tq, LANES), lambda kv, i: (ix(i), kv))
    prev = lambda ix: pl.BlockSpec((SWA_WINDOW, LANES), lambda kv, i: (jnp.maximum(ix(i) * r - 1, 0), kv))
    return cur, prev


W2 = 2 * SWA_WINDOW
SWA_RB = 32


def _swa_visible(tile):
    r = lax.broadcasted_iota(jnp.int32, (SWA_WINDOW, W2), 0)
    c = lax.broadcasted_iota(jnp.int32, (SWA_WINDOW, W2), 1)
    inside = (c > r) & (c <= r + SWA_WINDOW)
    return inside & ((c >= SWA_WINDOW) | (tile > 0)), inside


def swa_fwd(qkv, kdup, vdup, sinks_b, name):
    T = qkv.shape[0]
    tq = min(SWA_TQ, T)
    nsub = tq // SWA_WINDOW
    cur, prev = _swa_specs(tq)
    ident = lambda i: i

    def body(q_ref, kc_ref, kp_ref, vc_ref, vp_ref, sk_ref, o_ref, lse_ref, s_sc, e_sc):
        i = pl.program_id(1)
        kcat = jnp.concatenate([kp_ref[...], kc_ref[...]], axis=0)
        vcat = jnp.concatenate([vp_ref[...], vc_ref[...]], axis=0)
        vis_first, vis_in = _swa_visible(i)
        lane = lax.broadcasted_iota(jnp.int32, (1, LANES), 1)
        lse_all = jnp.zeros((tq, LANES), F32)
        for pp in range(2):
            q2 = q_ref[:, pp * LANES:(pp + 1) * LANES]
            outs = []
            for hf in range(2):
                g = 2 * pp + hf
                qm = jnp.where(_half(lane, hf), q2, jnp.zeros_like(q2))
                for nb in range(nsub):
                    rows = slice(nb * SWA_WINDOW, (nb + 1) * SWA_WINDOW)
                    s = _nt(qm[rows], kcat[nb * SWA_WINDOW:nb * SWA_WINDOW + W2]) * 0.125
                    s_sc[rows, :] = jnp.where(vis_first if nb == 0 else vis_in, s, -1e30)
                sk = sk_ref[g:g + 1, 0:1]
                m = jnp.maximum(jnp.max(s_sc[...], axis=1, keepdims=True), sk)
                m_rep = jnp.broadcast_to(m, (tq, LANES))
                parts = []
                for r0 in range(0, tq, SWA_RB):
                    rs = slice(r0, r0 + SWA_RB)
                    e0 = jnp.exp(s_sc[rs, 0:LANES] - m_rep[rs])
                    e1 = jnp.exp(s_sc[rs, LANES:W2] - m_rep[rs])
                    e_sc[rs, :] = jnp.concatenate([e0.astype(BF16), e1.astype(BF16)], axis=1)
                    parts.append(e0 + e1)
                den = jnp.sum(jnp.concatenate(parts, axis=0), axis=1, keepdims=True) + jnp.exp(sk - m)
                pv = [_dot(e_sc[nb * SWA_WINDOW:(nb + 1) * SWA_WINDOW, :], vcat[nb * SWA_WINDOW:nb * SWA_WINDOW + W2])
                      for nb in range(nsub)]
                outs.append(jnp.concatenate(pv, axis=0) * (1.0 / den))
                lse_all = jnp.where(lane == g, m + jnp.log(den), lse_all)
            o_ref[:, pp * LANES:(pp + 1) * LANES] = jnp.where(lane < 64, outs[0], outs[1]).astype(BF16)
        lse_ref[...] = lse_all

    return pl.pallas_call(
        body, name=name, grid=(4, T // tq),
        in_specs=[pl.BlockSpec((tq, 2 * LANES), lambda kv, i: (i, kv)), cur(ident), prev(ident), cur(ident), prev(ident),
                  pl.BlockSpec((None, 8, LANES), lambda kv, i: (kv, 0, 0))],
        out_specs=[pl.BlockSpec((tq, 2 * LANES), lambda kv, i: (i, kv)), cur(ident)],
        out_shape=[jax.ShapeDtypeStruct((T, D_MODEL), BF16), jax.ShapeDtypeStruct((T, 4 * LANES), F32)],
        scratch_shapes=[pltpu.VMEM((tq, W2), F32), pltpu.VMEM((tq, W2), BF16)],
        compiler_params=_cparams("parallel", "arbitrary"),
    )(qkv, kdup, kdup, vdup, vdup, sinks_b)


def swa_bwd(qkv, kdup, vdup, sinks_b, o, lse, do, name):
    T = qkv.shape[0]
    tq = min(SWA_TQ, T)
    n = T // tq
    nsub = tq // SWA_WINDOW
    cur, prev = _swa_specs(tq)
    rev = lambda i: n - 1 - i

    def body(q_ref, kc_ref, kp_ref, vc_ref, vp_ref, sk_ref, o_ref, lse_ref, do_ref, dq_ref, dk_ref, dv_ref, dsk_ref,
             ck_sc, cv_sc, dkc_sc, dvc_sc):
        i = pl.program_id(1)

        @pl.when(i == 0)
        def _():
            ck_sc[...] = jnp.zeros(ck_sc.shape, F32)
            cv_sc[...] = jnp.zeros(cv_sc.shape, F32)
            dsk_ref[...] = jnp.zeros(dsk_ref.shape, F32)

        kcat = jnp.concatenate([kp_ref[...], kc_ref[...]], axis=0)
        vcat = jnp.concatenate([vp_ref[...], vc_ref[...]], axis=0)
        vis_first, vis_in = _swa_visible(n - 1 - i)
        lane = lax.broadcasted_iota(jnp.int32, (1, LANES), 1)
        dkc_sc[...] = jnp.zeros(dkc_sc.shape, F32)
        dvc_sc[...] = jnp.zeros(dvc_sc.shape, F32)
        for pp in range(2):
            sl = slice(pp * LANES, (pp + 1) * LANES)
            q2, do2, o2 = q_ref[:, sl], do_ref[:, sl], o_ref[:, sl]
            dqs = []
            for hf in range(2):
                g = 2 * pp + hf
                lm = _half(lane, hf)
                qm = jnp.where(lm, q2, jnp.zeros_like(q2))
                dom = jnp.where(lm, do2, jnp.zeros_like(do2))
                delta = jnp.sum(dom.astype(F32) * o2.astype(F32), axis=1, keepdims=True)
                lse_g = lse_ref[:, g:g + 1]
                psk = jnp.exp(sk_ref[g:g + 1, 0:1] - lse_g)
                dsk_ref[g:g + 1, :] += jnp.zeros((1, LANES), F32) - jnp.sum(psk * delta, axis=0, keepdims=True)
                dq_parts = []
                for nb in range(nsub):
                    rows = slice(nb * SWA_WINDOW, (nb + 1) * SWA_WINDOW)
                    band = slice(nb * SWA_WINDOW, nb * SWA_WINDOW + W2)
                    s = jnp.where(vis_first if nb == 0 else vis_in, _nt(qm[rows], kcat[band]) * 0.125, -1e30)
                    p = jnp.exp(s - lse_g[rows])
                    dsb = (p * (_nt(dom[rows], vcat[band]) - delta[rows]) * 0.125).astype(BF16)
                    dq_parts.append(_dot(dsb, kcat[band]))
                    dkc_sc[band, :] += _tn(dsb, qm[rows])
                    dvc_sc[band, :] += _tn(p.astype(BF16), dom[rows])
                dqs.append(jnp.concatenate(dq_parts, axis=0))
            dq_ref[:, sl] = jnp.where(lane < 64, dqs[0], dqs[1]).astype(BF16)
        dkc = dkc_sc[...]
        dvc = dvc_sc[...]
        dkc = dkc + pltpu.roll(dkc, 64, 1)
        dvc = dvc + pltpu.roll(dvc, 64, 1)
        for full, ref, carry in ((dkc, dk_ref, ck_sc), (dvc, dv_ref, cv_sc)):
            if tq > SWA_WINDOW:
                ref[0:tq - SWA_WINDOW, :] = full[SWA_WINDOW:tq, :]
            ref[tq - SWA_WINDOW:tq, :] = full[tq:tq + SWA_WINDOW, :] + carry[...]
            carry[...] = full[0:SWA_WINDOW, :]

    wide = pl.BlockSpec((tq, 2 * LANES), lambda kv, i: (rev(i), kv))
    return pl.pallas_call(
        body, name=name, grid=(4, n),
        in_specs=[wide, cur(rev), prev(rev), cur(rev), prev(rev),
                  pl.BlockSpec((None, 8, LANES), lambda kv, i: (kv, 0, 0)), wide, cur(rev), wide],
        out_specs=[wide, cur(rev), cur(rev), pl.BlockSpec((None, 8, LANES), lambda kv, i: (kv, 0, 0))],
        out_shape=[jax.ShapeDtypeStruct((T, D_MODEL), BF16), jax.ShapeDtypeStruct((T, 4 * LANES), F32),
                   jax.ShapeDtypeStruct((T, 4 * LANES), F32), jax.ShapeDtypeStruct((4, 8, LANES), F32)],
        scratch_shapes=[pltpu.VMEM((SWA_WINDOW, LANES), F32), pltpu.VMEM((SWA_WINDOW, LANES), F32),
                        pltpu.VMEM((tq + SWA_WINDOW, LANES), F32), pltpu.VMEM((tq + SWA_WINDOW, LANES), F32)],
        compiler_params=_cparams("arbitrary", "arbitrary"),
    )(qkv, kdup, kdup, vdup, vdup, sinks_b, o, lse, do)


def fox_gate_fwd(fl, qkv, name):
    T = fl.shape[0]
    ts = min(SCAN_T, T)
    per_tile = min(FOX_T, T) // ts

    def body(fl_ref, q_ref, k_ref, qa_ref, ka_ref, st_ref, carry):
        @pl.when(pl.program_id(0) == 0)
        def _():
            carry[...] = jnp.zeros(carry.shape, F32)

        xv = fl_ref[...]
        ls = jnp.minimum(xv, 0.0) - jnp.log(1.0 + jnp.exp(-jnp.abs(xv)))
        tri = (lax.broadcasted_iota(jnp.int32, (ts, ts), 0) >= lax.broadcasted_iota(jnp.int32, (ts, ts), 1)).astype(F32)
        cs = jnp.dot(tri, ls, precision=HI, preferred_element_type=F32) + carry[...]
        carry[...] = cs[ts - 1:ts, :]
        c1 = cs.astype(BF16).astype(F32)
        c2 = (cs - c1).astype(BF16).astype(F32)
        c3 = (cs - c1 - c2).astype(BF16).astype(F32)
        lane = lax.broadcasted_iota(jnp.int32, (1, LANES), 1)
        ones_q = jnp.where((lane >= 67) & (lane < 70), 1.0, 0.0)
        ones_k = jnp.where((lane >= 64) & (lane < 67), 1.0, 0.0)
        nq = jnp.zeros((1, LANES), F32)
        nk = jnp.zeros((1, LANES), F32)
        for b in range(8):
            qf = q_ref[:, b * LANES:(b + 1) * LANES].astype(F32) * 0.125
            kf = k_ref[:, b * LANES:(b + 1) * LANES].astype(F32)
            for hf in range(2):
                h = 2 * b + hf
                a1, a2, a3 = c1[:, h:h + 1], c2[:, h:h + 1], c3[:, h:h + 1]
                aux_q = jnp.where(lane == 64, a1, jnp.where(lane == 65, a2, jnp.where(lane == 66, a3, ones_q)))
                aux_k = jnp.where(lane == 67, -a1, jnp.where(lane == 68, -a2, jnp.where(lane == 69, -a3, ones_k)))
                qs = qf if hf == 0 else pltpu.roll(qf, 64, 1)
                ks = kf if hf == 0 else pltpu.roll(kf, 64, 1)
                qa_ref[:, h * LANES:(h + 1) * LANES] = jnp.where(lane < 64, qs, aux_q).astype(BF16)
                ka_ref[:, h * LANES:(h + 1) * LANES] = jnp.where(lane < 64, ks, aux_k).astype(BF16)
                for src, is_q in ((qf, True), (kf, False)):
                    sq = jnp.sum(jnp.where(_half(lane, hf), src * src, 0.0), axis=1, keepdims=True)
                    big = jnp.sqrt(jnp.max(sq, axis=0, keepdims=True))
                    if is_q:
                        nq = jnp.where(lane == h, big, nq)
                    else:
                        nk = jnp.where(lane == h, big, nk)
        new = jnp.concatenate([nq, nk, jnp.max(cs, axis=0, keepdims=True), jnp.min(cs, axis=0, keepdims=True),
                               jnp.zeros((4, LANES), F32)], axis=0)
        first = pl.program_id(0) % per_tile == 0
        row = lax.broadcasted_iota(jnp.int32, (8, LANES), 0)

        @pl.when(first)
        def _():
            st_ref[...] = new

        @pl.when(jnp.logical_not(first))
        def _():
            old = st_ref[...]
            st_ref[...] = jnp.where(row == 3, jnp.minimum(old, new), jnp.maximum(old, new))

    out = pl.BlockSpec((ts, 16 * LANES), lambda i: (i, 0))
    return pl.pallas_call(
        body, name=name, grid=(T // ts,),
        in_specs=[pl.BlockSpec((ts, LANES), lambda i: (i, 0)), pl.BlockSpec((ts, D_MODEL), lambda i: (i, 0)),
                  pl.BlockSpec((ts, D_MODEL), lambda i: (i, 1))],
        out_specs=[out, out, pl.BlockSpec((None, 8, LANES), lambda i: (i // per_tile, 0, 0))],
        out_shape=[jax.ShapeDtypeStruct((T, 16 * LANES), BF16), jax.ShapeDtypeStruct((T, 16 * LANES), BF16),
                   jax.ShapeDtypeStruct((T // ts // per_tile, 8, LANES), F32)],
        scratch_shapes=[pltpu.VMEM((1, LANES), F32)],
        compiler_params=_cparams("arbitrary"),
    )(fl, qkv, qkv)


def fox_gate_bwd(fl, dc, name):
    T = fl.shape[0]
    ts = min(SCAN_T, T)
    n = T // ts

    def body(fl_ref, dc_ref, o_ref, carry):
        @pl.when(pl.program_id(0) == 0)
        def _():
            carry[...] = jnp.zeros(carry.shape, F32)

        tri = (lax.broadcasted_iota(jnp.int32, (ts, ts), 0) <= lax.broadcasted_iota(jnp.int32, (ts, ts), 1)).astype(F32)
        rs = jnp.dot(tri, dc_ref[...], precision=HI, preferred_element_type=F32) + carry[...]
        carry[...] = rs[0:1, :]
        o_ref[...] = rs * (1.0 / (1.0 + jnp.exp(fl_ref[...])))

    blk = pl.BlockSpec((ts, LANES), lambda i: (n - 1 - i, 0))
    return pl.pallas_call(
        body, name=name, grid=(n,), in_specs=[blk, blk], out_specs=blk,
        out_shape=jax.ShapeDtypeStruct((T, LANES), F32), scratch_shapes=[pltpu.VMEM((1, LANES), F32)],
        compiler_params=_cparams("arbitrary"),
    )(fl, dc)


FOX_RB = 32


def _half(lane, hf):
    return (lane < 64) if hf == 0 else (lane >= 64)


def _pair(lane, a, b):
    return jnp.where(lane < 64, a, pltpu.roll(b, 64, 1)), jnp.where(lane < 64, pltpu.roll(a, 64, 1), b)


FOX_SKIP = 110.0


def _tile_bound(st_ref, i, j, h):
    return st_ref[i, h] * st_ref[j, 16 + h] * 1.01 + (st_ref[i, 32 + h] - st_ref[j, 48 + h]) + 1.0


def fox_fwd(st, qa, ka, qkv, name):
    T = qa.shape[0]
    t = min(FOX_T, T)
    n = T // t

    def body(st_ref, qa_ref, ka_ref, v_ref, o_ref, lse_ref, lmin_ref, m_sc, l_sc, acc_sc, ls_sc, s_sc, p_sc, mmin_sc):
        i, jj = pl.program_id(1), pl.program_id(2)
        j = jnp.maximum(i - jj, 0)
        lane = lax.broadcasted_iota(jnp.int32, (1, LANES), 1)

        def head_tile(hf, diag):
            hs = slice(hf * LANES, (hf + 1) * LANES)
            sv = _nt(qa_ref[:, hs], ka_ref[:, hs])
            if diag:
                vis = lax.broadcasted_iota(jnp.int32, (t, t), 0) >= lax.broadcasted_iota(jnp.int32, (t, t), 1)
                sv = jnp.where(vis, sv, -1e30)
            s_sc[...] = sv
            m_old = m_sc[hf]
            m_new = jnp.maximum(m_old, jnp.max(s_sc[...], axis=1, keepdims=True))
            al = jnp.exp(m_old - m_new)
            m_sc[hf] = m_new
            mmin_sc[hf] = jnp.min(m_new)
            for r0 in range(0, t, FOX_RB):
                rs = slice(r0, r0 + FOX_RB)
                mrow = m_new[rs, :]
                part, pieces = None, []
                for cb in range(0, t, LANES):
                    pc = jnp.exp(s_sc[rs, cb:cb + LANES] - mrow)
                    part = pc if part is None else part + pc
                    pieces.append(pc.astype(BF16))
                p_sc[rs, :] = jnp.concatenate(pieces, axis=1)
                ls_sc[rs, :] = part
            l_sc[hf] = al * l_sc[hf] + ls_sc[...]
            acc_sc[hf] = al * acc_sc[hf] + _dot(p_sc[...], v_ref[...])

        @pl.when(jj == 0)
        def _():
            m_sc[...] = jnp.full(m_sc.shape, -1e30, F32)
            l_sc[...] = jnp.zeros(l_sc.shape, F32)
            acc_sc[...] = jnp.zeros(acc_sc.shape, F32)
            head_tile(0, True)
            head_tile(1, True)

        for hf in range(2):
            h = 2 * pl.program_id(0) + hf
            live = _tile_bound(st_ref, i, j, h) >= mmin_sc[hf] - FOX_SKIP

            @pl.when((jj > 0) & (jj <= i) & live)
            def _():
                head_tile(hf, False)

        @pl.when(jj == i)
        def _():
            l0 = jnp.sum(l_sc[0], axis=1, keepdims=True)
            l1 = jnp.sum(l_sc[1], axis=1, keepdims=True)
            lse0, lse1 = m_sc[0] + jnp.log(l0), m_sc[1] + jnp.log(l1)
            o_ref[...] = jnp.where(lane < 64, acc_sc[0] / l0, acc_sc[1] / l1).astype(BF16)
            lse_ref[...] = jnp.where(lane < 64, lse0, lse1)
            row = lax.broadcasted_iota(jnp.int32, (8, LANES), 0)
            lmin_ref[...] = jnp.where(row == 0, jnp.min(lse0), jnp.where(row == 1, jnp.min(lse1), 0.0))

    oblk = pl.BlockSpec((t, LANES), lambda p, i, jj, st: (i, p))
    return pl.pallas_call(
        body, name=name,
        grid_spec=pltpu.PrefetchScalarGridSpec(
            num_scalar_prefetch=1, grid=(8, n, n),
            in_specs=[pl.BlockSpec((t, 2 * LANES), lambda p, i, jj, st: (i, p)),
                      pl.BlockSpec((t, 2 * LANES), lambda p, i, jj, st: (jnp.maximum(i - jj, 0), p)),
                      pl.BlockSpec((t, LANES), lambda p, i, jj, st: (jnp.maximum(i - jj, 0), 16 + p))],
            out_specs=[oblk, oblk, pl.BlockSpec((None, None, 8, LANES), lambda p, i, jj, st: (i, p, 0, 0))],
            scratch_shapes=[pltpu.VMEM((2, t, LANES), F32), pltpu.VMEM((2, t, LANES), F32),
                            pltpu.VMEM((2, t, LANES), F32), pltpu.VMEM((t, LANES), F32), pltpu.VMEM((t, t), F32),
                            pltpu.VMEM((t, t), BF16), pltpu.SMEM((2,), F32)]),
        out_shape=[jax.ShapeDtypeStruct((T, D_MODEL), BF16), jax.ShapeDtypeStruct((T, D_MODEL), F32),
                   jax.ShapeDtypeStruct((n, 8, 8, LANES), F32)],
        compiler_params=_cparams("parallel", "parallel", "arbitrary"),
    )(st, qa, ka, qkv)


def fox_delta(do, o, name):
    T = do.shape[0]
    tm = min(TM, T)

    def body(do_ref, o_ref, d_ref):
        lane = lax.broadcasted_iota(jnp.int32, (1, LANES), 1)
        out = jnp.zeros((tm, LANES), F32)
        for b in range(8):
            d = do_ref[:, b * LANES:(b + 1) * LANES].astype(F32) * o_ref[:, b * LANES:(b + 1) * LANES].astype(F32)
            for hf in range(2):
                out = jnp.where(lane == 2 * b + hf, jnp.sum(jnp.where(_half(lane, hf), d, 0.0), axis=1, keepdims=True), out)
        d_ref[...] = out

    row = pl.BlockSpec((tm, D_MODEL), lambda i: (i, 0))
    return pl.pallas_call(
        body, name=name, grid=(T // tm,), in_specs=[row, row],
        out_specs=pl.BlockSpec((tm, LANES), lambda i: (i, 0)),
        out_shape=jax.ShapeDtypeStruct((T, LANES), F32),
        compiler_params=_cparams("parallel"),
    )(do, o)


def fox_bwd(st, qa, ka, qkv, lse_row, delta_row, do, name):
    T = qa.shape[0]
    t = min(FOX_T, T)
    n = T // t

    def body(st_ref, qa_ref, ka_ref, v_ref, lr_ref, dr_ref, do_ref, dq_ref, auxq_ref, dk_ref, dv_ref, aux_ref,
             dq_sc, dk_sc, dv_sc, s_sc, dp_sc, p_sc, ds_sc, dqo_sc, auxo_sc, out_sems):
        j, i = pl.program_id(1), pl.program_id(2)
        lane = lax.broadcasted_iota(jnp.int32, (1, LANES), 1)
        qrows = pl.ds(pl.multiple_of(i * t, t), t)

        @pl.when(i == 0)
        def _():
            dk_sc[...] = jnp.zeros(dk_sc.shape, F32)
            dv_sc[...] = jnp.zeros(dv_sc.shape, F32)

        @pl.when(j == 0)
        def _():
            dq_sc[:, qrows, :] = jnp.zeros((2, t, LANES), F32)

        def head_tile(hf, diag):
            v2, do2 = v_ref[...], do_ref[...]
            hs = slice(hf * LANES, (hf + 1) * LANES)
            lm = _half(lane, hf)
            qh = qa_ref[:, hs]
            s_sc[...] = _nt(ka_ref[:, hs], qh)
            dp_sc[...] = _nt(jnp.where(lm, v2, jnp.zeros_like(v2)), do2)
            lrow, drow = lr_ref[hf:hf + 1, :], dr_ref[hf:hf + 1, :]
            for r0 in range(0, t, FOX_RB):
                rs = slice(r0, r0 + FOX_RB)
                sv = s_sc[rs, :]
                if diag:
                    vis = lax.broadcasted_iota(jnp.int32, (FOX_RB, t), 1) >= (r0 + lax.broadcasted_iota(jnp.int32, (FOX_RB, t), 0))
                    sv = jnp.where(vis, sv, -1e30)
                p = jnp.exp(sv - lrow)
                p_sc[rs, :] = p.astype(BF16)
                ds_sc[rs, :] = (p * (dp_sc[rs, :] - drow)).astype(BF16)
            dv_sc[...] += _dot(p_sc[...], jnp.where(lm, do2, jnp.zeros_like(do2)))
            dk_sc[hf] += _dot(ds_sc[...], qh)
            dq_sc[hf, qrows, :] += _tn(ds_sc[...], ka_ref[:, hs])

        for hf in range(2):
            h = 2 * pl.program_id(0) + hf
            live = _tile_bound(st_ref, i, j, h) >= st_ref[i, 64 + h] - FOX_SKIP

            @pl.when((i > j) & live)
            def _():
                head_tile(hf, False)

        @pl.when(i == j)
        def _():
            head_tile(0, True)
            head_tile(1, True)
            dq, aux = _pair(lane, dq_sc[0, qrows, :], dq_sc[1, qrows, :])
            dqo_sc[...] = (dq * 0.125).astype(BF16)
            auxo_sc[...] = aux
            cols = pl.ds(pl.multiple_of(pl.program_id(0) * LANES, LANES), LANES)
            c1 = pltpu.make_async_copy(dqo_sc, dq_ref.at[qrows, cols], out_sems.at[0])
            c2 = pltpu.make_async_copy(auxo_sc, auxq_ref.at[qrows, cols], out_sems.at[1])
            c1.start()
            c2.start()
            c1.wait()
            c2.wait()

        @pl.when(i == n - 1)
        def _():
            dk, aux = _pair(lane, dk_sc[0], dk_sc[1])
            dk_ref[...] = dk.astype(BF16)
            aux_ref[...] = aux
            dv_ref[...] = dv_sc[...].astype(BF16)

    qblk = pl.BlockSpec((t, LANES), lambda p, j, i, st: (jnp.maximum(i, j), p))
    kblk = pl.BlockSpec((t, LANES), lambda p, j, i, st: (j, p))
    rblk = pl.BlockSpec((None, 2, t), lambda p, j, i, st: (p, 0, jnp.maximum(i, j)))
    bf, f32 = jax.ShapeDtypeStruct((T, D_MODEL), BF16), jax.ShapeDtypeStruct((T, D_MODEL), F32)
    return pl.pallas_call(
        body, name=name,
        grid_spec=pltpu.PrefetchScalarGridSpec(
            num_scalar_prefetch=1, grid=(8, n, n),
            in_specs=[pl.BlockSpec((t, 2 * LANES), lambda p, j, i, st: (jnp.maximum(i, j), p)),
                      pl.BlockSpec((t, 2 * LANES), lambda p, j, i, st: (j, p)),
                      pl.BlockSpec((t, LANES), lambda p, j, i, st: (j, 16 + p)), rblk, rblk, qblk],
            out_specs=[pl.BlockSpec(memory_space=pl.ANY), pl.BlockSpec(memory_space=pl.ANY), kblk, kblk, kblk],
            scratch_shapes=[pltpu.VMEM((2, T, LANES), F32), pltpu.VMEM((2, t, LANES), F32), pltpu.VMEM((t, LANES), F32),
                            pltpu.VMEM((t, t), F32), pltpu.VMEM((t, t), F32), pltpu.VMEM((t, t), BF16),
                            pltpu.VMEM((t, t), BF16), pltpu.VMEM((t, LANES), BF16), pltpu.VMEM((t, LANES), F32),
                            pltpu.SemaphoreType.DMA((2,))]),
        out_shape=[bf, f32, bf, bf, f32],
        compiler_params=_cparams("arbitrary", "arbitrary", "arbitrary"),
    )(st, qa, ka, qkv, lse_row, delta_row, do)


C = HGRN_CHUNK
LEVELS = (64, 32, 16, 8, 4, 2)


def _pivot(b, B, row):
    if B == C:
        return jnp.broadcast_to(b[C // 2 - 1:C // 2, :], b.shape)
    if B >= 8:
        b3 = b.reshape(C // B, B, LANES)
        return jnp.broadcast_to(b3[:, B // 2 - 1:B // 2, :], b3.shape).reshape(C, LANES)
    if B == 4:
        y = jnp.where((row & 3) == 1, b, 0.0)
        return y + pltpu.roll(y, 1, 0) + pltpu.roll(y, 2, 0) + pltpu.roll(y, C - 1, 0)
    y = jnp.where((row & 1) == 0, b, 0.0)
    return y + pltpu.roll(y, 1, 0)


def _level_factors(bcum):
    row = lax.broadcasted_iota(jnp.int32, (C, 1), 0)
    out = []
    for B in LEVELS:
        upper = (row & (B - 1)) >= B // 2
        e = jnp.exp(-jnp.abs(bcum - _pivot(bcum, B, row)))
        out.append((B, jnp.where(upper, e, 0.0), jnp.where(upper, 0.0, e)))
    return out


def _same_block(B):
    sh = B.bit_length() - 1
    r = lax.broadcasted_iota(jnp.int32, (C, C), 0)
    c = lax.broadcasted_iota(jnp.int32, (C, C), 1)
    return (r >> sh) == (c >> sh)


def _hgrn_gates(q, fl, lb):
    sg = _sigmoid(fl)
    f = lb + (1.0 - lb) * sg
    sq = _sigmoid(q)
    return sg, f, jnp.log(f), 1.0 - f, sq, q * sq


def _cumsum_rows(x, reverse=False):
    r = lax.broadcasted_iota(jnp.int32, (C, C), 0)
    c = lax.broadcasted_iota(jnp.int32, (C, C), 1)
    tri = ((r <= c) if reverse else (r >= c)).astype(F32)
    return jnp.dot(tri, x, precision=HI, preferred_element_type=F32)


def _intra(qs, k, factors):
    r = lax.broadcasted_iota(jnp.int32, (C, C), 0)
    c = lax.broadcasted_iota(jnp.int32, (C, C), 1)
    a = jnp.where(r == c, jnp.sum(qs * k, axis=1, keepdims=True), 0.0)
    ops = []
    for B, eq, ek in factors:
        ql, kl = (qs * eq).astype(BF16), (k * ek).astype(BF16)
        al = _nt(ql, kl)
        a = a + (al if B == C else jnp.where(_same_block(B), al, 0.0))
        ops.append((ql, kl))
    return a, ops


def hgrn_fwd(proj, lb, gn, name):
    T = proj.shape[0]
    tg = min(HGRN_TG, T)
    nch = tg // C

    def body(q_ref, fl_ref, v_ref, g_ref, lb_ref, gn_ref, ao_ref, o_ref, st_ref, st_sc):
        @pl.when(pl.program_id(1) == 0)
        def _():
            st_sc[...] = jnp.zeros(st_sc.shape, F32)

        lb_v, gn_v = lb_ref[...], gn_ref[...]

        def chunk(ci, carry):
            rows = pl.ds(pl.multiple_of(ci * C, C), C)
            _, f, lf, k, _, qs = _hgrn_gates(q_ref[rows, :], fl_ref[rows, :], lb_v)
            vb = v_ref[rows, :].astype(BF16)
            gv = g_ref[rows, :]
            bcum = _cumsum_rows(lf)
            blast = bcum[C - 1:C, :]
            a, _ = _intra(qs, k, _level_factors(bcum))
            st = st_sc[...]
            st_ref[ci] = st
            o = _dot(a.astype(BF16), vb) + _nt((qs * jnp.exp(bcum)).astype(BF16), st.astype(BF16))
            st_sc[...] = st * jnp.exp(blast) + _tn(vb, (k * jnp.exp(blast - bcum)).astype(BF16))
            o_ref[rows, :] = o
            ao_ref[rows, :] = (o * _rms(o) * gn_v * (gv * _sigmoid(gv))).astype(BF16)
            return carry

        lax.fori_loop(0, nch, chunk, 0, unroll=8)

    col = lambda off: pl.BlockSpec((tg, LANES), lambda h, i: (i, off + h))
    one = pl.BlockSpec((1, LANES), lambda h, i: (0, h))
    return pl.pallas_call(
        body, name=name, grid=(8, T // tg),
        in_specs=[col(0), col(8), col(16), col(24), one, one],
        out_specs=[col(0), col(0), pl.BlockSpec((None, nch, LANES, LANES), lambda h, i: (h, i, 0, 0))],
        out_shape=[jax.ShapeDtypeStruct((T, D_MODEL), BF16), jax.ShapeDtypeStruct((T, D_MODEL), F32),
                   jax.ShapeDtypeStruct((8, T // C, LANES, LANES), F32)],
        scratch_shapes=[pltpu.VMEM((LANES, LANES), F32)],
        compiler_params=_cparams("parallel", "arbitrary"),
    )(proj, proj, proj, proj, lb, gn)


def hgrn_bwd(proj, lb, gn, o_raw, states, dao, name):
    T = proj.shape[0]
    tg = min(HGRN_TG, T)
    nch = tg // C
    n = T // tg

    def body(q_ref, fl_ref, v_ref, g_ref, lb_ref, gn_ref, o_ref, st_ref, dao_ref,
             dq_ref, dfl_ref, dv_ref, dg_ref, dlb_ref, dgn_ref, dst_sc):
        @pl.when(pl.program_id(1) == 0)
        def _():
            dst_sc[...] = jnp.zeros(dst_sc.shape, F32)
            dlb_ref[...] = jnp.zeros(dlb_ref.shape, F32)
            dgn_ref[...] = jnp.zeros(dgn_ref.shape, F32)

        lb_v, gn_v = lb_ref[...], gn_ref[...]
        r64 = lax.broadcasted_iota(jnp.int32, (C, C), 0)
        c64 = lax.broadcasted_iota(jnp.int32, (C, C), 1)
        row = lax.broadcasted_iota(jnp.int32, (C, 1), 0)

        def chunk(cr, carry):
            ci = nch - 1 - cr
            rows = pl.ds(pl.multiple_of(ci * C, C), C)
            q, fl, gv = q_ref[rows, :], fl_ref[rows, :], g_ref[rows, :]
            sg, f, lf, k, sq, qs = _hgrn_gates(q, fl, lb_v)
            vb = v_ref[rows, :].astype(BF16)
            o = o_ref[rows, :]
            ro = _rms(o)
            on = o * ro
            sgg = _sigmoid(gv)
            gate = gv * sgg
            dao_v = dao_ref[rows, :].astype(F32)
            dg_ref[rows, :] = (dao_v * on * gn_v * (sgg * (1.0 + gv * (1.0 - sgg)))).astype(BF16)
            dgn_ref[...] += jnp.sum(dao_v * on * gate, axis=0, keepdims=True)
            don = dao_v * gn_v * gate
            do = ro * (don - on * jnp.mean(don * on, axis=-1, keepdims=True))
            dob = do.astype(BF16)
            bcum = _cumsum_rows(lf)
            blast = bcum[C - 1:C, :]
            factors = _level_factors(bcum)
            a, ops = _intra(qs, k, factors)
            eb = jnp.exp(bcum)
            ekb = jnp.exp(blast - bcum)
            qb = qs * eb
            kb = k * ekb
            st = st_ref[ci]
            dst = dst_sc[...]
            dstb = dst.astype(BF16)
            da = jnp.where(r64 >= c64, _nt(dob, vb), 0.0)
            dv_ref[rows, :] = (_tn(a.astype(BF16), dob) + _nt(kb.astype(BF16), dstb)).astype(BF16)
            dqb = _dot(dob, st.astype(BF16))
            dkb = _dot(vb, dstb)
            eblast = jnp.exp(blast)
            dst_sc[...] = dst * eblast + _tn(dob, qb.astype(BF16))
            dblast = eblast * jnp.sum(dst * st, axis=0, keepdims=True) + jnp.sum(dkb * kb, axis=0, keepdims=True)
            dad = jnp.sum(jnp.where(r64 == c64, da, 0.0), axis=1, keepdims=True)
            dqs = dqb * eb + dad * k
            dk = dkb * ekb + dad * qs
            dbcum = dqb * qb - dkb * kb + jnp.where(row == C - 1, dblast, 0.0)
            for (B, eq, ek), (ql, kl) in zip(factors, ops):
                dal = (da if B == C else jnp.where(_same_block(B), da, 0.0)).astype(BF16)
                dql, dkl = _dot(dal, kl), _tn(dal, ql)
                dqs = dqs + dql * eq
                dk = dk + dkl * ek
                dbcum = dbcum + (dql * ql.astype(F32) - dkl * kl.astype(F32))
            df = _cumsum_rows(dbcum, reverse=True) / f - dk
            dfl_ref[rows, :] = (df * (1.0 - lb_v) * sg * (1.0 - sg)).astype(BF16)
            dlb_ref[...] += jnp.sum(df * (1.0 - sg), axis=0, keepdims=True)
            dq_ref[rows, :] = (dqs * (sq * (1.0 + q * (1.0 - sq)))).astype(BF16)
            return carry

        lax.fori_loop(0, nch, chunk, 0, unroll=8)

    col = lambda off: pl.BlockSpec((tg, LANES), lambda h, i: (n - 1 - i, off + h))
    one = pl.BlockSpec((1, LANES), lambda h, i: (0, h))
    big = jax.ShapeDtypeStruct((T, D_MODEL), BF16)
    small = jax.ShapeDtypeStruct((1, D_MODEL), F32)
    return pl.pallas_call(
        body, name=name, grid=(8, n),
        in_specs=[col(0), col(8), col(16), col(24), one, one, col(0),
                  pl.BlockSpec((None, nch, LANES, LANES), lambda h, i: (h, n - 1 - i, 0, 0)), col(0)],
        out_specs=[col(0), col(0), col(0), col(0), one, one],
        out_shape=[big, big, big, big, small, small],
        scratch_shapes=[pltpu.VMEM((LANES, LANES), F32)],
        compiler_params=_cparams("arbitrary", "arbitrary"),
    )(proj, proj, proj, proj, lb, gn, o_raw, states, dao)


def lower_bound_fwd(logits, name):
    def body(l_ref, s_ref):
        lv = l_ref[...]
        e = jnp.exp(lv - jnp.max(lv, axis=0, keepdims=True))
        s_ref[...] = e / jnp.sum(e, axis=0, keepdims=True)

    return pl.pallas_call(body, name=name, out_shape=jax.ShapeDtypeStruct(logits.shape, F32))(logits)


def lower_bound_bwd(sm, dlb, name):
    def body(s_ref, d_ref, o_ref):
        s = s_ref[...]
        row = lax.broadcasted_iota(jnp.int32, s.shape, 0)
        o_ref[...] = d_ref[...] * s[1:2, :] * (jnp.where(row == 1, 1.0, 0.0) - s)

    return pl.pallas_call(body, name=name, out_shape=jax.ShapeDtypeStruct(sm.shape, F32))(sm, dlb)


def _pad_rows(flat, mult):
    rows = -(-flat.shape[-1] // D_MODEL)
    rows = -(-rows // mult) * mult
    pad = rows * D_MODEL - flat.shape[-1]
    flat = jnp.pad(flat, [(0, 0)] * (flat.ndim - 1) + [(0, pad)])
    return flat.reshape(flat.shape[:-1] + (rows, D_MODEL))


def _gather_weights(w):
    direct = [n for n in SHARDED if n not in BIASES and w[n].shape[SHARD_AXIS[n]] % LANES == 0]
    packed = [n for n in SHARDED if n not in direct]
    pieces = []
    for nme in packed:
        a = w[nme]
        if nme in BIASES:
            pieces.append(lax.bitcast_convert_type(a, BF16).reshape(-1))
        else:
            pieces.append(a.astype(BF16).reshape(-1))
    flat = _pad_rows(jnp.concatenate(pieces), 16)
    out = all_gather_shards([(w[n].astype(BF16), SHARD_AXIS[n]) for n in direct] + [(flat, None)])
    full = dict(zip(direct, out[:-1]))
    got, off = out[-1].reshape(N_DEV, -1), 0
    for nme in packed:
        shp = w[nme].shape
        cnt = 1
        for s in shp:
            cnt *= s
        if nme in BIASES:
            seg = got[:, off:off + 2 * cnt].reshape((N_DEV,) + shp + (2,))
            seg = lax.bitcast_convert_type(seg, F32)
            off += 2 * cnt
        else:
            seg = got[:, off:off + cnt].reshape((N_DEV,) + shp)
            off += cnt
        full[nme] = jnp.concatenate([seg[d] for d in range(N_DEV)], axis=SHARD_AXIS[nme])
    return full


def _pieces(gfull, axis):
    shp = gfull.shape
    a = gfull.reshape(shp[:axis] + (N_DEV, shp[axis] // N_DEV) + shp[axis + 1:])
    return jnp.moveaxis(a, axis, 0).reshape(N_DEV, -1)


def kernel(x, norm_mix, norm_mlp, norm_final, w_up, w_down, swa_w_qkv, swa_b_qkv, swa_sinks, swa_w_o, hgrn_w_in, hgrn_lb_logits, hgrn_g_norm, hgrn_w_o, fox_w_in, fox_b_in, fox_w_o, loss_target, m_norm_mix, m_norm_mlp, m_norm_final, m_w_up, m_w_down, m_swa_w_qkv, m_swa_b_qkv, m_swa_sinks, m_swa_w_o, m_hgrn_w_in, m_hgrn_lb_logits, m_hgrn_g_norm, m_hgrn_w_o, m_fox_w_in, m_fox_b_in, m_fox_w_o, v_norm_mix, v_norm_mlp, v_norm_final, v_w_up, v_w_down, v_swa_w_qkv, v_swa_b_qkv, v_swa_sinks, v_swa_w_o, v_hgrn_w_in, v_hgrn_lb_logits, v_hgrn_g_norm, v_hgrn_w_o, v_fox_w_in, v_fox_b_in, v_fox_w_o):
    w = dict(norm_mix=norm_mix, norm_mlp=norm_mlp, norm_final=norm_final, w_up=w_up, w_down=w_down,
             swa_w_qkv=swa_w_qkv, swa_b_qkv=swa_b_qkv, swa_sinks=swa_sinks, swa_w_o=swa_w_o, hgrn_w_in=hgrn_w_in,
             hgrn_lb_logits=hgrn_lb_logits, hgrn_g_norm=hgrn_g_norm, hgrn_w_o=hgrn_w_o, fox_w_in=fox_w_in,
             fox_b_in=fox_b_in, fox_w_o=fox_w_o)
    mom = dict(norm_mix=m_norm_mix, norm_mlp=m_norm_mlp, norm_final=m_norm_final, w_up=m_w_up, w_down=m_w_down,
               swa_w_qkv=m_swa_w_qkv, swa_b_qkv=m_swa_b_qkv, swa_sinks=m_swa_sinks, swa_w_o=m_swa_w_o,
               hgrn_w_in=m_hgrn_w_in, hgrn_lb_logits=m_hgrn_lb_logits, hgrn_g_norm=m_hgrn_g_norm, hgrn_w_o=m_hgrn_w_o,
               fox_w_in=m_fox_w_in, fox_b_in=m_fox_b_in, fox_w_o=m_fox_w_o)
    var = dict(norm_mix=v_norm_mix, norm_mlp=v_norm_mlp, norm_final=v_norm_final, w_up=v_w_up, w_down=v_w_down,
               swa_w_qkv=v_swa_w_qkv, swa_b_qkv=v_swa_b_qkv, swa_sinks=v_swa_sinks, swa_w_o=v_swa_w_o,
               hgrn_w_in=v_hgrn_w_in, hgrn_lb_logits=v_hgrn_lb_logits, hgrn_g_norm=v_hgrn_g_norm, hgrn_w_o=v_hgrn_w_o,
               fox_w_in=v_fox_w_in, fox_b_in=v_fox_b_in, fox_w_o=v_fox_w_o)
    T = x.shape[1]
    x0 = x[0]
    tgt = loss_target[0]
    W = _gather_weights(w)
    zeros_b = jnp.zeros((1, 4 * D_MODEL), F32)

    def swa_layer(xin, i, j):
        qkv = norm_matmul(xin, norm_mix[i:i + 1], W['swa_w_qkv'][j], W['swa_b_qkv'][j:j + 1], BF16, f"swa_qkv_L{i}")
        dup = lambda a: jnp.broadcast_to(a.reshape(T, 4, 1, 64), (T, 4, 2, 64)).reshape(T, 4 * LANES)
        kdup, vdup = dup(qkv[:, 1024:1280]), dup(qkv[:, 1280:1536])
        sk = jnp.broadcast_to(jnp.pad(swa_sinks[j].reshape(4, 4), ((0, 0), (0, 4)))[:, :, None], (4, 8, LANES))
        ao, lse = swa_fwd(qkv, kdup, vdup, sk, f"swa_fwd_L{i}")
        xmid = matmul(ao, W['swa_w_o'][j], F32, f"swa_out_L{i}", res=xin)
        return xmid, (qkv, kdup, vdup, sk, ao, lse)

    def swa_layer_bwd(xin, saved, dmid, i, j, grads):
        qkv, kdup, vdup, sk, ao, lse = saved
        dao = matmul(dmid, W['swa_w_o'][j].T, BF16, f"swa_dout_L{i}")
        grads['swa_w_o'][j] = tn_matmul(ao, dmid, f"swa_dwo_L{i}")
        dq, dk, dv, dsk = swa_bwd(qkv, kdup, vdup, sk, ao, lse, dao, f"swa_bwd_L{i}")
        wt = W['swa_w_qkv'][j].T
        spread = lambda a: jnp.pad(a.reshape(4, 64, D_MODEL), ((0, 0), (0, 64), (0, 0))).reshape(4 * LANES, D_MODEL)
        gather = lambda a: a.reshape(a.shape[0], 4, LANES)[:, :, :64].reshape(a.shape[0], 256)
        dx, h, dg = proj_bwd(xin, norm_mix[i:i + 1], dmid,
                             [(dq, wt[:1024]), (dk, spread(wt[1024:1280])), (dv, spread(wt[1280:]))], f"swa_din_L{i}")
        gq, bq = tn_matmul(h, dq, f"swa_dwq_L{i}", colsum=True)
        gk, bk = tn_matmul(h, dk, f"swa_dwk_L{i}", colsum=True)
        gv, bv = tn_matmul(h, dv, f"swa_dwv_L{i}", colsum=True)
        grads['swa_w_qkv'][j] = jnp.concatenate([gq, gather(gk), gather(gv)], axis=1)
        grads['swa_b_qkv'][j] = jnp.concatenate([bq, gather(bk), gather(bv)], axis=1)[0]
        grads['swa_sinks'][j] = dsk[:, :4, 0].reshape(16)
        grads['norm_mix'][i] = dg[0]
        return dx

    lb_soft = lower_bound_fwd(hgrn_lb_logits, "hgrn_lb_fwd")
    lb = lb_soft[1:2]

    def hgrn_layer(xin, i, j):
        proj = norm_matmul(xin, norm_mix[i:i + 1], W['hgrn_w_in'][j], zeros_b, F32, f"hgrn_in_L{i}")
        ao, o_raw, states = hgrn_fwd(proj, lb, hgrn_g_norm[j:j + 1], f"hgrn_fwd_L{i}")
        xmid = matmul(ao, W['hgrn_w_o'][j], F32, f"hgrn_out_L{i}", res=xin)
        return xmid, (proj, ao, o_raw, states)

    def hgrn_layer_bwd(xin, saved, dmid, i, j, grads):
        proj, ao, o_raw, states = saved
        dao = matmul(dmid, W['hgrn_w_o'][j].T, BF16, f"hgrn_dout_L{i}")
        grads['hgrn_w_o'][j] = tn_matmul(ao, dmid, f"hgrn_dwo_L{i}")
        dq, dfl, dv, dgt, dlb, dgn = hgrn_bwd(proj, lb, hgrn_g_norm[j:j + 1], o_raw, states, dao, f"hgrn_bwd_L{i}")
        wt = W['hgrn_w_in'][j].T
        parts = [dq, dfl, dv, dgt]
        dx, h, dg = proj_bwd(xin, norm_mix[i:i + 1], dmid,
                             [(d, wt[n * D_MODEL:(n + 1) * D_MODEL]) for n, d in enumerate(parts)], f"hgrn_din_L{i}")
        grads['hgrn_w_in'][j] = jnp.concatenate(
            [tn_matmul(h, d, f"hgrn_dwin{n}_L{i}") for n, d in enumerate(parts)], axis=1)
        grads['hgrn_g_norm'][j] = dgn[0]
        grads['hgrn_lb_logits'] = lower_bound_bwd(lb_soft, dlb, "hgrn_lb_bwd")
        grads['norm_mix'][i] = dg[0]
        return dx

    def fox_layer(xin, i, j):
        w_in = W['fox_w_in'][j]
        b_in = W['fox_b_in'][j:j + 1]
        qkv = norm_matmul(xin, norm_mix[i:i + 1], w_in[:, :3072], b_in[:, :3072], BF16, f"fox_qkv_L{i}")
        wf = jnp.pad(w_in[:, 3072:], ((0, 0), (0, LANES - 16)))
        bf = jnp.pad(b_in[:, 3072:], ((0, 0), (0, LANES - 16)))
        fl = norm_matmul(xin, norm_mix[i:i + 1], wf, bf, F32, f"fox_f_L{i}")
        qa, ka, bounds = fox_gate_fwd(fl, qkv, f"fox_gate_L{i}")
        st = bounds[:, :4, :16].reshape(bounds.shape[0], 64)
        ao, lse, lmin = fox_fwd(st, qa, ka, qkv, f"fox_fwd_L{i}")
        st = jnp.concatenate([st, lmin[:, :, :2, 0].reshape(lmin.shape[0], 16)], axis=1)
        xmid = matmul(ao, W['fox_w_o'][j], F32, f"fox_out_L{i}", res=xin)
        return xmid, (qkv, fl, qa, ka, ao, lse, wf, st)

    def fox_layer_bwd(xin, saved, dmid, i, j, grads):
        qkv, fl, qa, ka, ao, lse, wf, st = saved
        dao = matmul(dmid, W['fox_w_o'][j].T, BF16, f"fox_dout_L{i}")
        grads['fox_w_o'][j] = tn_matmul(ao, dmid, f"fox_dwo_L{i}")
        delta = fox_delta(dao, ao, f"fox_delta_L{i}")
        dq, aux_q, dk, dv, aux_k = fox_bwd(st, qa, ka, qkv, lse[:, ::64].T.reshape(8, 2, T),
                                           delta[:, :16].T.reshape(8, 2, T), dao, f"fox_bwd_L{i}")
        dcp = jnp.pad(aux_q[:, ::64] - aux_k[:, 3::64], ((0, 0), (0, LANES - 16)))
        dfl = fox_gate_bwd(fl, dcp, f"fox_dgate_L{i}")
        wt = W['fox_w_in'][j][:, :3072].T
        parts = [dq, dk, dv]
        dx, h, dg = proj_bwd(xin, norm_mix[i:i + 1], dmid,
                             [(d, wt[n * D_MODEL:(n + 1) * D_MODEL]) for n, d in enumerate(parts)] + [(dfl, wf.T)],
                             f"fox_din_L{i}")
        gw = [tn_matmul(h, d, f"fox_dw{n}_L{i}", colsum=True) for n, d in enumerate(parts + [dfl])]
        grads['fox_w_in'][j] = jnp.concatenate([g for g, _ in gw[:3]] + [gw[3][0][:, :16]], axis=1)
        grads['fox_b_in'][j] = jnp.concatenate([b for _, b in gw[:3]] + [gw[3][1][:, :16]], axis=1)[0]
        grads['norm_mix'][i] = dg[0]
        return dx

    mixers = [(swa_layer, swa_layer_bwd), (hgrn_layer, hgrn_layer_bwd), (fox_layer, fox_layer_bwd)]

    xs, mids, saves = [x0], [], []
    for i in range(DEPTH):
        xmid, saved = mixers[i % 3][0](xs[-1], i, i // 3)
        mids.append(xmid)
        saves.append(saved)
        xs.append(mlp_fwd(xmid, norm_mlp[i:i + 1], W['w_up'][i], W['w_down'][i], f"mlp_fwd_L{i}"))

    grads = {n: [None] * w[n].shape[0] for n in WEIGHTS if n not in ('norm_final', 'hgrn_lb_logits')}
    loss_part, dx, dgf = final_loss(xs[-1], norm_final.reshape(1, D_MODEL), tgt, "final_loss")
    grads['norm_final'] = dgf[0]
    for i in reversed(range(DEPTH)):
        dmid, h, a, du, dg = mlp_bwd(mids[i], norm_mlp[i:i + 1], W['w_up'][i], W['w_up'][i].T, W['w_down'][i].T, dx,
                                     f"mlp_bwd_L{i}")
        grads['w_up'][i] = tn_matmul(h, du, f"mlp_dwup_L{i}")
        grads['w_down'][i] = tn_matmul(a, dx, f"mlp_dwdown_L{i}")
        grads['norm_mlp'][i] = dg[0]
        dx = mixers[i % 3][1](xs[i], saves[i], dmid, i, i // 3, grads)
    gfull = {n: (g if not isinstance(g, list) else jnp.stack(g)) for n, g in grads.items()}

    mats = [n for n in SHARDED if n not in BIASES]
    view = lambda a: a.reshape(-1, a.shape[-1])
    sends = [_pieces(gfull[n], SHARD_AXIS[n]).astype(BF16).reshape((N_DEV,) + view(w[n]).shape) for n in mats]
    common = jnp.concatenate([gfull[n].reshape(-1) for n in REPLICATED] + [loss_part[0, 0:1]])
    small = jnp.concatenate([jnp.broadcast_to(common[None], (N_DEV, common.shape[0]))]
                            + [_pieces(gfull[n], SHARD_AXIS[n]) for n in BIASES], axis=1)
    recvs = all_to_all_rows(sends + [_pad_rows(small, 16)])
    tail = lambda vals: _pad_rows(jnp.concatenate([vals[n].reshape(-1) for n in REPLICATED] + [jnp.zeros((1,), F32)]
                                                  + [vals[n].reshape(-1) for n in BIASES]), 16)
    res = [{}, {}, {}, {}]
    for nme, rv in zip(mats, recvs):
        outs = reduce_adamw(rv, view(w[nme]), view(mom[nme]), view(var[nme]), f"adamw_{nme}")
        for o, r in zip(outs, res):
            r[nme] = o.reshape(w[nme].shape)
    outs = reduce_adamw(recvs[-1], tail(w), tail(mom), tail(var), "adamw_small")
    off = 0
    for nme in REPLICATED + ['loss'] + list(BIASES):
        cnt = 1 if nme == 'loss' else w[nme].size
        if nme == 'loss':
            loss = outs[0].reshape(-1)[off]
        else:
            for o, r in zip(outs, res):
                r[nme] = o.reshape(-1)[off:off + cnt].reshape(w[nme].shape)
        off += cnt
    return (loss, dx[None], *[res[0][n] for n in WEIGHTS], *[res[1][n] for n in WEIGHTS],
            *[res[2][n] for n in WEIGHTS], *[res[3][n] for n in WEIGHTS])
```

```python
import functools

import jax
import jax.numpy as jnp
from jax import lax
from jax.experimental import pallas as pl
from jax.experimental.pallas import tpu as pltpu

F32 = jnp.float32
BF16 = jnp.bfloat16
HI = lax.Precision.HIGHEST

N_DEV = 8
D_MODEL = 1024
DEPTH = 4
EPS = 1e-6
SWA_WINDOW = 128
HGRN_CHUNK = 64
LANES = 128
VMEM_LIMIT = 56 << 20

ADAM_LR, ADAM_B1, ADAM_B2, ADAM_EPS, ADAM_WD, ADAM_STEP = 0.001, 0.9, 0.999, 1e-08, 0.01, 10

TM = 512
TF = 512
TK = 512
FOX_T = 1024
SWA_TQ = 512
HGRN_TG = 512
SCAN_T = 256

WEIGHTS = ['norm_mix', 'norm_mlp', 'norm_final', 'w_up', 'w_down', 'swa_w_qkv', 'swa_b_qkv', 'swa_sinks', 'swa_w_o',
           'hgrn_w_in', 'hgrn_lb_logits', 'hgrn_g_norm', 'hgrn_w_o', 'fox_w_in', 'fox_b_in', 'fox_w_o']
SHARD_AXIS = {'norm_mix': None, 'norm_mlp': None, 'norm_final': None, 'w_up': 2, 'w_down': 1, 'swa_w_qkv': 2,
              'swa_b_qkv': 1, 'swa_sinks': None, 'swa_w_o': 1, 'hgrn_w_in': 2, 'hgrn_lb_logits': None,
              'hgrn_g_norm': None, 'hgrn_w_o': 1, 'fox_w_in': 2, 'fox_b_in': 1, 'fox_w_o': 1}
SHARDED = [n for n in WEIGHTS if SHARD_AXIS[n] is not None]
REPLICATED = [n for n in WEIGHTS if SHARD_AXIS[n] is None]
BIASES = ('swa_b_qkv', 'fox_b_in')


def _cparams(*sem):
    return pltpu.CompilerParams(dimension_semantics=sem, vmem_limit_bytes=VMEM_LIMIT)


def _nt(a, b):
    return lax.dot_general(a, b, (((1,), (1,)), ((), ())), preferred_element_type=F32)


def _tn(a, b):
    return lax.dot_general(a, b, (((0,), (0,)), ((), ())), preferred_element_type=F32)


def _dot(a, b):
    return jnp.dot(a, b, preferred_element_type=F32)


def _sigmoid(x):
    return 1.0 / (1.0 + jnp.exp(-x))


def _rms(xv):
    return lax.rsqrt(jnp.mean(xv * xv, axis=-1, keepdims=True) + EPS)


def _rms_bwd(xv, g, dh):
    r = _rms(xv)
    xhat = xv * r
    dhg = dh * g
    dx = r * (dhg - xhat * jnp.mean(dhg * xhat, axis=-1, keepdims=True))
    return dx, jnp.sum(dh * xhat, axis=0, keepdims=True)


def _my_id():
    return lax.axis_index("x"), lax.axis_index("y"), lax.axis_index("c")


def _peer(x, y, c, k):
    return (lax.rem(x + ((k >> 2) & 1), 2), lax.rem(y + ((k >> 1) & 1), 2), lax.rem(c + (k & 1), 2))


def all_gather_shards(shards):
    n = len(shards)

    def place(o_ref, local, axis, dev):
        if axis is None:
            return o_ref.at[dev]
        idx = [slice(None)] * local.ndim
        idx[axis] = pl.ds(pl.multiple_of(dev * local.shape[axis], local.shape[axis]), local.shape[axis])
        return o_ref.at[tuple(idx)]

    def body(*refs):
        x_refs, o_refs = refs[:n], refs[n:2 * n]
        send_sems, recv_sems, loc_sems = refs[2 * n:]
        x, y, c = _my_id()
        sibling = (x, y, 1 - c)
        chips = [(1 - x, y), (x, 1 - y), (1 - x, 1 - y)]
        dev = lambda px, py, pc: 4 * px + 2 * py + pc

        def copy(a, k, block, to, src=None):
            local, axis = shards[a]
            spot = place(o_refs[a], local, axis, dev(*block))
            return pltpu.make_async_remote_copy(
                src_ref=spot if src is None else src, dst_ref=spot, send_sem=send_sems.at[a * 7 + k],
                recv_sem=recv_sems.at[a * 7 + k], device_id=to, device_id_type=pl.DeviceIdType.MESH)

        mines, sent = [], []
        for a in range(n):
            local, axis = shards[a]
            mine = pltpu.make_async_copy(x_refs[a], place(o_refs[a], local, axis, dev(x, y, c)), loc_sems.at[a])
            mine.start()
            mines.append(mine)
            for k, to in enumerate([sibling] + [(*chip, c) for chip in chips]):
                cp = copy(a, k, (x, y, c), to, src=x_refs[a])
                cp.start()
                sent.append(cp)
        passed = []
        for j, chip in enumerate(chips):
            for a in range(n):
                copy(a, 1 + j, (*chip, c), (x, y, c)).wait_recv()
                cp = copy(a, 4 + j, (*chip, c), sibling)
                cp.start()
                passed.append(cp)
        for a in range(n):
            copy(a, 0, sibling, (x, y, c)).wait_recv()
            for j, chip in enumerate(chips):
                copy(a, 4 + j, (*chip, 1 - c), (x, y, c)).wait_recv()
        for cp in sent + passed:
            cp.wait_send()
        for mine in mines:
            mine.wait()

    def full_shape(local, axis):
        if axis is None:
            return (N_DEV,) + local.shape
        return local.shape[:axis] + (N_DEV * local.shape[axis],) + local.shape[axis + 1:]

    hbm = pl.BlockSpec(memory_space=pl.ANY)
    return pl.pallas_call(
        body, name="all_gather_weights",
        out_shape=[jax.ShapeDtypeStruct(full_shape(l, ax), l.dtype) for l, ax in shards],
        in_specs=[hbm] * n, out_specs=[hbm] * n,
        scratch_shapes=[pltpu.SemaphoreType.DMA((n * (N_DEV - 1),)), pltpu.SemaphoreType.DMA((n * (N_DEV - 1),)),
                        pltpu.SemaphoreType.DMA((n,))],
    )(*[l for l, _ in shards])


def all_to_all_rows(sends):
    n = len(sends)

    def body(*refs):
        s_refs, r_refs = refs[:n], refs[n:2 * n]
        send_sems, recv_sems, loc_sems = refs[2 * n:]
        x, y, c = _my_id()
        me = 4 * x + 2 * y + c
        copies = []
        for a, (s_ref, r_ref) in enumerate(zip(s_refs, r_refs)):
            mine = pltpu.make_async_copy(s_ref.at[me], r_ref.at[me], loc_sems.at[a])
            mine.start()
            copies.append(mine)
            for k in range(1, N_DEV):
                px, py, pc = _peer(x, y, c, k)
                sem = a * (N_DEV - 1) + k - 1
                cp = pltpu.make_async_remote_copy(
                    src_ref=s_ref.at[4 * px + 2 * py + pc], dst_ref=r_ref.at[me],
                    send_sem=send_sems.at[sem], recv_sem=recv_sems.at[sem],
                    device_id=(px, py, pc), device_id_type=pl.DeviceIdType.MESH)
                cp.start()
                copies.append(cp)
        for cp in copies:
            cp.wait()

    hbm = pl.BlockSpec(memory_space=pl.ANY)
    return pl.pallas_call(
        body, name="all_to_all_grads",
        out_shape=[jax.ShapeDtypeStruct(s.shape, s.dtype) for s in sends],
        in_specs=[hbm] * n, out_specs=[hbm] * n,
        scratch_shapes=[pltpu.SemaphoreType.DMA((n * (N_DEV - 1),)), pltpu.SemaphoreType.DMA((n * (N_DEV - 1),)),
                        pltpu.SemaphoreType.DMA((n,))],
    )(*sends)


def reduce_adamw(recv, w, m, v, name):
    R, C = w.shape
    tr = max(t for t in range(16, (1 << 18) // C + 1, 16) if R % t == 0)
    c1 = 1.0 / (1.0 - ADAM_B1 ** ADAM_STEP)
    c2 = 1.0 / (1.0 - ADAM_B2 ** ADAM_STEP)

    def body(r_ref, w_ref, m_ref, v_ref, g_ref, d_ref, nm_ref, nv_ref):
        g = r_ref[0].astype(F32)
        for s in range(1, N_DEV):
            g = g + r_ref[s].astype(F32)
        m2 = ADAM_B1 * m_ref[...] + (1.0 - ADAM_B1) * g
        v2 = ADAM_B2 * v_ref[...] + (1.0 - ADAM_B2) * (g * g)
        g_ref[...] = g
        nm_ref[...] = m2
        nv_ref[...] = v2
        d_ref[...] = -ADAM_LR * ((m2 * c1) / (jnp.sqrt(v2 * c2) + ADAM_EPS) + ADAM_WD * w_ref[...])

    row = pl.BlockSpec((tr, C), lambda i: (i, 0))
    shp = jax.ShapeDtypeStruct((R, C), F32)
    return pl.pallas_call(
        body, name=name, grid=(R // tr,),
        in_specs=[pl.BlockSpec((N_DEV, tr, C), lambda i: (0, i, 0)), row, row, row],
        out_specs=[row, row, row, row], out_shape=[shp, shp, shp, shp],
        compiler_params=_cparams("parallel"),
    )(recv, w, m, v)


def norm_matmul(x, g, w, b, out_dtype, name):
    T, N = x.shape[0], w.shape[1]
    tm, tn = min(TM, T), min(512, N)

    def body(x_ref, g_ref, w_ref, b_ref, o_ref, h_sc):
        @pl.when(pl.program_id(1) == 0)
        def _():
            xv = x_ref[...]
            h_sc[...] = (xv * _rms(xv) * g_ref[...]).astype(BF16)
        o_ref[...] = (_dot(h_sc[...], w_ref[...]) + b_ref[...]).astype(o_ref.dtype)

    return pl.pallas_call(
        body, name=name, grid=(T // tm, N // tn),
        in_specs=[pl.BlockSpec((tm, D_MODEL), lambda i, j: (i, 0)), pl.BlockSpec((1, D_MODEL), lambda i, j: (0, 0)),
                  pl.BlockSpec((D_MODEL, tn), lambda i, j: (0, j)), pl.BlockSpec((1, tn), lambda i, j: (0, j))],
        out_specs=pl.BlockSpec((tm, tn), lambda i, j: (i, j)),
        out_shape=jax.ShapeDtypeStruct((T, N), out_dtype),
        scratch_shapes=[pltpu.VMEM((tm, D_MODEL), BF16)],
        compiler_params=_cparams("parallel", "arbitrary"),
    )(x, g, w, b)


def matmul(a, w, out_dtype, name, res=None):
    T, K = a.shape
    N = w.shape[1]
    tm = min(TM, T)

    def body(*refs):
        if res is None:
            a_ref, w_ref, o_ref = refs
            acc = _dot(a_ref[...].astype(BF16), w_ref[...])
        else:
            a_ref, w_ref, r_ref, o_ref = refs
            acc = r_ref[...] + _dot(a_ref[...].astype(BF16), w_ref[...])
        o_ref[...] = acc.astype(o_ref.dtype)

    in_specs = [pl.BlockSpec((tm, K), lambda i: (i, 0)), pl.BlockSpec((K, N), lambda i: (0, 0))]
    ops = [a, w]
    if res is not None:
        in_specs.append(pl.BlockSpec((tm, N), lambda i: (i, 0)))
        ops.append(res)
    return pl.pallas_call(
        body, name=name, grid=(T // tm,), in_specs=in_specs,
        out_specs=pl.BlockSpec((tm, N), lambda i: (i, 0)),
        out_shape=jax.ShapeDtypeStruct((T, N), out_dtype),
        compiler_params=_cparams("parallel"),
    )(*ops)


def tn_matmul(a, b, name, colsum=False):
    T, M = a.shape
    N = b.shape[1]
    tk = min(TK, T)
    tmm = min(1024, M)
    tn = N if N <= 1024 else (1024 if N % 1024 == 0 else N)

    def body(a_ref, b_ref, o_ref, *rest):
        k = pl.program_id(2)
        bv = b_ref[...]

        @pl.when(k == 0)
        def _():
            o_ref[...] = jnp.zeros(o_ref.shape, F32)
            if colsum:
                rest[0][...] = jnp.zeros(rest[0].shape, F32)

        o_ref[...] += _tn(a_ref[...].astype(BF16), bv.astype(BF16))
        if colsum:
            rest[0][...] += jnp.sum(bv.astype(F32), axis=0, keepdims=True)

    out_specs = [pl.BlockSpec((tmm, tn), lambda i, j, k: (i, j))]
    out_shape = [jax.ShapeDtypeStruct((M, N), F32)]
    if colsum:
        assert M == tmm
        out_specs.append(pl.BlockSpec((1, tn), lambda i, j, k: (0, j)))
        out_shape.append(jax.ShapeDtypeStruct((1, N), F32))
    out = pl.pallas_call(
        body, name=name, grid=(M // tmm, N // tn, T // tk),
        in_specs=[pl.BlockSpec((tk, tmm), lambda i, j, k: (k, i)), pl.BlockSpec((tk, tn), lambda i, j, k: (k, j))],
        out_specs=out_specs, out_shape=out_shape,
        compiler_params=_cparams("parallel", "parallel", "arbitrary"),
    )(a, b)
    return out if colsum else out[0]


def mlp_fwd(x, g, w_up, w_down, name):
    T, F = x.shape[0], w_up.shape[1]
    tm, tf = min(TM, T), min(TF, F)
    nf = F // tf

    def body(x_ref, g_ref, wu_ref, wd_ref, o_ref, h_sc, acc_sc):
        f = pl.program_id(1)

        @pl.when(f == 0)
        def _():
            xv = x_ref[...]
            h_sc[...] = (xv * _rms(xv) * g_ref[...]).astype(BF16)
            acc_sc[...] = xv

        u = jnp.maximum(_dot(h_sc[...], wu_ref[...]), 0.0)
        acc_sc[...] += _dot((u * u).astype(BF16), wd_ref[...])

        @pl.when(f == nf - 1)
        def _():
            o_ref[...] = acc_sc[...]

    return pl.pallas_call(
        body, name=name, grid=(T // tm, nf),
        in_specs=[pl.BlockSpec((tm, D_MODEL), lambda i, f: (i, 0)), pl.BlockSpec((1, D_MODEL), lambda i, f: (0, 0)),
                  pl.BlockSpec((D_MODEL, tf), lambda i, f: (0, f)), pl.BlockSpec((tf, D_MODEL), lambda i, f: (f, 0))],
        out_specs=pl.BlockSpec((tm, D_MODEL), lambda i, f: (i, 0)),
        out_shape=jax.ShapeDtypeStruct((T, D_MODEL), F32),
        scratch_shapes=[pltpu.VMEM((tm, D_MODEL), BF16), pltpu.VMEM((tm, D_MODEL), F32)],
        compiler_params=_cparams("parallel", "arbitrary"),
    )(x, g, w_up, w_down)


def mlp_bwd(x, g, w_up, w_up_t, w_down_t, dy, name):
    T, F = x.shape[0], w_up.shape[1]
    tm, tf = min(TM, T), min(TF, F)
    nf = F // tf

    def body(x_ref, g_ref, wu_ref, wut_ref, wdt_ref, dy_ref, dx_ref, h_ref, a_ref, du_ref, dg_ref, h_sc, dyb_sc, dh_sc):
        i, f = pl.program_id(0), pl.program_id(1)

        @pl.when(f == 0)
        def _():
            xv = x_ref[...]
            h = (xv * _rms(xv) * g_ref[...]).astype(BF16)
            h_sc[...] = h
            h_ref[...] = h
            dyb_sc[...] = dy_ref[...].astype(BF16)
            dh_sc[...] = jnp.zeros(dh_sc.shape, F32)

        @pl.when((i == 0) & (f == 0))
        def _():
            dg_ref[...] = jnp.zeros(dg_ref.shape, F32)

        u = jnp.maximum(_dot(h_sc[...], wu_ref[...]), 0.0)
        a_ref[...] = (u * u).astype(BF16)
        du = (_dot(dyb_sc[...], wdt_ref[...]) * (2.0 * u)).astype(BF16)
        du_ref[...] = du
        dh_sc[...] += _dot(du, wut_ref[...])

        @pl.when(f == nf - 1)
        def _():
            dx, dg = _rms_bwd(x_ref[...], g_ref[...], dh_sc[...])
            dx_ref[...] = dy_ref[...] + dx
            dg_ref[...] += dg

    row = pl.BlockSpec((tm, D_MODEL), lambda i, f: (i, 0))
    hid = pl.BlockSpec((tm, tf), lambda i, f: (i, f))
    return pl.pallas_call(
        body, name=name, grid=(T // tm, nf),
        in_specs=[row, pl.BlockSpec((1, D_MODEL), lambda i, f: (0, 0)),
                  pl.BlockSpec((D_MODEL, tf), lambda i, f: (0, f)), pl.BlockSpec((tf, D_MODEL), lambda i, f: (f, 0)),
                  pl.BlockSpec((D_MODEL, tf), lambda i, f: (0, f)), row],
        out_specs=[row, row, hid, hid, pl.BlockSpec((1, D_MODEL), lambda i, f: (0, 0))],
        out_shape=[jax.ShapeDtypeStruct((T, D_MODEL), F32), jax.ShapeDtypeStruct((T, D_MODEL), BF16),
                   jax.ShapeDtypeStruct((T, F), BF16), jax.ShapeDtypeStruct((T, F), BF16),
                   jax.ShapeDtypeStruct((1, D_MODEL), F32)],
        scratch_shapes=[pltpu.VMEM((tm, D_MODEL), BF16), pltpu.VMEM((tm, D_MODEL), BF16),
                        pltpu.VMEM((tm, D_MODEL), F32)],
        compiler_params=_cparams("arbitrary", "arbitrary"),
    )(x, g, w_up, w_up_t, w_down_t, dy)


def proj_bwd(x, g, dres, parts, name):
    T = x.shape[0]
    tm = min(TM, T)
    n = len(parts)

    def body(*refs):
        x_ref, g_ref, dr_ref = refs[:3]
        da_refs, wt_refs = refs[3:3 + n], refs[3 + n:3 + 2 * n]
        dx_ref, h_ref, dg_ref = refs[3 + 2 * n:]

        @pl.when(pl.program_id(0) == 0)
        def _():
            dg_ref[...] = jnp.zeros(dg_ref.shape, F32)

        xv = x_ref[...]
        dh = _dot(da_refs[0][...].astype(BF16), wt_refs[0][...])
        for a_ref, w_ref in zip(da_refs[1:], wt_refs[1:]):
            dh = dh + _dot(a_ref[...].astype(BF16), w_ref[...])
        h_ref[...] = (xv * _rms(xv) * g_ref[...]).astype(BF16)
        dx, dg = _rms_bwd(xv, g_ref[...], dh)
        dx_ref[...] = dr_ref[...] + dx
        dg_ref[...] += dg

    row = pl.BlockSpec((tm, D_MODEL), lambda i: (i, 0))
    one = pl.BlockSpec((1, D_MODEL), lambda i: (0, 0))
    in_specs = [row, one, row]
    in_specs += [pl.BlockSpec((tm, da.shape[1]), lambda i: (i, 0)) for da, _ in parts]
    in_specs += [pl.BlockSpec(wt.shape, lambda i: (0, 0)) for _, wt in parts]
    return pl.pallas_call(
        body, name=name, grid=(T // tm,), in_specs=in_specs,
        out_specs=[row, row, one],
        out_shape=[jax.ShapeDtypeStruct((T, D_MODEL), F32), jax.ShapeDtypeStruct((T, D_MODEL), BF16),
                   jax.ShapeDtypeStruct((1, D_MODEL), F32)],
        compiler_params=_cparams("arbitrary"),
    )(x, g, dres, *[da for da, _ in parts], *[wt for _, wt in parts])


def final_loss(x, g, tgt, name):
    T = x.shape[0]
    tm = min(TM, T)

    def body(x_ref, g_ref, t_ref, l_ref, dx_ref, dg_ref):
        @pl.when(pl.program_id(0) == 0)
        def _():
            l_ref[...] = jnp.zeros(l_ref.shape, F32)
            dg_ref[...] = jnp.zeros(dg_ref.shape, F32)

        xv = x_ref[...]
        gv = g_ref[...]
        err = xv * _rms(xv) * gv - t_ref[...]
        l_ref[...] += 0.5 * jnp.sum(jnp.mean(err * err, axis=-1, keepdims=True), axis=0, keepdims=True)
        dx, dg = _rms_bwd(xv, gv, err * (1.0 / D_MODEL))
        dx_ref[...] = dx
        dg_ref[...] += dg

    row = pl.BlockSpec((tm, D_MODEL), lambda i: (i, 0))
    one = pl.BlockSpec((1, D_MODEL), lambda i: (0, 0))
    return pl.pallas_call(
        body, name=name, grid=(T // tm,), in_specs=[row, one, row],
        out_specs=[pl.BlockSpec((8, LANES), lambda i: (0, 0)), row, one],
        out_shape=[jax.ShapeDtypeStruct((8, LANES), F32), jax.ShapeDtypeStruct((T, D_MODEL), F32),
                   jax.ShapeDtypeStruct((1, D_MODEL), F32)],
        compiler_params=_cparams("arbitrary"),
    )(x, g, tgt)


def _swa_specs(tq):
    r = tq // SWA_WINDOW
    cur = lambda ix: pl.BlockSpec((tq, LANES), lambda kv, i: (ix(i), kv))
    prev = lambda ix: pl.BlockSpec((SWA_WINDOW, LANES), lambda kv, i: (jnp.maximum(ix(i) * r - 1, 0), kv))
    return cur, prev


W2 = 2 * SWA_WINDOW
SWA_RB = 32


def _swa_visible(tile):
    r = lax.broadcasted_iota(jnp.int32, (SWA_WINDOW, W2), 0)
    c = lax.broadcasted_iota(jnp.int32, (SWA_WINDOW, W2), 1)
    inside = (c > r) & (c <= r + SWA_WINDOW)
    return inside & ((c >= SWA_WINDOW) | (tile > 0)), inside


def swa_fwd(qkv, kdup, vdup, sinks_b, name):
    T = qkv.shape[0]
    tq = min(SWA_TQ, T)
    nsub = tq // SWA_WINDOW
    cur, prev = _swa_specs(tq)
    ident = lambda i: i

    def body(q_ref, kc_ref, kp_ref, vc_ref, vp_ref, sk_ref, o_ref, lse_ref, s_sc, e_sc):
        i = pl.program_id(1)
        kcat = jnp.concatenate([kp_ref[...], kc_ref[...]], axis=0)
        vcat = jnp.concatenate([vp_ref[...], vc_ref[...]], axis=0)
        vis_first, vis_in = _swa_visible(i)
        lane = lax.broadcasted_iota(jnp.int32, (1, LANES), 1)
        lse_all = jnp.zeros((tq, LANES), F32)
        for pp in range(2):
            q2 = q_ref[:, pp * LANES:(pp + 1) * LANES]
            outs = []
            for hf in range(2):
                g = 2 * pp + hf
                qm = jnp.where(_half(lane, hf), q2, jnp.zeros_like(q2))
                for nb in range(nsub):
                    rows = slice(nb * SWA_WINDOW, (nb + 1) * SWA_WINDOW)
                    s = _nt(qm[rows], kcat[nb * SWA_WINDOW:nb * SWA_WINDOW + W2]) * 0.125
                    s_sc[rows, :] = jnp.where(vis_first if nb == 0 else vis_in, s, -1e30)
                sk = sk_ref[g:g + 1, 0:1]
                m = jnp.maximum(jnp.max(s_sc[...], axis=1, keepdims=True), sk)
                m_rep = jnp.broadcast_to(m, (tq, LANES))
                parts = []
                for r0 in range(0, tq, SWA_RB):
                    rs = slice(r0, r0 + SWA_RB)
                    e0 = jnp.exp(s_sc[rs, 0:LANES] - m_rep[rs])
                    e1 = jnp.exp(s_sc[rs, LANES:W2] - m_rep[rs])
                    e_sc[rs, :] = jnp.concatenate([e0.astype(BF16), e1.astype(BF16)], axis=1)
                    parts.append(e0 + e1)
                den = jnp.sum(jnp.concatenate(parts, axis=0), axis=1, keepdims=True) + jnp.exp(sk - m)
                pv = [_dot(e_sc[nb * SWA_WINDOW:(nb + 1) * SWA_WINDOW, :], vcat[nb * SWA_WINDOW:nb * SWA_WINDOW + W2])
                      for nb in range(nsub)]
                outs.append(jnp.concatenate(pv, axis=0) * (1.0 / den))
                lse_all = jnp.where(lane == g, m + jnp.log(den), lse_all)
            o_ref[:, pp * LANES:(pp + 1) * LANES] = jnp.where(lane < 64, outs[0], outs[1]).astype(BF16)
        lse_ref[...] = lse_all

    return pl.pallas_call(
        body, name=name, grid=(4, T // tq),
        in_specs=[pl.BlockSpec((tq, 2 * LANES), lambda kv, i: (i, kv)), cur(ident), prev(ident), cur(ident), prev(ident),
                  pl.BlockSpec((None, 8, LANES), lambda kv, i: (kv, 0, 0))],
        out_specs=[pl.BlockSpec((tq, 2 * LANES), lambda kv, i: (i, kv)), cur(ident)],
        out_shape=[jax.ShapeDtypeStruct((T, D_MODEL), BF16), jax.ShapeDtypeStruct((T, 4 * LANES), F32)],
        scratch_shapes=[pltpu.VMEM((tq, W2), F32), pltpu.VMEM((tq, W2), BF16)],
        compiler_params=_cparams("parallel", "arbitrary"),
    )(qkv, kdup, kdup, vdup, vdup, sinks_b)


def swa_bwd(qkv, kdup, vdup, sinks_b, o, lse, do, name):
    T = qkv.shape[0]
    tq = min(SWA_TQ, T)
    n = T // tq
    nsub = tq // SWA_WINDOW
    cur, prev = _swa_specs(tq)
    rev = lambda i: n - 1 - i

    def body(q_ref, kc_ref, kp_ref, vc_ref, vp_ref, sk_ref, o_ref, lse_ref, do_ref, dq_ref, dk_ref, dv_ref, dsk_ref,
             ck_sc, cv_sc, dkc_sc, dvc_sc):
        i = pl.program_id(1)

        @pl.when(i == 0)
        def _():
            ck_sc[...] = jnp.zeros(ck_sc.shape, F32)
            cv_sc[...] = jnp.zeros(cv_sc.shape, F32)
            dsk_ref[...] = jnp.zeros(dsk_ref.shape, F32)

        kcat = jnp.concatenate([kp_ref[...], kc_ref[...]], axis=0)
        vcat = jnp.concatenate([vp_ref[...], vc_ref[...]], axis=0)
        vis_first, vis_in = _swa_visible(n - 1 - i)
        lane = lax.broadcasted_iota(jnp.int32, (1, LANES), 1)
        dkc_sc[...] = jnp.zeros(dkc_sc.shape, F32)
        dvc_sc[...] = jnp.zeros(dvc_sc.shape, F32)
        for pp in range(2):
            sl = slice(pp * LANES, (pp + 1) * LANES)
            q2, do2, o2 = q_ref[:, sl], do_ref[:, sl], o_ref[:, sl]
            dqs = []
            for hf in range(2):
                g = 2 * pp + hf
                lm = _half(lane, hf)
                qm = jnp.where(lm, q2, jnp.zeros_like(q2))
                dom = jnp.where(lm, do2, jnp.zeros_like(do2))
                delta = jnp.sum(dom.astype(F32) * o2.astype(F32), axis=1, keepdims=True)
                lse_g = lse_ref[:, g:g + 1]
                psk = jnp.exp(sk_ref[g:g + 1, 0:1] - lse_g)
                dsk_ref[g:g + 1, :] += jnp.zeros((1, LANES), F32) - jnp.sum(psk * delta, axis=0, keepdims=True)
                dq_parts = []
                for nb in range(nsub):
                    rows = slice(nb * SWA_WINDOW, (nb + 1) * SWA_WINDOW)
                    band = slice(nb * SWA_WINDOW, nb * SWA_WINDOW + W2)
                    s = jnp.where(vis_first if nb == 0 else vis_in, _nt(qm[rows], kcat[band]) * 0.125, -1e30)
                    p = jnp.exp(s - lse_g[rows])
                    dsb = (p * (_nt(dom[rows], vcat[band]) - delta[rows]) * 0.125).astype(BF16)
                    dq_parts.append(_dot(dsb, kcat[band]))
                    dkc_sc[band, :] += _tn(dsb, qm[rows])
                    dvc_sc[band, :] += _tn(p.astype(BF16), dom[rows])
                dqs.append(jnp.concatenate(dq_parts, axis=0))
            dq_ref[:, sl] = jnp.where(lane < 64, dqs[0], dqs[1]).astype(BF16)
        dkc = dkc_sc[...]
        dvc = dvc_sc[...]
        dkc = dkc + pltpu.roll(dkc, 64, 1)
        dvc = dvc + pltpu.roll(dvc, 64, 1)
        for full, ref, carry in ((dkc, dk_ref, ck_sc), (dvc, dv_ref, cv_sc)):
            if tq > SWA_WINDOW:
                ref[0:tq - SWA_WINDOW, :] = full[SWA_WINDOW:tq, :]
            ref[tq - SWA_WINDOW:tq, :] = full[tq:tq + SWA_WINDOW, :] + carry[...]
            carry[...] = full[0:SWA_WINDOW, :]

    wide = pl.BlockSpec((tq, 2 * LANES), lambda kv, i: (rev(i), kv))
    return pl.pallas_call(
        body, name=name, grid=(4, n),
        in_specs=[wide, cur(rev), prev(rev), cur(rev), prev(rev),
                  pl.BlockSpec((None, 8, LANES), lambda kv, i: (kv, 0, 0)), wide, cur(rev), wide],
        out_specs=[wide, cur(rev), cur(rev), pl.BlockSpec((None, 8, LANES), lambda kv, i: (kv, 0, 0))],
        out_shape=[jax.ShapeDtypeStruct((T, D_MODEL), BF16), jax.ShapeDtypeStruct((T, 4 * LANES), F32),
                   jax.ShapeDtypeStruct((T, 4 * LANES), F32), jax.ShapeDtypeStruct((4, 8, LANES), F32)],
        scratch_shapes=[pltpu.VMEM((SWA_WINDOW, LANES), F32), pltpu.VMEM((SWA_WINDOW, LANES), F32),
                        pltpu.VMEM((tq + SWA_WINDOW, LANES), F32), pltpu.VMEM((tq + SWA_WINDOW, LANES), F32)],
        compiler_params=_cparams("arbitrary", "arbitrary"),
    )(qkv, kdup, kdup, vdup, vdup, sinks_b, o, lse, do)


def fox_gate_fwd(fl, qkv, name):
    T = fl.shape[0]
    ts = min(SCAN_T, T)
    per_tile = min(FOX_T, T) // ts

    def body(fl_ref, q_ref, k_ref, qa_ref, ka_ref, st_ref, carry):
        @pl.when(pl.program_id(0) == 0)
        def _():
            carry[...] = jnp.zeros(carry.shape, F32)

        xv = fl_ref[...]
        ls = jnp.minimum(xv, 0.0) - jnp.log(1.0 + jnp.exp(-jnp.abs(xv)))
        tri = (lax.broadcasted_iota(jnp.int32, (ts, ts), 0) >= lax.broadcasted_iota(jnp.int32, (ts, ts), 1)).astype(F32)
        cs = jnp.dot(tri, ls, precision=HI, preferred_element_type=F32) + carry[...]
        carry[...] = cs[ts - 1:ts, :]
        c1 = cs.astype(BF16).astype(F32)
        c2 = (cs - c1).astype(BF16).astype(F32)
        c3 = (cs - c1 - c2).astype(BF16).astype(F32)
        lane = lax.broadcasted_iota(jnp.int32, (1, LANES), 1)
        ones_q = jnp.where((lane >= 67) & (lane < 70), 1.0, 0.0)
        ones_k = jnp.where((lane >= 64) & (lane < 67), 1.0, 0.0)
        nq = jnp.zeros((1, LANES), F32)
        nk = jnp.zeros((1, LANES), F32)
        for b in range(8):
            qf = q_ref[:, b * LANES:(b + 1) * LANES].astype(F32) * 0.125
            kf = k_ref[:, b * LANES:(b + 1) * LANES].astype(F32)
            for hf in range(2):
                h = 2 * b + hf
                a1, a2, a3 = c1[:, h:h + 1], c2[:, h:h + 1], c3[:, h:h + 1]
                aux_q = jnp.where(lane == 64, a1, jnp.where(lane == 65, a2, jnp.where(lane == 66, a3, ones_q)))
                aux_k = jnp.where(lane == 67, -a1, jnp.where(lane == 68, -a2, jnp.where(lane == 69, -a3, ones_k)))
                qs = qf if hf == 0 else pltpu.roll(qf, 64, 1)
                ks = kf if hf == 0 else pltpu.roll(kf, 64, 1)
                qa_ref[:, h * LANES:(h + 1) * LANES] = jnp.where(lane < 64, qs, aux_q).astype(BF16)
                ka_ref[:, h * LANES:(h + 1) * LANES] = jnp.where(lane < 64, ks, aux_k).astype(BF16)
                for src, is_q in ((qf, True), (kf, False)):
                    sq = jnp.sum(jnp.where(_half(lane, hf), src * src, 0.0), axis=1, keepdims=True)
                    big = jnp.sqrt(jnp.max(sq, axis=0, keepdims=True))
                    if is_q:
                        nq = jnp.where(lane == h, big, nq)
                    else:
                        nk = jnp.where(lane == h, big, nk)
        new = jnp.concatenate([nq, nk, jnp.max(cs, axis=0, keepdims=True), jnp.min(cs, axis=0, keepdims=True),
                               jnp.zeros((4, LANES), F32)], axis=0)
        first = pl.program_id(0) % per_tile == 0
        row = lax.broadcasted_iota(jnp.int32, (8, LANES), 0)

        @pl.when(first)
        def _():
            st_ref[...] = new

        @pl.when(jnp.logical_not(first))
        def _():
            old = st_ref[...]
            st_ref[...] = jnp.where(row == 3, jnp.minimum(old, new), jnp.maximum(old, new))

    out = pl.BlockSpec((ts, 16 * LANES), lambda i: (i, 0))
    return pl.pallas_call(
        body, name=name, grid=(T // ts,),
        in_specs=[pl.BlockSpec((ts, LANES), lambda i: (i, 0)), pl.BlockSpec((ts, D_MODEL), lambda i: (i, 0)),
                  pl.BlockSpec((ts, D_MODEL), lambda i: (i, 1))],
        out_specs=[out, out, pl.BlockSpec((None, 8, LANES), lambda i: (i // per_tile, 0, 0))],
        out_shape=[jax.ShapeDtypeStruct((T, 16 * LANES), BF16), jax.ShapeDtypeStruct((T, 16 * LANES), BF16),
                   jax.ShapeDtypeStruct((T // ts // per_tile, 8, LANES), F32)],
        scratch_shapes=[pltpu.VMEM((1, LANES), F32)],
        compiler_params=_cparams("arbitrary"),
    )(fl, qkv, qkv)


def fox_gate_bwd(fl, dc, name):
    T = fl.shape[0]
    ts = min(SCAN_T, T)
    n = T // ts

    def body(fl_ref, dc_ref, o_ref, carry):
        @pl.when(pl.program_id(0) == 0)
        def _():
            carry[...] = jnp.zeros(carry.shape, F32)

        tri = (lax.broadcasted_iota(jnp.int32, (ts, ts), 0) <= lax.broadcasted_iota(jnp.int32, (ts, ts), 1)).astype(F32)
        rs = jnp.dot(tri, dc_ref[...], precision=HI, preferred_element_type=F32) + carry[...]
        carry[...] = rs[0:1, :]
        o_ref[...] = rs * (1.0 / (1.0 + jnp.exp(fl_ref[...])))

    blk = pl.BlockSpec((ts, LANES), lambda i: (n - 1 - i, 0))
    return pl.pallas_call(
        body, name=name, grid=(n,), in_specs=[blk, blk], out_specs=blk,
        out_shape=jax.ShapeDtypeStruct((T, LANES), F32), scratch_shapes=[pltpu.VMEM((1, LANES), F32)],
        compiler_params=_cparams("arbitrary"),
    )(fl, dc)


FOX_RB = 32


def _half(lane, hf):
    return (lane < 64) if hf == 0 else (lane >= 64)


def _pair(lane, a, b):
    return jnp.where(lane < 64, a, pltpu.roll(b, 64, 1)), jnp.where(lane < 64, pltpu.roll(a, 64, 1), b)


FOX_SLOTS = 3
FOX_SKIP = 110.0


def _tile_bound(st_ref, i, j, h):
    return st_ref[i, h] * st_ref[j, 16 + h] * 1.01 + (st_ref[i, 32 + h] - st_ref[j, 48 + h]) + 1.0


def fox_fwd(st, qa, ka, qkv, name):
    T = qa.shape[0]
    t = min(FOX_T, T)
    n = T // t
    ns = min(n, FOX_SLOTS)

    def body(st_ref, qa_ref, ka_ref, v_ref, ka_hbm, v_hbm, o_ref, lse_ref, lmin_ref,
             m_sc, l_sc, acc_sc, ls_sc, s_sc, p_sc, mmin_sc, kb_sc, vb_sc, in_sems):
        pair, i, jj = pl.program_id(0), pl.program_id(1), pl.program_id(2)
        j = jnp.maximum(i - jj, 0)
        lane = lax.broadcasted_iota(jnp.int32, (1, LANES), 1)

        def head_tile(hf, diag, k_src, v_src):
            hs = slice(hf * LANES, (hf + 1) * LANES)
            sv = _nt(qa_ref[:, hs], k_src[:, hs])
            if diag:
                vis = lax.broadcasted_iota(jnp.int32, (t, t), 0) >= lax.broadcasted_iota(jnp.int32, (t, t), 1)
                sv = jnp.where(vis, sv, -1e30)
            s_sc[...] = sv
            m_old = m_sc[hf]
            m_new = jnp.maximum(m_old, jnp.max(s_sc[...], axis=1, keepdims=True))
            al = jnp.exp(m_old - m_new)
            m_sc[hf] = m_new
            mmin_sc[hf] = jnp.min(m_new)
            for r0 in range(0, t, FOX_RB):
                rs = slice(r0, r0 + FOX_RB)
                mrow = m_new[rs, :]
                part, pieces = None, []
                for cb in range(0, t, LANES):
                    pc = jnp.exp(s_sc[rs, cb:cb + LANES] - mrow)
                    part = pc if part is None else part + pc
                    pieces.append(pc.astype(BF16))
                p_sc[rs, :] = jnp.concatenate(pieces, axis=1)
                ls_sc[rs, :] = part
            l_sc[hf] = al * l_sc[hf] + ls_sc[...]
            acc_sc[hf] = al * acc_sc[hf] + _dot(p_sc[...], v_src[...])

        @pl.when(jj == 0)
        def _():
            m_sc[...] = jnp.full(m_sc.shape, -1e30, F32)
            l_sc[...] = jnp.zeros(l_sc.shape, F32)
            acc_sc[...] = jnp.zeros(acc_sc.shape, F32)
            head_tile(0, True, ka_ref, v_ref)
            head_tile(1, True, ka_ref, v_ref)

        def live(hf, key_tile):
            return _tile_bound(st_ref, i, key_tile, 2 * pair + hf) >= mmin_sc[hf] - FOX_SKIP

        for hf in range(2):
            @pl.when((jj > 0) & (jj <= i) & live(hf, j))
            def _():
                head_tile(hf, False, ka_ref, v_ref)

        @pl.when(jj == ns - 1)
        def _():
            def older(r, carry):
                j2 = i - ns - r

                @pl.when(live(0, j2) | live(1, j2))
                def _():
                    rows = pl.ds(pl.multiple_of(j2 * t, t), t)
                    ck = pltpu.make_async_copy(
                        ka_hbm.at[rows, pl.ds(pl.multiple_of(pair * 2 * LANES, 2 * LANES), 2 * LANES)], kb_sc, in_sems.at[0])
                    cv = pltpu.make_async_copy(
                        v_hbm.at[rows, pl.ds(pl.multiple_of((16 + pair) * LANES, LANES), LANES)], vb_sc, in_sems.at[1])
                    ck.start()
                    cv.start()
                    ck.wait()
                    cv.wait()
                    for hf in range(2):
                        @pl.when(live(hf, j2))
                        def _():
                            head_tile(hf, False, kb_sc, vb_sc)
                return carry

            lax.fori_loop(0, jnp.maximum(i - ns + 1, 0), older, 0)
            l0 = jnp.sum(l_sc[0], axis=1, keepdims=True)
            l1 = jnp.sum(l_sc[1], axis=1, keepdims=True)
            lse0, lse1 = m_sc[0] + jnp.log(l0), m_sc[1] + jnp.log(l1)
            o_ref[...] = jnp.where(lane < 64, acc_sc[0] / l0, acc_sc[1] / l1).astype(BF16)
            lse_ref[...] = jnp.where(lane < 64, lse0, lse1)
            row = lax.broadcasted_iota(jnp.int32, (8, LANES), 0)
            lmin_ref[...] = jnp.where(row == 0, jnp.min(lse0), jnp.where(row == 1, jnp.min(lse1), 0.0))

    oblk = pl.BlockSpec((t, LANES), lambda p, i, jj, st: (i, p))
    return pl.pallas_call(
        body, name=name,
        grid_spec=pltpu.PrefetchScalarGridSpec(
            num_scalar_prefetch=1, grid=(8, n, ns),
            in_specs=[pl.BlockSpec((t, 2 * LANES), lambda p, i, jj, st: (i, p)),
                      pl.BlockSpec((t, 2 * LANES), lambda p, i, jj, st: (jnp.maximum(i - jj, 0), p)),
                      pl.BlockSpec((t, LANES), lambda p, i, jj, st: (jnp.maximum(i - jj, 0), 16 + p)),
                      pl.BlockSpec(memory_space=pl.ANY), pl.BlockSpec(memory_space=pl.ANY)],
            out_specs=[oblk, oblk, pl.BlockSpec((None, None, 8, LANES), lambda p, i, jj, st: (i, p, 0, 0))],
            scratch_shapes=[pltpu.VMEM((2, t, LANES), F32), pltpu.VMEM((2, t, LANES), F32),
                            pltpu.VMEM((2, t, LANES), F32), pltpu.VMEM((t, LANES), F32), pltpu.VMEM((t, t), F32),
                            pltpu.VMEM((t, t), BF16), pltpu.SMEM((2,), F32), pltpu.VMEM((t, 2 * LANES), BF16),
                            pltpu.VMEM((t, LANES), BF16), pltpu.SemaphoreType.DMA((2,))]),
        out_shape=[jax.ShapeDtypeStruct((T, D_MODEL), BF16), jax.ShapeDtypeStruct((T, D_MODEL), F32),
                   jax.ShapeDtypeStruct((n, 8, 8, LANES), F32)],
        compiler_params=_cparams("parallel", "parallel", "arbitrary"),
    )(st, qa, ka, qkv, ka, qkv)


def fox_delta(do, o, name):
    T = do.shape[0]
    tm = min(TM, T)

    def body(do_ref, o_ref, d_ref):
        lane = lax.broadcasted_iota(jnp.int32, (1, LANES), 1)
        out = jnp.zeros((tm, LANES), F32)
        for b in range(8):
            d = do_ref[:, b * LANES:(b + 1) * LANES].astype(F32) * o_ref[:, b * LANES:(b + 1) * LANES].astype(F32)
            for hf in range(2):
                out = jnp.where(lane == 2 * b + hf, jnp.sum(jnp.where(_half(lane, hf), d, 0.0), axis=1, keepdims=True), out)
        d_ref[...] = out

    row = pl.BlockSpec((tm, D_MODEL), lambda i: (i, 0))
    return pl.pallas_call(
        body, name=name, grid=(T // tm,), in_specs=[row, row],
        out_specs=pl.BlockSpec((tm, LANES), lambda i: (i, 0)),
        out_shape=jax.ShapeDtypeStruct((T, LANES), F32),
        compiler_params=_cparams("parallel"),
    )(do, o)


def fox_bwd(st, qa, ka, qkv, lse_row, delta_row, do, name):
    T = qa.shape[0]
    t = min(FOX_T, T)
    n = T // t

    ns = min(n, FOX_SLOTS)

    def body(st_ref, qa_ref, ka_ref, v_ref, lr_ref, dr_ref, do_ref, qa_hbm, do_hbm, lr_hbm, dr_hbm,
             dq_ref, auxq_ref, dk_ref, dv_ref, aux_ref,
             dq_sc, dk_sc, dv_sc, s_sc, dp_sc, p_sc, ds_sc, dqo_sc, auxo_sc, out_sems, qb_sc, dob_sc, lrb_sc, drb_sc, in_sems):
        pair, j, ii = pl.program_id(0), pl.program_id(1), pl.program_id(2)
        i = j + ii
        lane = lax.broadcasted_iota(jnp.int32, (1, LANES), 1)

        @pl.when(ii == 0)
        def _():
            dk_sc[...] = jnp.zeros(dk_sc.shape, F32)
            dv_sc[...] = jnp.zeros(dv_sc.shape, F32)

        @pl.when((j == 0) & (ii == 0))
        def _():
            dq_sc[...] = jnp.zeros(dq_sc.shape, F32)

        def head_tile(hf, diag, q_tile, q_src, do_src, lr_src, dr_src):
            qrows = pl.ds(pl.multiple_of(q_tile * t, t), t)
            v2, do2 = v_ref[...], do_src[...]
            hs = slice(hf * LANES, (hf + 1) * LANES)
            lm = _half(lane, hf)
            qh = q_src[:, hs]
            s_sc[...] = _nt(ka_ref[:, hs], qh)
            dp_sc[...] = _nt(jnp.where(lm, v2, jnp.zeros_like(v2)), do2)
            lrow, drow = lr_src[hf:hf + 1, :], dr_src[hf:hf + 1, :]
            for r0 in range(0, t, FOX_RB):
                rs = slice(r0, r0 + FOX_RB)
                sv = s_sc[rs, :]
                if diag:
                    vis = lax.broadcasted_iota(jnp.int32, (FOX_RB, t), 1) >= (r0 + lax.broadcasted_iota(jnp.int32, (FOX_RB, t), 0))
                    sv = jnp.where(vis, sv, -1e30)
                p = jnp.exp(sv - lrow)
                p_sc[rs, :] = p.astype(BF16)
                ds_sc[rs, :] = (p * (dp_sc[rs, :] - drow)).astype(BF16)
            dv_sc[...] += _dot(p_sc[...], jnp.where(lm, do2, jnp.zeros_like(do2)))
            dk_sc[hf] += _dot(ds_sc[...], qh)
            dq_sc[hf, qrows, :] += _tn(ds_sc[...], ka_ref[:, hs])

        def live(hf, q_tile):
            h = 2 * pair + hf
            return _tile_bound(st_ref, q_tile, j, h) >= st_ref[q_tile, 64 + h] - FOX_SKIP

        @pl.when(ii == 0)
        def _():
            head_tile(0, True, j, qa_ref, do_ref, lr_ref, dr_ref)
            head_tile(1, True, j, qa_ref, do_ref, lr_ref, dr_ref)
            qrows = pl.ds(pl.multiple_of(j * t, t), t)
            dq, aux = _pair(lane, dq_sc[0, qrows, :], dq_sc[1, qrows, :])
            dqo_sc[...] = (dq * 0.125).astype(BF16)
            auxo_sc[...] = aux
            cols = pl.ds(pl.multiple_of(pair * LANES, LANES), LANES)
            c1 = pltpu.make_async_copy(dqo_sc, dq_ref.at[qrows, cols], out_sems.at[0])
            c2 = pltpu.make_async_copy(auxo_sc, auxq_ref.at[qrows, cols], out_sems.at[1])
            c1.start()
            c2.start()
            c1.wait()
            c2.wait()

        q_tile = jnp.minimum(i, n - 1)
        for hf in range(2):
            @pl.when((ii > 0) & (i <= n - 1) & live(hf, q_tile))
            def _():
                head_tile(hf, False, q_tile, qa_ref, do_ref, lr_ref, dr_ref)

        @pl.when(ii == ns - 1)
        def _():
            def later(r, carry):
                i2 = j + ns + r

                @pl.when(live(0, i2) | live(1, i2))
                def _():
                    rows = pl.ds(pl.multiple_of(i2 * t, t), t)
                    cps = [pltpu.make_async_copy(
                               qa_hbm.at[rows, pl.ds(pl.multiple_of(pair * 2 * LANES, 2 * LANES), 2 * LANES)], qb_sc, in_sems.at[0]),
                           pltpu.make_async_copy(
                               do_hbm.at[rows, pl.ds(pl.multiple_of(pair * LANES, LANES), LANES)], dob_sc, in_sems.at[1]),
                           pltpu.make_async_copy(lr_hbm.at[pair, :, rows], lrb_sc, in_sems.at[2]),
                           pltpu.make_async_copy(dr_hbm.at[pair, :, rows], drb_sc, in_sems.at[3])]
                    for cp in cps:
                        cp.start()
                    for cp in cps:
                        cp.wait()
                    for hf in range(2):
                        @pl.when(live(hf, i2))
                        def _():
                            head_tile(hf, False, i2, qb_sc, dob_sc, lrb_sc, drb_sc)
                return carry

            lax.fori_loop(0, jnp.maximum(n - ns - j, 0), later, 0)
            dk, aux = _pair(lane, dk_sc[0], dk_sc[1])
            dk_ref[...] = dk.astype(BF16)
            aux_ref[...] = aux
            dv_ref[...] = dv_sc[...].astype(BF16)

    qix = lambda j, ii: jnp.minimum(j + ii, n - 1)
    qblk = pl.BlockSpec((t, LANES), lambda p, j, ii, st: (qix(j, ii), p))
    kblk = pl.BlockSpec((t, LANES), lambda p, j, ii, st: (j, p))
    rblk = pl.BlockSpec((None, 2, t), lambda p, j, ii, st: (p, 0, qix(j, ii)))
    hbm = pl.BlockSpec(memory_space=pl.ANY)
    bf, f32 = jax.ShapeDtypeStruct((T, D_MODEL), BF16), jax.ShapeDtypeStruct((T, D_MODEL), F32)
    return pl.pallas_call(
        body, name=name,
        grid_spec=pltpu.PrefetchScalarGridSpec(
            num_scalar_prefetch=1, grid=(8, n, ns),
            in_specs=[pl.BlockSpec((t, 2 * LANES), lambda p, j, ii, st: (qix(j, ii), p)),
                      pl.BlockSpec((t, 2 * LANES), lambda p, j, ii, st: (j, p)),
                      pl.BlockSpec((t, LANES), lambda p, j, ii, st: (j, 16 + p)), rblk, rblk, qblk,
                      hbm, hbm, hbm, hbm],
            out_specs=[hbm, hbm, kblk, kblk, kblk],
            scratch_shapes=[pltpu.VMEM((2, T, LANES), F32), pltpu.VMEM((2, t, LANES), F32), pltpu.VMEM((t, LANES), F32),
                            pltpu.VMEM((t, t), F32), pltpu.VMEM((t, t), F32), pltpu.VMEM((t, t), BF16),
                            pltpu.VMEM((t, t), BF16), pltpu.VMEM((t, LANES), BF16), pltpu.VMEM((t, LANES), F32),
                            pltpu.SemaphoreType.DMA((2,)), pltpu.VMEM((t, 2 * LANES), BF16),
                            pltpu.VMEM((t, LANES), BF16), pltpu.VMEM((2, t), F32), pltpu.VMEM((2, t), F32),
                            pltpu.SemaphoreType.DMA((4,))]),
        out_shape=[bf, f32, bf, bf, f32],
        compiler_params=_cparams("arbitrary", "arbitrary", "arbitrary"),
    )(st, qa, ka, qkv, lse_row, delta_row, do, qa, do, lse_row, delta_row)


C = HGRN_CHUNK
LEVELS = (64, 32, 16, 8, 4, 2)


def _pivot(b, B, row):
    if B == C:
        return jnp.broadcast_to(b[C // 2 - 1:C // 2, :], b.shape)
    if B >= 8:
        b3 = b.reshape(C // B, B, LANES)
        return jnp.broadcast_to(b3[:, B // 2 - 1:B // 2, :], b3.shape).reshape(C, LANES)
    if B == 4:
        y = jnp.where((row & 3) == 1, b, 0.0)
        return y + pltpu.roll(y, 1, 0) + pltpu.roll(y, 2, 0) + pltpu.roll(y, C - 1, 0)
    y = jnp.where((row & 1) == 0, b, 0.0)
    return y + pltpu.roll(y, 1, 0)


def _level_factors(bcum):
    row = lax.broadcasted_iota(jnp.int32, (C, 1), 0)
    out = []
    for B in LEVELS:
        upper = (row & (B - 1)) >= B // 2
        e = jnp.exp(-jnp.abs(bcum - _pivot(bcum, B, row)))
        out.append((B, jnp.where(upper, e, 0.0), jnp.where(upper, 0.0, e)))
    return out


def _same_block(B):
    sh = B.bit_length() - 1
    r = lax.broadcasted_iota(jnp.int32, (C, C), 0)
    c = lax.broadcasted_iota(jnp.int32, (C, C), 1)
    return (r >> sh) == (c >> sh)


def _hgrn_gates(q, fl, lb):
    sg = _sigmoid(fl)
    f = lb + (1.0 - lb) * sg
    sq = _sigmoid(q)
    return sg, f, jnp.log(f), 1.0 - f, sq, q * sq


def _cumsum_rows(x, reverse=False):
    r = lax.broadcasted_iota(jnp.int32, (C, C), 0)
    c = lax.broadcasted_iota(jnp.int32, (C, C), 1)
    tri = ((r <= c) if reverse else (r >= c)).astype(F32)
    return jnp.dot(tri, x, precision=HI, preferred_element_type=F32)


def _intra(qs, k, factors):
    r = lax.broadcasted_iota(jnp.int32, (C, C), 0)
    c = lax.broadcasted_iota(jnp.int32, (C, C), 1)
    a = jnp.where(r == c, jnp.sum(qs * k, axis=1, keepdims=True), 0.0)
    ops = []
    for B, eq, ek in factors:
        ql, kl = (qs * eq).astype(BF16), (k * ek).astype(BF16)
        al = _nt(ql, kl)
        a = a + (al if B == C else jnp.where(_same_block(B), al, 0.0))
        ops.append((ql, kl))
    return a, ops


def hgrn_fwd(proj, lb, gn, name):
    T = proj.shape[0]
    tg = min(HGRN_TG, T)
    nch = tg // C

    def body(q_ref, fl_ref, v_ref, g_ref, lb_ref, gn_ref, ao_ref, o_ref, st_ref, st_sc):
        @pl.when(pl.program_id(1) == 0)
        def _():
            st_sc[...] = jnp.zeros(st_sc.shape, F32)

        lb_v, gn_v = lb_ref[...], gn_ref[...]

        def chunk(ci, carry):
            rows = pl.ds(pl.multiple_of(ci * C, C), C)
            _, f, lf, k, _, qs = _hgrn_gates(q_ref[rows, :], fl_ref[rows, :], lb_v)
            vb = v_ref[rows, :].astype(BF16)
            gv = g_ref[rows, :]
            bcum = _cumsum_rows(lf)
            blast = bcum[C - 1:C, :]
            a, _ = _intra(qs, k, _level_factors(bcum))
            st = st_sc[...]
            st_ref[ci] = st
            o = _dot(a.astype(BF16), vb) + _nt((qs * jnp.exp(bcum)).astype(BF16), st.astype(BF16))
            st_sc[...] = st * jnp.exp(blast) + _tn(vb, (k * jnp.exp(blast - bcum)).astype(BF16))
            o_ref[rows, :] = o
            ao_ref[rows, :] = (o * _rms(o) * gn_v * (gv * _sigmoid(gv))).astype(BF16)
            return carry

        lax.fori_loop(0, nch, chunk, 0, unroll=8)

    col = lambda off: pl.BlockSpec((tg, LANES), lambda h, i: (i, off + h))
    one = pl.BlockSpec((1, LANES), lambda h, i: (0, h))
    return pl.pallas_call(
        body, name=name, grid=(8, T // tg),
        in_specs=[col(0), col(8), col(16), col(24), one, one],
        out_specs=[col(0), col(0), pl.BlockSpec((None, nch, LANES, LANES), lambda h, i: (h, i, 0, 0))],
        out_shape=[jax.ShapeDtypeStruct((T, D_MODEL), BF16), jax.ShapeDtypeStruct((T, D_MODEL), F32),
                   jax.ShapeDtypeStruct((8, T // C, LANES, LANES), F32)],
        scratch_shapes=[pltpu.VMEM((LANES, LANES), F32)],
        compiler_params=_cparams("parallel", "arbitrary"),
    )(proj, proj, proj, proj, lb, gn)


def hgrn_bwd(proj, lb, gn, o_raw, states, dao, name):
    T = proj.shape[0]
    tg = min(HGRN_TG, T)
    nch = tg // C
    n = T // tg

    def body(q_ref, fl_ref, v_ref, g_ref, lb_ref, gn_ref, o_ref, st_ref, dao_ref,
             dq_ref, dfl_ref, dv_ref, dg_ref, dlb_ref, dgn_ref, dst_sc):
        @pl.when(pl.program_id(1) == 0)
        def _():
            dst_sc[...] = jnp.zeros(dst_sc.shape, F32)
            dlb_ref[...] = jnp.zeros(dlb_ref.shape, F32)
            dgn_ref[...] = jnp.zeros(dgn_ref.shape, F32)

        lb_v, gn_v = lb_ref[...], gn_ref[...]
        r64 = lax.broadcasted_iota(jnp.int32, (C, C), 0)
        c64 = lax.broadcasted_iota(jnp.int32, (C, C), 1)
        row = lax.broadcasted_iota(jnp.int32, (C, 1), 0)

        def chunk(cr, carry):
            ci = nch - 1 - cr
            rows = pl.ds(pl.multiple_of(ci * C, C), C)
            q, fl, gv = q_ref[rows, :], fl_ref[rows, :], g_ref[rows, :]
            sg, f, lf, k, sq, qs = _hgrn_gates(q, fl, lb_v)
            vb = v_ref[rows, :].astype(BF16)
            o = o_ref[rows, :]
            ro = _rms(o)
            on = o * ro
            sgg = _sigmoid(gv)
            gate = gv * sgg
            dao_v = dao_ref[rows, :].astype(F32)
            dg_ref[rows, :] = (dao_v * on * gn_v * (sgg * (1.0 + gv * (1.0 - sgg)))).astype(BF16)
            dgn_ref[...] += jnp.sum(dao_v * on * gate, axis=0, keepdims=True)
            don = dao_v * gn_v * gate
            do = ro * (don - on * jnp.mean(don * on, axis=-1, keepdims=True))
            dob = do.astype(BF16)
            bcum = _cumsum_rows(lf)
            blast = bcum[C - 1:C, :]
            factors = _level_factors(bcum)
            a, ops = _intra(qs, k, factors)
            eb = jnp.exp(bcum)
            ekb = jnp.exp(blast - bcum)
            qb = qs * eb
            kb = k * ekb
            st = st_ref[ci]
            dst = dst_sc[...]
            dstb = dst.astype(BF16)
            da = jnp.where(r64 >= c64, _nt(dob, vb), 0.0)
            dv_ref[rows, :] = (_tn(a.astype(BF16), dob) + _nt(kb.astype(BF16), dstb)).astype(BF16)
            dqb = _dot(dob, st.astype(BF16))
            dkb = _dot(vb, dstb)
            eblast = jnp.exp(blast)
            dst_sc[...] = dst * eblast + _tn(dob, qb.astype(BF16))
            dblast = eblast * jnp.sum(dst * st, axis=0, keepdims=True) + jnp.sum(dkb * kb, axis=0, keepdims=True)
            dad = jnp.sum(jnp.where(r64 == c64, da, 0.0), axis=1, keepdims=True)
            dqs = dqb * eb + dad * k
            dk = dkb * ekb + dad * qs
            dbcum = dqb * qb - dkb * kb + jnp.where(row == C - 1, dblast, 0.0)
            for (B, eq, ek), (ql, kl) in zip(factors, ops):
                dal = (da if B == C else jnp.where(_same_block(B), da, 0.0)).astype(BF16)
                dql, dkl = _dot(dal, kl), _tn(dal, ql)
                dqs = dqs + dql * eq
                dk = dk + dkl * ek
                dbcum = dbcum + (dql * ql.astype(F32) - dkl * kl.astype(F32))
            df = _cumsum_rows(dbcum, reverse=True) / f - dk
            dfl_ref[rows, :] = (df * (1.0 - lb_v) * sg * (1.0 - sg)).astype(BF16)
            dlb_ref[...] += jnp.sum(df * (1.0 - sg), axis=0, keepdims=True)
            dq_ref[rows, :] = (dqs * (sq * (1.0 + q * (1.0 - sq)))).astype(BF16)
            return carry

        lax.fori_loop(0, nch, chunk, 0, unroll=8)

    col = lambda off: pl.BlockSpec((tg, LANES), lambda h, i: (n - 1 - i, off + h))
    one = pl.BlockSpec((1, LANES), lambda h, i: (0, h))
    big = jax.ShapeDtypeStruct((T, D_MODEL), BF16)
    small = jax.ShapeDtypeStruct((1, D_MODEL), F32)
    return pl.pallas_call(
        body, name=name, grid=(8, n),
        in_specs=[col(0), col(8), col(16), col(24), one, one, col(0),
                  pl.BlockSpec((None, nch, LANES, LANES), lambda h, i: (h, n - 1 - i, 0, 0)), col(0)],
        out_specs=[col(0), col(0), col(0), col(0), one, one],
        out_shape=[big, big, big, big, small, small],
        scratch_shapes=[pltpu.VMEM((LANES, LANES), F32)],
        compiler_params=_cparams("arbitrary", "arbitrary"),
    )(proj, proj, proj, proj, lb, gn, o_raw, states, dao)


def lower_bound_fwd(logits, name):
    def body(l_ref, s_ref):
        lv = l_ref[...]
        e = jnp.exp(lv - jnp.max(lv, axis=0, keepdims=True))
        s_ref[...] = e / jnp.sum(e, axis=0, keepdims=True)

    return pl.pallas_call(body, name=name, out_shape=jax.ShapeDtypeStruct(logits.shape, F32))(logits)


def lower_bound_bwd(sm, dlb, name):
    def body(s_ref, d_ref, o_ref):
        s = s_ref[...]
        row = lax.broadcasted_iota(jnp.int32, s.shape, 0)
        o_ref[...] = d_ref[...] * s[1:2, :] * (jnp.where(row == 1, 1.0, 0.0) - s)

    return pl.pallas_call(body, name=name, out_shape=jax.ShapeDtypeStruct(sm.shape, F32))(sm, dlb)


def _pad_rows(flat, mult):
    rows = -(-flat.shape[-1] // D_MODEL)
    rows = -(-rows // mult) * mult
    pad = rows * D_MODEL - flat.shape[-1]
    flat = jnp.pad(flat, [(0, 0)] * (flat.ndim - 1) + [(0, pad)])
    return flat.reshape(flat.shape[:-1] + (rows, D_MODEL))


def _gather_weights(w):
    direct = [n for n in SHARDED if n not in BIASES and w[n].shape[SHARD_AXIS[n]] % LANES == 0]
    packed = [n for n in SHARDED if n not in direct]
    pieces = []
    for nme in packed:
        a = w[nme]
        if nme in BIASES:
            pieces.append(lax.bitcast_convert_type(a, BF16).reshape(-1))
        else:
            pieces.append(a.astype(BF16).reshape(-1))
    flat = _pad_rows(jnp.concatenate(pieces), 16)
    out = all_gather_shards([(w[n].astype(BF16), SHARD_AXIS[n]) for n in direct] + [(flat, None)])
    full = dict(zip(direct, out[:-1]))
    got, off = out[-1].reshape(N_DEV, -1), 0
    for nme in packed:
        shp = w[nme].shape
        cnt = 1
        for s in shp:
            cnt *= s
        if nme in BIASES:
            seg = got[:, off:off + 2 * cnt].reshape((N_DEV,) + shp + (2,))
            seg = lax.bitcast_convert_type(seg, F32)
            off += 2 * cnt
        else:
            seg = got[:, off:off + cnt].reshape((N_DEV,) + shp)
            off += cnt
        full[nme] = jnp.concatenate([seg[d] for d in range(N_DEV)], axis=SHARD_AXIS[nme])
    return full


def _pieces(gfull, axis):
    shp = gfull.shape
    a = gfull.reshape(shp[:axis] + (N_DEV, shp[axis] // N_DEV) + shp[axis + 1:])
    return jnp.moveaxis(a, axis, 0).reshape(N_DEV, -1)


def kernel(x, norm_mix, norm_mlp, norm_final, w_up, w_down, swa_w_qkv, swa_b_qkv, swa_sinks, swa_w_o, hgrn_w_in, hgrn_lb_logits, hgrn_g_norm, hgrn_w_o, fox_w_in, fox_b_in, fox_w_o, loss_target, m_norm_mix, m_norm_mlp, m_norm_final, m_w_up, m_w_down, m_swa_w_qkv, m_swa_b_qkv, m_swa_sinks, m_swa_w_o, m_hgrn_w_in, m_hgrn_lb_logits, m_hgrn_g_norm, m_hgrn_w_o, m_fox_w_in, m_fox_b_in, m_fox_w_o, v_norm_mix, v_norm_mlp, v_norm_final, v_w_up, v_w_down, v_swa_w_qkv, v_swa_b_qkv, v_swa_sinks, v_swa_w_o, v_hgrn_w_in, v_hgrn_lb_logits, v_hgrn_g_norm, v_hgrn_w_o, v_fox_w_in, v_fox_b_in, v_fox_w_o):
    w = dict(norm_mix=norm_mix, norm_mlp=norm_mlp, norm_final=norm_final, w_up=w_up, w_down=w_down,
             swa_w_qkv=swa_w_qkv, swa_b_qkv=swa_b_qkv, swa_sinks=swa_sinks, swa_w_o=swa_w_o, hgrn_w_in=hgrn_w_in,
             hgrn_lb_logits=hgrn_lb_logits, hgrn_g_norm=hgrn_g_norm, hgrn_w_o=hgrn_w_o, fox_w_in=fox_w_in,
             fox_b_in=fox_b_in, fox_w_o=fox_w_o)
    mom = dict(norm_mix=m_norm_mix, norm_mlp=m_norm_mlp, norm_final=m_norm_final, w_up=m_w_up, w_down=m_w_down,
               swa_w_qkv=m_swa_w_qkv, swa_b_qkv=m_swa_b_qkv, swa_sinks=m_swa_sinks, swa_w_o=m_swa_w_o,
               hgrn_w_in=m_hgrn_w_in, hgrn_lb_logits=m_hgrn_lb_logits, hgrn_g_norm=m_hgrn_g_norm, hgrn_w_o=m_hgrn_w_o,
               fox_w_in=m_fox_w_in, fox_b_in=m_fox_b_in, fox_w_o=m_fox_w_o)
    var = dict(norm_mix=v_norm_mix, norm_mlp=v_norm_mlp, norm_final=v_norm_final, w_up=v_w_up, w_down=v_w_down,
               swa_w_qkv=v_swa_w_qkv, swa_b_qkv=v_swa_b_qkv, swa_sinks=v_swa_sinks, swa_w_o=v_swa_w_o,
               hgrn_w_in=v_hgrn_w_in, hgrn_lb_logits=v_hgrn_lb_logits, hgrn_g_norm=v_hgrn_g_norm, hgrn_w_o=v_hgrn_w_o,
               fox_w_in=v_fox_w_in, fox_b_in=v_fox_b_in, fox_w_o=v_fox_w_o)
    T = x.shape[1]
    x0 = x[0]
    tgt = loss_target[0]
    W = _gather_weights(w)
    zeros_b = jnp.zeros((1, 4 * D_MODEL), F32)

    def swa_layer(xin, i, j):
        qkv = norm_matmul(xin, norm_mix[i:i + 1], W['swa_w_qkv'][j], W['swa_b_qkv'][j:j + 1], BF16, f"swa_qkv_L{i}")
        dup = lambda a: jnp.broadcast_to(a.reshape(T, 4, 1, 64), (T, 4, 2, 64)).reshape(T, 4 * LANES)
        kdup, vdup = dup(qkv[:, 1024:1280]), dup(qkv[:, 1280:1536])
        sk = jnp.broadcast_to(jnp.pad(swa_sinks[j].reshape(4, 4), ((0, 0), (0, 4)))[:, :, None], (4, 8, LANES))
        ao, lse = swa_fwd(qkv, kdup, vdup, sk, f"swa_fwd_L{i}")
        xmid = matmul(ao, W['swa_w_o'][j], F32, f"swa_out_L{i}", res=xin)
        return xmid, (qkv, kdup, vdup, sk, ao, lse)

    def swa_layer_bwd(xin, saved, dmid, i, j, grads):
        qkv, kdup, vdup, sk, ao, lse = saved
        dao = matmul(dmid, W['swa_w_o'][j].T, BF16, f"swa_dout_L{i}")
        grads['swa_w_o'][j] = tn_matmul(ao, dmid, f"swa_dwo_L{i}")
        dq, dk, dv, dsk = swa_bwd(qkv, kdup, vdup, sk, ao, lse, dao, f"swa_bwd_L{i}")
        wt = W['swa_w_qkv'][j].T
        spread = lambda a: jnp.pad(a.reshape(4, 64, D_MODEL), ((0, 0), (0, 64), (0, 0))).reshape(4 * LANES, D_MODEL)
        gather = lambda a: a.reshape(a.shape[0], 4, LANES)[:, :, :64].reshape(a.shape[0], 256)
        dx, h, dg = proj_bwd(xin, norm_mix[i:i + 1], dmid,
                             [(dq, wt[:1024]), (dk, spread(wt[1024:1280])), (dv, spread(wt[1280:]))], f"swa_din_L{i}")
        gq, bq = tn_matmul(h, dq, f"swa_dwq_L{i}", colsum=True)
        gk, bk = tn_matmul(h, dk, f"swa_dwk_L{i}", colsum=True)
        gv, bv = tn_matmul(h, dv, f"swa_dwv_L{i}", colsum=True)
        grads['swa_w_qkv'][j] = jnp.concatenate([gq, gather(gk), gather(gv)], axis=1)
        grads['swa_b_qkv'][j] = jnp.concatenate([bq, gather(bk), gather(bv)], axis=1)[0]
        grads['swa_sinks'][j] = dsk[:, :4, 0].reshape(16)
        grads['norm_mix'][i] = dg[0]
        return dx

    lb_soft = lower_bound_fwd(hgrn_lb_logits, "hgrn_lb_fwd")
    lb = lb_soft[1:2]

    def hgrn_layer(xin, i, j):
        proj = norm_matmul(xin, norm_mix[i:i + 1], W['hgrn_w_in'][j], zeros_b, F32, f"hgrn_in_L{i}")
        ao, o_raw, states = hgrn_fwd(proj, lb, hgrn_g_norm[j:j + 1], f"hgrn_fwd_L{i}")
        xmid = matmul(ao, W['hgrn_w_o'][j], F32, f"hgrn_out_L{i}", res=xin)
        return xmid, (proj, ao, o_raw, states)

    def hgrn_layer_bwd(xin, saved, dmid, i, j, grads):
        proj, ao, o_raw, states = saved
        dao = matmul(dmid, W['hgrn_w_o'][j].T, BF16, f"hgrn_dout_L{i}")
        grads['hgrn_w_o'][j] = tn_matmul(ao, dmid, f"hgrn_dwo_L{i}")
        dq, dfl, dv, dgt, dlb, dgn = hgrn_bwd(proj, lb, hgrn_g_norm[j:j + 1], o_raw, states, dao, f"hgrn_bwd_L{i}")
        wt = W['hgrn_w_in'][j].T
        parts = [dq, dfl, dv, dgt]
        dx, h, dg = proj_bwd(xin, norm_mix[i:i + 1], dmid,
                             [(d, wt[n * D_MODEL:(n + 1) * D_MODEL]) for n, d in enumerate(parts)], f"hgrn_din_L{i}")
        grads['hgrn_w_in'][j] = jnp.concatenate(
            [tn_matmul(h, d, f"hgrn_dwin{n}_L{i}") for n, d in enumerate(parts)], axis=1)
        grads['hgrn_g_norm'][j] = dgn[0]
        grads['hgrn_lb_logits'] = lower_bound_bwd(lb_soft, dlb, "hgrn_lb_bwd")
        grads['norm_mix'][i] = dg[0]
        return dx

    def fox_layer(xin, i, j):
        w_in = W['fox_w_in'][j]
        b_in = W['fox_b_in'][j:j + 1]
        qkv = norm_matmul(xin, norm_mix[i:i + 1], w_in[:, :3072], b_in[:, :3072], BF16, f"fox_qkv_L{i}")
        wf = jnp.pad(w_in[:, 3072:], ((0, 0), (0, LANES - 16)))
        bf = jnp.pad(b_in[:, 3072:], ((0, 0), (0, LANES - 16)))
        fl = norm_matmul(xin, norm_mix[i:i + 1], wf, bf, F32, f"fox_f_L{i}")
        qa, ka, bounds = fox_gate_fwd(fl, qkv, f"fox_gate_L{i}")
        st = bounds[:, :4, :16].reshape(bounds.shape[0], 64)
        ao, lse, lmin = fox_fwd(st, qa, ka, qkv, f"fox_fwd_L{i}")
        st = jnp.concatenate([st, lmin[:, :, :2, 0].reshape(lmin.shape[0], 16)], axis=1)
        xmid = matmul(ao, W['fox_w_o'][j], F32, f"fox_out_L{i}", res=xin)
        return xmid, (qkv, fl, qa, ka, ao, lse, wf, st)

    def fox_layer_bwd(xin, saved, dmid, i, j, grads):
        qkv, fl, qa, ka, ao, lse, wf, st = saved
        dao = matmul(dmid, W['fox_w_o'][j].T, BF16, f"fox_dout_L{i}")
        grads['fox_w_o'][j] = tn_matmul(ao, dmid, f"fox_dwo_L{i}")
        delta = fox_delta(dao, ao, f"fox_delta_L{i}")
        dq, aux_q, dk, dv, aux_k = fox_bwd(st, qa, ka, qkv, lse[:, ::64].T.reshape(8, 2, T),
                                           delta[:, :16].T.reshape(8, 2, T), dao, f"fox_bwd_L{i}")
        dcp = jnp.pad(aux_q[:, ::64] - aux_k[:, 3::64], ((0, 0), (0, LANES - 16)))
        dfl = fox_gate_bwd(fl, dcp, f"fox_dgate_L{i}")
        wt = W['fox_w_in'][j][:, :3072].T
        parts = [dq, dk, dv]
        dx, h, dg = proj_bwd(xin, norm_mix[i:i + 1], dmid,
                             [(d, wt[n * D_MODEL:(n + 1) * D_MODEL]) for n, d in enumerate(parts)] + [(dfl, wf.T)],
                             f"fox_din_L{i}")
        gw = [tn_matmul(h, d, f"fox_dw{n}_L{i}", colsum=True) for n, d in enumerate(parts + [dfl])]
        grads['fox_w_in'][j] = jnp.concatenate([g for g, _ in gw[:3]] + [gw[3][0][:, :16]], axis=1)
        grads['fox_b_in'][j] = jnp.concatenate([b for _, b in gw[:3]] + [gw[3][1][:, :16]], axis=1)[0]
        grads['norm_mix'][i] = dg[0]
        return dx

    mixers = [(swa_layer, swa_layer_bwd), (hgrn_layer, hgrn_layer_bwd), (fox_layer, fox_layer_bwd)]

    xs, mids, saves = [x0], [], []
    for i in range(DEPTH):
        xmid, saved = mixers[i % 3][0](xs[-1], i, i // 3)
        mids.append(xmid)
        saves.append(saved)
        xs.append(mlp_fwd(xmid, norm_mlp[i:i + 1], W['w_up'][i], W['w_down'][i], f"mlp_fwd_L{i}"))

    grads = {n: [None] * w[n].shape[0] for n in WEIGHTS if n not in ('norm_final', 'hgrn_lb_logits')}
    loss_part, dx, dgf = final_loss(xs[-1], norm_final.reshape(1, D_MODEL), tgt, "final_loss")
    grads['norm_final'] = dgf[0]
    for i in reversed(range(DEPTH)):
        dmid, h, a, du, dg = mlp_bwd(mids[i], norm_mlp[i:i + 1], W['w_up'][i], W['w_up'][i].T, W['w_down'][i].T, dx,
                                     f"mlp_bwd_L{i}")
        grads['w_up'][i] = tn_matmul(h, du, f"mlp_dwup_L{i}")
        grads['w_down'][i] = tn_matmul(a, dx, f"mlp_dwdown_L{i}")
        grads['norm_mlp'][i] = dg[0]
        dx = mixers[i % 3][1](xs[i], saves[i], dmid, i, i // 3, grads)
    gfull = {n: (g if not isinstance(g, list) else jnp.stack(g)) for n, g in grads.items()}

    mats = [n for n in SHARDED if n not in BIASES]
    view = lambda a: a.reshape(-1, a.shape[-1])
    sends = [_pieces(gfull[n], SHARD_AXIS[n]).astype(BF16).reshape((N_DEV,) + view(w[n]).shape) for n in mats]
    common = jnp.concatenate([gfull[n].reshape(-1) for n in REPLICATED] + [loss_part[0, 0:1]])
    small = jnp.concatenate([jnp.broadcast_to(common[None], (N_DEV, common.shape[0]))]
                            + [_pieces(gfull[n], SHARD_AXIS[n]) for n in BIASES], axis=1)
    recvs = all_to_all_rows(sends + [_pad_rows(small, 16)])
    tail = lambda vals: _pad_rows(jnp.concatenate([vals[n].reshape(-1) for n in REPLICATED] + [jnp.zeros((1,), F32)]
                                                  + [vals[n].reshape(-1) for n in BIASES]), 16)
    res = [{}, {}, {}, {}]
    for nme, rv in zip(mats, recvs):
        outs = reduce_adamw(rv, view(w[nme]), view(mom[nme]), view(var[nme]), f"adamw_{nme}")
        for o, r in zip(outs, res):
            r[nme] = o.reshape(w[nme].shape)
    outs = reduce_adamw(recvs[-1], tail(w), tail(mom), tail(var), "adamw_small")
    off = 0
    for nme in REPLICATED + ['loss'] + list(BIASES):
        cnt = 1 if nme == 'loss' else w[nme].size
        if nme == 'loss':
            loss = outs[0].reshape(-1)[off]
        else:
            for o, r in zip(outs, res):
                r[nme] = o.reshape(-1)[off:off + cnt].reshape(w[nme].shape)
        off += cnt
    return (loss, dx[None], *[res[0][n] for n in WEIGHTS], *[res[1][n] for n in WEIGHTS],
            *[res[2][n] for n in WEIGHTS], *[res[3][n] for n in WEIGHTS])
```

```python
import functools

import jax
import jax.numpy as jnp
from jax import lax
from jax.experimental import pallas as pl
from jax.experimental.pallas import tpu as pltpu

F32 = jnp.float32
BF16 = jnp.bfloat16
HI = lax.Precision.HIGHEST

N_DEV = 8
D_MODEL = 1024
DEPTH = 4
EPS = 1e-6
SWA_WINDOW = 128
HGRN_CHUNK = 64
LANES = 128
VMEM_LIMIT = 56 << 20

ADAM_LR, ADAM_B1, ADAM_B2, ADAM_EPS, ADAM_WD, ADAM_STEP = 0.001, 0.9, 0.999, 1e-08, 0.01, 10

TM = 512
TF = 512
TK = 512
FOX_T = 1024
SWA_TQ = 512
HGRN_TG = 512
SCAN_T = 256

WEIGHTS = ['norm_mix', 'norm_mlp', 'norm_final', 'w_up', 'w_down', 'swa_w_qkv', 'swa_b_qkv', 'swa_sinks', 'swa_w_o',
           'hgrn_w_in', 'hgrn_lb_logits', 'hgrn_g_norm', 'hgrn_w_o', 'fox_w_in', 'fox_b_in', 'fox_w_o']
SHARD_AXIS = {'norm_mix': None, 'norm_mlp': None, 'norm_final': None, 'w_up': 2, 'w_down': 1, 'swa_w_qkv': 2,
              'swa_b_qkv': 1, 'swa_sinks': None, 'swa_w_o': 1, 'hgrn_w_in': 2, 'hgrn_lb_logits': None,
              'hgrn_g_norm': None, 'hgrn_w_o': 1, 'fox_w_in': 2, 'fox_b_in': 1, 'fox_w_o': 1}
SHARDED = [n for n in WEIGHTS if SHARD_AXIS[n] is not None]
REPLICATED = [n for n in WEIGHTS if SHARD_AXIS[n] is None]
BIASES = ('swa_b_qkv', 'fox_b_in')


def _cparams(*sem):
    return pltpu.CompilerParams(dimension_semantics=sem, vmem_limit_bytes=VMEM_LIMIT)


def _nt(a, b):
    return lax.dot_general(a, b, (((1,), (1,)), ((), ())), preferred_element_type=F32)


def _tn(a, b):
    return lax.dot_general(a, b, (((0,), (0,)), ((), ())), preferred_element_type=F32)


def _dot(a, b):
    return jnp.dot(a, b, preferred_element_type=F32)


def _sigmoid(x):
    return 1.0 / (1.0 + jnp.exp(-x))


def _rms(xv):
    return lax.rsqrt(jnp.mean(xv * xv, axis=-1, keepdims=True) + EPS)


def _rms_bwd(xv, g, dh):
    r = _rms(xv)
    xhat = xv * r
    dhg = dh * g
    dx = r * (dhg - xhat * jnp.mean(dhg * xhat, axis=-1, keepdims=True))
    return dx, jnp.sum(dh * xhat, axis=0, keepdims=True)


def _my_id():
    return lax.axis_index("x"), lax.axis_index("y"), lax.axis_index("c")


def _peer(x, y, c, k):
    return (lax.rem(x + ((k >> 2) & 1), 2), lax.rem(y + ((k >> 1) & 1), 2), lax.rem(c + (k & 1), 2))


def all_gather_shards(shards):
    n = len(shards)

    def place(o_ref, local, axis, dev):
        if axis is None:
            return o_ref.at[dev]
        idx = [slice(None)] * local.ndim
        idx[axis] = pl.ds(pl.multiple_of(dev * local.shape[axis], local.shape[axis]), local.shape[axis])
        return o_ref.at[tuple(idx)]

    def body(*refs):
        x_refs, o_refs = refs[:n], refs[n:2 * n]
        send_sems, recv_sems, loc_sems = refs[2 * n:]
        x, y, c = _my_id()
        sibling = (x, y, 1 - c)
        chips = [(1 - x, y), (x, 1 - y), (1 - x, 1 - y)]
        dev = lambda px, py, pc: 4 * px + 2 * py + pc

        def copy(a, k, block, to, src=None):
            local, axis = shards[a]
            spot = place(o_refs[a], local, axis, dev(*block))
            return pltpu.make_async_remote_copy(
                src_ref=spot if src is None else src, dst_ref=spot, send_sem=send_sems.at[a * 7 + k],
                recv_sem=recv_sems.at[a * 7 + k], device_id=to, device_id_type=pl.DeviceIdType.MESH)

        mines, sent = [], []
        for a in range(n):
            local, axis = shards[a]
            mine = pltpu.make_async_copy(x_refs[a], place(o_refs[a], local, axis, dev(x, y, c)), loc_sems.at[a])
            mine.start()
            mines.append(mine)
            for k, to in enumerate([sibling] + [(*chip, c) for chip in chips]):
                cp = copy(a, k, (x, y, c), to, src=x_refs[a])
                cp.start()
                sent.append(cp)
        passed = []
        for j, chip in enumerate(chips):
            for a in range(n):
                copy(a, 1 + j, (*chip, c), (x, y, c)).wait_recv()
                cp = copy(a, 4 + j, (*chip, c), sibling)
                cp.start()
                passed.append(cp)
        for a in range(n):
            copy(a, 0, sibling, (x, y, c)).wait_recv()
            for j, chip in enumerate(chips):
                copy(a, 4 + j, (*chip, 1 - c), (x, y, c)).wait_recv()
        for cp in sent + passed:
            cp.wait_send()
        for mine in mines:
            mine.wait()

    def full_shape(local, axis):
        if axis is None:
            return (N_DEV,) + local.shape
        return local.shape[:axis] + (N_DEV * local.shape[axis],) + local.shape[axis + 1:]

    hbm = pl.BlockSpec(memory_space=pl.ANY)
    return pl.pallas_call(
        body, name="all_gather_weights",
        out_shape=[jax.ShapeDtypeStruct(full_shape(l, ax), l.dtype) for l, ax in shards],
        in_specs=[hbm] * n, out_specs=[hbm] * n,
        scratch_shapes=[pltpu.SemaphoreType.DMA((n * (N_DEV - 1),)), pltpu.SemaphoreType.DMA((n * (N_DEV - 1),)),
                        pltpu.SemaphoreType.DMA((n,))],
    )(*[l for l, _ in shards])


def all_to_all_rows(sends):
    n = len(sends)

    def body(*refs):
        s_refs, r_refs = refs[:n], refs[n:2 * n]
        send_sems, recv_sems, loc_sems = refs[2 * n:]
        x, y, c = _my_id()
        me = 4 * x + 2 * y + c
        copies = []
        for a, (s_ref, r_ref) in enumerate(zip(s_refs, r_refs)):
            mine = pltpu.make_async_copy(s_ref.at[me], r_ref.at[me], loc_sems.at[a])
            mine.start()
            copies.append(mine)
            for k in range(1, N_DEV):
                px, py, pc = _peer(x, y, c, k)
                sem = a * (N_DEV - 1) + k - 1
                cp = pltpu.make_async_remote_copy(
                    src_ref=s_ref.at[4 * px + 2 * py + pc], dst_ref=r_ref.at[me],
                    send_sem=send_sems.at[sem], recv_sem=recv_sems.at[sem],
                    device_id=(px, py, pc), device_id_type=pl.DeviceIdType.MESH)
                cp.start()
                copies.append(cp)
        for cp in copies:
            cp.wait()

    hbm = pl.BlockSpec(memory_space=pl.ANY)
    return pl.pallas_call(
        body, name="all_to_all_grads",
        out_shape=[jax.ShapeDtypeStruct(s.shape, s.dtype) for s in sends],
        in_specs=[hbm] * n, out_specs=[hbm] * n,
        scratch_shapes=[pltpu.SemaphoreType.DMA((n * (N_DEV - 1),)), pltpu.SemaphoreType.DMA((n * (N_DEV - 1),)),
                        pltpu.SemaphoreType.DMA((n,))],
    )(*sends)


def reduce_adamw(recv, w, m, v, name):
    R, C = w.shape
    tr = max(t for t in range(16, (1 << 18) // C + 1, 16) if R % t == 0)
    c1 = 1.0 / (1.0 - ADAM_B1 ** ADAM_STEP)
    c2 = 1.0 / (1.0 - ADAM_B2 ** ADAM_STEP)

    def body(r_ref, w_ref, m_ref, v_ref, g_ref, d_ref, nm_ref, nv_ref):
        g = r_ref[0].astype(F32)
        for s in range(1, N_DEV):
            g = g + r_ref[s].astype(F32)
        m2 = ADAM_B1 * m_ref[...] + (1.0 - ADAM_B1) * g
        v2 = ADAM_B2 * v_ref[...] + (1.0 - ADAM_B2) * (g * g)
        g_ref[...] = g
        nm_ref[...] = m2
        nv_ref[...] = v2
        d_ref[...] = -ADAM_LR * ((m2 * c1) / (jnp.sqrt(v2 * c2) + ADAM_EPS) + ADAM_WD * w_ref[...])

    row = pl.BlockSpec((tr, C), lambda i: (i, 0))
    shp = jax.ShapeDtypeStruct((R, C), F32)
    return pl.pallas_call(
        body, name=name, grid=(R // tr,),
        in_specs=[pl.BlockSpec((N_DEV, tr, C), lambda i: (0, i, 0)), row, row, row],
        out_specs=[row, row, row, row], out_shape=[shp, shp, shp, shp],
        compiler_params=_cparams("parallel"),
    )(recv, w, m, v)


def norm_matmul(x, g, w, b, out_dtype, name):
    T, N = x.shape[0], w.shape[1]
    tm, tn = min(TM, T), min(512, N)

    def body(x_ref, g_ref, w_ref, b_ref, o_ref, h_sc):
        @pl.when(pl.program_id(1) == 0)
        def _():
            xv = x_ref[...]
            h_sc[...] = (xv * _rms(xv) * g_ref[...]).astype(BF16)
        o_ref[...] = (_dot(h_sc[...], w_ref[...]) + b_ref[...]).astype(o_ref.dtype)

    return pl.pallas_call(
        body, name=name, grid=(T // tm, N // tn),
        in_specs=[pl.BlockSpec((tm, D_MODEL), lambda i, j: (i, 0)), pl.BlockSpec((1, D_MODEL), lambda i, j: (0, 0)),
                  pl.BlockSpec((D_MODEL, tn), lambda i, j: (0, j)), pl.BlockSpec((1, tn), lambda i, j: (0, j))],
        out_specs=pl.BlockSpec((tm, tn), lambda i, j: (i, j)),
        out_shape=jax.ShapeDtypeStruct((T, N), out_dtype),
        scratch_shapes=[pltpu.VMEM((tm, D_MODEL), BF16)],
        compiler_params=_cparams("parallel", "arbitrary"),
    )(x, g, w, b)


def matmul(a, w, out_dtype, name, res=None):
    T, K = a.shape
    N = w.shape[1]
    tm = min(TM, T)

    def body(*refs):
        if res is None:
            a_ref, w_ref, o_ref = refs
            acc = _dot(a_ref[...].astype(BF16), w_ref[...])
        else:
            a_ref, w_ref, r_ref, o_ref = refs
            acc = r_ref[...] + _dot(a_ref[...].astype(BF16), w_ref[...])
        o_ref[...] = acc.astype(o_ref.dtype)

    in_specs = [pl.BlockSpec((tm, K), lambda i: (i, 0)), pl.BlockSpec((K, N), lambda i: (0, 0))]
    ops = [a, w]
    if res is not None:
        in_specs.append(pl.BlockSpec((tm, N), lambda i: (i, 0)))
        ops.append(res)
    return pl.pallas_call(
        body, name=name, grid=(T // tm,), in_specs=in_specs,
        out_specs=pl.BlockSpec((tm, N), lambda i: (i, 0)),
        out_shape=jax.ShapeDtypeStruct((T, N), out_dtype),
        compiler_params=_cparams("parallel"),
    )(*ops)


def tn_matmul(a, b, name, colsum=False):
    T, M = a.shape
    N = b.shape[1]
    tk = min(TK, T)
    tmm = min(1024, M)
    tn = N if N <= 1024 else (1024 if N % 1024 == 0 else N)

    def body(a_ref, b_ref, o_ref, *rest):
        k = pl.program_id(2)
        bv = b_ref[...]

        @pl.when(k == 0)
        def _():
            o_ref[...] = jnp.zeros(o_ref.shape, F32)
            if colsum:
                rest[0][...] = jnp.zeros(rest[0].shape, F32)

        o_ref[...] += _tn(a_ref[...].astype(BF16), bv.astype(BF16))
        if colsum:
            rest[0][...] += jnp.sum(bv.astype(F32), axis=0, keepdims=True)

    out_specs = [pl.BlockSpec((tmm, tn), lambda i, j, k: (i, j))]
    out_shape = [jax.ShapeDtypeStruct((M, N), F32)]
    if colsum:
        assert M == tmm
        out_specs.append(pl.BlockSpec((1, tn), lambda i, j, k: (0, j)))
        out_shape.append(jax.ShapeDtypeStruct((1, N), F32))
    out = pl.pallas_call(
        body, name=name, grid=(M // tmm, N // tn, T // tk),
        in_specs=[pl.BlockSpec((tk, tmm), lambda i, j, k: (k, i)), pl.BlockSpec((tk, tn), lambda i, j, k: (k, j))],
        out_specs=out_specs, out_shape=out_shape,
        compiler_params=_cparams("parallel", "parallel", "arbitrary"),
    )(a, b)
    return out if colsum else out[0]


def mlp_fwd(x, g, w_up, w_down, name):
    T, F = x.shape[0], w_up.shape[1]
    tm, tf = min(TM, T), min(TF, F)
    nf = F // tf

    def body(x_ref, g_ref, wu_ref, wd_ref, o_ref, h_sc, acc_sc):
        f = pl.program_id(1)

        @pl.when(f == 0)
        def _():
            xv = x_ref[...]
            h_sc[...] = (xv * _rms(xv) * g_ref[...]).astype(BF16)
            acc_sc[...] = xv

        u = jnp.maximum(_dot(h_sc[...], wu_ref[...]), 0.0)
        acc_sc[...] += _dot((u * u).astype(BF16), wd_ref[...])

        @pl.when(f == nf - 1)
        def _():
            o_ref[...] = acc_sc[...]

    return pl.pallas_call(
        body, name=name, grid=(T // tm, nf),
        in_specs=[pl.BlockSpec((tm, D_MODEL), lambda i, f: (i, 0)), pl.BlockSpec((1, D_MODEL), lambda i, f: (0, 0)),
                  pl.BlockSpec((D_MODEL, tf), lambda i, f: (0, f)), pl.BlockSpec((tf, D_MODEL), lambda i, f: (f, 0))],
        out_specs=pl.BlockSpec((tm, D_MODEL), lambda i, f: (i, 0)),
        out_shape=jax.ShapeDtypeStruct((T, D_MODEL), F32),
        scratch_shapes=[pltpu.VMEM((tm, D_MODEL), BF16), pltpu.VMEM((tm, D_MODEL), F32)],
        compiler_params=_cparams("parallel", "arbitrary"),
    )(x, g, w_up, w_down)


def mlp_bwd(x, g, w_up, w_up_t, w_down_t, dy, name):
    T, F = x.shape[0], w_up.shape[1]
    tm, tf = min(TM, T), min(TF, F)
    nf = F // tf

    def body(x_ref, g_ref, wu_ref, wut_ref, wdt_ref, dy_ref, dx_ref, h_ref, a_ref, du_ref, dg_ref, h_sc, dyb_sc, dh_sc):
        i, f = pl.program_id(0), pl.program_id(1)

        @pl.when(f == 0)
        def _():
            xv = x_ref[...]
            h = (xv * _rms(xv) * g_ref[...]).astype(BF16)
            h_sc[...] = h
            h_ref[...] = h
            dyb_sc[...] = dy_ref[...].astype(BF16)
            dh_sc[...] = jnp.zeros(dh_sc.shape, F32)

        @pl.when((i == 0) & (f == 0))
        def _():
            dg_ref[...] = jnp.zeros(dg_ref.shape, F32)

        u = jnp.maximum(_dot(h_sc[...], wu_ref[...]), 0.0)
        a_ref[...] = (u * u).astype(BF16)
        du = (_dot(dyb_sc[...], wdt_ref[...]) * (2.0 * u)).astype(BF16)
        du_ref[...] = du
        dh_sc[...] += _dot(du, wut_ref[...])

        @pl.when(f == nf - 1)
        def _():
            dx, dg = _rms_bwd(x_ref[...], g_ref[...], dh_sc[...])
            dx_ref[...] = dy_ref[...] + dx
            dg_ref[...] += dg

    row = pl.BlockSpec((tm, D_MODEL), lambda i, f: (i, 0))
    hid = pl.BlockSpec((tm, tf), lambda i, f: (i, f))
    return pl.pallas_call(
        body, name=name, grid=(T // tm, nf),
        in_specs=[row, pl.BlockSpec((1, D_MODEL), lambda i, f: (0, 0)),
                  pl.BlockSpec((D_MODEL, tf), lambda i, f: (0, f)), pl.BlockSpec((tf, D_MODEL), lambda i, f: (f, 0)),
                  pl.BlockSpec((D_MODEL, tf), lambda i, f: (0, f)), row],
        out_specs=[row, row, hid, hid, pl.BlockSpec((1, D_MODEL), lambda i, f: (0, 0))],
        out_shape=[jax.ShapeDtypeStruct((T, D_MODEL), F32), jax.ShapeDtypeStruct((T, D_MODEL), BF16),
                   jax.ShapeDtypeStruct((T, F), BF16), jax.ShapeDtypeStruct((T, F), BF16),
                   jax.ShapeDtypeStruct((1, D_MODEL), F32)],
        scratch_shapes=[pltpu.VMEM((tm, D_MODEL), BF16), pltpu.VMEM((tm, D_MODEL), BF16),
                        pltpu.VMEM((tm, D_MODEL), F32)],
        compiler_params=_cparams("arbitrary", "arbitrary"),
    )(x, g, w_up, w_up_t, w_down_t, dy)


def proj_bwd(x, g, dres, parts, name):
    T = x.shape[0]
    tm = min(TM, T)
    n = len(parts)

    def body(*refs):
        x_ref, g_ref, dr_ref = refs[:3]
        da_refs, wt_refs = refs[3:3 + n], refs[3 + n:3 + 2 * n]
        dx_ref, h_ref, dg_ref = refs[3 + 2 * n:]

        @pl.when(pl.program_id(0) == 0)
        def _():
            dg_ref[...] = jnp.zeros(dg_ref.shape, F32)

        xv = x_ref[...]
        dh = _dot(da_refs[0][...].astype(BF16), wt_refs[0][...])
        for a_ref, w_ref in zip(da_refs[1:], wt_refs[1:]):
            dh = dh + _dot(a_ref[...].astype(BF16), w_ref[...])
        h_ref[...] = (xv * _rms(xv) * g_ref[...]).astype(BF16)
        dx, dg = _rms_bwd(xv, g_ref[...], dh)
        dx_ref[...] = dr_ref[...] + dx
        dg_ref[...] += dg

    row = pl.BlockSpec((tm, D_MODEL), lambda i: (i, 0))
    one = pl.BlockSpec((1, D_MODEL), lambda i: (0, 0))
    in_specs = [row, one, row]
    in_specs += [pl.BlockSpec((tm, da.shape[1]), lambda i: (i, 0)) for da, _ in parts]
    in_specs += [pl.BlockSpec(wt.shape, lambda i: (0, 0)) for _, wt in parts]
    return pl.pallas_call(
        body, name=name, grid=(T // tm,), in_specs=in_specs,
        out_specs=[row, row, one],
        out_shape=[jax.ShapeDtypeStruct((T, D_MODEL), F32), jax.ShapeDtypeStruct((T, D_MODEL), BF16),
                   jax.ShapeDtypeStruct((1, D_MODEL), F32)],
        compiler_params=_cparams("arbitrary"),
    )(x, g, dres, *[da for da, _ in parts], *[wt for _, wt in parts])


def final_loss(x, g, tgt, name):
    T = x.shape[0]
    tm = min(TM, T)

    def body(x_ref, g_ref, t_ref, l_ref, dx_ref, dg_ref):
        @pl.when(pl.program_id(0) == 0)
        def _():
            l_ref[...] = jnp.zeros(l_ref.shape, F32)
            dg_ref[...] = jnp.zeros(dg_ref.shape, F32)

        xv = x_ref[...]
        gv = g_ref[...]
        err = xv * _rms(xv) * gv - t_ref[...]
        l_ref[...] += 0.5 * jnp.sum(jnp.mean(err * err, axis=-1, keepdims=True), axis=0, keepdims=True)
        dx, dg = _rms_bwd(xv, gv, err * (1.0 / D_MODEL))
        dx_ref[...] = dx
        dg_ref[...] += dg

    row = pl.BlockSpec((tm, D_MODEL), lambda i: (i, 0))
    one = pl.BlockSpec((1, D_MODEL), lambda i: (0, 0))
    return pl.pallas_call(
        body, name=name, grid=(T // tm,), in_specs=[row, one, row],
        out_specs=[pl.BlockSpec((8, LANES), lambda i: (0, 0)), row, one],
        out_shape=[jax.ShapeDtypeStruct((8, LANES), F32), jax.ShapeDtypeStruct((T, D_MODEL), F32),
                   jax.ShapeDtypeStruct((1, D_MODEL), F32)],
        compiler_params=_cparams("arbitrary"),
    )(x, g, tgt)


def _swa_specs(tq):
    r = tq // SWA_WINDOW
    cur = lambda ix: pl.BlockSpec((tq, LANES), lambda kv, i: (ix(i), kv))
    prev = lambda ix: pl.BlockSpec((SWA_WINDOW, LANES), lambda kv, i: (jnp.maximum(ix(i) * r - 1, 0), kv))
    return cur, prev


W2 = 2 * SWA_WINDOW
SWA_RB = 32


def _swa_visible(tile):
    r = lax.broadcasted_iota(jnp.int32, (SWA_WINDOW, W2), 0)
    c = lax.broadcasted_iota(jnp.int32, (SWA_WINDOW, W2), 1)
    inside = (c > r) & (c <= r + SWA_WINDOW)
    return inside & ((c >= SWA_WINDOW) | (tile > 0)), inside


def swa_fwd(qkv, kdup, vdup, sinks_b, name):
    T = qkv.shape[0]
    tq = min(SWA_TQ, T)
    nsub = tq // SWA_WINDOW
    cur, prev = _swa_specs(tq)
    ident = lambda i: i

    def body(q_ref, kc_ref, kp_ref, vc_ref, vp_ref, sk_ref, o_ref, lse_ref, s_sc, e_sc):
        i = pl.program_id(1)
        kcat = jnp.concatenate([kp_ref[...], kc_ref[...]], axis=0)
        vcat = jnp.concatenate([vp_ref[...], vc_ref[...]], axis=0)
        vis_first, vis_in = _swa_visible(i)
        lane = lax.broadcasted_iota(jnp.int32, (1, LANES), 1)
        lse_all = jnp.zeros((tq, LANES), F32)
        for pp in range(2):
            q2 = q_ref[:, pp * LANES:(pp + 1) * LANES]
            outs = []
            for hf in range(2):
                g = 2 * pp + hf
                qm = jnp.where(_half(lane, hf), q2, jnp.zeros_like(q2))
                for nb in range(nsub):
                    rows = slice(nb * SWA_WINDOW, (nb + 1) * SWA_WINDOW)
                    s = _nt(qm[rows], kcat[nb * SWA_WINDOW:nb * SWA_WINDOW + W2]) * 0.125
                    s_sc[rows, :] = jnp.where(vis_first if nb == 0 else vis_in, s, -1e30)
                sk = sk_ref[g:g + 1, 0:1]
                m = jnp.maximum(jnp.max(s_sc[...], axis=1, keepdims=True), sk)
                m_rep = jnp.broadcast_to(m, (tq, LANES))
                parts = []
                for r0 in range(0, tq, SWA_RB):
                    rs = slice(r0, r0 + SWA_RB)
                    e0 = jnp.exp(s_sc[rs, 0:LANES] - m_rep[rs])
                    e1 = jnp.exp(s_sc[rs, LANES:W2] - m_rep[rs])
                    e_sc[rs, :] = jnp.concatenate([e0.astype(BF16), e1.astype(BF16)], axis=1)
                    parts.append(e0 + e1)
                den = jnp.sum(jnp.concatenate(parts, axis=0), axis=1, keepdims=True) + jnp.exp(sk - m)
                pv = [_dot(e_sc[nb * SWA_WINDOW:(nb + 1) * SWA_WINDOW, :], vcat[nb * SWA_WINDOW:nb * SWA_WINDOW + W2])
                      for nb in range(nsub)]
                outs.append(jnp.concatenate(pv, axis=0) * (1.0 / den))
                lse_all = jnp.where(lane == g, m + jnp.log(den), lse_all)
            o_ref[:, pp * LANES:(pp + 1) * LANES] = jnp.where(lane < 64, outs[0], outs[1]).astype(BF16)
        lse_ref[...] = lse_all

    return pl.pallas_call(
        body, name=name, grid=(4, T // tq),
        in_specs=[pl.BlockSpec((tq, 2 * LANES), lambda kv, i: (i, kv)), cur(ident), prev(ident), cur(ident), prev(ident),
                  pl.BlockSpec((None, 8, LANES), lambda kv, i: (kv, 0, 0))],
        out_specs=[pl.BlockSpec((tq, 2 * LANES), lambda kv, i: (i, kv)), cur(ident)],
        out_shape=[jax.ShapeDtypeStruct((T, D_MODEL), BF16), jax.ShapeDtypeStruct((T, 4 * LANES), F32)],
        scratch_shapes=[pltpu.VMEM((tq, W2), F32), pltpu.VMEM((tq, W2), BF16)],
        compiler_params=_cparams("parallel", "arbitrary"),
    )(qkv, kdup, kdup, vdup, vdup, sinks_b)


def swa_bwd(qkv, kdup, vdup, sinks_b, o, lse, do, name):
    T = qkv.shape[0]
    tq = min(SWA_TQ, T)
    n = T // tq
    nsub = tq // SWA_WINDOW
    cur, prev = _swa_specs(tq)
    rev = lambda i: n - 1 - i

    def body(q_ref, kc_ref, kp_ref, vc_ref, vp_ref, sk_ref, o_ref, lse_ref, do_ref, dq_ref, dk_ref, dv_ref, dsk_ref,
             ck_sc, cv_sc, dkc_sc, dvc_sc):
        i = pl.program_id(1)

        @pl.when(i == 0)
        def _():
            ck_sc[...] = jnp.zeros(ck_sc.shape, F32)
            cv_sc[...] = jnp.zeros(cv_sc.shape, F32)
            dsk_ref[...] = jnp.zeros(dsk_ref.shape, F32)

        kcat = jnp.concatenate([kp_ref[...], kc_ref[...]], axis=0)
        vcat = jnp.concatenate([vp_ref[...], vc_ref[...]], axis=0)
        vis_first, vis_in = _swa_visible(n - 1 - i)
        lane = lax.broadcasted_iota(jnp.int32, (1, LANES), 1)
        dkc_sc[...] = jnp.zeros(dkc_sc.shape, F32)
        dvc_sc[...] = jnp.zeros(dvc_sc.shape, F32)
        for pp in range(2):
            sl = slice(pp * LANES, (pp + 1) * LANES)
            q2, do2, o2 = q_ref[:, sl], do_ref[:, sl], o_ref[:, sl]
            dqs = []
            for hf in range(2):
                g = 2 * pp + hf
                lm = _half(lane, hf)
                qm = jnp.where(lm, q2, jnp.zeros_like(q2))
                dom = jnp.where(lm, do2, jnp.zeros_like(do2))
                delta = jnp.sum(dom.astype(F32) * o2.astype(F32), axis=1, keepdims=True)
                lse_g = lse_ref[:, g:g + 1]
                psk = jnp.exp(sk_ref[g:g + 1, 0:1] - lse_g)
                dsk_ref[g:g + 1, :] += jnp.zeros((1, LANES), F32) - jnp.sum(psk * delta, axis=0, keepdims=True)
                dq_parts = []
                for nb in range(nsub):
                    rows = slice(nb * SWA_WINDOW, (nb + 1) * SWA_WINDOW)
                    band = slice(nb * SWA_WINDOW, nb * SWA_WINDOW + W2)
                    s = jnp.where(vis_first if nb == 0 else vis_in, _nt(qm[rows], kcat[band]) * 0.125, -1e30)
                    p = jnp.exp(s - lse_g[rows])
                    dsb = (p * (_nt(dom[rows], vcat[band]) - delta[rows]) * 0.125).astype(BF16)
                    dq_parts.append(_dot(dsb, kcat[band]))
                    dkc_sc[band, :] += _tn(dsb, qm[rows])
                    dvc_sc[band, :] += _tn(p.astype(BF16), dom[rows])
                dqs.append(jnp.concatenate(dq_parts, axis=0))
            dq_ref[:, sl] = jnp.where(lane < 64, dqs[0], dqs[1]).astype(BF16)
        dkc = dkc_sc[...]
        dvc = dvc_sc[...]
        dkc = dkc + pltpu.roll(dkc, 64, 1)
        dvc = dvc + pltpu.roll(dvc, 64, 1)
        for full, ref, carry in ((dkc, dk_ref, ck_sc), (dvc, dv_ref, cv_sc)):
            if tq > SWA_WINDOW:
                ref[0:tq - SWA_WINDOW, :] = full[SWA_WINDOW:tq, :]
            ref[tq - SWA_WINDOW:tq, :] = full[tq:tq + SWA_WINDOW, :] + carry[...]
            carry[...] = full[0:SWA_WINDOW, :]

    wide = pl.BlockSpec((tq, 2 * LANES), lambda kv, i: (rev(i), kv))
    return pl.pallas_call(
        body, name=name, grid=(4, n),
        in_specs=[wide, cur(rev), prev(rev), cur(rev), prev(rev),
                  pl.BlockSpec((None, 8, LANES), lambda kv, i: (kv, 0, 0)), wide, cur(rev), wide],
        out_specs=[wide, cur(rev), cur(rev), pl.BlockSpec((None, 8, LANES), lambda kv, i: (kv, 0, 0))],
        out_shape=[jax.ShapeDtypeStruct((T, D_MODEL), BF16), jax.ShapeDtypeStruct((T, 4 * LANES), F32),
                   jax.ShapeDtypeStruct((T, 4 * LANES), F32), jax.ShapeDtypeStruct((4, 8, LANES), F32)],
        scratch_shapes=[pltpu.VMEM((SWA_WINDOW, LANES), F32), pltpu.VMEM((SWA_WINDOW, LANES), F32),
                        pltpu.VMEM((tq + SWA_WINDOW, LANES), F32), pltpu.VMEM((tq + SWA_WINDOW, LANES), F32)],
        compiler_params=_cparams("arbitrary", "arbitrary"),
    )(qkv, kdup, kdup, vdup, vdup, sinks_b, o, lse, do)


def fox_gate_fwd(fl, qkv, name):
    T = fl.shape[0]
    ts = min(SCAN_T, T)
    per_tile = min(FOX_T, T) // ts

    def body(fl_ref, q_ref, k_ref, qa_ref, ka_ref, st_ref, carry):
        @pl.when(pl.program_id(0) == 0)
        def _():
            carry[...] = jnp.zeros(carry.shape, F32)

        xv = fl_ref[...]
        ls = jnp.minimum(xv, 0.0) - jnp.log(1.0 + jnp.exp(-jnp.abs(xv)))
        tri = (lax.broadcasted_iota(jnp.int32, (ts, ts), 0) >= lax.broadcasted_iota(jnp.int32, (ts, ts), 1)).astype(F32)
        cs = jnp.dot(tri, ls, precision=HI, preferred_element_type=F32) + carry[...]
        carry[...] = cs[ts - 1:ts, :]
        c1 = cs.astype(BF16).astype(F32)
        c2 = (cs - c1).astype(BF16).astype(F32)
        c3 = (cs - c1 - c2).astype(BF16).astype(F32)
        lane = lax.broadcasted_iota(jnp.int32, (1, LANES), 1)
        ones_q = jnp.where((lane >= 67) & (lane < 70), 1.0, 0.0)
        ones_k = jnp.where((lane >= 64) & (lane < 67), 1.0, 0.0)
        nq = jnp.zeros((1, LANES), F32)
        nk = jnp.zeros((1, LANES), F32)
        for b in range(8):
            qf = q_ref[:, b * LANES:(b + 1) * LANES].astype(F32) * 0.125
            kf = k_ref[:, b * LANES:(b + 1) * LANES].astype(F32)
            for hf in range(2):
                h = 2 * b + hf
                a1, a2, a3 = c1[:, h:h + 1], c2[:, h:h + 1], c3[:, h:h + 1]
                aux_q = jnp.where(lane == 64, a1, jnp.where(lane == 65, a2, jnp.where(lane == 66, a3, ones_q)))
                aux_k = jnp.where(lane == 67, -a1, jnp.where(lane == 68, -a2, jnp.where(lane == 69, -a3, ones_k)))
                qs = qf if hf == 0 else pltpu.roll(qf, 64, 1)
                ks = kf if hf == 0 else pltpu.roll(kf, 64, 1)
                qa_ref[:, h * LANES:(h + 1) * LANES] = jnp.where(lane < 64, qs, aux_q).astype(BF16)
                ka_ref[:, h * LANES:(h + 1) * LANES] = jnp.where(lane < 64, ks, aux_k).astype(BF16)
                for src, is_q in ((qf, True), (kf, False)):
                    sq = jnp.sum(jnp.where(_half(lane, hf), src * src, 0.0), axis=1, keepdims=True)
                    big = jnp.sqrt(jnp.max(sq, axis=0, keepdims=True))
                    if is_q:
                        nq = jnp.where(lane == h, big, nq)
                    else:
                        nk = jnp.where(lane == h, big, nk)
        new = jnp.concatenate([nq, nk, jnp.max(cs, axis=0, keepdims=True), jnp.min(cs, axis=0, keepdims=True),
                               jnp.zeros((4, LANES), F32)], axis=0)
        first = pl.program_id(0) % per_tile == 0
        row = lax.broadcasted_iota(jnp.int32, (8, LANES), 0)

        @pl.when(first)
        def _():
            st_ref[...] = new

        @pl.when(jnp.logical_not(first))
        def _():
            old = st_ref[...]
            st_ref[...] = jnp.where(row == 3, jnp.minimum(old, new), jnp.maximum(old, new))

    out = pl.BlockSpec((ts, 16 * LANES), lambda i: (i, 0))
    return pl.pallas_call(
        body, name=name, grid=(T // ts,),
        in_specs=[pl.BlockSpec((ts, LANES), lambda i: (i, 0)), pl.BlockSpec((ts, D_MODEL), lambda i: (i, 0)),
                  pl.BlockSpec((ts, D_MODEL), lambda i: (i, 1))],
        out_specs=[out, out, pl.BlockSpec((None, 8, LANES), lambda i: (i // per_tile, 0, 0))],
        out_shape=[jax.ShapeDtypeStruct((T, 16 * LANES), BF16), jax.ShapeDtypeStruct((T, 16 * LANES), BF16),
                   jax.ShapeDtypeStruct((T // ts // per_tile, 8, LANES), F32)],
        scratch_shapes=[pltpu.VMEM((1, LANES), F32)],
        compiler_params=_cparams("arbitrary"),
    )(fl, qkv, qkv)


def fox_gate_bwd(fl, dc, name):
    T = fl.shape[0]
    ts = min(SCAN_T, T)
    n = T // ts

    def body(fl_ref, dc_ref, o_ref, carry):
        @pl.when(pl.program_id(0) == 0)
        def _():
            carry[...] = jnp.zeros(carry.shape, F32)

        tri = (lax.broadcasted_iota(jnp.int32, (ts, ts), 0) <= lax.broadcasted_iota(jnp.int32, (ts, ts), 1)).astype(F32)
        rs = jnp.dot(tri, dc_ref[...], precision=HI, preferred_element_type=F32) + carry[...]
        carry[...] = rs[0:1, :]
        o_ref[...] = rs * (1.0 / (1.0 + jnp.exp(fl_ref[...])))

    blk = pl.BlockSpec((ts, LANES), lambda i: (n - 1 - i, 0))
    return pl.pallas_call(
        body, name=name, grid=(n,), in_specs=[blk, blk], out_specs=blk,
        out_shape=jax.ShapeDtypeStruct((T, LANES), F32), scratch_shapes=[pltpu.VMEM((1, LANES), F32)],
        compiler_params=_cparams("arbitrary"),
    )(fl, dc)


FOX_RB = 32


def _half(lane, hf):
    return (lane < 64) if hf == 0 else (lane >= 64)


def _pair(lane, a, b):
    return jnp.where(lane < 64, a, pltpu.roll(b, 64, 1)), jnp.where(lane < 64, pltpu.roll(a, 64, 1), b)


FOX_SLOTS = 2
FOX_SKIP = 110.0


def _tile_bound(st_ref, i, j, h):
    return st_ref[i, h] * st_ref[j, 16 + h] * 1.01 + (st_ref[i, 32 + h] - st_ref[j, 48 + h]) + 1.0


def fox_fwd(st, qa, ka, qkv, name):
    T = qa.shape[0]
    t = min(FOX_T, T)
    n = T // t
    ns = min(n, FOX_SLOTS)

    def body(st_ref, qa_ref, ka_ref, v_ref, ka_hbm, v_hbm, o_ref, lse_ref, lmin_ref,
             m_sc, l_sc, acc_sc, ls_sc, s_sc, p_sc, mmin_sc, kb_sc, vb_sc, in_sems):
        pair, i, jj = pl.program_id(0), pl.program_id(1), pl.program_id(2)
        j = jnp.maximum(i - jj, 0)
        lane = lax.broadcasted_iota(jnp.int32, (1, LANES), 1)

        def head_tile(hf, diag, k_src, v_src):
            hs = slice(hf * LANES, (hf + 1) * LANES)
            sv = _nt(qa_ref[:, hs], k_src[:, hs])
            if diag:
                vis = lax.broadcasted_iota(jnp.int32, (t, t), 0) >= lax.broadcasted_iota(jnp.int32, (t, t), 1)
                sv = jnp.where(vis, sv, -1e30)
            s_sc[...] = sv
            m_old = m_sc[hf]
            m_new = jnp.maximum(m_old, jnp.max(s_sc[...], axis=1, keepdims=True))
            al = jnp.exp(m_old - m_new)
            m_sc[hf] = m_new
            mmin_sc[hf] = jnp.min(m_new)
            for r0 in range(0, t, FOX_RB):
                rs = slice(r0, r0 + FOX_RB)
                mrow = m_new[rs, :]
                part, pieces = None, []
                for cb in range(0, t, LANES):
                    pc = jnp.exp(s_sc[rs, cb:cb + LANES] - mrow)
                    part = pc if part is None else part + pc
                    pieces.append(pc.astype(BF16))
                p_sc[rs, :] = jnp.concatenate(pieces, axis=1)
                ls_sc[rs, :] = part
            l_sc[hf] = al * l_sc[hf] + ls_sc[...]
            acc_sc[hf] = al * acc_sc[hf] + _dot(p_sc[...], v_src[...])

        @pl.when(jj == 0)
        def _():
            m_sc[...] = jnp.full(m_sc.shape, -1e30, F32)
            l_sc[...] = jnp.zeros(l_sc.shape, F32)
            acc_sc[...] = jnp.zeros(acc_sc.shape, F32)
            head_tile(0, True, ka_ref, v_ref)
            head_tile(1, True, ka_ref, v_ref)

        def live(hf, key_tile):
            return _tile_bound(st_ref, i, key_tile, 2 * pair + hf) >= mmin_sc[hf] - FOX_SKIP

        for hf in range(2):
            @pl.when((jj > 0) & (jj <= i) & live(hf, j))
            def _():
                head_tile(hf, False, ka_ref, v_ref)

        @pl.when(jj == ns - 1)
        def _():
            def older(r, carry):
                j2 = i - ns - r

                @pl.when(live(0, j2) | live(1, j2))
                def _():
                    rows = pl.ds(pl.multiple_of(j2 * t, t), t)
                    ck = pltpu.make_async_copy(
                        ka_hbm.at[rows, pl.ds(pl.multiple_of(pair * 2 * LANES, 2 * LANES), 2 * LANES)], kb_sc, in_sems.at[0])
                    cv = pltpu.make_async_copy(
                        v_hbm.at[rows, pl.ds(pl.multiple_of((16 + pair) * LANES, LANES), LANES)], vb_sc, in_sems.at[1])
                    ck.start()
                    cv.start()
                    ck.wait()
                    cv.wait()
                    for hf in range(2):
                        @pl.when(live(hf, j2))
                        def _():
                            head_tile(hf, False, kb_sc, vb_sc)
                return carry

            lax.fori_loop(0, jnp.maximum(i - ns + 1, 0), older, 0)
            l0 = jnp.sum(l_sc[0], axis=1, keepdims=True)
            l1 = jnp.sum(l_sc[1], axis=1, keepdims=True)
            lse0, lse1 = m_sc[0] + jnp.log(l0), m_sc[1] + jnp.log(l1)
            o_ref[...] = jnp.where(lane < 64, acc_sc[0] / l0, acc_sc[1] / l1).astype(BF16)
            lse_ref[...] = jnp.where(lane < 64, lse0, lse1)
            row = lax.broadcasted_iota(jnp.int32, (8, LANES), 0)
            lmin_ref[...] = jnp.where(row == 0, jnp.min(lse0), jnp.where(row == 1, jnp.min(lse1), 0.0))

    oblk = pl.BlockSpec((t, LANES), lambda p, i, jj, st: (i, p))
    return pl.pallas_call(
        body, name=name,
        grid_spec=pltpu.PrefetchScalarGridSpec(
            num_scalar_prefetch=1, grid=(8, n, ns),
            in_specs=[pl.BlockSpec((t, 2 * LANES), lambda p, i, jj, st: (i, p)),
                      pl.BlockSpec((t, 2 * LANES), lambda p, i, jj, st: (jnp.maximum(i - jj, 0), p)),
                      pl.BlockSpec((t, LANES), lambda p, i, jj, st: (jnp.maximum(i - jj, 0), 16 + p)),
                      pl.BlockSpec(memory_space=pl.ANY), pl.BlockSpec(memory_space=pl.ANY)],
            out_specs=[oblk, oblk, pl.BlockSpec((None, None, 8, LANES), lambda p, i, jj, st: (i, p, 0, 0))],
            scratch_shapes=[pltpu.VMEM((2, t, LANES), F32), pltpu.VMEM((2, t, LANES), F32),
                            pltpu.VMEM((2, t, LANES), F32), pltpu.VMEM((t, LANES), F32), pltpu.VMEM((t, t), F32),
                            pltpu.VMEM((t, t), BF16), pltpu.SMEM((2,), F32), pltpu.VMEM((t, 2 * LANES), BF16),
                            pltpu.VMEM((t, LANES), BF16), pltpu.SemaphoreType.DMA((2,))]),
        out_shape=[jax.ShapeDtypeStruct((T, D_MODEL), BF16), jax.ShapeDtypeStruct((T, D_MODEL), F32),
                   jax.ShapeDtypeStruct((n, 8, 8, LANES), F32)],
        compiler_params=_cparams("parallel", "parallel", "arbitrary"),
    )(st, qa, ka, qkv, ka, qkv)


def fox_delta(do, o, name):
    T = do.shape[0]
    tm = min(TM, T)

    def body(do_ref, o_ref, d_ref):
        lane = lax.broadcasted_iota(jnp.int32, (1, LANES), 1)
        out = jnp.zeros((tm, LANES), F32)
        for b in range(8):
            d = do_ref[:, b * LANES:(b + 1) * LANES].astype(F32) * o_ref[:, b * LANES:(b + 1) * LANES].astype(F32)
            for hf in range(2):
                out = jnp.where(lane == 2 * b + hf, jnp.sum(jnp.where(_half(lane, hf), d, 0.0), axis=1, keepdims=True), out)
        d_ref[...] = out

    row = pl.BlockSpec((tm, D_MODEL), lambda i: (i, 0))
    return pl.pallas_call(
        body, name=name, grid=(T // tm,), in_specs=[row, row],
        out_specs=pl.BlockSpec((tm, LANES), lambda i: (i, 0)),
        out_shape=jax.ShapeDtypeStruct((T, LANES), F32),
        compiler_params=_cparams("parallel"),
    )(do, o)


def fox_bwd(st, qa, ka, qkv, lse_row, delta_row, do, name):
    T = qa.shape[0]
    t = min(FOX_T, T)
    n = T // t

    ns = min(n, FOX_SLOTS)

    def body(st_ref, qa_ref, ka_ref, v_ref, lr_ref, dr_ref, do_ref, qa_hbm, do_hbm, lr_hbm, dr_hbm,
             dq_ref, auxq_ref, dk_ref, dv_ref, aux_ref,
             dq_sc, dk_sc, dv_sc, s_sc, dp_sc, p_sc, ds_sc, dqo_sc, auxo_sc, out_sems, qb_sc, dob_sc, lrb_sc, drb_sc, in_sems):
        pair, j, ii = pl.program_id(0), pl.program_id(1), pl.program_id(2)
        i = j + ii
        lane = lax.broadcasted_iota(jnp.int32, (1, LANES), 1)

        @pl.when(ii == 0)
        def _():
            dk_sc[...] = jnp.zeros(dk_sc.shape, F32)
            dv_sc[...] = jnp.zeros(dv_sc.shape, F32)

        @pl.when((j == 0) & (ii == 0))
        def _():
            dq_sc[...] = jnp.zeros(dq_sc.shape, F32)

        def head_tile(hf, diag, q_tile, q_src, do_src, lr_src, dr_src):
            qrows = pl.ds(pl.multiple_of(q_tile * t, t), t)
            v2, do2 = v_ref[...], do_src[...]
            hs = slice(hf * LANES, (hf + 1) * LANES)
            lm = _half(lane, hf)
            qh = q_src[:, hs]
            s_sc[...] = _nt(ka_ref[:, hs], qh)
            dp_sc[...] = _nt(jnp.where(lm, v2, jnp.zeros_like(v2)), do2)
            lrow, drow = lr_src[hf:hf + 1, :], dr_src[hf:hf + 1, :]
            for r0 in range(0, t, FOX_RB):
                rs = slice(r0, r0 + FOX_RB)
                sv = s_sc[rs, :]
                if diag:
                    vis = lax.broadcasted_iota(jnp.int32, (FOX_RB, t), 1) >= (r0 + lax.broadcasted_iota(jnp.int32, (FOX_RB, t), 0))
                    sv = jnp.where(vis, sv, -1e30)
                p = jnp.exp(sv - lrow)
                p_sc[rs, :] = p.astype(BF16)
                ds_sc[rs, :] = (p * (dp_sc[rs, :] - drow)).astype(BF16)
            dv_sc[...] += _dot(p_sc[...], jnp.where(lm, do2, jnp.zeros_like(do2)))
            dk_sc[hf] += _dot(ds_sc[...], qh)
            dq_sc[hf, qrows, :] += _tn(ds_sc[...], ka_ref[:, hs])

        def live(hf, q_tile):
            h = 2 * pair + hf
            return _tile_bound(st_ref, q_tile, j, h) >= st_ref[q_tile, 64 + h] - FOX_SKIP

        @pl.when(ii == 0)
        def _():
            head_tile(0, True, j, qa_ref, do_ref, lr_ref, dr_ref)
            head_tile(1, True, j, qa_ref, do_ref, lr_ref, dr_ref)
            qrows = pl.ds(pl.multiple_of(j * t, t), t)
            dq, aux = _pair(lane, dq_sc[0, qrows, :], dq_sc[1, qrows, :])
            dqo_sc[...] = (dq * 0.125).astype(BF16)
            auxo_sc[...] = aux
            cols = pl.ds(pl.multiple_of(pair * LANES, LANES), LANES)
            c1 = pltpu.make_async_copy(dqo_sc, dq_ref.at[qrows, cols], out_sems.at[0])
            c2 = pltpu.make_async_copy(auxo_sc, auxq_ref.at[qrows, cols], out_sems.at[1])
            c1.start()
            c2.start()
            c1.wait()
            c2.wait()

        q_tile = jnp.minimum(i, n - 1)
        for hf in range(2):
            @pl.when((ii > 0) & (i <= n - 1) & live(hf, q_tile))
            def _():
                head_tile(hf, False, q_tile, qa_ref, do_ref, lr_ref, dr_ref)

        @pl.when(ii == ns - 1)
        def _():
            def later(r, carry):
                i2 = j + ns + r

                @pl.when(live(0, i2) | live(1, i2))
                def _():
                    rows = pl.ds(pl.multiple_of(i2 * t, t), t)
                    cps = [pltpu.make_async_copy(
                               qa_hbm.at[rows, pl.ds(pl.multiple_of(pair * 2 * LANES, 2 * LANES), 2 * LANES)], qb_sc, in_sems.at[0]),
                           pltpu.make_async_copy(
                               do_hbm.at[rows, pl.ds(pl.multiple_of(pair * LANES, LANES), LANES)], dob_sc, in_sems.at[1]),
                           pltpu.make_async_copy(lr_hbm.at[pair, :, rows], lrb_sc, in_sems.at[2]),
                           pltpu.make_async_copy(dr_hbm.at[pair, :, rows], drb_sc, in_sems.at[3])]
                    for cp in cps:
                        cp.start()
                    for cp in cps:
                        cp.wait()
                    for hf in range(2):
                        @pl.when(live(hf, i2))
                        def _():
                            head_tile(hf, False, i2, qb_sc, dob_sc, lrb_sc, drb_sc)
                return carry

            lax.fori_loop(0, jnp.maximum(n - ns - j, 0), later, 0)
            dk, aux = _pair(lane, dk_sc[0], dk_sc[1])
            dk_ref[...] = dk.astype(BF16)
            aux_ref[...] = aux
            dv_ref[...] = dv_sc[...].astype(BF16)

    qix = lambda j, ii: jnp.minimum(j + ii, n - 1)
    qblk = pl.BlockSpec((t, LANES), lambda p, j, ii, st: (qix(j, ii), p))
    kblk = pl.BlockSpec((t, LANES), lambda p, j, ii, st: (j, p))
    rblk = pl.BlockSpec((None, 2, t), lambda p, j, ii, st: (p, 0, qix(j, ii)))
    hbm = pl.BlockSpec(memory_space=pl.ANY)
    bf, f32 = jax.ShapeDtypeStruct((T, D_MODEL), BF16), jax.ShapeDtypeStruct((T, D_MODEL), F32)
    return pl.pallas_call(
        body, name=name,
        grid_spec=pltpu.PrefetchScalarGridSpec(
            num_scalar_prefetch=1, grid=(8, n, ns),
            in_specs=[pl.BlockSpec((t, 2 * LANES), lambda p, j, ii, st: (qix(j, ii), p)),
                      pl.BlockSpec((t, 2 * LANES), lambda p, j, ii, st: (j, p)),
                      pl.BlockSpec((t, LANES), lambda p, j, ii, st: (j, 16 + p)), rblk, rblk, qblk,
                      hbm, hbm, hbm, hbm],
            out_specs=[hbm, hbm, kblk, kblk, kblk],
            scratch_shapes=[pltpu.VMEM((2, T, LANES), F32), pltpu.VMEM((2, t, LANES), F32), pltpu.VMEM((t, LANES), F32),
                            pltpu.VMEM((t, t), F32), pltpu.VMEM((t, t), F32), pltpu.VMEM((t, t), BF16),
                            pltpu.VMEM((t, t), BF16), pltpu.VMEM((t, LANES), BF16), pltpu.VMEM((t, LANES), F32),
                            pltpu.SemaphoreType.DMA((2,)), pltpu.VMEM((t, 2 * LANES), BF16),
                            pltpu.VMEM((t, LANES), BF16), pltpu.VMEM((2, t), F32), pltpu.VMEM((2, t), F32),
                            pltpu.SemaphoreType.DMA((4,))]),
        out_shape=[bf, f32, bf, bf, f32],
        compiler_params=_cparams("arbitrary", "arbitrary", "arbitrary"),
    )(st, qa, ka, qkv, lse_row, delta_row, do, qa, do, lse_row, delta_row)


C = HGRN_CHUNK
LEVELS = (64, 32, 16, 8, 4, 2)


def _pivot(b, B, row):
    if B == C:
        return jnp.broadcast_to(b[C // 2 - 1:C // 2, :], b.shape)
    if B >= 8:
        b3 = b.reshape(C // B, B, LANES)
        return jnp.broadcast_to(b3[:, B // 2 - 1:B // 2, :], b3.shape).reshape(C, LANES)
    if B == 4:
        y = jnp.where((row & 3) == 1, b, 0.0)
        return y + pltpu.roll(y, 1, 0) + pltpu.roll(y, 2, 0) + pltpu.roll(y, C - 1, 0)
    y = jnp.where((row & 1) == 0, b, 0.0)
    return y + pltpu.roll(y, 1, 0)


def _level_factors(bcum):
    row = lax.broadcasted_iota(jnp.int32, (C, 1), 0)
    out = []
    for B in LEVELS:
        upper = (row & (B - 1)) >= B // 2
        e = jnp.exp(-jnp.abs(bcum - _pivot(bcum, B, row)))
        out.append((B, jnp.where(upper, e, 0.0), jnp.where(upper, 0.0, e)))
    return out


def _same_block(B):
    sh = B.bit_length() - 1
    r = lax.broadcasted_iota(jnp.int32, (C, C), 0)
    c = lax.broadcasted_iota(jnp.int32, (C, C), 1)
    return (r >> sh) == (c >> sh)


def _hgrn_gates(q, fl, lb):
    sg = _sigmoid(fl)
    f = lb + (1.0 - lb) * sg
    sq = _sigmoid(q)
    return sg, f, jnp.log(f), 1.0 - f, sq, q * sq


def _cumsum_rows(x, reverse=False):
    r = lax.broadcasted_iota(jnp.int32, (C, C), 0)
    c = lax.broadcasted_iota(jnp.int32, (C, C), 1)
    tri = ((r <= c) if reverse else (r >= c)).astype(F32)
    return jnp.dot(tri, x, precision=HI, preferred_element_type=F32)


def _intra(qs, k, factors):
    r = lax.broadcasted_iota(jnp.int32, (C, C), 0)
    c = lax.broadcasted_iota(jnp.int32, (C, C), 1)
    a = jnp.where(r == c, jnp.sum(qs * k, axis=1, keepdims=True), 0.0)
    ops = []
    for B, eq, ek in factors:
        ql, kl = (qs * eq).astype(BF16), (k * ek).astype(BF16)
        al = _nt(ql, kl)
        a = a + (al if B == C else jnp.where(_same_block(B), al, 0.0))
        ops.append((ql, kl))
    return a, ops


def hgrn_fwd(proj, lb, gn, name):
    T = proj.shape[0]
    tg = min(HGRN_TG, T)
    nch = tg // C

    def body(q_ref, fl_ref, v_ref, g_ref, lb_ref, gn_ref, ao_ref, o_ref, st_ref, st_sc):
        @pl.when(pl.program_id(1) == 0)
        def _():
            st_sc[...] = jnp.zeros(st_sc.shape, F32)

        lb_v, gn_v = lb_ref[...], gn_ref[...]

        def chunk(ci, carry):
            rows = pl.ds(pl.multiple_of(ci * C, C), C)
            _, f, lf, k, _, qs = _hgrn_gates(q_ref[rows, :], fl_ref[rows, :], lb_v)
            vb = v_ref[rows, :].astype(BF16)
            gv = g_ref[rows, :]
            bcum = _cumsum_rows(lf)
            blast = bcum[C - 1:C, :]
            a, _ = _intra(qs, k, _level_factors(bcum))
            st = st_sc[...]
            st_ref[ci] = st
            o = _dot(a.astype(BF16), vb) + _nt((qs * jnp.exp(bcum)).astype(BF16), st.astype(BF16))
            st_sc[...] = st * jnp.exp(blast) + _tn(vb, (k * jnp.exp(blast - bcum)).astype(BF16))
            o_ref[rows, :] = o
            ao_ref[rows, :] = (o * _rms(o) * gn_v * (gv * _sigmoid(gv))).astype(BF16)
            return carry

        lax.fori_loop(0, nch, chunk, 0, unroll=8)

    col = lambda off: pl.BlockSpec((tg, LANES), lambda h, i: (i, off + h))
    one = pl.BlockSpec((1, LANES), lambda h, i: (0, h))
    return pl.pallas_call(
        body, name=name, grid=(8, T // tg),
        in_specs=[col(0), col(8), col(16), col(24), one, one],
        out_specs=[col(0), col(0), pl.BlockSpec((None, nch, LANES, LANES), lambda h, i: (h, i, 0, 0))],
        out_shape=[jax.ShapeDtypeStruct((T, D_MODEL), BF16), jax.ShapeDtypeStruct((T, D_MODEL), F32),
                   jax.ShapeDtypeStruct((8, T // C, LANES, LANES), F32)],
        scratch_shapes=[pltpu.VMEM((LANES, LANES), F32)],
        compiler_params=_cparams("parallel", "arbitrary"),
    )(proj, proj, proj, proj, lb, gn)


def hgrn_bwd(proj, lb, gn, o_raw, states, dao, name):
    T = proj.shape[0]
    tg = min(HGRN_TG, T)
    nch = tg // C
    n = T // tg

    def body(q_ref, fl_ref, v_ref, g_ref, lb_ref, gn_ref, o_ref, st_ref, dao_ref,
             dq_ref, dfl_ref, dv_ref, dg_ref, dlb_ref, dgn_ref, dst_sc):
        @pl.when(pl.program_id(1) == 0)
        def _():
            dst_sc[...] = jnp.zeros(dst_sc.shape, F32)
            dlb_ref[...] = jnp.zeros(dlb_ref.shape, F32)
            dgn_ref[...] = jnp.zeros(dgn_ref.shape, F32)

        lb_v, gn_v = lb_ref[...], gn_ref[...]
        r64 = lax.broadcasted_iota(jnp.int32, (C, C), 0)
        c64 = lax.broadcasted_iota(jnp.int32, (C, C), 1)
        row = lax.broadcasted_iota(jnp.int32, (C, 1), 0)

        def chunk(cr, carry):
            ci = nch - 1 - cr
            rows = pl.ds(pl.multiple_of(ci * C, C), C)
            q, fl, gv = q_ref[rows, :], fl_ref[rows, :], g_ref[rows, :]
            sg, f, lf, k, sq, qs = _hgrn_gates(q, fl, lb_v)
            vb = v_ref[rows, :].astype(BF16)
            o = o_ref[rows, :]
            ro = _rms(o)
            on = o * ro
            sgg = _sigmoid(gv)
            gate = gv * sgg
            dao_v = dao_ref[rows, :].astype(F32)
            dg_ref[rows, :] = (dao_v * on * gn_v * (sgg * (1.0 + gv * (1.0 - sgg)))).astype(BF16)
            dgn_ref[...] += jnp.sum(dao_v * on * gate, axis=0, keepdims=True)
            don = dao_v * gn_v * gate
            do = ro * (don - on * jnp.mean(don * on, axis=-1, keepdims=True))
            dob = do.astype(BF16)
            bcum = _cumsum_rows(lf)
            blast = bcum[C - 1:C, :]
            factors = _level_factors(bcum)
            a, ops = _intra(qs, k, factors)
            eb = jnp.exp(bcum)
            ekb = jnp.exp(blast - bcum)
            qb = qs * eb
            kb = k * ekb
            st = st_ref[ci]
            dst = dst_sc[...]
            dstb = dst.astype(BF16)
            da = jnp.where(r64 >= c64, _nt(dob, vb), 0.0)
            dv_ref[rows, :] = (_tn(a.astype(BF16), dob) + _nt(kb.astype(BF16), dstb)).astype(BF16)
            dqb = _dot(dob, st.astype(BF16))
            dkb = _dot(vb, dstb)
            eblast = jnp.exp(blast)
            dst_sc[...] = dst * eblast + _tn(dob, qb.astype(BF16))
            dblast = eblast * jnp.sum(dst * st, axis=0, keepdims=True) + jnp.sum(dkb * kb, axis=0, keepdims=True)
            dad = jnp.sum(jnp.where(r64 == c64, da, 0.0), axis=1, keepdims=True)
            dqs = dqb * eb + dad * k
            dk = dkb * ekb + dad * qs
            dbcum = dqb * qb - dkb * kb + jnp.where(row == C - 1, dblast, 0.0)
            for (B, eq, ek), (ql, kl) in zip(factors, ops):
                dal = (da if B == C else jnp.where(_same_block(B), da, 0.0)).astype(BF16)
                dql, dkl = _dot(dal, kl), _tn(dal, ql)
                dqs = dqs + dql * eq
                dk = dk + dkl * ek
                dbcum = dbcum + (dql * ql.astype(F32) - dkl * kl.astype(F32))
            df = _cumsum_rows(dbcum, reverse=True) / f - dk
            dfl_ref[rows, :] = (df * (1.0 - lb_v) * sg * (1.0 - sg)).astype(BF16)
            dlb_ref[...] += jnp.sum(df * (1.0 - sg), axis=0, keepdims=True)
            dq_ref[rows, :] = (dqs * (sq * (1.0 + q * (1.0 - sq)))).astype(BF16)
            return carry

        lax.fori_loop(0, nch, chunk, 0, unroll=8)

    col = lambda off: pl.BlockSpec((tg, LANES), lambda h, i: (n - 1 - i, off + h))
    one = pl.BlockSpec((1, LANES), lambda h, i: (0, h))
    big = jax.ShapeDtypeStruct((T, D_MODEL), BF16)
    small = jax.ShapeDtypeStruct((1, D_MODEL), F32)
    return pl.pallas_call(
        body, name=name, grid=(8, n),
        in_specs=[col(0), col(8), col(16), col(24), one, one, col(0),
                  pl.BlockSpec((None, nch, LANES, LANES), lambda h, i: (h, n - 1 - i, 0, 0)), col(0)],
        out_specs=[col(0), col(0), col(0), col(0), one, one],
        out_shape=[big, big, big, big, small, small],
        scratch_shapes=[pltpu.VMEM((LANES, LANES), F32)],
        compiler_params=_cparams("arbitrary", "arbitrary"),
    )(proj, proj, proj, proj, lb, gn, o_raw, states, dao)


def lower_bound_fwd(logits, name):
    def body(l_ref, s_ref):
        lv = l_ref[...]
        e = jnp.exp(lv - jnp.max(lv, axis=0, keepdims=True))
        s_ref[...] = e / jnp.sum(e, axis=0, keepdims=True)

    return pl.pallas_call(body, name=name, out_shape=jax.ShapeDtypeStruct(logits.shape, F32))(logits)


def lower_bound_bwd(sm, dlb, name):
    def body(s_ref, d_ref, o_ref):
        s = s_ref[...]
        row = lax.broadcasted_iota(jnp.int32, s.shape, 0)
        o_ref[...] = d_ref[...] * s[1:2, :] * (jnp.where(row == 1, 1.0, 0.0) - s)

    return pl.pallas_call(body, name=name, out_shape=jax.ShapeDtypeStruct(sm.shape, F32))(sm, dlb)


def _pad_rows(flat, mult):
    rows = -(-flat.shape[-1] // D_MODEL)
    rows = -(-rows // mult) * mult
    pad = rows * D_MODEL - flat.shape[-1]
    flat = jnp.pad(flat, [(0, 0)] * (flat.ndim - 1) + [(0, pad)])
    return flat.reshape(flat.shape[:-1] + (rows, D_MODEL))


def _gather_weights(w):
    direct = [n for n in SHARDED if n not in BIASES and w[n].shape[SHARD_AXIS[n]] % LANES == 0]
    packed = [n for n in SHARDED if n not in direct]
    pieces = []
    for nme in packed:
        a = w[nme]
        if nme in BIASES:
            pieces.append(lax.bitcast_convert_type(a, BF16).reshape(-1))
        else:
            pieces.append(a.astype(BF16).reshape(-1))
    flat = _pad_rows(jnp.concatenate(pieces), 16)
    out = all_gather_shards([(w[n].astype(BF16), SHARD_AXIS[n]) for n in direct] + [(flat, None)])
    full = dict(zip(direct, out[:-1]))
    got, off = out[-1].reshape(N_DEV, -1), 0
    for nme in packed:
        shp = w[nme].shape
        cnt = 1
        for s in shp:
            cnt *= s
        if nme in BIASES:
            seg = got[:, off:off + 2 * cnt].reshape((N_DEV,) + shp + (2,))
            seg = lax.bitcast_convert_type(seg, F32)
            off += 2 * cnt
        else:
            seg = got[:, off:off + cnt].reshape((N_DEV,) + shp)
            off += cnt
        full[nme] = jnp.concatenate([seg[d] for d in range(N_DEV)], axis=SHARD_AXIS[nme])
    return full


def _pieces(gfull, axis):
    shp = gfull.shape
    a = gfull.reshape(shp[:axis] + (N_DEV, shp[axis] // N_DEV) + shp[axis + 1:])
    return jnp.moveaxis(a, axis, 0).reshape(N_DEV, -1)


def kernel(x, norm_mix, norm_mlp, norm_final, w_up, w_down, swa_w_qkv, swa_b_qkv, swa_sinks, swa_w_o, hgrn_w_in, hgrn_lb_logits, hgrn_g_norm, hgrn_w_o, fox_w_in, fox_b_in, fox_w_o, loss_target, m_norm_mix, m_norm_mlp, m_norm_final, m_w_up, m_w_down, m_swa_w_qkv, m_swa_b_qkv, m_swa_sinks, m_swa_w_o, m_hgrn_w_in, m_hgrn_lb_logits, m_hgrn_g_norm, m_hgrn_w_o, m_fox_w_in, m_fox_b_in, m_fox_w_o, v_norm_mix, v_norm_mlp, v_norm_final, v_w_up, v_w_down, v_swa_w_qkv, v_swa_b_qkv, v_swa_sinks, v_swa_w_o, v_hgrn_w_in, v_hgrn_lb_logits, v_hgrn_g_norm, v_hgrn_w_o, v_fox_w_in, v_fox_b_in, v_fox_w_o):
    w = dict(norm_mix=norm_mix, norm_mlp=norm_mlp, norm_final=norm_final, w_up=w_up, w_down=w_down,
             swa_w_qkv=swa_w_qkv, swa_b_qkv=swa_b_qkv, swa_sinks=swa_sinks, swa_w_o=swa_w_o, hgrn_w_in=hgrn_w_in,
             hgrn_lb_logits=hgrn_lb_logits, hgrn_g_norm=hgrn_g_norm, hgrn_w_o=hgrn_w_o, fox_w_in=fox_w_in,
             fox_b_in=fox_b_in, fox_w_o=fox_w_o)
    mom = dict(norm_mix=m_norm_mix, norm_mlp=m_norm_mlp, norm_final=m_norm_final, w_up=m_w_up, w_down=m_w_down,
               swa_w_qkv=m_swa_w_qkv, swa_b_qkv=m_swa_b_qkv, swa_sinks=m_swa_sinks, swa_w_o=m_swa_w_o,
               hgrn_w_in=m_hgrn_w_in, hgrn_lb_logits=m_hgrn_lb_logits, hgrn_g_norm=m_hgrn_g_norm, hgrn_w_o=m_hgrn_w_o,
               fox_w_in=m_fox_w_in, fox_b_in=m_fox_b_in, fox_w_o=m_fox_w_o)
    var = dict(norm_mix=v_norm_mix, norm_mlp=v_norm_mlp, norm_final=v_norm_final, w_up=v_w_up, w_down=v_w_down,
               swa_w_qkv=v_swa_w_qkv, swa_b_qkv=v_swa_b_qkv, swa_sinks=v_swa_sinks, swa_w_o=v_swa_w_o,
               hgrn_w_in=v_hgrn_w_in, hgrn_lb_logits=v_hgrn_lb_logits, hgrn_g_norm=v_hgrn_g_norm, hgrn_w_o=v_hgrn_w_o,
               fox_w_in=v_fox_w_in, fox_b_in=v_fox_b_in, fox_w_o=v_fox_w_o)
    T = x.shape[1]
    x0 = x[0]
    tgt = loss_target[0]
    W = _gather_weights(w)
    zeros_b = jnp.zeros((1, 4 * D_MODEL), F32)

    def swa_layer(xin, i, j):
        qkv = norm_matmul(xin, norm_mix[i:i + 1], W['swa_w_qkv'][j], W['swa_b_qkv'][j:j + 1], BF16, f"swa_qkv_L{i}")
        dup = lambda a: jnp.broadcast_to(a.reshape(T, 4, 1, 64), (T, 4, 2, 64)).reshape(T, 4 * LANES)
        kdup, vdup = dup(qkv[:, 1024:1280]), dup(qkv[:, 1280:1536])
        sk = jnp.broadcast_to(jnp.pad(swa_sinks[j].reshape(4, 4), ((0, 0), (0, 4)))[:, :, None], (4, 8, LANES))
        ao, lse = swa_fwd(qkv, kdup, vdup, sk, f"swa_fwd_L{i}")
        xmid = matmul(ao, W['swa_w_o'][j], F32, f"swa_out_L{i}", res=xin)
        return xmid, (qkv, kdup, vdup, sk, ao, lse)

    def swa_layer_bwd(xin, saved, dmid, i, j, grads):
        qkv, kdup, vdup, sk, ao, lse = saved
        dao = matmul(dmid, W['swa_w_o'][j].T, BF16, f"swa_dout_L{i}")
        grads['swa_w_o'][j] = tn_matmul(ao, dmid, f"swa_dwo_L{i}")
        dq, dk, dv, dsk = swa_bwd(qkv, kdup, vdup, sk, ao, lse, dao, f"swa_bwd_L{i}")
        wt = W['swa_w_qkv'][j].T
        spread = lambda a: jnp.pad(a.reshape(4, 64, D_MODEL), ((0, 0), (0, 64), (0, 0))).reshape(4 * LANES, D_MODEL)
        gather = lambda a: a.reshape(a.shape[0], 4, LANES)[:, :, :64].reshape(a.shape[0], 256)
        dx, h, dg = proj_bwd(xin, norm_mix[i:i + 1], dmid,
                             [(dq, wt[:1024]), (dk, spread(wt[1024:1280])), (dv, spread(wt[1280:]))], f"swa_din_L{i}")
        gq, bq = tn_matmul(h, dq, f"swa_dwq_L{i}", colsum=True)
        gk, bk = tn_matmul(h, dk, f"swa_dwk_L{i}", colsum=True)
        gv, bv = tn_matmul(h, dv, f"swa_dwv_L{i}", colsum=True)
        grads['swa_w_qkv'][j] = jnp.concatenate([gq, gather(gk), gather(gv)], axis=1)
        grads['swa_b_qkv'][j] = jnp.concatenate([bq, gather(bk), gather(bv)], axis=1)[0]
        grads['swa_sinks'][j] = dsk[:, :4, 0].reshape(16)
        grads['norm_mix'][i] = dg[0]
        return dx

    lb_soft = lower_bound_fwd(hgrn_lb_logits, "hgrn_lb_fwd")
    lb = lb_soft[1:2]

    def hgrn_layer(xin, i, j):
        proj = norm_matmul(xin, norm_mix[i:i + 1], W['hgrn_w_in'][j], zeros_b, F32, f"hgrn_in_L{i}")
        ao, o_raw, states = hgrn_fwd(proj, lb, hgrn_g_norm[j:j + 1], f"hgrn_fwd_L{i}")
        xmid = matmul(ao, W['hgrn_w_o'][j], F32, f"hgrn_out_L{i}", res=xin)
        return xmid, (proj, ao, o_raw, states)

    def hgrn_layer_bwd(xin, saved, dmid, i, j, grads):
        proj, ao, o_raw, states = saved
        dao = matmul(dmid, W['hgrn_w_o'][j].T, BF16, f"hgrn_dout_L{i}")
        grads['hgrn_w_o'][j] = tn_matmul(ao, dmid, f"hgrn_dwo_L{i}")
        dq, dfl, dv, dgt, dlb, dgn = hgrn_bwd(proj, lb, hgrn_g_norm[j:j + 1], o_raw, states, dao, f"hgrn_bwd_L{i}")
        wt = W['hgrn_w_in'][j].T
        parts = [dq, dfl, dv, dgt]
        dx, h, dg = proj_bwd(xin, norm_mix[i:i + 1], dmid,
                             [(d, wt[n * D_MODEL:(n + 1) * D_MODEL]) for n, d in enumerate(parts)], f"hgrn_din_L{i}")
        grads['hgrn_w_in'][j] = jnp.concatenate(
            [tn_matmul(h, d, f"hgrn_dwin{n}_L{i}") for n, d in enumerate(parts)], axis=1)
        grads['hgrn_g_norm'][j] = dgn[0]
        grads['hgrn_lb_logits'] = lower_bound_bwd(lb_soft, dlb, "hgrn_lb_bwd")
        grads['norm_mix'][i] = dg[0]
        return dx

    def fox_layer(xin, i, j):
        w_in = W['fox_w_in'][j]
        b_in = W['fox_b_in'][j:j + 1]
        qkv = norm_matmul(xin, norm_mix[i:i + 1], w_in[:, :3072], b_in[:, :3072], BF16, f"fox_qkv_L{i}")
        wf = jnp.pad(w_in[:, 3072:], ((0, 0), (0, LANES - 16)))
        bf = jnp.pad(b_in[:, 3072:], ((0, 0), (0, LANES - 16)))
        fl = norm_matmul(xin, norm_mix[i:i + 1], wf, bf, F32, f"fox_f_L{i}")
        qa, ka, bounds = fox_gate_fwd(fl, qkv, f"fox_gate_L{i}")
        st = bounds[:, :4, :16].reshape(bounds.shape[0], 64)
        ao, lse, lmin = fox_fwd(st, qa, ka, qkv, f"fox_fwd_L{i}")
        st = jnp.concatenate([st, lmin[:, :, :2, 0].reshape(lmin.shape[0], 16)], axis=1)
        xmid = matmul(ao, W['fox_w_o'][j], F32, f"fox_out_L{i}", res=xin)
        return xmid, (qkv, fl, qa, ka, ao, lse, wf, st)

    def fox_layer_bwd(xin, saved, dmid, i, j, grads):
        qkv, fl, qa, ka, ao, lse, wf, st = saved
        dao = matmul(dmid, W['fox_w_o'][j].T, BF16, f"fox_dout_L{i}")
        grads['fox_w_o'][j] = tn_matmul(ao, dmid, f"fox_dwo_L{i}")
        delta = fox_delta(dao, ao, f"fox_delta_L{i}")
        dq, aux_q, dk, dv, aux_k = fox_bwd(st, qa, ka, qkv, lse[:, ::64].T.reshape(8, 2, T),
                                           delta[:, :16].T.reshape(8, 2, T), dao, f"fox_bwd_L{i}")
        dcp = jnp.pad(aux_q[:, ::64] - aux_k[:, 3::64], ((0, 0), (0, LANES - 16)))
        dfl = fox_gate_bwd(fl, dcp, f"fox_dgate_L{i}")
        wt = W['fox_w_in'][j][:, :3072].T
        parts = [dq, dk, dv]
        dx, h, dg = proj_bwd(xin, norm_mix[i:i + 1], dmid,
                             [(d, wt[n * D_MODEL:(n + 1) * D_MODEL]) for n, d in enumerate(parts)] + [(dfl, wf.T)],
                             f"fox_din_L{i}")
        gw = [tn_matmul(h, d, f"fox_dw{n}_L{i}", colsum=True) for n, d in enumerate(parts + [dfl])]
        grads['fox_w_in'][j] = jnp.concatenate([g for g, _ in gw[:3]] + [gw[3][0][:, :16]], axis=1)
        grads['fox_b_in'][j] = jnp.concatenate([b for _, b in gw[:3]] + [gw[3][1][:, :16]], axis=1)[0]
        grads['norm_mix'][i] = dg[0]
        return dx

    mixers = [(swa_layer, swa_layer_bwd), (hgrn_layer, hgrn_layer_bwd), (fox_layer, fox_layer_bwd)]

    xs, mids, saves = [x0], [], []
    for i in range(DEPTH):
        xmid, saved = mixers[i % 3][0](xs[-1], i, i // 3)
        mids.append(xmid)
        saves.append(saved)
        xs.append(mlp_fwd(xmid, norm_mlp[i:i + 1], W['w_up'][i], W['w_down'][i], f"mlp_fwd_L{i}"))

    grads = {n: [None] * w[n].shape[0] for n in WEIGHTS if n not in ('norm_final', 'hgrn_lb_logits')}
    loss_part, dx, dgf = final_loss(xs[-1], norm_final.reshape(1, D_MODEL), tgt, "final_loss")
    grads['norm_final'] = dgf[0]
    for i in reversed(range(DEPTH)):
        dmid, h, a, du, dg = mlp_bwd(mids[i], norm_mlp[i:i + 1], W['w_up'][i], W['w_up'][i].T, W['w_down'][i].T, dx,
                                     f"mlp_bwd_L{i}")
        grads['w_up'][i] = tn_matmul(h, du, f"mlp_dwup_L{i}")
        grads['w_down'][i] = tn_matmul(a, dx, f"mlp_dwdown_L{i}")
        grads['norm_mlp'][i] = dg[0]
        dx = mixers[i % 3][1](xs[i], saves[i], dmid, i, i // 3, grads)
    gfull = {n: (g if not isinstance(g, list) else jnp.stack(g)) for n, g in grads.items()}

    mats = [n for n in SHARDED if n not in BIASES]
    view = lambda a: a.reshape(-1, a.shape[-1])
    sends = [_pieces(gfull[n], SHARD_AXIS[n]).astype(BF16).reshape((N_DEV,) + view(w[n]).shape) for n in mats]
    common = jnp.concatenate([gfull[n].reshape(-1) for n in REPLICATED] + [loss_part[0, 0:1]])
    small = jnp.concatenate([jnp.broadcast_to(common[None], (N_DEV, common.shape[0]))]
                            + [_pieces(gfull[n], SHARD_AXIS[n]) for n in BIASES], axis=1)
    recvs = all_to_all_rows(sends + [_pad_rows(small, 16)])
    tail = lambda vals: _pad_rows(jnp.concatenate([vals[n].reshape(-1) for n in REPLICATED] + [jnp.zeros((1,), F32)]
                                                  + [vals[n].reshape(-1) for n in BIASES]), 16)
    res = [{}, {}, {}, {}]
    for nme, rv in zip(mats, recvs):
        outs = reduce_adamw(rv, view(w[nme]), view(mom[nme]), view(var[nme]), f"adamw_{nme}")
        for o, r in zip(outs, res):
            r[nme] = o.reshape(w[nme].shape)
    outs = reduce_adamw(recvs[-1], tail(w), tail(mom), tail(var), "adamw_small")
    off = 0
    for nme in REPLICATED + ['loss'] + list(BIASES):
        cnt = 1 if nme == 'loss' else w[nme].size
        if nme == 'loss':
            loss = outs[0].reshape(-1)[off]
        else:
            for o, r in zip(outs, res):
                r[nme] = o.reshape(-1)[off:off + cnt].reshape(w[nme].shape)
        off += cnt
    return (loss, dx[None], *[res[0][n] for n in WEIGHTS], *[res[1][n] for n in WEIGHTS],
            *[res[2][n] for n in WEIGHTS], *[res[3][n] for n in WEIGHTS])
```
